```python
import math
import jax
import jax.numpy as jnp
from jax import lax
import numpy as np


D_MODEL = 1024
BATCH = 4
SEQ = 4096
DEPTH = 4

N_EVEN = (DEPTH + 1) // 2
N_ODD = DEPTH // 2
ALPHA_DN = (2 * DEPTH) ** 0.25
BETA_DN = (8 * DEPTH) ** -0.25
LN_EPS = 1e-5
RMS_EPS = 1e-6

N_BUCKETS = 32
MAX_DISTANCE = 1024
N_ATT_HEADS = 8
QUERY_CHUNK = 32

MOBA_HEADS = N_ATT_HEADS
MOBA_HEAD_DIM = 64
MOBA_BLOCK = 256
MOBA_TOPK = 3
MOBA_WIDTH = MOBA_HEADS * MOBA_HEAD_DIM

GLA_HEADS = 4
GLA_DK = 64
GLA_DV = 128
GLA_GATE_RANK = 16
GLA_TAU = 16.0
GLA_CHUNK = 64
GLA_QK_WIDTH = GLA_HEADS * GLA_DK
GLA_V_WIDTH = GLA_HEADS * GLA_DV

AB_IN_SIZES = (MOBA_WIDTH, MOBA_WIDTH, MOBA_WIDTH, GLA_QK_WIDTH, GLA_QK_WIDTH, GLA_V_WIDTH, GLA_V_WIDTH, GLA_GATE_RANK)
AB_IN_WIDTH = sum(AB_IN_SIZES)
AB_OUT_WIDTH = MOBA_WIDTH + GLA_V_WIDTH

NSA_HEADS = N_ATT_HEADS
NSA_KV_GROUPS = 2
NSA_HEAD_DIM = 128
NSA_CMP_LEN = 32
NSA_CMP_STRIDE = 16
NSA_SLC_BLOCK = 64
NSA_SLC_TOPN = 16
NSA_WINDOW = 512
NSA_Q_WIDTH = NSA_HEADS * NSA_HEAD_DIM
NSA_KV_WIDTH = NSA_KV_GROUPS * NSA_HEAD_DIM
NSA_IN_SIZES = (NSA_Q_WIDTH,) + (NSA_KV_WIDTH,) * 6 + (NSA_HEADS * 3,)
NSA_IN_WIDTH = sum(NSA_IN_SIZES)

D_FF = 2816
N_EXPERTS = 8
TOP_K = 2
D_FF_EXPERT = 3584
MOE_ROW_BLOCK = 512

kernel_name = "hybrid_moba_gla_nsa_moe_deepnorm"


def _split(a, sizes):
    cuts = np.cumsum(sizes)[:-1].tolist()
    return jnp.split(a, cuts, axis=-1)


def layer_norm(x, g, b):
    xf = x.astype(jnp.float32)
    mu = jnp.mean(xf, axis=-1, keepdims=True)
    var = jnp.mean(jnp.square(xf - mu), axis=-1, keepdims=True)
    return ((xf - mu) * lax.rsqrt(var + LN_EPS) * g.astype(jnp.float32) + b.astype(jnp.float32)).astype(x.dtype)


def t5_bucket(dist):
    max_exact = N_BUCKETS // 2
    n = jnp.maximum(dist, 0)
    nf = jnp.maximum(n, 1).astype(jnp.float32)
    large = max_exact + (jnp.log(nf / max_exact) / math.log(MAX_DISTANCE / max_exact) * (N_BUCKETS - max_exact)).astype(jnp.int32)
    large = jnp.minimum(large, N_BUCKETS - 1)
    return jnp.where(n < max_exact, n, large)


def masked_softmax(logits, mask):
    l = jnp.where(mask, logits.astype(jnp.float32), -jnp.inf)
    m = jnp.max(l, axis=-1, keepdims=True)
    m = jnp.where(jnp.isfinite(m), m, 0.0)
    p = jnp.exp(l - m)
    return p / jnp.maximum(jnp.sum(p, axis=-1, keepdims=True), jnp.finfo(jnp.float32).tiny)


def moba_attention(q, k, v, rel_bias):
    B, S, H, dh = q.shape
    nb = -(-S // MOBA_BLOCK)
    sp = nb * MOBA_BLOCK
    n_sel = min(MOBA_TOPK, nb)
    pad = ((0, 0), (0, 0), (0, sp - S), (0, 0))
    qh = q.transpose(0, 2, 1, 3) * (dh ** -0.5)
    kh = jnp.pad(k.transpose(0, 2, 1, 3), pad)
    vh = jnp.pad(v.transpose(0, 2, 1, 3), pad)
    kb = kh.reshape(B, H, nb, MOBA_BLOCK, dh)
    vb = vh.reshape(B, H, nb, MOBA_BLOCK, dh)
    k_mean = jnp.mean(kb, axis=3)
    blk = jnp.arange(nb)
    offs = jnp.arange(MOBA_BLOCK)
    b_ix = jnp.arange(B)[:, None, None, None]
    h_ix = jnp.arange(H)[None, :, None, None]
    n_g = n_sel * MOBA_BLOCK

    def chunk(q0):
        qc = lax.dynamic_slice_in_dim(qh, q0, QUERY_CHUNK, axis=2)
        t = q0 + jnp.arange(QUERY_CHUNK)
        qb = q0 // MOBA_BLOCK
        gate = jnp.einsum('bhqd,bhnd->bhqn', qc, k_mean).astype(jnp.float32)
        gate = jnp.where(blk < qb, gate, -jnp.inf)
        _, sel = lax.top_k(gate, n_sel)
        k_sel = kb[b_ix, h_ix, sel]
        v_sel = vb[b_ix, h_ix, sel]
        pos_sel = sel[..., None] * MOBA_BLOCK + offs
        s_sel = jnp.einsum('bhqd,bhqnld->bhqnl', qc, k_sel) + rel_bias[t5_bucket(t[:, None, None] - pos_sel), h_ix[..., None]]
        m_sel = jnp.broadcast_to((sel < qb)[..., None], s_sel.shape)
        k_own = lax.dynamic_slice_in_dim(kh, qb * MOBA_BLOCK, MOBA_BLOCK, axis=2)
        v_own = lax.dynamic_slice_in_dim(vh, qb * MOBA_BLOCK, MOBA_BLOCK, axis=2)
        pos_own = qb * MOBA_BLOCK + offs
        s_own = jnp.einsum('bhqd,bhld->bhql', qc, k_own) + rel_bias[t5_bucket(t[:, None] - pos_own[None, :])].transpose(2, 0, 1)
        m_own = jnp.broadcast_to(pos_own[None, :] <= t[:, None], s_own.shape)
        logits = jnp.concatenate([s_sel.reshape(B, H, QUERY_CHUNK, n_g), s_own], axis=-1)
        mask = jnp.concatenate([m_sel.reshape(B, H, QUERY_CHUNK, n_g), m_own], axis=-1)
        p = masked_softmax(logits, mask).astype(v.dtype)
        o = jnp.einsum('bhqnl,bhqnld->bhqd', p[..., :n_g].reshape(B, H, QUERY_CHUNK, n_sel, MOBA_BLOCK), v_sel)
        return o + jnp.einsum('bhql,bhld->bhqd', p[..., n_g:], v_own)

    out = lax.map(chunk, jnp.arange(0, S, QUERY_CHUNK))
    return out.transpose(1, 0, 3, 2, 4).reshape(B, S, H * dh)


def gla_attention(q, k, v, log_alpha):
    B, S, H, dk = q.shape
    dv = v.shape[-1]
    C = GLA_CHUNK
    nc = S // C

    def chunks(a):
        return a.astype(jnp.float32).reshape(B, nc, C, H, a.shape[-1]).transpose(1, 0, 3, 2, 4)

    qc = chunks(q) * (dk ** -0.5)
    kc = chunks(k)
    vc = chunks(v)
    bc = jnp.cumsum(chunks(log_alpha), axis=3)
    causal = jnp.tril(jnp.ones((C, C), dtype=bool))

    def step(state, inp):
        q_, k_, v_, b_ = inp
        b_last = b_[:, :, -1:, :]
        q_dec = q_ * jnp.exp(b_)
        attn = jnp.where(causal, jnp.einsum('bhtd,bhsd->bhts', q_dec, k_ * jnp.exp(-b_)), 0.0)
        o = jnp.einsum('bhts,bhsv->bhtv', attn, v_) + jnp.einsum('bhtd,bhdv->bhtv', q_dec, state)
        state = jnp.exp(b_last[:, :, 0, :, None]) * state + jnp.einsum('bhsd,bhsv->bhdv', k_ * jnp.exp(b_last - b_), v_)
        return state, o

    state0 = jnp.zeros((B, H, dk, dv), jnp.float32)
    _, o = lax.scan(step, state0, (qc, kc, vc, bc))
    return o.transpose(1, 0, 3, 2, 4).reshape(B, S, H, dv)


def mixer_moba_gla(h, w_in, gla_wg2, gla_bg2, gla_norm_g, w_out, rel_bias):
    B, S, _ = h.shape
    mq, mk, mv, gq, gk, gv, gr, glr = _split(h @ w_in, AB_IN_SIZES)
    y_moba = moba_attention(mq.reshape(B, S, MOBA_HEADS, MOBA_HEAD_DIM), mk.reshape(B, S, MOBA_HEADS, MOBA_HEAD_DIM),
                            mv.reshape(B, S, MOBA_HEADS, MOBA_HEAD_DIM), rel_bias)
    log_alpha = jax.nn.log_sigmoid((glr @ gla_wg2 + gla_bg2).astype(jnp.float32)) / GLA_TAU
    o = gla_attention(gq.reshape(B, S, GLA_HEADS, GLA_DK), gk.reshape(B, S, GLA_HEADS, GLA_DK),
                      gv.reshape(B, S, GLA_HEADS, GLA_DV), log_alpha.reshape(B, S, GLA_HEADS, GLA_DK))
    o = o * lax.rsqrt(jnp.mean(jnp.square(o), axis=-1, keepdims=True) + RMS_EPS)
    y_gla = (o.reshape(B, S, GLA_V_WIDTH) * gla_norm_g.astype(jnp.float32)).astype(h.dtype) * jax.nn.silu(gr)
    return jnp.concatenate([y_moba, y_gla], axis=-1) @ w_out


def _compress(kv, pos, w1, w2):
    B, S, G, dh = kv.shape
    n_cmp = (S - NSA_CMP_LEN) // NSA_CMP_STRIDE + 1
    idx = jnp.arange(n_cmp)[:, None] * NSA_CMP_STRIDE + jnp.arange(NSA_CMP_LEN)[None, :]
    blocks = kv[:, idx] + pos[:, None, :]
    blocks = blocks.transpose(0, 1, 3, 2, 4).reshape(B, n_cmp, G, NSA_CMP_LEN * dh)
    return (jax.nn.silu(blocks @ w1) @ w2).transpose(0, 2, 1, 3)


def nsa_attention(q, kc, vc, ks, vs, kw, vw, gates, rel_bias, cmp_pos, w1k, w2k, w1v, w2v):
    B, S, H, dh = q.shape
    G = kc.shape[2]
    HPG = H // G
    QC = QUERY_CHUNK
    W = NSA_WINDOW
    LS = NSA_SLC_BLOCK
    qh = q.reshape(B, S, G, HPG, dh).transpose(0, 2, 3, 1, 4) * (dh ** -0.5)
    gh = gates.reshape(B, S, G, HPG, 3).transpose(0, 2, 3, 1, 4)
    rb = rel_bias.reshape(N_BUCKETS, G, HPG)
    k_cmp = _compress(kc, cmp_pos, w1k, w2k)
    v_cmp = _compress(vc, cmp_pos, w1v, w2v)
    n_cmp = k_cmp.shape[2]
    cmp_start = jnp.arange(n_cmp) * NSA_CMP_STRIDE
    cmp_end = cmp_start + NSA_CMP_LEN - 1
    n_slc = S // LS
    n_top = min(NSA_SLC_TOPN, n_slc)
    slc_start = jnp.arange(n_slc) * LS
    overlap = jnp.maximum(jnp.minimum(cmp_start[:, None] + NSA_CMP_LEN, slc_start[None, :] + LS)
                          - jnp.maximum(cmp_start[:, None], slc_start[None, :]), 0).astype(jnp.float32) / NSA_CMP_LEN
    k_slc = ks.transpose(0, 2, 1, 3).reshape(B, G, n_slc, LS, dh)
    v_slc = vs.transpose(0, 2, 1, 3).reshape(B, G, n_slc, LS, dh)
    wpad = ((0, 0), (0, 0), (W, 0), (0, 0))
    k_win = jnp.pad(kw.transpose(0, 2, 1, 3), wpad)
    v_win = jnp.pad(vw.transpose(0, 2, 1, 3), wpad)
    b_ix = jnp.arange(B)[:, None, None, None]
    g_ix = jnp.arange(G)[None, :, None, None]
    blk = jnp.arange(n_slc)
    offs = jnp.arange(LS)

    def chunk(q0):
        qc = lax.dynamic_slice_in_dim(qh, q0, QC, axis=3)
        gc = lax.dynamic_slice_in_dim(gh, q0, QC, axis=3)
        t = q0 + jnp.arange(QC)
        s_c = jnp.einsum('bghqd,bgnd->bghqn', qc, k_cmp) + rb[t5_bucket(t[:, None] - cmp_end[None, :])].transpose(2, 3, 0, 1)
        p_c = masked_softmax(s_c, cmp_end[None, :] <= t[:, None])
        o_c = jnp.einsum('bghqn,bgnd->bghqd', p_c.astype(qc.dtype), v_cmp)
        imp = jnp.einsum('bghqn,nm->bgqm', p_c, overlap)
        tb = (t // LS)[:, None]
        forced = (blk == 0) | (blk == tb) | (blk == tb - 1)
        imp = jnp.where(blk > tb, -jnp.inf, jnp.where(forced, jnp.inf, imp))
        _, sel = lax.top_k(imp, n_top)
        k_sel = k_slc[b_ix, g_ix, sel].reshape(B, G, QC, n_top * LS, dh)
        v_sel = v_slc[b_ix, g_ix, sel].reshape(B, G, QC, n_top * LS, dh)
        pos = (sel[..., None] * LS + offs).reshape(B, G, QC, n_top * LS)
        bias_s = rb[t5_bucket(t[:, None] - pos), g_ix].transpose(0, 1, 4, 2, 3)
        s_s = jnp.einsum('bghqd,bgqkd->bghqk', qc, k_sel) + bias_s
        p_s = masked_softmax(s_s, (pos <= t[:, None])[:, :, None])
        o_s = jnp.einsum('bghqk,bgqkd->bghqd', p_s.astype(qc.dtype), v_sel)
        kwc = lax.dynamic_slice_in_dim(k_win, q0, W + QC, axis=2)
        vwc = lax.dynamic_slice_in_dim(v_win, q0, W + QC, axis=2)
        pos_w = q0 - W + jnp.arange(W + QC)
        dist = t[:, None] - pos_w[None, :]
        s_w = jnp.einsum('bghqd,bgkd->bghqk', qc, kwc) + rb[t5_bucket(dist)].transpose(2, 3, 0, 1)
        p_w = masked_softmax(s_w, (dist >= 0) & (dist < W) & (pos_w[None, :] >= 0))
        o_w = jnp.einsum('bghqk,bgkd->bghqd', p_w.astype(qc.dtype), vwc)
        o = gc[..., 0:1] * o_c + gc[..., 1:2] * o_s + gc[..., 2:3] * o_w
        return o.astype(q.dtype)

    out = lax.map(chunk, jnp.arange(0, S, QC))
    return out.transpose(1, 0, 4, 2, 3, 5).reshape(B, S, H * dh)


def mixer_nsa(h, w_in, cmp_pos, w1k, w2k, w1v, w2v, w_out, rel_bias):
    B, S, _ = h.shape
    q, kc, vc, ks, vs, kw, vw, g = _split(h @ w_in, NSA_IN_SIZES)
    kvs = lambda a: a.reshape(B, S, NSA_KV_GROUPS, NSA_HEAD_DIM)
    gates = jax.nn.sigmoid(g.astype(jnp.float32)).reshape(B, S, NSA_HEADS, 3)
    y = nsa_attention(q.reshape(B, S, NSA_HEADS, NSA_HEAD_DIM), kvs(kc), kvs(vc), kvs(ks), kvs(vs), kvs(kw), kvs(vw),
                      gates, rel_bias, cmp_pos, w1k, w2k, w1v, w2v)
    return y @ w_out


def swiglu(x, w1, w3, w2):
    return (jax.nn.silu(x @ w1) * (x @ w3)) @ w2


def moe_swiglu(h, w_router, w1, w3, w2):
    B, S, D = h.shape
    n_tok = B * S
    n_assign = n_tok * TOP_K
    n_blocks = (n_assign + N_EXPERTS * (MOE_ROW_BLOCK - 1) + MOE_ROW_BLOCK - 1) // MOE_ROW_BLOCK
    x = h.reshape(n_tok, D)
    logits = (x @ w_router).astype(jnp.float32)
    top_logit, top_e = lax.top_k(logits, TOP_K)
    top_w = jax.nn.softmax(top_logit, axis=-1)
    flat_e = top_e.reshape(-1)
    flat_tok = jnp.repeat(jnp.arange(n_tok), TOP_K)
    flat_w = top_w.reshape(-1)
    order = jnp.argsort(flat_e)
    se, st, sw = flat_e[order], flat_tok[order], flat_w[order]
    counts = jnp.bincount(flat_e, length=N_EXPERTS)
    starts = jnp.cumsum(counts) - counts
    padded = (counts + MOE_ROW_BLOCK - 1) // MOE_ROW_BLOCK * MOE_ROW_BLOCK
    pad_end = jnp.cumsum(padded)
    pad_start = pad_end - padded
    dest = pad_start[se] + jnp.arange(n_assign) - starts[se]
    n_rows = n_blocks * MOE_ROW_BLOCK
    row_tok = jnp.zeros((n_rows,), jnp.int32).at[dest].set(st)
    row_w = jnp.zeros((n_rows,), jnp.float32).at[dest].set(sw)
    blk_e = jnp.minimum(jnp.searchsorted(pad_end, jnp.arange(n_blocks) * MOE_ROW_BLOCK, side='right'), N_EXPERTS - 1)

    def run(args):
        tok, wgt, e = args
        y = swiglu(x[tok], w1[e], w3[e], w2[e])
        return y * wgt[:, None].astype(y.dtype)

    ys = lax.map(run, (row_tok.reshape(n_blocks, MOE_ROW_BLOCK), row_w.reshape(n_blocks, MOE_ROW_BLOCK), blk_e))
    out = jnp.zeros_like(x).at[row_tok].add(ys.reshape(n_rows, D))
    return out.reshape(B, S, D)


def setup_inputs(seed: int = 0) -> dict:
    key = jax.random.key(seed)
    keys = iter(jax.random.split(key, 32))
    D = D_MODEL

    def nrm(shape, scale):
        return jax.random.normal(next(keys), shape, jnp.float32) * scale

    cmp_in = NSA_CMP_LEN * NSA_HEAD_DIM
    return {
        "x": nrm((BATCH, SEQ, D), 1.0),
        "c": nrm((BATCH, D), 1.0),
        "rel_bias": nrm((N_BUCKETS, N_ATT_HEADS), 0.2),
        "ada_w": nrm((DEPTH, D, 6 * D), 0.2 * D ** -0.5),
        "ada_b": nrm((DEPTH, 6 * D), 0.01),
        "ln_g": 1.0 + nrm((DEPTH, 2, D), 0.02),
        "ln_b": nrm((DEPTH, 2, D), 0.02),
        "ab_w_in": nrm((N_EVEN, D, AB_IN_WIDTH), D ** -0.5),
        "gla_wg2": nrm((N_EVEN, GLA_GATE_RANK, GLA_QK_WIDTH), GLA_GATE_RANK ** -0.5),
        "gla_bg2": nrm((N_EVEN, GLA_QK_WIDTH), 0.1),
        "gla_norm_g": 1.0 + nrm((N_EVEN, GLA_V_WIDTH), 0.02),
        "ab_w_out": nrm((N_EVEN, AB_OUT_WIDTH, D), BETA_DN * AB_OUT_WIDTH ** -0.5),
        "ffn_w1": nrm((N_EVEN, D, D_FF), D ** -0.5),
        "ffn_w3": nrm((N_EVEN, D, D_FF), D ** -0.5),
        "ffn_w2": nrm((N_EVEN, D_FF, D), BETA_DN * D_FF ** -0.5),
        "nsa_w_in": nrm((N_ODD, D, NSA_IN_WIDTH), D ** -0.5),
        "nsa_cmp_pos": nrm((N_ODD, NSA_CMP_LEN, NSA_HEAD_DIM), 0.02),
        "nsa_cmp_w1k": nrm((N_ODD, cmp_in, NSA_HEAD_DIM), cmp_in ** -0.5),
        "nsa_cmp_w2k": nrm((N_ODD, NSA_HEAD_DIM, NSA_HEAD_DIM), NSA_HEAD_DIM ** -0.5),
        "nsa_cmp_w1v": nrm((N_ODD, cmp_in, NSA_HEAD_DIM), cmp_in ** -0.5),
        "nsa_cmp_w2v": nrm((N_ODD, NSA_HEAD_DIM, NSA_HEAD_DIM), NSA_HEAD_DIM ** -0.5),
        "nsa_w_out": nrm((N_ODD, NSA_Q_WIDTH, D), BETA_DN * NSA_Q_WIDTH ** -0.5),
        "moe_router": nrm((N_ODD, D, N_EXPERTS), D ** -0.5),
        "moe_w1": nrm((N_ODD, N_EXPERTS, D, D_FF_EXPERT), D ** -0.5),
        "moe_w3": nrm((N_ODD, N_EXPERTS, D, D_FF_EXPERT), D ** -0.5),
        "moe_w2": nrm((N_ODD, N_EXPERTS, D_FF_EXPERT, D), BETA_DN * D_FF_EXPERT ** -0.5),
    }


def reference(x, c, rel_bias, ada_w, ada_b, ln_g, ln_b, ab_w_in, gla_wg2, gla_bg2, gla_norm_g, ab_w_out,
              ffn_w1, ffn_w3, ffn_w2, nsa_w_in, nsa_cmp_pos, nsa_cmp_w1k, nsa_cmp_w2k, nsa_cmp_w1v, nsa_cmp_w2v,
              nsa_w_out, moe_router, moe_w1, moe_w3, moe_w2):
    cond = jax.nn.silu(c)
    for i in range(DEPTH):
        j = i // 2
        mod = (cond @ ada_w[i] + ada_b[i])[:, None, :]
        sh1, sc1, g1, sh2, sc2, g2 = jnp.split(mod, 6, axis=-1)
        h = x * (1 + sc1) + sh1
        if i % 2 == 0:
            y = mixer_moba_gla(h, ab_w_in[j], gla_wg2[j], gla_bg2[j], gla_norm_g[j], ab_w_out[j], rel_bias)
        else:
            y = mixer_nsa(h, nsa_w_in[j], nsa_cmp_pos[j], nsa_cmp_w1k[j], nsa_cmp_w2k[j], nsa_cmp_w1v[j],
                          nsa_cmp_w2v[j], nsa_w_out[j], rel_bias)
        x = layer_norm(ALPHA_DN * x + (1 + g1) * y, ln_g[i, 0], ln_b[i, 0])
        h = x * (1 + sc2) + sh2
        if i % 2 == 0:
            y = swiglu(h, ffn_w1[j], ffn_w3[j], ffn_w2[j])
        else:
            y = moe_swiglu(h, moe_router[j], moe_w1[j], moe_w3[j], moe_w2[j])
        x = layer_norm(ALPHA_DN * x + (1 + g2) * y, ln_g[i, 1], ln_b[i, 1])
    return x
```

```python
import functools
import math

import jax
import jax.numpy as jnp
import numpy as np
from jax import lax
from jax.experimental import pallas as pl
from jax.experimental.pallas import tpu as pltpu

F32 = jnp.float32
BF16 = jnp.bfloat16
NEG_INF = float("-inf")

DEPTH = 4
ALPHA_DN = (2 * DEPTH) ** 0.25
LN_EPS = 1e-5
RMS_EPS = 1e-6

N_BUCKETS = 32
MAX_DISTANCE = 1024

MOBA_HEADS = 8
MOBA_HEAD_DIM = 64
MOBA_BLOCK = 256
MOBA_TOPK = 3
MOBA_WIDTH = MOBA_HEADS * MOBA_HEAD_DIM

GLA_HEADS = 4
GLA_DK = 64
GLA_DV = 128
GLA_GATE_RANK = 16
GLA_TAU = 16.0
GLA_CHUNK = 64
GLA_QK_WIDTH = GLA_HEADS * GLA_DK
GLA_V_WIDTH = GLA_HEADS * GLA_DV

NSA_HEADS = 8
NSA_KV_GROUPS = 2
NSA_HPG = NSA_HEADS // NSA_KV_GROUPS
NSA_HEAD_DIM = 128
NSA_CMP_LEN = 32
NSA_CMP_STRIDE = 16
NSA_SLC_BLOCK = 64
NSA_SLC_TOPN = 16
NSA_WINDOW = 512
NSA_Q_WIDTH = NSA_HEADS * NSA_HEAD_DIM
NSA_KV_WIDTH = NSA_KV_GROUPS * NSA_HEAD_DIM

N_EXPERTS = 8
TOP_K = 2
MOE_ROW_BLOCK = 512

LANE = 128
BIAS_ROWS = 128


def _cparams(sem, vmem_mib=40):
    return pltpu.CompilerParams(dimension_semantics=sem, vmem_limit_bytes=vmem_mib << 20)


def _adaln_kernel(c_ref, w_ref, b_ref, o_ref):
    c = c_ref[...]
    cond = c * jax.nn.sigmoid(c)
    o_ref[0] = jnp.dot(cond, w_ref[0], preferred_element_type=F32) + b_ref[0]


def adaln_mod(c, ada_w, ada_b):
    B, D = c.shape
    L, _, N = ada_w.shape
    rows = 8
    cp = jnp.zeros((rows, D), F32).at[:B].set(c)
    tn = N // 4
    out = pl.pallas_call(
        _adaln_kernel,
        grid=(L, N // tn),
        in_specs=[
            pl.BlockSpec((rows, D), lambda l, j: (0, 0)),
            pl.BlockSpec((1, D, tn), lambda l, j: (l, 0, j)),
            pl.BlockSpec((1, 1, tn), lambda l, j: (l, 0, j)),
        ],
        out_specs=pl.BlockSpec((1, rows, tn), lambda l, j: (l, 0, j)),
        out_shape=jax.ShapeDtypeStruct((L, rows, N), F32),
        compiler_params=_cparams(("arbitrary", "arbitrary")),
        name="adaln_mod",
    )(cp, ada_w, ada_b.reshape(L, 1, N))
    return out[:, :B]


def _modulate_kernel(x_ref, sc_ref, sh_ref, o_ref):
    o_ref[0] = (x_ref[0] * (1.0 + sc_ref[0]) + sh_ref[0]).astype(o_ref.dtype)


def modulate(x, sc, sh, tm=512):
    B, S, D = x.shape
    return pl.pallas_call(
        _modulate_kernel,
        grid=(B, S // tm),
        in_specs=[
            pl.BlockSpec((1, tm, D), lambda b, i: (b, i, 0)),
            pl.BlockSpec((1, 1, D), lambda b, i: (b, 0, 0)),
            pl.BlockSpec((1, 1, D), lambda b, i: (b, 0, 0)),
        ],
        out_specs=pl.BlockSpec((1, tm, D), lambda b, i: (b, i, 0)),
        out_shape=jax.ShapeDtypeStruct((B, S, D), BF16),
        compiler_params=_cparams(("arbitrary", "arbitrary")),
        name="modulate",
    )(x, sc, sh)


def _t5_bucket(dist):
    max_exact = N_BUCKETS // 2
    n = jnp.maximum(dist, 0)
    nf = jnp.maximum(n, 1).astype(jnp.float32)
    large = max_exact + (jnp.log(nf / max_exact) / math.log(MAX_DISTANCE / max_exact) * (N_BUCKETS - max_exact)).astype(jnp.int32)
    large = jnp.minimum(large, N_BUCKETS - 1)
    return jnp.where(n < max_exact, n, large)


def _bias_table_kernel(rb_ref, bm_ref, o_ref, *, rows_per_step):
    h = pl.program_id(0)
    n_steps = bm_ref.shape[0] // rows_per_step

    def body(i, carry):
        r0 = pl.multiple_of(i * rows_per_step, rows_per_step)
        bm = bm_ref[pl.ds(r0, rows_per_step), :]
        acc = jnp.zeros(bm.shape, F32)
        for b in range(N_BUCKETS):
            acc = jnp.where(bm == b, rb_ref[b, h], acc)
        o_ref[0, pl.ds(r0, rows_per_step), :] = acc
        return carry

    lax.fori_loop(0, n_steps, body, 0)


def bias_table(rel_bias, bucket_map, rows_per_step):
    R, W = bucket_map.shape
    H = rel_bias.shape[1]
    return pl.pallas_call(
        functools.partial(_bias_table_kernel, rows_per_step=rows_per_step),
        grid=(H,),
        in_specs=[
            pl.BlockSpec(memory_space=pltpu.SMEM),
            pl.BlockSpec((R, W), lambda h: (0, 0)),
        ],
        out_specs=pl.BlockSpec((1, R, W), lambda h: (h, 0, 0)),
        out_shape=jax.ShapeDtypeStruct((H, R, W), F32),
        compiler_params=_cparams(("arbitrary",)),
        name="bias_table",
    )(rel_bias, bucket_map)


def _toeplitz_geometry(S):
    off = S - BIAS_ROWS
    width = S + BIAS_ROWS
    return off, width


def build_bias_tables(rel_bias, S):
    off, width = _toeplitz_geometry(S)
    i = jnp.arange(BIAS_ROWS, dtype=jnp.int32)[:, None]
    c = jnp.arange(width, dtype=jnp.int32)[None, :]
    toep_map = _t5_bucket(i - c + off)
    n_cmp_pad = S // NSA_CMP_STRIDE
    t = jnp.arange(S, dtype=jnp.int32)[:, None]
    cmp_end = jnp.arange(n_cmp_pad, dtype=jnp.int32)[None, :] * NSA_CMP_STRIDE + NSA_CMP_LEN - 1
    cmp_map = _t5_bucket(t - cmp_end)
    return bias_table(rel_bias, toep_map, 8), bias_table(rel_bias, cmp_map, 128)


def _proj_kernel(h_ref, w_ref, *o_refs, segs):
    h = h_ref[0]
    for (off, width, kind, scale), o_ref in zip(segs, o_refs):
        acc = jnp.dot(h, w_ref[:, off:off + width], preferred_element_type=F32)
        if scale != 1.0:
            acc = acc * scale
        if kind == "flat":
            o_ref[0] = acc.astype(o_ref.dtype)
        else:
            for g in range(width // LANE):
                o_ref[0, g] = acc[:, g * LANE:(g + 1) * LANE].astype(o_ref.dtype)


def project(h, w_cat, segs, out_dtypes, tm=512):
    B, S, D = h.shape
    out_shapes, out_specs = [], []
    for (off, width, kind, scale), dt in zip(segs, out_dtypes):
        if kind == "flat":
            out_shapes.append(jax.ShapeDtypeStruct((B, S, width), dt))
            out_specs.append(pl.BlockSpec((1, tm, width), lambda b, i: (b, i, 0)))
        else:
            G = width // LANE
            out_shapes.append(jax.ShapeDtypeStruct((B, G, S, LANE), dt))
            out_specs.append(pl.BlockSpec((1, G, tm, LANE), lambda b, i: (b, 0, i, 0)))
    return pl.pallas_call(
        functools.partial(_proj_kernel, segs=segs),
        grid=(B, S // tm),
        in_specs=[
            pl.BlockSpec((1, tm, D), lambda b, i: (b, i, 0)),
            pl.BlockSpec(w_cat.shape, lambda b, i: (0, 0)),
        ],
        out_specs=out_specs,
        out_shape=out_shapes,
        compiler_params=_cparams(("arbitrary", "arbitrary"), 48),
        name="in_proj",
    )(h, w_cat)


def _moba_kernel(q_ref, k_ref, v_ref, tb_ref, o_ref, kmean_sc, *, S, heads_per_step):
    L = MOBA_BLOCK
    dh = MOBA_HEAD_DIM
    nb = S // L
    qi = pl.program_id(2)
    off, _ = _toeplitz_geometry(S)

    @pl.when(qi == 0)
    def _():
        kf = k_ref[0].astype(F32)
        kmean_sc[...] = jnp.mean(kf.reshape(nb, L, kf.shape[-1]), axis=1)

    q0 = qi * L
    row = lax.broadcasted_iota(jnp.int32, (L, L), 0)
    col = lax.broadcasted_iota(jnp.int32, (L, L), 1)
    blk_lane = lax.broadcasted_iota(jnp.int32, (L, nb), 1)
    outs = []
    for hh in range(heads_per_step):
        lanes = slice(hh * dh, (hh + 1) * dh)
        q = q_ref[0, :, lanes]
        kmean = kmean_sc[:, lanes]
        gate = lax.dot_general(q.astype(F32), kmean, (((1,), (1,)), ((), ())),
                               precision=lax.Precision.HIGHEST, preferred_element_type=F32)
        valid = blk_lane < qi
        gm = jnp.where(valid, gate, NEG_INF)
        rank = jnp.zeros((L, nb), jnp.int32)
        for m in range(nb):
            gmm = gm[:, m:m + 1]
            beats = (gmm > gm) | ((gmm == gm) & (m < blk_lane))
            rank = rank + beats.astype(jnp.int32)
        sel = jnp.where(valid & (rank < MOBA_TOPK), 1.0, 0.0)

        def bias_tile(k0):
            halves = []
            for r in range(L // BIAS_ROWS):
                c0 = pl.multiple_of(off - (q0 + r * BIAS_ROWS) + k0, LANE)
                halves.append(tb_ref[hh, :, pl.ds(c0, L)])
            return jnp.concatenate(halves, axis=0)

        k_own = k_ref[0, pl.ds(pl.multiple_of(q0, L), L), lanes]
        v_own = v_ref[0, pl.ds(pl.multiple_of(q0, L), L), lanes]
        s = lax.dot_general(q, k_own, (((1,), (1,)), ((), ())), preferred_element_type=F32) + bias_tile(q0)
        s = jnp.where(col <= row, s, NEG_INF)
        m0 = jnp.max(s, axis=-1, keepdims=True)
        p = jnp.exp(s - m0)
        l0 = jnp.sum(p, axis=-1, keepdims=True)
        acc0 = jnp.dot(p.astype(BF16), v_own, preferred_element_type=F32)

        def body(kb, carry):
            m, l, acc = carry
            k0 = pl.multiple_of(kb * L, L)
            kt = k_ref[0, pl.ds(k0, L), lanes]
            vt = v_ref[0, pl.ds(k0, L), lanes]
            selcol = jnp.max(jnp.where(blk_lane == kb, sel, 0.0), axis=-1, keepdims=True) > 0.5
            s = lax.dot_general(q, kt, (((1,), (1,)), ((), ())), preferred_element_type=F32) + bias_tile(k0)
            s = jnp.where(selcol, s, NEG_INF)
            m_new = jnp.maximum(m, jnp.max(s, axis=-1, keepdims=True))
            alpha = jnp.exp(m - m_new)
            p = jnp.exp(s - m_new)
            l_new = alpha * l + jnp.sum(p, axis=-1, keepdims=True)
            acc_new = alpha * acc + jnp.dot(p.astype(BF16), vt, preferred_element_type=F32)
            return m_new, l_new, acc_new

        m, l, acc = lax.fori_loop(0, qi, body, (m0, l0, acc0))
        outs.append(acc / l)
    o_ref[0] = jnp.concatenate(outs, axis=-1).astype(o_ref.dtype)


def moba_attention(q, k, v, toep):
    B, S, _ = q.shape
    L = MOBA_BLOCK
    hps = LANE // MOBA_HEAD_DIM
    n_hp = MOBA_HEADS // hps
    nb = S // L
    _, width = _toeplitz_geometry(S)
    return pl.pallas_call(
        functools.partial(_moba_kernel, S=S, heads_per_step=hps),
        grid=(n_hp, B, nb),
        in_specs=[
            pl.BlockSpec((1, L, LANE), lambda hp, b, i: (b, i, hp)),
            pl.BlockSpec((1, S, LANE), lambda hp, b, i: (b, 0, hp)),
            pl.BlockSpec((1, S, LANE), lambda hp, b, i: (b, 0, hp)),
            pl.BlockSpec((hps, BIAS_ROWS, width), lambda hp, b, i: (hp, 0, 0)),
        ],
        out_specs=pl.BlockSpec((1, L, LANE), lambda hp, b, i: (b, i, hp)),
        out_shape=jax.ShapeDtypeStruct((B, S, MOBA_WIDTH), BF16),
        scratch_shapes=[pltpu.VMEM((nb, LANE), F32)],
        compiler_params=_cparams(("arbitrary", "arbitrary", "arbitrary"), 40),
        name="moba_attn",
    )(q, k, v, toep)


def _log_sigmoid(z):
    return jnp.minimum(z, 0.0) - jnp.log1p(jnp.exp(-jnp.abs(z)))


def _gla_kernel(q_ref, k_ref, v_ref, r_ref, lr_ref, wg2_ref, bg2_ref, ng_ref, o_ref, st_sc, *, batch):
    C = GLA_CHUNK
    dk, dv = GLA_DK, GLA_DV

    @pl.when(pl.program_id(0) == 0)
    def _():
        st_sc[...] = jnp.zeros(st_sc.shape, F32)

    row = lax.broadcasted_iota(jnp.int32, (C, C), 0)
    col = lax.broadcasted_iota(jnp.int32, (C, C), 1)
    causal = col <= row
    tri = jnp.where(causal, 1.0, 0.0)
    for b in range(batch):
        z = jnp.dot(lr_ref[b], wg2_ref[...], preferred_element_type=F32) + bg2_ref[...]
        la = _log_sigmoid(z) / GLA_TAU
        bc = jnp.dot(tri, la, precision=lax.Precision.HIGHEST, preferred_element_type=F32)
        b_last = bc[C - 1:C, :]
        qd = q_ref[b] * (dk ** -0.5) * jnp.exp(bc)
        kd = k_ref[b] * jnp.exp(-bc)
        kl = k_ref[b] * jnp.exp(b_last - bc)
        e_last = jnp.exp(b_last)
        v = v_ref[b]
        r = r_ref[b]
        outs = []
        for h in range(GLA_HEADS):
            ks = slice(h * dk, (h + 1) * dk)
            vs = slice(h * dv, (h + 1) * dv)
            qh = qd[:, ks].astype(BF16)
            kh = kd[:, ks].astype(BF16)
            vh = v[:, vs].astype(BF16)
            attn = lax.dot_general(qh, kh, (((1,), (1,)), ((), ())), preferred_element_type=F32)
            attn = jnp.where(causal, attn, 0.0)
            st = st_sc[b, h]
            o = jnp.dot(attn.astype(BF16), vh, preferred_element_type=F32)
            o = o + lax.dot_general(qh, st.astype(BF16), (((1,), (1,)), ((), ())), preferred_element_type=F32)
            upd = lax.dot_general(vh, kl[:, ks].astype(BF16), (((0,), (0,)), ((), ())), preferred_element_type=F32)
            st_sc[b, h] = st * e_last[:, ks] + upd
            o = o * lax.rsqrt(jnp.mean(o * o, axis=-1, keepdims=True) + RMS_EPS)
            rg = r[:, vs]
            outs.append(o * ng_ref[:, vs] * (rg * jax.nn.sigmoid(rg)))
        o_ref[b] = jnp.concatenate(outs, axis=-1).astype(o_ref.dtype)


def gla_mixer(q, k, v, r, lr, wg2_pad, bg2, norm_g):
    B, S, _ = q.shape
    C = GLA_CHUNK
    nc = S // C
    blk = lambda w: pl.BlockSpec((B, C, w), lambda c: (0, c, 0))
    full = lambda a: pl.BlockSpec(a.shape, lambda c: (0,) * a.ndim)
    return pl.pallas_call(
        functools.partial(_gla_kernel, batch=B),
        grid=(nc,),
        in_specs=[blk(GLA_QK_WIDTH), blk(GLA_QK_WIDTH), blk(GLA_V_WIDTH), blk(GLA_V_WIDTH), blk(LANE),
                  full(wg2_pad), full(bg2), full(norm_g)],
        out_specs=blk(GLA_V_WIDTH),
        out_shape=jax.ShapeDtypeStruct((B, S, GLA_V_WIDTH), BF16),
        scratch_shapes=[pltpu.VMEM((B, GLA_HEADS, GLA_DV, GLA_DK), F32)],
        compiler_params=_cparams(("arbitrary",)),
        name="gla",
    )(q, k, v, r, lr, wg2_pad, bg2, norm_g)


def _deepnorm_ln(x, y, gate, ln_g, ln_b):
    z = ALPHA_DN * x + (1.0 + gate) * y
    mu = jnp.mean(z, axis=-1, keepdims=True)
    zc = z - mu
    var = jnp.mean(zc * zc, axis=-1, keepdims=True)
    return zc * lax.rsqrt(var + LN_EPS) * ln_g + ln_b


def _out_ln_kernel(*refs, n_parts, matmul, has_next):
    y_refs = refs[:n_parts]
    pos = n_parts
    if matmul:
        w_refs = refs[pos:pos + n_parts]
        pos += n_parts
    x_ref, g_ref, lng_ref, lnb_ref = refs[pos:pos + 4]
    pos += 4
    if has_next:
        sc_ref, sh_ref = refs[pos:pos + 2]
        pos += 2
    o_ref = refs[pos]
    y = None
    for i in range(n_parts):
        part = jnp.dot(y_refs[i][0], w_refs[i][...], preferred_element_type=F32) if matmul else y_refs[i][0]
        y = part if y is None else y + part
    xn = _deepnorm_ln(x_ref[0], y, g_ref[0], lng_ref[...], lnb_ref[...])
    o_ref[0] = xn
    if has_next:
        refs[pos + 1][0] = (xn * (1.0 + sc_ref[0]) + sh_ref[0]).astype(BF16)


def out_ln(ys, ws, x, gate, ln_g, ln_b, nxt, tm=512):
    B, S, D = x.shape
    tile = lambda w: pl.BlockSpec((1, tm, w), lambda b, i: (b, i, 0))
    vec = pl.BlockSpec((1, 1, D), lambda b, i: (b, 0, 0))
    par = pl.BlockSpec((1, D), lambda b, i: (0, 0))
    args = list(ys)
    in_specs = [tile(y.shape[-1]) for y in ys]
    if ws is not None:
        args += list(ws)
        in_specs += [pl.BlockSpec(w.shape, lambda b, i: (0, 0)) for w in ws]
    args += [x, gate, ln_g, ln_b]
    in_specs += [tile(D), vec, par, par]
    out_shape = [jax.ShapeDtypeStruct((B, S, D), F32)]
    out_specs = [tile(D)]
    if nxt is not None:
        args += list(nxt)
        in_specs += [vec, vec]
        out_shape.append(jax.ShapeDtypeStruct((B, S, D), BF16))
        out_specs.append(tile(D))
    res = pl.pallas_call(
        functools.partial(_out_ln_kernel, n_parts=len(ys), matmul=ws is not None, has_next=nxt is not None),
        grid=(B, S // tm),
        in_specs=in_specs,
        out_specs=out_specs,
        out_shape=out_shape,
        compiler_params=_cparams(("arbitrary", "arbitrary")),
        name="out_ln",
    )(*args)
    return (res[0], res[1]) if nxt is not None else (res[0], None)


def _ffn_ln_kernel(h_ref, w1_ref, w3_ref, w2_ref, x_ref, g_ref, lng_ref, lnb_ref, sc_ref, sh_ref,
                   o_ref, hn_ref, acc_sc):
    j = pl.program_id(2)

    @pl.when(j == 0)
    def _():
        acc_sc[...] = jnp.zeros(acc_sc.shape, F32)

    h = h_ref[0]
    a = jnp.dot(h, w1_ref[...], preferred_element_type=F32)
    b = jnp.dot(h, w3_ref[...], preferred_element_type=F32)
    u = (a * jax.nn.sigmoid(a)) * b
    acc_sc[...] += jnp.dot(u.astype(BF16), w2_ref[...], preferred_element_type=F32)

    @pl.when(j == pl.num_programs(2) - 1)
    def _():
        xn = _deepnorm_ln(x_ref[0], acc_sc[...], g_ref[0], lng_ref[...], lnb_ref[...])
        o_ref[0] = xn
        hn_ref[0] = (xn * (1.0 + sc_ref[0]) + sh_ref[0]).astype(BF16)


def ffn_ln(h, w1, w3, w2, x, gate, ln_g, ln_b, nxt, tm=1024, tf=256):
    B, S, D = x.shape
    F = w1.shape[1]
    tile = pl.BlockSpec((1, tm, D), lambda b, i, j: (b, i, 0))
    vec = pl.BlockSpec((1, 1, D), lambda b, i, j: (b, 0, 0))
    par = pl.BlockSpec((1, D), lambda b, i, j: (0, 0))
    return pl.pallas_call(
        _ffn_ln_kernel,
        grid=(B, S // tm, F // tf),
        in_specs=[tile,
                  pl.BlockSpec((D, tf), lambda b, i, j: (0, j)),
                  pl.BlockSpec((D, tf), lambda b, i, j: (0, j)),
                  pl.BlockSpec((tf, D), lambda b, i, j: (j, 0)),
                  tile, vec, par, par, vec, vec],
        out_specs=[tile, tile],
        out_shape=[jax.ShapeDtypeStruct((B, S, D), F32), jax.ShapeDtypeStruct((B, S, D), BF16)],
        scratch_shapes=[pltpu.VMEM((tm, D), F32)],
        compiler_params=_cparams(("arbitrary", "arbitrary", "arbitrary"), 48),
        name="ffn_ln",
    )(h, w1, w3, w2, x, gate, ln_g, ln_b, nxt[0], nxt[1])


def _compress_kernel(x_ref, pos_ref, w1_ref, w2_ref, o_ref):
    dh = NSA_HEAD_DIM
    x = x_ref[0, 0]
    uv = jnp.dot(x, w1_ref[...], preferred_element_type=F32)
    pc = jnp.dot(pos_ref[...], w1_ref[...], preferred_element_type=F32)
    c = pc[0:1, :dh] + pc[1:2, dh:]
    n = uv.shape[0]
    nxt = pltpu.roll(uv[:, dh:], n - 1, 0)
    hid = uv[:, :dh] + nxt + c
    act = hid * jax.nn.sigmoid(hid)
    o_ref[0, 0] = jnp.dot(act.astype(BF16), w2_ref[...], preferred_element_type=F32).astype(o_ref.dtype)


def nsa_compress(kv, pos2, w1cat, w2):
    B, G, S, dh = kv.shape
    seg = NSA_CMP_STRIDE
    n = S // seg
    x = kv.reshape(B, G, n, seg * dh)
    return pl.pallas_call(
        _compress_kernel,
        grid=(B, G),
        in_specs=[
            pl.BlockSpec((1, 1, n, seg * dh), lambda b, g: (b, g, 0, 0)),
            pl.BlockSpec(pos2.shape, lambda b, g: (0, 0)),
            pl.BlockSpec(w1cat.shape, lambda b, g: (0, 0)),
            pl.BlockSpec(w2.shape, lambda b, g: (0, 0)),
        ],
        out_specs=pl.BlockSpec((1, 1, n, dh), lambda b, g: (b, g, 0, 0)),
        out_shape=jax.ShapeDtypeStruct((B, G, n, dh), BF16),
        compiler_params=_cparams(("arbitrary", "arbitrary")),
        name="nsa_compress",
    )(x, pos2, w1cat, w2)


def _nsa_kernel(q_ref, kc_ref, vc_ref, ks_ref, vs_ref, kw_ref, vw_ref, gt_ref, cb_ref, tb_ref, o_ref, *, S, TQ):
    dh = NSA_HEAD_DIM
    HPG = NSA_HPG
    LS = NSA_SLC_BLOCK
    TK = 2 * TQ
    R = HPG * TQ
    n_cmp = (S - NSA_CMP_LEN) // NSA_CMP_STRIDE + 1
    n_cp = S // NSA_CMP_STRIDE
    n_slc = S // LS
    n_top = min(NSA_SLC_TOPN, n_slc)
    off, _ = _toeplitz_geometry(S)
    qi = pl.program_id(2)
    q0 = qi * TQ
    dims_nt = (((1,), (1,)), ((), ()))

    q4 = jnp.concatenate([q_ref[0, :, h * dh:(h + 1) * dh] for h in range(HPG)], axis=0)
    t_row = q0 + (lax.broadcasted_iota(jnp.int32, (R, 1), 0) & (TQ - 1))

    s = lax.dot_general(q4, kc_ref[0, 0], dims_nt, preferred_element_type=F32)
    s = s + jnp.concatenate([cb_ref[h] for h in range(HPG)], axis=0)
    n_lane = lax.broadcasted_iota(jnp.int32, (R, n_cp), 1)
    valid_c = (n_lane * NSA_CMP_STRIDE + (NSA_CMP_LEN - 1) <= t_row) & (n_lane < n_cmp)
    s = jnp.where(valid_c, s, NEG_INF)
    m = jnp.max(s, axis=-1, keepdims=True)
    m = jnp.where(m == NEG_INF, 0.0, m)
    e = jnp.exp(s - m)
    p_c = e / jnp.maximum(jnp.sum(e, axis=-1, keepdims=True), jnp.finfo(F32).tiny)
    o_c = jnp.dot(p_c.astype(BF16), vc_ref[0, 0], preferred_element_type=F32)

    p_sum = p_c[0:TQ]
    for h in range(1, HPG):
        p_sum = p_sum + p_c[h * TQ:(h + 1) * TQ]
    cn = lax.broadcasted_iota(jnp.int32, (n_cp, LANE), 0)
    cm = lax.broadcasted_iota(jnp.int32, (n_cp, LANE), 1)
    c_lo = cn * NSA_CMP_STRIDE
    ov = jnp.maximum(jnp.minimum(c_lo + NSA_CMP_LEN, cm * LS + LS) - jnp.maximum(c_lo, cm * LS), 0)
    overlap = ov.astype(F32) / NSA_CMP_LEN
    imp = jnp.dot(p_sum, overlap, precision=lax.Precision.HIGHEST, preferred_element_type=F32)
    imp_t = imp.T
    blk = lax.broadcasted_iota(jnp.int32, (LANE, TQ), 0)
    tq_lane = q0 + lax.broadcasted_iota(jnp.int32, (LANE, TQ), 1)
    tb = lax.shift_right_logical(tq_lane, LS.bit_length() - 1)
    forced = (blk == 0) | (blk == tb) | (blk == tb - 1)
    key = jnp.where((blk > tb) | (blk >= n_slc), NEG_INF, jnp.where(forced, jnp.inf, imp_t))
    rank = jnp.zeros((LANE, TQ), jnp.int32)
    for mm in range(n_slc):
        km = key[mm:mm + 1, :]
        beats = (km > key) | ((km == key) & (mm < blk))
        rank = rank + beats.astype(jnp.int32)
    sel_t = jnp.where((rank < n_top) & (blk <= tb) & (blk < n_slc), 1.0, 0.0)
    sel = sel_t.T.astype(BF16)

    def toep_tile(k0, width):
        c0 = pl.multiple_of(off - q0 + k0, LANE)
        return jnp.concatenate([tb_ref[h, :, pl.ds(c0, width)] for h in range(HPG)], axis=0)

    def flash_step(carry, s, mask, v_tile):
        m, l, acc = carry
        s = jnp.where(mask, s, NEG_INF)
        m_new = jnp.maximum(m, jnp.max(s, axis=-1, keepdims=True))
        alpha = jnp.exp(m - m_new)
        p = jnp.exp(s - m_new)
        l_new = alpha * l + jnp.sum(p, axis=-1, keepdims=True)
        acc_new = alpha * acc + jnp.dot(p.astype(BF16), v_tile, preferred_element_type=F32)
        return m_new, l_new, acc_new

    init = (jnp.full((R, 1), NEG_INF, F32), jnp.zeros((R, 1), F32), jnp.zeros((R, dh), F32))

    e_blk = lax.broadcasted_iota(jnp.int32, (LANE, TK), 0)
    e_key = lax.broadcasted_iota(jnp.int32, (LANE, TK), 1)
    kpos_s = lax.broadcasted_iota(jnp.int32, (R, TK), 1)

    def slc_body(kt, carry):
        k0 = pl.multiple_of(kt * TK, TK)
        kt_ = ks_ref[0, 0, pl.ds(k0, TK), :]
        vt_ = vs_ref[0, 0, pl.ds(k0, TK), :]
        expand = jnp.where(lax.shift_right_logical(k0 + e_key, LS.bit_length() - 1) == e_blk, 1.0, 0.0).astype(BF16)
        m_q = jnp.dot(sel, expand, preferred_element_type=F32)
        m_r = jnp.concatenate([m_q] * HPG, axis=0)
        s = lax.dot_general(q4, kt_, dims_nt, preferred_element_type=F32) + toep_tile(k0, TK)
        mask = (m_r > 0.5) & (k0 + kpos_s <= t_row)
        return flash_step(carry, s, mask, vt_)

    n_kt = (q0 + TQ - 1) // TK + 1
    m_s, l_s, acc_s = lax.fori_loop(0, n_kt, slc_body, init)
    o_s = acc_s / l_s

    kpos_w = lax.broadcasted_iota(jnp.int32, (R, TQ), 1)

    def win_body(j, carry):
        k0 = pl.multiple_of(q0 - j * TQ, TQ)
        kt_ = kw_ref[0, 0, pl.ds(k0, TQ), :]
        vt_ = vw_ref[0, 0, pl.ds(k0, TQ), :]
        s = lax.dot_general(q4, kt_, dims_nt, preferred_element_type=F32) + toep_tile(k0, TQ)
        dist = t_row - (k0 + kpos_w)
        mask = (dist >= 0) & (dist < NSA_WINDOW)
        return flash_step(carry, s, mask, vt_)

    n_wt = jnp.minimum(qi, NSA_WINDOW // TQ) + 1
    m_w, l_w, acc_w = lax.fori_loop(0, n_wt, win_body, init)
    o_w = acc_w / l_w

    gates = jax.nn.sigmoid(gt_ref[0, 0])
    outs = []
    for h in range(HPG):
        rows = slice(h * TQ, (h + 1) * TQ)
        o = (gates[:, h:h + 1] * o_c[rows] + gates[:, HPG + h:HPG + h + 1] * o_s[rows]
             + gates[:, 2 * HPG + h:2 * HPG + h + 1] * o_w[rows])
        outs.append(o)
    o_ref[0] = jnp.concatenate(outs, axis=-1).astype(o_ref.dtype)


def nsa_attention(q, kc, vc, ks, vs, kw, vw, gates, cmpb, toep, TQ=128):
    B, S, _ = q.shape
    G, HPG, dh = NSA_KV_GROUPS, NSA_HPG, NSA_HEAD_DIM
    n_cp = S // NSA_CMP_STRIDE
    _, width = _toeplitz_geometry(S)
    full_kv = pl.BlockSpec((1, 1, S, dh), lambda b, g, i: (b, g, 0, 0))
    cmp_kv = pl.BlockSpec((1, 1, n_cp, dh), lambda b, g, i: (b, g, 0, 0))
    return pl.pallas_call(
        functools.partial(_nsa_kernel, S=S, TQ=TQ),
        grid=(B, G, S // TQ),
        in_specs=[
            pl.BlockSpec((1, TQ, HPG * dh), lambda b, g, i: (b, i, g)),
            cmp_kv, cmp_kv, full_kv, full_kv, full_kv, full_kv,
            pl.BlockSpec((1, 1, TQ, LANE), lambda b, g, i: (b, g, i, 0)),
            pl.BlockSpec((HPG, TQ, n_cp), lambda b, g, i: (g, i, 0)),
            pl.BlockSpec((HPG, BIAS_ROWS, width), lambda b, g, i: (g, 0, 0)),
        ],
        out_specs=pl.BlockSpec((1, TQ, HPG * dh), lambda b, g, i: (b, i, g)),
        out_shape=jax.ShapeDtypeStruct((B, S, NSA_Q_WIDTH), BF16),
        compiler_params=_cparams(("arbitrary", "arbitrary", "arbitrary"), 48),
        name="nsa_attn",
    )(q, kc, vc, ks, vs, kw, vw, gates, cmpb, toep)


def _router_kernel(h_ref, wr_ref, e_ref, w_ref):
    logits = jnp.dot(h_ref[...].astype(F32), wr_ref[...], precision=lax.Precision.HIGHEST,
                     preferred_element_type=F32)
    lane = lax.broadcasted_iota(jnp.int32, logits.shape, 1)
    logits = jnp.where(lane < N_EXPERTS, logits, NEG_INF)
    l1 = jnp.max(logits, axis=-1, keepdims=True)
    i1 = jnp.min(jnp.where(logits == l1, lane, LANE), axis=-1, keepdims=True)
    rest = jnp.where(lane == i1, NEG_INF, logits)
    l2 = jnp.max(rest, axis=-1, keepdims=True)
    i2 = jnp.min(jnp.where(rest == l2, lane, LANE), axis=-1, keepdims=True)
    e2 = jnp.exp(l2 - l1)
    den = 1.0 + e2
    e_ref[...] = jnp.where(lane == 0, i1, jnp.where(lane == 1, i2, 0))
    w_ref[...] = jnp.where(lane == 0, 1.0 / den, jnp.where(lane == 1, e2 / den, 0.0))


def moe_route(h2, wr_pad, tm=1024):
    N, D = h2.shape
    return pl.pallas_call(
        _router_kernel,
        grid=(N // tm,),
        in_specs=[pl.BlockSpec((tm, D), lambda i: (i, 0)), pl.BlockSpec(wr_pad.shape, lambda i: (0, 0))],
        out_specs=[pl.BlockSpec((tm, LANE), lambda i: (i, 0)), pl.BlockSpec((tm, LANE), lambda i: (i, 0))],
        out_shape=[jax.ShapeDtypeStruct((N, LANE), jnp.int32), jax.ShapeDtypeStruct((N, LANE), F32)],
        compiler_params=_cparams(("arbitrary",)),
        name="moe_router",
    )(h2, wr_pad)


def _moe_ffn_kernel(be_ref, nu_ref, x_ref, rw_ref, w1_ref, w3_ref, w2_ref, o_ref, acc_sc):
    i = pl.program_id(0)
    j = pl.program_id(1)
    last = pl.num_programs(1) - 1
    used = i < nu_ref[0]

    @pl.when(j == 0)
    def _():
        acc_sc[...] = jnp.zeros(acc_sc.shape, F32)

    @pl.when(used)
    def _():
        x = x_ref[...]
        a = jnp.dot(x, w1_ref[0], preferred_element_type=F32)
        b = jnp.dot(x, w3_ref[0], preferred_element_type=F32)
        u = (a * jax.nn.sigmoid(a)) * b
        acc_sc[...] += jnp.dot(u.astype(BF16), w2_ref[0], preferred_element_type=F32)

    @pl.when(j == last)
    def _():
        o_ref[...] = acc_sc[...] * rw_ref[0]


def moe_ffn(xs, row_w, blk_e, n_used, w1, w3, w2, tf=256):
    n_rows, D = xs.shape
    RB = MOE_ROW_BLOCK
    n_blocks = n_rows // RB
    F = w1.shape[2]
    nf = F // tf

    def wcol(i, j, be, nu):
        return (be[i], 0, jnp.where(i < nu[0], j, nf - 1))

    def wrow(i, j, be, nu):
        return (be[i], jnp.where(i < nu[0], j, nf - 1), 0)

    grid_spec = pltpu.PrefetchScalarGridSpec(
        num_scalar_prefetch=2,
        grid=(n_blocks, nf),
        in_specs=[
            pl.BlockSpec((RB, D), lambda i, j, be, nu: (i, 0)),
            pl.BlockSpec((1, RB, 1), lambda i, j, be, nu: (i, 0, 0)),
            pl.BlockSpec((1, D, tf), wcol),
            pl.BlockSpec((1, D, tf), wcol),
            pl.BlockSpec((1, tf, D), wrow),
        ],
        out_specs=pl.BlockSpec((RB, D), lambda i, j, be, nu: (i, 0)),
        scratch_shapes=[pltpu.VMEM((RB, D), F32)],
    )
    return pl.pallas_call(
        _moe_ffn_kernel,
        grid_spec=grid_spec,
        out_shape=jax.ShapeDtypeStruct((n_rows, D), F32),
        compiler_params=_cparams(("arbitrary", "arbitrary")),
        name="moe_ffn",
    )(blk_e, n_used, xs, row_w, w1, w3, w2)


def moe_dispatch(e_idx, w_top, n_tok):
    RB = MOE_ROW_BLOCK
    n_assign = n_tok * TOP_K
    n_blocks = (n_assign + N_EXPERTS * (RB - 1) + RB - 1) // RB
    n_rows = n_blocks * RB
    flat_e = e_idx[:, :TOP_K].reshape(-1)
    flat_w = w_top[:, :TOP_K].reshape(-1)
    onehot = (flat_e[:, None] == jnp.arange(N_EXPERTS, dtype=jnp.int32)[None, :]).astype(jnp.int32)
    csum = jnp.cumsum(onehot, axis=0)
    counts = csum[-1]
    pos_in_e = jnp.sum(csum * onehot, axis=1) - 1
    padded = (counts + RB - 1) // RB * RB
    pad_end = jnp.cumsum(padded)
    pad_start = pad_end - padded
    dest = (jnp.sum(pad_start[None, :] * onehot, axis=1) + pos_in_e).astype(jnp.int32)
    flat_tok = jnp.arange(n_assign, dtype=jnp.int32) // TOP_K
    row_tok = jnp.zeros((n_rows,), jnp.int32).at[dest].set(flat_tok)
    row_w = jnp.zeros((n_rows,), F32).at[dest].set(flat_w)
    blk_start = jnp.arange(n_blocks, dtype=jnp.int32) * RB
    blk_e = jnp.minimum(jnp.sum((blk_start[:, None] >= pad_end[None, :]).astype(jnp.int32), axis=1), N_EXPERTS - 1)
    n_used = (pad_end[-1] // RB).astype(jnp.int32).reshape(1)
    return dest.reshape(n_tok, TOP_K), row_tok, row_w.reshape(n_blocks, RB, 1), blk_e.astype(jnp.int32), n_used


def _pad_cols(w, width):
    return jnp.pad(w, ((0, 0), (0, width - w.shape[1])))


def kernel(x, c, rel_bias, ada_w, ada_b, ln_g, ln_b, ab_w_in, gla_wg2, gla_bg2, gla_norm_g, ab_w_out,
           ffn_w1, ffn_w3, ffn_w2, nsa_w_in, nsa_cmp_pos, nsa_cmp_w1k, nsa_cmp_w2k, nsa_cmp_w1v, nsa_cmp_w2v,
           nsa_w_out, moe_router, moe_w1, moe_w3, moe_w2):
    B, S, D = x.shape
    n_tok = B * S
    mod = adaln_mod(c, ada_w, ada_b)
    mods = [[m.reshape(B, 1, D) for m in jnp.split(mod[i], 6, axis=-1)] for i in range(DEPTH)]
    toep, cmpb = build_bias_tables(rel_bias, S)

    sh1, sc1 = mods[0][0], mods[0][1]
    h = modulate(x, sc1, sh1)
    for i in range(DEPTH):
        j = i // 2
        _, _, g1, sh2, sc2, g2 = mods[i]
        nxt2 = (mods[i + 1][1], mods[i + 1][0]) if i + 1 < DEPTH else None
        lng = ln_g[i].reshape(2, 1, D)
        lnb = ln_b[i].reshape(2, 1, D)
        if i % 2 == 0:
            w = ab_w_in[j]
            cuts = np.cumsum((MOBA_WIDTH, MOBA_WIDTH, MOBA_WIDTH, GLA_QK_WIDTH, GLA_QK_WIDTH, GLA_V_WIDTH,
                              GLA_V_WIDTH))
            parts = jnp.split(w, cuts.tolist(), axis=1)
            parts[-1] = _pad_cols(parts[-1], LANE)
            w_cat = jnp.concatenate(parts, axis=1).astype(BF16)
            widths = [p.shape[1] for p in parts]
            offs = np.concatenate([[0], np.cumsum(widths)[:-1]]).tolist()
            scales = [MOBA_HEAD_DIM ** -0.5] + [1.0] * 7
            segs = tuple((o, wd, "flat", s) for o, wd, s in zip(offs, widths, scales))
            dts = [BF16, BF16, BF16, F32, F32, F32, F32, F32]
            mq, mk, mv, gq, gk, gv, gr, glr = project(h, w_cat, segs, dts)
            y_moba = moba_attention(mq, mk, mv, toep)
            wg2_pad = jnp.pad(gla_wg2[j], ((0, LANE - GLA_GATE_RANK), (0, 0)))
            y_gla = gla_mixer(gq, gk, gv, gr, glr, wg2_pad, gla_bg2[j].reshape(1, -1), gla_norm_g[j].reshape(1, -1))
            w_out = ab_w_out[j].astype(BF16)
            x, h = out_ln([y_moba, y_gla], [w_out[:MOBA_WIDTH], w_out[MOBA_WIDTH:]], x, g1, lng[0], lnb[0],
                          (sc2, sh2))
            x, h = ffn_ln(h, ffn_w1[j].astype(BF16), ffn_w3[j].astype(BF16), ffn_w2[j].astype(BF16), x, g2,
                          lng[1], lnb[1], nxt2)
        else:
            G, HPG, dh = NSA_KV_GROUPS, NSA_HPG, NSA_HEAD_DIM
            w = nsa_w_in[j]
            wq = w[:, :NSA_Q_WIDTH]
            wkv = w[:, NSA_Q_WIDTH:NSA_Q_WIDTH + 6 * NSA_KV_WIDTH]
            wg = w[:, NSA_Q_WIDTH + 6 * NSA_KV_WIDTH:].reshape(D, G, HPG, 3)
            wg = _pad_cols(wg.transpose(0, 1, 3, 2).reshape(D * G, 3 * HPG), LANE).reshape(D, G * LANE)
            w_cat = jnp.concatenate([wq, wkv, wg], axis=1).astype(BF16)
            segs = [(0, NSA_Q_WIDTH, "flat", dh ** -0.5)]
            segs += [(NSA_Q_WIDTH + n * NSA_KV_WIDTH, NSA_KV_WIDTH, "group", 1.0) for n in range(6)]
            segs += [(NSA_Q_WIDTH + 6 * NSA_KV_WIDTH, G * LANE, "group", 1.0)]
            dts = [BF16] * 7 + [F32]
            q, kc, vc, ks, vs, kw, vw, gates = project(h, w_cat, tuple(segs), dts)
            half = NSA_CMP_STRIDE * dh
            pos2 = jnp.pad(nsa_cmp_pos[j].reshape(2, half), ((0, 6), (0, 0))).astype(BF16)
            w1k = jnp.concatenate([nsa_cmp_w1k[j][:half], nsa_cmp_w1k[j][half:]], axis=1).astype(BF16)
            w1v = jnp.concatenate([nsa_cmp_w1v[j][:half], nsa_cmp_w1v[j][half:]], axis=1).astype(BF16)
            k_cmp = nsa_compress(kc, pos2, w1k, nsa_cmp_w2k[j].astype(BF16))
            v_cmp = nsa_compress(vc, pos2, w1v, nsa_cmp_w2v[j].astype(BF16))
            y = nsa_attention(q, k_cmp, v_cmp, ks, vs, kw, vw, gates, cmpb, toep)
            x, h = out_ln([y], [nsa_w_out[j].astype(BF16)], x, g1, lng[0], lnb[0], (sc2, sh2))
            h2 = h.reshape(n_tok, D)
            e_idx, w_top = moe_route(h2, _pad_cols(moe_router[j], LANE))
            dest, row_tok, row_w, blk_e, n_used = moe_dispatch(e_idx, w_top, n_tok)
            xs = jnp.take(h2, row_tok, axis=0)
            ys = moe_ffn(xs, row_w, blk_e, n_used, moe_w1[j].astype(BF16), moe_w3[j].astype(BF16),
                         moe_w2[j].astype(BF16))
            y0 = jnp.take(ys, dest[:, 0], axis=0).reshape(B, S, D)
            y1 = jnp.take(ys, dest[:, 1], axis=0).reshape(B, S, D)
            x, h = out_ln([y0, y1], None, x, g2, lng[1], lnb[1], nxt2)
    return x
```

```python
import functools
import math

import jax
import jax.numpy as jnp
import numpy as np
from jax import lax
from jax.experimental import pallas as pl
from jax.experimental.pallas import tpu as pltpu

F32 = jnp.float32
BF16 = jnp.bfloat16
NEG_INF = float("-inf")

DEPTH = 4
ALPHA_DN = (2 * DEPTH) ** 0.25
LN_EPS = 1e-5
RMS_EPS = 1e-6

N_BUCKETS = 32
MAX_DISTANCE = 1024

MOBA_HEADS = 8
MOBA_HEAD_DIM = 64
MOBA_BLOCK = 256
MOBA_TOPK = 3
MOBA_WIDTH = MOBA_HEADS * MOBA_HEAD_DIM

GLA_HEADS = 4
GLA_DK = 64
GLA_DV = 128
GLA_GATE_RANK = 16
GLA_TAU = 16.0
GLA_CHUNK = 64
GLA_QK_WIDTH = GLA_HEADS * GLA_DK
GLA_V_WIDTH = GLA_HEADS * GLA_DV

NSA_HEADS = 8
NSA_KV_GROUPS = 2
NSA_HPG = NSA_HEADS // NSA_KV_GROUPS
NSA_HEAD_DIM = 128
NSA_CMP_LEN = 32
NSA_CMP_STRIDE = 16
NSA_SLC_BLOCK = 64
NSA_SLC_TOPN = 16
NSA_WINDOW = 512
NSA_Q_WIDTH = NSA_HEADS * NSA_HEAD_DIM
NSA_KV_WIDTH = NSA_KV_GROUPS * NSA_HEAD_DIM

N_EXPERTS = 8
TOP_K = 2
MOE_ROW_BLOCK = 512

LANE = 128
BIAS_ROWS = 128


def _cparams(sem, vmem_mib=40):
    return pltpu.CompilerParams(dimension_semantics=sem, vmem_limit_bytes=vmem_mib << 20)


def _adaln_kernel(c_ref, w_ref, b_ref, o_ref):
    c = c_ref[...]
    cond = c * jax.nn.sigmoid(c)
    o_ref[0] = jnp.dot(cond, w_ref[0], preferred_element_type=F32) + b_ref[0]


def adaln_mod(c, ada_w, ada_b):
    B, D = c.shape
    L, _, N = ada_w.shape
    rows = 8
    cp = jnp.zeros((rows, D), F32).at[:B].set(c)
    tn = N // 4
    out = pl.pallas_call(
        _adaln_kernel,
        grid=(L, N // tn),
        in_specs=[
            pl.BlockSpec((rows, D), lambda l, j: (0, 0)),
            pl.BlockSpec((1, D, tn), lambda l, j: (l, 0, j)),
            pl.BlockSpec((1, 1, tn), lambda l, j: (l, 0, j)),
        ],
        out_specs=pl.BlockSpec((1, rows, tn), lambda l, j: (l, 0, j)),
        out_shape=jax.ShapeDtypeStruct((L, rows, N), F32),
        compiler_params=_cparams(("arbitrary", "arbitrary")),
        name="adaln_mod",
    )(cp, ada_w, ada_b.reshape(L, 1, N))
    return out[:, :B]


def _modulate_kernel(x_ref, sc_ref, sh_ref, o_ref):
    o_ref[0] = (x_ref[0] * (1.0 + sc_ref[0]) + sh_ref[0]).astype(o_ref.dtype)


def modulate(x, sc, sh, tm=512):
    B, S, D = x.shape
    return pl.pallas_call(
        _modulate_kernel,
        grid=(B, S // tm),
        in_specs=[
            pl.BlockSpec((1, tm, D), lambda b, i: (b, i, 0)),
            pl.BlockSpec((1, 1, D), lambda b, i: (b, 0, 0)),
            pl.BlockSpec((1, 1, D), lambda b, i: (b, 0, 0)),
        ],
        out_specs=pl.BlockSpec((1, tm, D), lambda b, i: (b, i, 0)),
        out_shape=jax.ShapeDtypeStruct((B, S, D), BF16),
        compiler_params=_cparams(("arbitrary", "arbitrary")),
        name="modulate",
    )(x, sc, sh)


def _t5_bucket(dist):
    max_exact = N_BUCKETS // 2
    n = jnp.maximum(dist, 0)
    nf = jnp.maximum(n, 1).astype(jnp.float32)
    large = max_exact + (jnp.log(nf / max_exact) / math.log(MAX_DISTANCE / max_exact) * (N_BUCKETS - max_exact)).astype(jnp.int32)
    large = jnp.minimum(large, N_BUCKETS - 1)
    return jnp.where(n < max_exact, n, large)


def _bias_table_kernel(rb_ref, bm_ref, o_ref, *, rows_per_step):
    h = pl.program_id(0)
    n_steps = bm_ref.shape[0] // rows_per_step

    def body(i, carry):
        r0 = pl.multiple_of(i * rows_per_step, rows_per_step)
        bm = bm_ref[pl.ds(r0, rows_per_step), :]
        acc = jnp.zeros(bm.shape, F32)
        for b in range(N_BUCKETS):
            acc = jnp.where(bm == b, rb_ref[b, h], acc)
        o_ref[0, pl.ds(r0, rows_per_step), :] = acc
        return carry

    lax.fori_loop(0, n_steps, body, 0)


def bias_table(rel_bias, bucket_map, rows_per_step):
    R, W = bucket_map.shape
    H = rel_bias.shape[1]
    return pl.pallas_call(
        functools.partial(_bias_table_kernel, rows_per_step=rows_per_step),
        grid=(H,),
        in_specs=[
            pl.BlockSpec(memory_space=pltpu.SMEM),
            pl.BlockSpec((R, W), lambda h: (0, 0)),
        ],
        out_specs=pl.BlockSpec((1, R, W), lambda h: (h, 0, 0)),
        out_shape=jax.ShapeDtypeStruct((H, R, W), F32),
        compiler_params=_cparams(("arbitrary",)),
        name="bias_table",
    )(rel_bias, bucket_map)


def _toep_width(S):
    return S + BIAS_ROWS


def _toep_col(q0, k0):
    return pl.multiple_of(q0 - k0 + BIAS_ROWS, LANE)


def build_bias_tables(rel_bias, S):
    j = jnp.arange(BIAS_ROWS, dtype=jnp.int32)[:, None]
    c = jnp.arange(_toep_width(S), dtype=jnp.int32)[None, :]
    toep_map = _t5_bucket(c - j - BIAS_ROWS)
    n_cmp_pad = S // NSA_CMP_STRIDE
    t = jnp.arange(S, dtype=jnp.int32)[None, :]
    cmp_end = jnp.arange(n_cmp_pad, dtype=jnp.int32)[:, None] * NSA_CMP_STRIDE + NSA_CMP_LEN - 1
    cmp_map = _t5_bucket(t - cmp_end)
    return bias_table(rel_bias, toep_map, 8), bias_table(rel_bias, cmp_map, 8)


def _proj_kernel(h_ref, w_ref, *o_refs, segs):
    h = h_ref[0]
    for (off, width, kind, scale), o_ref in zip(segs, o_refs):
        acc = jnp.dot(h, w_ref[:, off:off + width], preferred_element_type=F32)
        if scale != 1.0:
            acc = acc * scale
        if kind == "flat":
            o_ref[0] = acc.astype(o_ref.dtype)
        else:
            for g in range(width // LANE):
                o_ref[0, g] = acc[:, g * LANE:(g + 1) * LANE].astype(o_ref.dtype)


def project(h, w_cat, segs, out_dtypes, tm=512):
    B, S, D = h.shape
    out_shapes, out_specs = [], []
    for (off, width, kind, scale), dt in zip(segs, out_dtypes):
        if kind == "flat":
            out_shapes.append(jax.ShapeDtypeStruct((B, S, width), dt))
            out_specs.append(pl.BlockSpec((1, tm, width), lambda b, i: (b, i, 0)))
        else:
            G = width // LANE
            out_shapes.append(jax.ShapeDtypeStruct((B, G, S, LANE), dt))
            out_specs.append(pl.BlockSpec((1, G, tm, LANE), lambda b, i: (b, 0, i, 0)))
    return pl.pallas_call(
        functools.partial(_proj_kernel, segs=segs),
        grid=(B, S // tm),
        in_specs=[
            pl.BlockSpec((1, tm, D), lambda b, i: (b, i, 0)),
            pl.BlockSpec(w_cat.shape, lambda b, i: (0, 0)),
        ],
        out_specs=out_specs,
        out_shape=out_shapes,
        compiler_params=_cparams(("arbitrary", "arbitrary"), 48),
        name="in_proj",
    )(h, w_cat)


_NT = (((1,), (1,)), ((), ()))
_TN = (((0,), (0,)), ((), ()))


def _softmax_step(carry, s, v_tile):
    m, l, acc = carry
    m_new = jnp.maximum(m, jnp.max(s, axis=0, keepdims=True))
    alpha = jnp.exp(m - m_new)
    p = jnp.exp(s - m_new)
    l_new = alpha * l + jnp.sum(p, axis=0, keepdims=True)
    acc_new = alpha * acc + lax.dot_general(v_tile, p.astype(BF16), _TN, preferred_element_type=F32)
    return m_new, l_new, acc_new


def _moba_kernel(q_ref, k_ref, v_ref, tb_ref, o_ref, kmean_sc, sel_sc, *, S, heads_per_step):
    L = MOBA_BLOCK
    dh = MOBA_HEAD_DIM
    nb = S // L
    qi = pl.program_id(2)
    q0 = qi * L

    @pl.when(qi == 0)
    def _():
        kf = k_ref[0].astype(F32)
        kmean_sc[...] = jnp.mean(kf.reshape(nb, L, kf.shape[-1]), axis=1)

    q2 = q_ref[0]
    lane_q = lax.broadcasted_iota(jnp.int32, (L, LANE), 1)
    lane_m = lax.broadcasted_iota(jnp.int32, (nb, LANE), 1)
    blk = lax.broadcasted_iota(jnp.int32, (nb, L), 0)
    qz = []
    for hh in range(heads_per_step):
        in_head = (lane_q >= hh * dh) & (lane_q < (hh + 1) * dh)
        qz.append(jnp.where(in_head, q2, jnp.zeros_like(q2)))
        km = jnp.where((lane_m >= hh * dh) & (lane_m < (hh + 1) * dh), kmean_sc[...], 0.0)
        gate = lax.dot_general(km, q2.astype(F32), _NT, precision=lax.Precision.HIGHEST,
                               preferred_element_type=F32)
        valid = blk < qi
        gm = jnp.where(valid, gate, NEG_INF)
        rank = jnp.zeros((nb, L), jnp.int32)
        for m in range(nb):
            gmm = gm[m:m + 1, :]
            beats = (gmm > gm) | ((gmm == gm) & (m < blk))
            rank = rank + beats.astype(jnp.int32)
        sel_sc[hh] = jnp.where(valid & (rank < MOBA_TOPK), 1.0, 0.0)

    def bias_tile(hh, k0, n_keys):
        return jnp.concatenate([tb_ref[hh, :, pl.ds(_toep_col(q0, k0 + r * BIAS_ROWS), L)]
                                for r in range(n_keys // BIAS_ROWS)], axis=0)

    krow = lax.broadcasted_iota(jnp.int32, (L, L), 0)
    qcol = lax.broadcasted_iota(jnp.int32, (L, L), 1)
    k_own = k_ref[0, pl.ds(pl.multiple_of(q0, L), L), :]
    v_own = v_ref[0, pl.ds(pl.multiple_of(q0, L), L), :]
    carry = []
    for hh in range(heads_per_step):
        s = lax.dot_general(k_own, qz[hh], _NT, preferred_element_type=F32) + bias_tile(hh, q0, L)
        s = jnp.where(krow <= qcol, s, NEG_INF)
        m0 = jnp.max(s, axis=0, keepdims=True)
        p = jnp.exp(s - m0)
        l0 = jnp.sum(p, axis=0, keepdims=True)
        acc0 = lax.dot_general(v_own, p.astype(BF16), _TN, preferred_element_type=F32)
        carry += [m0, l0, acc0[hh * dh:(hh + 1) * dh]]

    def body(pp, carry):
        k0 = pl.multiple_of(pp * 2 * L, 2 * L)
        kt = k_ref[0, pl.ds(k0, 2 * L), :]
        vt = v_ref[0, pl.ds(k0, 2 * L), :]
        out = []
        for hh in range(heads_per_step):
            m, l, acc = carry[3 * hh:3 * hh + 3]
            s = lax.dot_general(kt, qz[hh], _NT, preferred_element_type=F32) + bias_tile(hh, k0, 2 * L)
            flags = jnp.concatenate(
                [jnp.broadcast_to(sel_sc[hh, pl.ds(2 * pp + j, 1), :], (L, L)) for j in range(2)], axis=0)
            s = jnp.where(flags > 0.5, s, NEG_INF)
            m_new = jnp.maximum(m, jnp.max(s, axis=0, keepdims=True))
            alpha = jnp.exp(m - m_new)
            p = jnp.exp(s - m_new)
            l_new = alpha * l + jnp.sum(p, axis=0, keepdims=True)
            pv = lax.dot_general(vt, p.astype(BF16), _TN, preferred_element_type=F32)
            out += [m_new, l_new, alpha * acc + pv[hh * dh:(hh + 1) * dh]]
        return tuple(out)

    carry = lax.fori_loop(0, (qi + 1) // 2, body, tuple(carry))
    o_t = jnp.concatenate([carry[3 * hh + 2] / carry[3 * hh + 1] for hh in range(heads_per_step)], axis=0)
    o_ref[0] = o_t.T.astype(o_ref.dtype)


def moba_attention(q, k, v, toep):
    B, S, _ = q.shape
    L = MOBA_BLOCK
    hps = LANE // MOBA_HEAD_DIM
    n_hp = MOBA_HEADS // hps
    nb = S // L
    return pl.pallas_call(
        functools.partial(_moba_kernel, S=S, heads_per_step=hps),
        grid=(n_hp, B, nb),
        in_specs=[
            pl.BlockSpec((1, L, LANE), lambda hp, b, i: (b, i, hp)),
            pl.BlockSpec((1, S, LANE), lambda hp, b, i: (b, 0, hp)),
            pl.BlockSpec((1, S, LANE), lambda hp, b, i: (b, 0, hp)),
            pl.BlockSpec((hps, BIAS_ROWS, _toep_width(S)), lambda hp, b, i: (hp, 0, 0)),
        ],
        out_specs=pl.BlockSpec((1, L, LANE), lambda hp, b, i: (b, i, hp)),
        out_shape=jax.ShapeDtypeStruct((B, S, MOBA_WIDTH), BF16),
        scratch_shapes=[pltpu.VMEM((nb, LANE), F32), pltpu.VMEM((hps, nb, L), F32)],
        compiler_params=_cparams(("arbitrary", "arbitrary", "arbitrary"), 40),
        name="moba_attn",
    )(q, k, v, toep)


def _log_sigmoid(z):
    return jnp.minimum(z, 0.0) - jnp.log1p(jnp.exp(-jnp.abs(z)))


def _gla_kernel(q_ref, k_ref, v_ref, r_ref, lr_ref, wg2_ref, bg2_ref, ng_ref, o_ref, st_sc, *, batch):
    C = GLA_CHUNK
    dk, dv = GLA_DK, GLA_DV

    @pl.when(pl.program_id(0) == 0)
    def _():
        st_sc[...] = jnp.zeros(st_sc.shape, F32)

    row = lax.broadcasted_iota(jnp.int32, (C, C), 0)
    col = lax.broadcasted_iota(jnp.int32, (C, C), 1)
    causal = col <= row
    tri = jnp.where(causal, 1.0, 0.0)
    for b in range(batch):
        z = jnp.dot(lr_ref[b], wg2_ref[...], preferred_element_type=F32) + bg2_ref[...]
        la = _log_sigmoid(z) / GLA_TAU
        bc = jnp.dot(tri, la, precision=lax.Precision.HIGHEST, preferred_element_type=F32)
        b_last = bc[C - 1:C, :]
        qd = q_ref[b] * (dk ** -0.5) * jnp.exp(bc)
        kd = k_ref[b] * jnp.exp(-bc)
        kl = k_ref[b] * jnp.exp(b_last - bc)
        e_last = jnp.exp(b_last)
        v = v_ref[b]
        r = r_ref[b]
        outs = []
        for h in range(GLA_HEADS):
            ks = slice(h * dk, (h + 1) * dk)
            vs = slice(h * dv, (h + 1) * dv)
            qh = qd[:, ks].astype(BF16)
            kh = kd[:, ks].astype(BF16)
            vh = v[:, vs].astype(BF16)
            attn = lax.dot_general(qh, kh, (((1,), (1,)), ((), ())), preferred_element_type=F32)
            attn = jnp.where(causal, attn, 0.0)
            st = st_sc[b, h]
            o = jnp.dot(attn.astype(BF16), vh, preferred_element_type=F32)
            o = o + lax.dot_general(qh, st.astype(BF16), (((1,), (1,)), ((), ())), preferred_element_type=F32)
            upd = lax.dot_general(vh, kl[:, ks].astype(BF16), (((0,), (0,)), ((), ())), preferred_element_type=F32)
            st_sc[b, h] = st * e_last[:, ks] + upd
            o = o * lax.rsqrt(jnp.mean(o * o, axis=-1, keepdims=True) + RMS_EPS)
            rg = r[:, vs]
            outs.append(o * ng_ref[:, vs] * (rg * jax.nn.sigmoid(rg)))
        o_ref[b] = jnp.concatenate(outs, axis=-1).astype(o_ref.dtype)


def gla_mixer(q, k, v, r, lr, wg2_pad, bg2, norm_g):
    B, S, _ = q.shape
    C = GLA_CHUNK
    nc = S // C
    blk = lambda w: pl.BlockSpec((B, C, w), lambda c: (0, c, 0))
    full = lambda a: pl.BlockSpec(a.shape, lambda c: (0,) * a.ndim)
    return pl.pallas_call(
        functools.partial(_gla_kernel, batch=B),
        grid=(nc,),
        in_specs=[blk(GLA_QK_WIDTH), blk(GLA_QK_WIDTH), blk(GLA_V_WIDTH), blk(GLA_V_WIDTH), blk(LANE),
                  full(wg2_pad), full(bg2), full(norm_g)],
        out_specs=blk(GLA_V_WIDTH),
        out_shape=jax.ShapeDtypeStruct((B, S, GLA_V_WIDTH), BF16),
        scratch_shapes=[pltpu.VMEM((B, GLA_HEADS, GLA_DV, GLA_DK), F32)],
        compiler_params=_cparams(("arbitrary",)),
        name="gla",
    )(q, k, v, r, lr, wg2_pad, bg2, norm_g)


def _deepnorm_ln(x, y, gate, ln_g, ln_b):
    z = ALPHA_DN * x + (1.0 + gate) * y
    mu = jnp.mean(z, axis=-1, keepdims=True)
    zc = z - mu
    var = jnp.mean(zc * zc, axis=-1, keepdims=True)
    return zc * lax.rsqrt(var + LN_EPS) * ln_g + ln_b


def _out_ln_kernel(*refs, n_parts, matmul, has_next):
    y_refs = refs[:n_parts]
    pos = n_parts
    if matmul:
        w_refs = refs[pos:pos + n_parts]
        pos += n_parts
    x_ref, g_ref, lng_ref, lnb_ref = refs[pos:pos + 4]
    pos += 4
    if has_next:
        sc_ref, sh_ref = refs[pos:pos + 2]
        pos += 2
    o_ref = refs[pos]
    y = None
    for i in range(n_parts):
        part = jnp.dot(y_refs[i][0], w_refs[i][...], preferred_element_type=F32) if matmul else y_refs[i][0]
        y = part if y is None else y + part
    xn = _deepnorm_ln(x_ref[0], y, g_ref[0], lng_ref[...], lnb_ref[...])
    o_ref[0] = xn
    if has_next:
        refs[pos + 1][0] = (xn * (1.0 + sc_ref[0]) + sh_ref[0]).astype(BF16)


def out_ln(ys, ws, x, gate, ln_g, ln_b, nxt, tm=512):
    B, S, D = x.shape
    tile = lambda w: pl.BlockSpec((1, tm, w), lambda b, i: (b, i, 0))
    vec = pl.BlockSpec((1, 1, D), lambda b, i: (b, 0, 0))
    par = pl.BlockSpec((1, D), lambda b, i: (0, 0))
    args = list(ys)
    in_specs = [tile(y.shape[-1]) for y in ys]
    if ws is not None:
        args += list(ws)
        in_specs += [pl.BlockSpec(w.shape, lambda b, i: (0, 0)) for w in ws]
    args += [x, gate, ln_g, ln_b]
    in_specs += [tile(D), vec, par, par]
    out_shape = [jax.ShapeDtypeStruct((B, S, D), F32)]
    out_specs = [tile(D)]
    if nxt is not None:
        args += list(nxt)
        in_specs += [vec, vec]
        out_shape.append(jax.ShapeDtypeStruct((B, S, D), BF16))
        out_specs.append(tile(D))
    res = pl.pallas_call(
        functools.partial(_out_ln_kernel, n_parts=len(ys), matmul=ws is not None, has_next=nxt is not None),
        grid=(B, S // tm),
        in_specs=in_specs,
        out_specs=out_specs,
        out_shape=out_shape,
        compiler_params=_cparams(("arbitrary", "arbitrary")),
        name="out_ln",
    )(*args)
    return (res[0], res[1]) if nxt is not None else (res[0], None)


def _ffn_ln_kernel(h_ref, w1_ref, w3_ref, w2_ref, x_ref, g_ref, lng_ref, lnb_ref, sc_ref, sh_ref,
                   o_ref, hn_ref, acc_sc):
    j = pl.program_id(2)

    @pl.when(j == 0)
    def _():
        acc_sc[...] = jnp.zeros(acc_sc.shape, F32)

    h = h_ref[0]
    a = jnp.dot(h, w1_ref[...], preferred_element_type=F32)
    b = jnp.dot(h, w3_ref[...], preferred_element_type=F32)
    u = (a * jax.nn.sigmoid(a)) * b
    acc_sc[...] += jnp.dot(u.astype(BF16), w2_ref[...], preferred_element_type=F32)

    @pl.when(j == pl.num_programs(2) - 1)
    def _():
        xn = _deepnorm_ln(x_ref[0], acc_sc[...], g_ref[0], lng_ref[...], lnb_ref[...])
        o_ref[0] = xn
        hn_ref[0] = (xn * (1.0 + sc_ref[0]) + sh_ref[0]).astype(BF16)


def ffn_ln(h, w1, w3, w2, x, gate, ln_g, ln_b, nxt, tm=1024, tf=256):
    B, S, D = x.shape
    F = w1.shape[1]
    tile = pl.BlockSpec((1, tm, D), lambda b, i, j: (b, i, 0))
    vec = pl.BlockSpec((1, 1, D), lambda b, i, j: (b, 0, 0))
    par = pl.BlockSpec((1, D), lambda b, i, j: (0, 0))
    return pl.pallas_call(
        _ffn_ln_kernel,
        grid=(B, S // tm, F // tf),
        in_specs=[tile,
                  pl.BlockSpec((D, tf), lambda b, i, j: (0, j)),
                  pl.BlockSpec((D, tf), lambda b, i, j: (0, j)),
                  pl.BlockSpec((tf, D), lambda b, i, j: (j, 0)),
                  tile, vec, par, par, vec, vec],
        out_specs=[tile, tile],
        out_shape=[jax.ShapeDtypeStruct((B, S, D), F32), jax.ShapeDtypeStruct((B, S, D), BF16)],
        scratch_shapes=[pltpu.VMEM((tm, D), F32)],
        compiler_params=_cparams(("arbitrary", "arbitrary", "arbitrary"), 48),
        name="ffn_ln",
    )(h, w1, w3, w2, x, gate, ln_g, ln_b, nxt[0], nxt[1])


def _compress_kernel(x_ref, pos_ref, w1_ref, w2_ref, o_ref):
    dh = NSA_HEAD_DIM
    x = x_ref[0, 0]
    uv = jnp.dot(x, w1_ref[...], preferred_element_type=F32)
    pc = jnp.dot(pos_ref[...], w1_ref[...], preferred_element_type=F32)
    c = pc[0:1, :dh] + pc[1:2, dh:]
    n = uv.shape[0]
    nxt = pltpu.roll(uv[:, dh:], n - 1, 0)
    hid = uv[:, :dh] + nxt + c
    act = hid * jax.nn.sigmoid(hid)
    o_ref[0, 0] = jnp.dot(act.astype(BF16), w2_ref[...], preferred_element_type=F32).astype(o_ref.dtype)


def nsa_compress(kv, pos2, w1cat, w2):
    B, G, S, dh = kv.shape
    seg = NSA_CMP_STRIDE
    n = S // seg
    x = kv.reshape(B, G, n, seg * dh)
    return pl.pallas_call(
        _compress_kernel,
        grid=(B, G),
        in_specs=[
            pl.BlockSpec((1, 1, n, seg * dh), lambda b, g: (b, g, 0, 0)),
            pl.BlockSpec(pos2.shape, lambda b, g: (0, 0)),
            pl.BlockSpec(w1cat.shape, lambda b, g: (0, 0)),
            pl.BlockSpec(w2.shape, lambda b, g: (0, 0)),
        ],
        out_specs=pl.BlockSpec((1, 1, n, dh), lambda b, g: (b, g, 0, 0)),
        out_shape=jax.ShapeDtypeStruct((B, G, n, dh), BF16),
        compiler_params=_cparams(("arbitrary", "arbitrary")),
        name="nsa_compress",
    )(x, pos2, w1cat, w2)


def _nsa_kernel(q_ref, kc_ref, vc_ref, ks_ref, vs_ref, kw_ref, vw_ref, gt_ref, cb_ref, tb_ref, o_ref, sel_sc,
                *, S, TQ):
    dh = NSA_HEAD_DIM
    HPG = NSA_HPG
    LS = NSA_SLC_BLOCK
    TK = 2 * TQ
    R = HPG * TQ
    n_cmp = (S - NSA_CMP_LEN) // NSA_CMP_STRIDE + 1
    n_cp = S // NSA_CMP_STRIDE
    n_slc = S // LS
    n_top = min(NSA_SLC_TOPN, n_slc)
    qi = pl.program_id(2)
    q0 = qi * TQ

    q4 = jnp.concatenate([q_ref[0, :, h * dh:(h + 1) * dh] for h in range(HPG)], axis=0)
    lanes4 = lambda a: jnp.concatenate([a] * HPG, axis=1)

    s = lax.dot_general(kc_ref[0, 0], q4, _NT, preferred_element_type=F32)
    s = s + jnp.concatenate([cb_ref[h] for h in range(HPG)], axis=1)
    n_row = lax.broadcasted_iota(jnp.int32, (n_cp, TQ), 0)
    t_q = q0 + lax.broadcasted_iota(jnp.int32, (n_cp, TQ), 1)
    valid_c = lanes4((n_row * NSA_CMP_STRIDE + (NSA_CMP_LEN - 1) <= t_q) & (n_row < n_cmp))
    s = jnp.where(valid_c, s, NEG_INF)
    m = jnp.max(s, axis=0, keepdims=True)
    m = jnp.where(m == NEG_INF, 0.0, m)
    e = jnp.exp(s - m)
    p_c = e / jnp.maximum(jnp.sum(e, axis=0, keepdims=True), jnp.finfo(F32).tiny)
    o_c = lax.dot_general(vc_ref[0, 0], p_c.astype(BF16), _TN, preferred_element_type=F32)

    p_sum = p_c[:, 0:TQ]
    for h in range(1, HPG):
        p_sum = p_sum + p_c[:, h * TQ:(h + 1) * TQ]
    om = lax.broadcasted_iota(jnp.int32, (LANE, n_cp), 0)
    c_lo = lax.broadcasted_iota(jnp.int32, (LANE, n_cp), 1) * NSA_CMP_STRIDE
    ov = jnp.maximum(jnp.minimum(c_lo + NSA_CMP_LEN, om * LS + LS) - jnp.maximum(c_lo, om * LS), 0)
    overlap_t = ov.astype(F32) / NSA_CMP_LEN
    imp_t = jnp.dot(overlap_t, p_sum, precision=lax.Precision.HIGHEST, preferred_element_type=F32)
    blk = lax.broadcasted_iota(jnp.int32, (LANE, TQ), 0)
    tq_lane = q0 + lax.broadcasted_iota(jnp.int32, (LANE, TQ), 1)
    tb = lax.shift_right_logical(tq_lane, LS.bit_length() - 1)
    forced = (blk == 0) | (blk == tb) | (blk == tb - 1)
    key = jnp.where((blk > tb) | (blk >= n_slc), NEG_INF, jnp.where(forced, jnp.inf, imp_t))
    rank = jnp.zeros((LANE, TQ), jnp.int32)
    for mm in range(n_slc):
        km = key[mm:mm + 1, :]
        beats = (km > key) | ((km == key) & (mm < blk))
        rank = rank + beats.astype(jnp.int32)
    sel_sc[...] = jnp.where((rank < n_top) & (blk <= tb) & (blk < n_slc), 1.0, 0.0)

    def toep_tile(k0, n_keys):
        return jnp.concatenate(
            [jnp.concatenate([tb_ref[h, :, pl.ds(_toep_col(q0, k0 + r * BIAS_ROWS), TQ)] for h in range(HPG)], axis=1)
             for r in range(n_keys // BIAS_ROWS)], axis=0)

    init = (jnp.full((1, R), NEG_INF, F32), jnp.zeros((1, R), F32), jnp.zeros((dh, R), F32))

    krow_s = lax.broadcasted_iota(jnp.int32, (TK, TQ), 0)
    tq_s = q0 + lax.broadcasted_iota(jnp.int32, (TK, TQ), 1)

    def slc_body(kt, carry):
        k0 = pl.multiple_of(kt * TK, TK)
        flags = jnp.concatenate(
            [jnp.broadcast_to(sel_sc[pl.ds(kt * (TK // LS) + j, 1), :], (LS, TQ)) for j in range(TK // LS)], axis=0)
        mask = lanes4((flags > 0.5) & (k0 + krow_s <= tq_s))
        s = lax.dot_general(ks_ref[0, 0, pl.ds(k0, TK), :], q4, _NT, preferred_element_type=F32) + toep_tile(k0, TK)
        return _softmax_step(carry, jnp.where(mask, s, NEG_INF), vs_ref[0, 0, pl.ds(k0, TK), :])

    n_kt = (q0 + TQ - 1) // TK + 1
    _, l_s, acc_s = lax.fori_loop(0, n_kt, slc_body, init)
    o_s = acc_s / l_s

    krow_w = lax.broadcasted_iota(jnp.int32, (TQ, TQ), 0)
    tq_w = q0 + lax.broadcasted_iota(jnp.int32, (TQ, TQ), 1)

    def win_body(j, carry):
        k0 = pl.multiple_of(q0 - j * TQ, TQ)
        dist = tq_w - (k0 + krow_w)
        mask = lanes4((dist >= 0) & (dist < NSA_WINDOW))
        s = lax.dot_general(kw_ref[0, 0, pl.ds(k0, TQ), :], q4, _NT, preferred_element_type=F32) + toep_tile(k0, TQ)
        return _softmax_step(carry, jnp.where(mask, s, NEG_INF), vw_ref[0, 0, pl.ds(k0, TQ), :])

    n_wt = jnp.minimum(qi, NSA_WINDOW // TQ) + 1
    _, l_w, acc_w = lax.fori_loop(0, n_wt, win_body, init)
    o_w = acc_w / l_w

    g_t = jax.nn.sigmoid(gt_ref[0, 0]).T
    outs = []
    for h in range(HPG):
        cols = slice(h * TQ, (h + 1) * TQ)
        o = (g_t[h:h + 1] * o_c[:, cols] + g_t[HPG + h:HPG + h + 1] * o_s[:, cols]
             + g_t[2 * HPG + h:2 * HPG + h + 1] * o_w[:, cols])
        outs.append(o.T)
    o_ref[0] = jnp.concatenate(outs, axis=1).astype(o_ref.dtype)


def nsa_attention(q, kc, vc, ks, vs, kw, vw, gates, cmpb, toep, TQ=128):
    B, S, _ = q.shape
    G, HPG, dh = NSA_KV_GROUPS, NSA_HPG, NSA_HEAD_DIM
    n_cp = S // NSA_CMP_STRIDE
    full_kv = pl.BlockSpec((1, 1, S, dh), lambda b, g, i: (b, g, 0, 0))
    cmp_kv = pl.BlockSpec((1, 1, n_cp, dh), lambda b, g, i: (b, g, 0, 0))
    return pl.pallas_call(
        functools.partial(_nsa_kernel, S=S, TQ=TQ),
        grid=(B, G, S // TQ),
        in_specs=[
            pl.BlockSpec((1, TQ, HPG * dh), lambda b, g, i: (b, i, g)),
            cmp_kv, cmp_kv, full_kv, full_kv, full_kv, full_kv,
            pl.BlockSpec((1, 1, TQ, LANE), lambda b, g, i: (b, g, i, 0)),
            pl.BlockSpec((HPG, n_cp, TQ), lambda b, g, i: (g, 0, i)),
            pl.BlockSpec((HPG, BIAS_ROWS, _toep_width(S)), lambda b, g, i: (g, 0, 0)),
        ],
        out_specs=pl.BlockSpec((1, TQ, HPG * dh), lambda b, g, i: (b, i, g)),
        out_shape=jax.ShapeDtypeStruct((B, S, NSA_Q_WIDTH), BF16),
        scratch_shapes=[pltpu.VMEM((LANE, TQ), F32)],
        compiler_params=_cparams(("arbitrary", "arbitrary", "arbitrary"), 48),
        name="nsa_attn",
    )(q, kc, vc, ks, vs, kw, vw, gates, cmpb, toep)


def _router_kernel(h_ref, wr_ref, e_ref, w_ref):
    logits = jnp.dot(h_ref[...].astype(F32), wr_ref[...], precision=lax.Precision.HIGHEST,
                     preferred_element_type=F32)
    lane = lax.broadcasted_iota(jnp.int32, logits.shape, 1)
    logits = jnp.where(lane < N_EXPERTS, logits, NEG_INF)
    l1 = jnp.max(logits, axis=-1, keepdims=True)
    i1 = jnp.min(jnp.where(logits == l1, lane, LANE), axis=-1, keepdims=True)
    rest = jnp.where(lane == i1, NEG_INF, logits)
    l2 = jnp.max(rest, axis=-1, keepdims=True)
    i2 = jnp.min(jnp.where(rest == l2, lane, LANE), axis=-1, keepdims=True)
    e2 = jnp.exp(l2 - l1)
    den = 1.0 + e2
    e_ref[...] = jnp.where(lane == 0, i1, jnp.where(lane == 1, i2, 0))
    w_ref[...] = jnp.where(lane == 0, 1.0 / den, jnp.where(lane == 1, e2 / den, 0.0))


def moe_route(h2, wr_pad, tm=1024):
    N, D = h2.shape
    return pl.pallas_call(
        _router_kernel,
        grid=(N // tm,),
        in_specs=[pl.BlockSpec((tm, D), lambda i: (i, 0)), pl.BlockSpec(wr_pad.shape, lambda i: (0, 0))],
        out_specs=[pl.BlockSpec((tm, LANE), lambda i: (i, 0)), pl.BlockSpec((tm, LANE), lambda i: (i, 0))],
        out_shape=[jax.ShapeDtypeStruct((N, LANE), jnp.int32), jax.ShapeDtypeStruct((N, LANE), F32)],
        compiler_params=_cparams(("arbitrary",)),
        name="moe_router",
    )(h2, wr_pad)


def _moe_ffn_kernel(be_ref, nu_ref, x_ref, rw_ref, w1_ref, w3_ref, w2_ref, o_ref, acc_sc):
    i = pl.program_id(0)
    j = pl.program_id(1)
    last = pl.num_programs(1) - 1
    used = i < nu_ref[0]

    @pl.when(j == 0)
    def _():
        acc_sc[...] = jnp.zeros(acc_sc.shape, F32)

    @pl.when(used)
    def _():
        x = x_ref[...]
        a = jnp.dot(x, w1_ref[0], preferred_element_type=F32)
        b = jnp.dot(x, w3_ref[0], preferred_element_type=F32)
        u = (a * jax.nn.sigmoid(a)) * b
        acc_sc[...] += jnp.dot(u.astype(BF16), w2_ref[0], preferred_element_type=F32)

    @pl.when(j == last)
    def _():
        o_ref[...] = acc_sc[...] * rw_ref[0]


def moe_ffn(xs, row_w, blk_e, n_used, w1, w3, w2, tf=256):
    n_rows, D = xs.shape
    RB = MOE_ROW_BLOCK
    n_blocks = n_rows // RB
    F = w1.shape[2]
    nf = F // tf

    def wcol(i, j, be, nu):
        return (be[i], 0, jnp.where(i < nu[0], j, nf - 1))

    def wrow(i, j, be, nu):
        return (be[i], jnp.where(i < nu[0], j, nf - 1), 0)

    grid_spec = pltpu.PrefetchScalarGridSpec(
        num_scalar_prefetch=2,
        grid=(n_blocks, nf),
        in_specs=[
            pl.BlockSpec((RB, D), lambda i, j, be, nu: (i, 0)),
            pl.BlockSpec((1, RB, 1), lambda i, j, be, nu: (i, 0, 0)),
            pl.BlockSpec((1, D, tf), wcol),
            pl.BlockSpec((1, D, tf), wcol),
            pl.BlockSpec((1, tf, D), wrow),
        ],
        out_specs=pl.BlockSpec((RB, D), lambda i, j, be, nu: (i, 0)),
        scratch_shapes=[pltpu.VMEM((RB, D), F32)],
    )
    return pl.pallas_call(
        _moe_ffn_kernel,
        grid_spec=grid_spec,
        out_shape=jax.ShapeDtypeStruct((n_rows, D), F32),
        compiler_params=_cparams(("arbitrary", "arbitrary")),
        name="moe_ffn",
    )(blk_e, n_used, xs, row_w, w1, w3, w2)


def moe_dispatch(e_idx, w_top, n_tok):
    RB = MOE_ROW_BLOCK
    n_assign = n_tok * TOP_K
    n_blocks = (n_assign + N_EXPERTS * (RB - 1) + RB - 1) // RB
    n_rows = n_blocks * RB
    flat_e = e_idx[:, :TOP_K].reshape(-1)
    flat_w = w_top[:, :TOP_K].reshape(-1)
    onehot = (flat_e[:, None] == jnp.arange(N_EXPERTS, dtype=jnp.int32)[None, :]).astype(jnp.int32)
    csum = jnp.cumsum(onehot, axis=0)
    counts = csum[-1]
    pos_in_e = jnp.sum(csum * onehot, axis=1) - 1
    padded = (counts + RB - 1) // RB * RB
    pad_end = jnp.cumsum(padded)
    pad_start = pad_end - padded
    dest = (jnp.sum(pad_start[None, :] * onehot, axis=1) + pos_in_e).astype(jnp.int32)
    flat_tok = jnp.arange(n_assign, dtype=jnp.int32) // TOP_K
    row_tok = jnp.zeros((n_rows,), jnp.int32).at[dest].set(flat_tok)
    row_w = jnp.zeros((n_rows,), F32).at[dest].set(flat_w)
    blk_start = jnp.arange(n_blocks, dtype=jnp.int32) * RB
    blk_e = jnp.minimum(jnp.sum((blk_start[:, None] >= pad_end[None, :]).astype(jnp.int32), axis=1), N_EXPERTS - 1)
    n_used = (pad_end[-1] // RB).astype(jnp.int32).reshape(1)
    return dest.reshape(n_tok, TOP_K), row_tok, row_w.reshape(n_blocks, RB, 1), blk_e.astype(jnp.int32), n_used


def _pad_cols(w, width):
    return jnp.pad(w, ((0, 0), (0, width - w.shape[1])))


def kernel(x, c, rel_bias, ada_w, ada_b, ln_g, ln_b, ab_w_in, gla_wg2, gla_bg2, gla_norm_g, ab_w_out,
           ffn_w1, ffn_w3, ffn_w2, nsa_w_in, nsa_cmp_pos, nsa_cmp_w1k, nsa_cmp_w2k, nsa_cmp_w1v, nsa_cmp_w2v,
           nsa_w_out, moe_router, moe_w1, moe_w3, moe_w2):
    B, S, D = x.shape
    n_tok = B * S
    mod = adaln_mod(c, ada_w, ada_b)
    mods = [[m.reshape(B, 1, D) for m in jnp.split(mod[i], 6, axis=-1)] for i in range(DEPTH)]
    toep, cmpb = build_bias_tables(rel_bias, S)

    sh1, sc1 = mods[0][0], mods[0][1]
    h = modulate(x, sc1, sh1)
    for i in range(DEPTH):
        j = i // 2
        _, _, g1, sh2, sc2, g2 = mods[i]
        nxt2 = (mods[i + 1][1], mods[i + 1][0]) if i + 1 < DEPTH else None
        lng = ln_g[i].reshape(2, 1, D)
        lnb = ln_b[i].reshape(2, 1, D)
        if i % 2 == 0:
            w = ab_w_in[j]
            cuts = np.cumsum((MOBA_WIDTH, MOBA_WIDTH, MOBA_WIDTH, GLA_QK_WIDTH, GLA_QK_WIDTH, GLA_V_WIDTH,
                              GLA_V_WIDTH))
            parts = jnp.split(w, cuts.tolist(), axis=1)
            parts[-1] = _pad_cols(parts[-1], LANE)
            w_cat = jnp.concatenate(parts, axis=1).astype(BF16)
            widths = [p.shape[1] for p in parts]
            offs = np.concatenate([[0], np.cumsum(widths)[:-1]]).tolist()
            scales = [MOBA_HEAD_DIM ** -0.5] + [1.0] * 7
            segs = tuple((o, wd, "flat", s) for o, wd, s in zip(offs, widths, scales))
            dts = [BF16, BF16, BF16, F32, F32, F32, F32, F32]
            mq, mk, mv, gq, gk, gv, gr, glr = project(h, w_cat, segs, dts)
            y_moba = moba_attention(mq, mk, mv, toep)
            wg2_pad = jnp.pad(gla_wg2[j], ((0, LANE - GLA_GATE_RANK), (0, 0)))
            y_gla = gla_mixer(gq, gk, gv, gr, glr, wg2_pad, gla_bg2[j].reshape(1, -1), gla_norm_g[j].reshape(1, -1))
            w_out = ab_w_out[j].astype(BF16)
            x, h = out_ln([y_moba, y_gla], [w_out[:MOBA_WIDTH], w_out[MOBA_WIDTH:]], x, g1, lng[0], lnb[0],
                          (sc2, sh2))
            x, h = ffn_ln(h, ffn_w1[j].astype(BF16), ffn_w3[j].astype(BF16), ffn_w2[j].astype(BF16), x, g2,
                          lng[1], lnb[1], nxt2)
        else:
            G, HPG, dh = NSA_KV_GROUPS, NSA_HPG, NSA_HEAD_DIM
            w = nsa_w_in[j]
            wq = w[:, :NSA_Q_WIDTH]
            wkv = w[:, NSA_Q_WIDTH:NSA_Q_WIDTH + 6 * NSA_KV_WIDTH]
            wg = w[:, NSA_Q_WIDTH + 6 * NSA_KV_WIDTH:].reshape(D, G, HPG, 3)
            wg = _pad_cols(wg.transpose(0, 1, 3, 2).reshape(D * G, 3 * HPG), LANE).reshape(D, G * LANE)
            w_cat = jnp.concatenate([wq, wkv, wg], axis=1).astype(BF16)
            segs = [(0, NSA_Q_WIDTH, "flat", dh ** -0.5)]
            segs += [(NSA_Q_WIDTH + n * NSA_KV_WIDTH, NSA_KV_WIDTH, "group", 1.0) for n in range(6)]
            segs += [(NSA_Q_WIDTH + 6 * NSA_KV_WIDTH, G * LANE, "group", 1.0)]
            dts = [BF16] * 7 + [F32]
            q, kc, vc, ks, vs, kw, vw, gates = project(h, w_cat, tuple(segs), dts)
            half = NSA_CMP_STRIDE * dh
            pos2 = jnp.pad(nsa_cmp_pos[j].reshape(2, half), ((0, 6), (0, 0))).astype(BF16)
            w1k = jnp.concatenate([nsa_cmp_w1k[j][:half], nsa_cmp_w1k[j][half:]], axis=1).astype(BF16)
            w1v = jnp.concatenate([nsa_cmp_w1v[j][:half], nsa_cmp_w1v[j][half:]], axis=1).astype(BF16)
            k_cmp = nsa_compress(kc, pos2, w1k, nsa_cmp_w2k[j].astype(BF16))
            v_cmp = nsa_compress(vc, pos2, w1v, nsa_cmp_w2v[j].astype(BF16))
            y = nsa_attention(q, k_cmp, v_cmp, ks, vs, kw, vw, gates, cmpb, toep)
            x, h = out_ln([y], [nsa_w_out[j].astype(BF16)], x, g1, lng[0], lnb[0], (sc2, sh2))
            h2 = h.reshape(n_tok, D)
            e_idx, w_top = moe_route(h2, _pad_cols(moe_router[j], LANE))
            dest, row_tok, row_w, blk_e, n_used = moe_dispatch(e_idx, w_top, n_tok)
            xs = jnp.take(h2, row_tok, axis=0)
            ys = moe_ffn(xs, row_w, blk_e, n_used, moe_w1[j].astype(BF16), moe_w3[j].astype(BF16),
                         moe_w2[j].astype(BF16))
            y0 = jnp.take(ys, dest[:, 0], axis=0).reshape(B, S, D)
            y1 = jnp.take(ys, dest[:, 1], axis=0).reshape(B, S, D)
            x, h = out_ln([y0, y1], None, x, g2, lng[1], lnb[1], nxt2)
    return x
```

```python
import functools
import math

import jax
import jax.numpy as jnp
import numpy as np
from jax import lax
from jax.experimental import pallas as pl
from jax.experimental.pallas import tpu as pltpu

F32 = jnp.float32
BF16 = jnp.bfloat16
NEG_INF = float("-inf")

DEPTH = 4
ALPHA_DN = (2 * DEPTH) ** 0.25
LN_EPS = 1e-5
RMS_EPS = 1e-6

N_BUCKETS = 32
MAX_DISTANCE = 1024

MOBA_HEADS = 8
MOBA_HEAD_DIM = 64
MOBA_BLOCK = 256
MOBA_TOPK = 3
MOBA_WIDTH = MOBA_HEADS * MOBA_HEAD_DIM

GLA_HEADS = 4
GLA_DK = 64
GLA_DV = 128
GLA_GATE_RANK = 16
GLA_TAU = 16.0
GLA_CHUNK = 64
GLA_QK_WIDTH = GLA_HEADS * GLA_DK
GLA_V_WIDTH = GLA_HEADS * GLA_DV

NSA_HEADS = 8
NSA_KV_GROUPS = 2
NSA_HPG = NSA_HEADS // NSA_KV_GROUPS
NSA_HEAD_DIM = 128
NSA_CMP_LEN = 32
NSA_CMP_STRIDE = 16
NSA_SLC_BLOCK = 64
NSA_SLC_TOPN = 16
NSA_WINDOW = 512
NSA_Q_WIDTH = NSA_HEADS * NSA_HEAD_DIM
NSA_KV_WIDTH = NSA_KV_GROUPS * NSA_HEAD_DIM

N_EXPERTS = 8
TOP_K = 2
MOE_ROW_BLOCK = 512

LANE = 128
BIAS_ROWS = 128


def _cparams(sem, vmem_mib=40):
    return pltpu.CompilerParams(dimension_semantics=sem, vmem_limit_bytes=vmem_mib << 20)


def _adaln_kernel(c_ref, w_ref, b_ref, o_ref):
    c = c_ref[...]
    cond = c * jax.nn.sigmoid(c)
    o_ref[0] = jnp.dot(cond, w_ref[0], preferred_element_type=F32) + b_ref[0]


def adaln_mod(c, ada_w, ada_b):
    B, D = c.shape
    L, _, N = ada_w.shape
    rows = 8
    cp = jnp.zeros((rows, D), F32).at[:B].set(c)
    tn = N // 4
    out = pl.pallas_call(
        _adaln_kernel,
        grid=(L, N // tn),
        in_specs=[
            pl.BlockSpec((rows, D), lambda l, j: (0, 0)),
            pl.BlockSpec((1, D, tn), lambda l, j: (l, 0, j)),
            pl.BlockSpec((1, 1, tn), lambda l, j: (l, 0, j)),
        ],
        out_specs=pl.BlockSpec((1, rows, tn), lambda l, j: (l, 0, j)),
        out_shape=jax.ShapeDtypeStruct((L, rows, N), F32),
        compiler_params=_cparams(("arbitrary", "arbitrary")),
        name="adaln_mod",
    )(cp, ada_w, ada_b.reshape(L, 1, N))
    return out[:, :B]


def _modulate_kernel(x_ref, sc_ref, sh_ref, o_ref):
    o_ref[0] = (x_ref[0] * (1.0 + sc_ref[0]) + sh_ref[0]).astype(o_ref.dtype)


def modulate(x, sc, sh, tm=512):
    B, S, D = x.shape
    return pl.pallas_call(
        _modulate_kernel,
        grid=(B, S // tm),
        in_specs=[
            pl.BlockSpec((1, tm, D), lambda b, i: (b, i, 0)),
            pl.BlockSpec((1, 1, D), lambda b, i: (b, 0, 0)),
            pl.BlockSpec((1, 1, D), lambda b, i: (b, 0, 0)),
        ],
        out_specs=pl.BlockSpec((1, tm, D), lambda b, i: (b, i, 0)),
        out_shape=jax.ShapeDtypeStruct((B, S, D), BF16),
        compiler_params=_cparams(("arbitrary", "arbitrary")),
        name="modulate",
    )(x, sc, sh)


def _t5_bucket(dist):
    max_exact = N_BUCKETS // 2
    n = jnp.maximum(dist, 0)
    nf = jnp.maximum(n, 1).astype(jnp.float32)
    large = max_exact + (jnp.log(nf / max_exact) / math.log(MAX_DISTANCE / max_exact) * (N_BUCKETS - max_exact)).astype(jnp.int32)
    large = jnp.minimum(large, N_BUCKETS - 1)
    return jnp.where(n < max_exact, n, large)


def _bias_table_kernel(rb_ref, bm_ref, o_ref, *, rows_per_step):
    h = pl.program_id(0)
    n_steps = bm_ref.shape[0] // rows_per_step

    def body(i, carry):
        r0 = pl.multiple_of(i * rows_per_step, rows_per_step)
        bm = bm_ref[pl.ds(r0, rows_per_step), :]
        acc = jnp.zeros(bm.shape, F32)
        for b in range(N_BUCKETS):
            acc = jnp.where(bm == b, rb_ref[b, h], acc)
        o_ref[0, pl.ds(r0, rows_per_step), :] = acc
        return carry

    lax.fori_loop(0, n_steps, body, 0)


def bias_table(rel_bias, bucket_map, rows_per_step):
    R, W = bucket_map.shape
    H = rel_bias.shape[1]
    return pl.pallas_call(
        functools.partial(_bias_table_kernel, rows_per_step=rows_per_step),
        grid=(H,),
        in_specs=[
            pl.BlockSpec(memory_space=pltpu.SMEM),
            pl.BlockSpec((R, W), lambda h: (0, 0)),
        ],
        out_specs=pl.BlockSpec((1, R, W), lambda h: (h, 0, 0)),
        out_shape=jax.ShapeDtypeStruct((H, R, W), F32),
        compiler_params=_cparams(("arbitrary",)),
        name="bias_table",
    )(rel_bias, bucket_map)


def _toep_width(S):
    return S + BIAS_ROWS


def _toep_col(q0, k0):
    return pl.multiple_of(q0 - k0 + BIAS_ROWS, LANE)


def build_bias_tables(rel_bias, S):
    j = jnp.arange(BIAS_ROWS, dtype=jnp.int32)[:, None]
    c = jnp.arange(_toep_width(S), dtype=jnp.int32)[None, :]
    toep_map = _t5_bucket(c - j - BIAS_ROWS)
    n_cmp_pad = S // NSA_CMP_STRIDE
    t = jnp.arange(S, dtype=jnp.int32)[None, :]
    cmp_end = jnp.arange(n_cmp_pad, dtype=jnp.int32)[:, None] * NSA_CMP_STRIDE + NSA_CMP_LEN - 1
    cmp_map = _t5_bucket(t - cmp_end)
    return bias_table(rel_bias, toep_map, 8), bias_table(rel_bias, cmp_map, 8)


def _proj_kernel(h_ref, w_ref, *o_refs, segs):
    h = h_ref[0]
    for (off, width, kind, scale), o_ref in zip(segs, o_refs):
        acc = jnp.dot(h, w_ref[:, off:off + width], preferred_element_type=F32)
        if scale != 1.0:
            acc = acc * scale
        if kind == "flat":
            o_ref[0] = acc.astype(o_ref.dtype)
        else:
            for g in range(width // LANE):
                o_ref[0, g] = acc[:, g * LANE:(g + 1) * LANE].astype(o_ref.dtype)


def project(h, w_cat, segs, out_dtypes, tm=512):
    B, S, D = h.shape
    out_shapes, out_specs = [], []
    for (off, width, kind, scale), dt in zip(segs, out_dtypes):
        if kind == "flat":
            out_shapes.append(jax.ShapeDtypeStruct((B, S, width), dt))
            out_specs.append(pl.BlockSpec((1, tm, width), lambda b, i: (b, i, 0)))
        else:
            G = width // LANE
            out_shapes.append(jax.ShapeDtypeStruct((B, G, S, LANE), dt))
            out_specs.append(pl.BlockSpec((1, G, tm, LANE), lambda b, i: (b, 0, i, 0)))
    return pl.pallas_call(
        functools.partial(_proj_kernel, segs=segs),
        grid=(B, S // tm),
        in_specs=[
            pl.BlockSpec((1, tm, D), lambda b, i: (b, i, 0)),
            pl.BlockSpec(w_cat.shape, lambda b, i: (0, 0)),
        ],
        out_specs=out_specs,
        out_shape=out_shapes,
        compiler_params=_cparams(("arbitrary", "arbitrary"), 48),
        name="in_proj",
    )(h, w_cat)


_NT = (((1,), (1,)), ((), ()))
_TN = (((0,), (0,)), ((), ()))


def _softmax_step(carry, s, v_tile):
    m, l, acc = carry
    m_new = jnp.maximum(m, jnp.max(s, axis=0, keepdims=True))
    alpha = jnp.exp(m - m_new)
    p = jnp.exp(s - m_new)
    l_new = alpha * l + jnp.sum(p, axis=0, keepdims=True)
    acc_new = alpha * acc + lax.dot_general(v_tile, p.astype(BF16), _TN, preferred_element_type=F32)
    return m_new, l_new, acc_new


def _moba_kernel(q_ref, k_ref, v_ref, tb_ref, o_ref, kmean_sc, sel_sc, *, S, heads_per_step):
    L = MOBA_BLOCK
    dh = MOBA_HEAD_DIM
    nb = S // L
    qi = pl.program_id(2)
    q0 = qi * L

    @pl.when(qi == 0)
    def _():
        kf = k_ref[0].astype(F32)
        kmean_sc[...] = jnp.mean(kf.reshape(nb, L, kf.shape[-1]), axis=1)

    q2 = q_ref[0]
    lane_q = lax.broadcasted_iota(jnp.int32, q2.shape, 1)
    lane_m = lax.broadcasted_iota(jnp.int32, kmean_sc.shape, 1)
    blk = lax.broadcasted_iota(jnp.int32, (nb, L), 0)
    qz = []
    for hh in range(heads_per_step):
        in_head = (lane_q >= hh * dh) & (lane_q < (hh + 1) * dh)
        qz.append(jnp.where(in_head, q2, jnp.zeros_like(q2)))
        km = jnp.where((lane_m >= hh * dh) & (lane_m < (hh + 1) * dh), kmean_sc[...], 0.0)
        gate = lax.dot_general(km, q2.astype(F32), _NT, precision=lax.Precision.HIGHEST,
                               preferred_element_type=F32)
        valid = blk < qi
        gm = jnp.where(valid, gate, NEG_INF)
        rank = jnp.zeros((nb, L), jnp.int32)
        for m in range(nb):
            gmm = gm[m:m + 1, :]
            beats = (gmm > gm) | ((gmm == gm) & (m < blk))
            rank = rank + beats.astype(jnp.int32)
        sel_sc[hh] = jnp.where(valid & (rank < MOBA_TOPK), 1.0, 0.0)

    def bias_tile(hh, k0, n_keys):
        return jnp.concatenate([tb_ref[hh, :, pl.ds(_toep_col(q0, k0 + r * BIAS_ROWS), L)]
                                for r in range(n_keys // BIAS_ROWS)], axis=0)

    krow = lax.broadcasted_iota(jnp.int32, (L, L), 0)
    qcol = lax.broadcasted_iota(jnp.int32, (L, L), 1)
    k_own = k_ref[0, pl.ds(pl.multiple_of(q0, L), L), :]
    v_own = v_ref[0, pl.ds(pl.multiple_of(q0, L), L), :]
    carry = []
    for hh in range(heads_per_step):
        s = lax.dot_general(k_own, qz[hh], _NT, preferred_element_type=F32) + bias_tile(hh, q0, L)
        s = jnp.where(krow <= qcol, s, NEG_INF)
        m0 = jnp.max(s, axis=0, keepdims=True)
        p = jnp.exp(s - m0)
        l0 = jnp.sum(p, axis=0, keepdims=True)
        acc0 = lax.dot_general(v_own, p.astype(BF16), _TN, preferred_element_type=F32)
        carry += [m0, l0, acc0[hh * dh:(hh + 1) * dh]]

    def body(pp, carry):
        k0 = pl.multiple_of(pp * 2 * L, 2 * L)
        kt = k_ref[0, pl.ds(k0, 2 * L), :]
        vt = v_ref[0, pl.ds(k0, 2 * L), :]
        out = []
        for hh in range(heads_per_step):
            m, l, acc = carry[3 * hh:3 * hh + 3]
            s = lax.dot_general(kt, qz[hh], _NT, preferred_element_type=F32) + bias_tile(hh, k0, 2 * L)
            flags = jnp.concatenate(
                [jnp.broadcast_to(sel_sc[hh, pl.ds(2 * pp + j, 1), :], (L, L)) for j in range(2)], axis=0)
            s = jnp.where(flags > 0.5, s, NEG_INF)
            m_new = jnp.maximum(m, jnp.max(s, axis=0, keepdims=True))
            alpha = jnp.exp(m - m_new)
            p = jnp.exp(s - m_new)
            l_new = alpha * l + jnp.sum(p, axis=0, keepdims=True)
            pv = lax.dot_general(vt, p.astype(BF16), _TN, preferred_element_type=F32)
            out += [m_new, l_new, alpha * acc + pv[hh * dh:(hh + 1) * dh]]
        return tuple(out)

    carry = lax.fori_loop(0, (qi + 1) // 2, body, tuple(carry))
    o_t = jnp.concatenate([carry[3 * hh + 2] / carry[3 * hh + 1] for hh in range(heads_per_step)], axis=0)
    o_ref[0] = o_t.T.astype(o_ref.dtype)


def moba_attention(q, k, v, toep):
    B, S, _ = q.shape
    L = MOBA_BLOCK
    hps = 4
    wl = hps * MOBA_HEAD_DIM
    n_hp = MOBA_HEADS // hps
    nb = S // L
    return pl.pallas_call(
        functools.partial(_moba_kernel, S=S, heads_per_step=hps),
        grid=(n_hp, B, nb),
        in_specs=[
            pl.BlockSpec((1, L, wl), lambda hp, b, i: (b, i, hp)),
            pl.BlockSpec((1, S, wl), lambda hp, b, i: (b, 0, hp)),
            pl.BlockSpec((1, S, wl), lambda hp, b, i: (b, 0, hp)),
            pl.BlockSpec((hps, BIAS_ROWS, _toep_width(S)), lambda hp, b, i: (hp, 0, 0)),
        ],
        out_specs=pl.BlockSpec((1, L, wl), lambda hp, b, i: (b, i, hp)),
        out_shape=jax.ShapeDtypeStruct((B, S, MOBA_WIDTH), BF16),
        scratch_shapes=[pltpu.VMEM((nb, wl), F32), pltpu.VMEM((hps, nb, L), F32)],
        compiler_params=_cparams(("arbitrary", "arbitrary", "arbitrary"), 40),
        name="moba_attn",
    )(q, k, v, toep)


def _log_sigmoid(z):
    return jnp.minimum(z, 0.0) - jnp.log1p(jnp.exp(-jnp.abs(z)))


def _gla_kernel(q_ref, k_ref, v_ref, r_ref, lr_ref, wg2_ref, bg2_ref, ng_ref, o_ref, st_sc, *, batch):
    C = GLA_CHUNK
    dk, dv = GLA_DK, GLA_DV

    @pl.when(pl.program_id(0) == 0)
    def _():
        st_sc[...] = jnp.zeros(st_sc.shape, F32)

    row = lax.broadcasted_iota(jnp.int32, (C, C), 0)
    col = lax.broadcasted_iota(jnp.int32, (C, C), 1)
    causal = col <= row
    tri = jnp.where(causal, 1.0, 0.0)
    for b in range(batch):
        z = jnp.dot(lr_ref[b], wg2_ref[...], preferred_element_type=F32) + bg2_ref[...]
        la = _log_sigmoid(z) / GLA_TAU
        bc = jnp.dot(tri, la, precision=lax.Precision.HIGHEST, preferred_element_type=F32)
        b_last = bc[C - 1:C, :]
        qd = q_ref[b] * (dk ** -0.5) * jnp.exp(bc)
        kd = k_ref[b] * jnp.exp(-bc)
        kl = k_ref[b] * jnp.exp(b_last - bc)
        e_last = jnp.exp(b_last)
        v = v_ref[b]
        r = r_ref[b]
        outs = []
        for h in range(GLA_HEADS):
            ks = slice(h * dk, (h + 1) * dk)
            vs = slice(h * dv, (h + 1) * dv)
            qh = qd[:, ks].astype(BF16)
            kh = kd[:, ks].astype(BF16)
            vh = v[:, vs].astype(BF16)
            attn = lax.dot_general(qh, kh, (((1,), (1,)), ((), ())), preferred_element_type=F32)
            attn = jnp.where(causal, attn, 0.0)
            st = st_sc[b, h]
            o = jnp.dot(attn.astype(BF16), vh, preferred_element_type=F32)
            o = o + lax.dot_general(qh, st.astype(BF16), (((1,), (1,)), ((), ())), preferred_element_type=F32)
            upd = lax.dot_general(vh, kl[:, ks].astype(BF16), (((0,), (0,)), ((), ())), preferred_element_type=F32)
            st_sc[b, h] = st * e_last[:, ks] + upd
            o = o * lax.rsqrt(jnp.mean(o * o, axis=-1, keepdims=True) + RMS_EPS)
            rg = r[:, vs]
            outs.append(o * ng_ref[:, vs] * (rg * jax.nn.sigmoid(rg)))
        o_ref[b] = jnp.concatenate(outs, axis=-1).astype(o_ref.dtype)


def gla_mixer(q, k, v, r, lr, wg2_pad, bg2, norm_g):
    B, S, _ = q.shape
    C = GLA_CHUNK
    nc = S // C
    blk = lambda w: pl.BlockSpec((B, C, w), lambda c: (0, c, 0))
    full = lambda a: pl.BlockSpec(a.shape, lambda c: (0,) * a.ndim)
    return pl.pallas_call(
        functools.partial(_gla_kernel, batch=B),
        grid=(nc,),
        in_specs=[blk(GLA_QK_WIDTH), blk(GLA_QK_WIDTH), blk(GLA_V_WIDTH), blk(GLA_V_WIDTH), blk(LANE),
                  full(wg2_pad), full(bg2), full(norm_g)],
        out_specs=blk(GLA_V_WIDTH),
        out_shape=jax.ShapeDtypeStruct((B, S, GLA_V_WIDTH), BF16),
        scratch_shapes=[pltpu.VMEM((B, GLA_HEADS, GLA_DV, GLA_DK), F32)],
        compiler_params=_cparams(("arbitrary",)),
        name="gla",
    )(q, k, v, r, lr, wg2_pad, bg2, norm_g)


def _deepnorm_ln(x, y, gate, ln_g, ln_b):
    z = ALPHA_DN * x + (1.0 + gate) * y
    mu = jnp.mean(z, axis=-1, keepdims=True)
    zc = z - mu
    var = jnp.mean(zc * zc, axis=-1, keepdims=True)
    return zc * lax.rsqrt(var + LN_EPS) * ln_g + ln_b


def _out_ln_kernel(*refs, n_parts, matmul, has_next):
    y_refs = refs[:n_parts]
    pos = n_parts
    if matmul:
        w_refs = refs[pos:pos + n_parts]
        pos += n_parts
    x_ref, g_ref, lng_ref, lnb_ref = refs[pos:pos + 4]
    pos += 4
    if has_next:
        sc_ref, sh_ref = refs[pos:pos + 2]
        pos += 2
    o_ref = refs[pos]
    y = None
    for i in range(n_parts):
        part = jnp.dot(y_refs[i][0], w_refs[i][...], preferred_element_type=F32) if matmul else y_refs[i][0]
        y = part if y is None else y + part
    xn = _deepnorm_ln(x_ref[0], y, g_ref[0], lng_ref[...], lnb_ref[...])
    o_ref[0] = xn
    if has_next:
        refs[pos + 1][0] = (xn * (1.0 + sc_ref[0]) + sh_ref[0]).astype(BF16)


def out_ln(ys, ws, x, gate, ln_g, ln_b, nxt, tm=512):
    B, S, D = x.shape
    tile = lambda w: pl.BlockSpec((1, tm, w), lambda b, i: (b, i, 0))
    vec = pl.BlockSpec((1, 1, D), lambda b, i: (b, 0, 0))
    par = pl.BlockSpec((1, D), lambda b, i: (0, 0))
    args = list(ys)
    in_specs = [tile(y.shape[-1]) for y in ys]
    if ws is not None:
        args += list(ws)
        in_specs += [pl.BlockSpec(w.shape, lambda b, i: (0, 0)) for w in ws]
    args += [x, gate, ln_g, ln_b]
    in_specs += [tile(D), vec, par, par]
    out_shape = [jax.ShapeDtypeStruct((B, S, D), F32)]
    out_specs = [tile(D)]
    if nxt is not None:
        args += list(nxt)
        in_specs += [vec, vec]
        out_shape.append(jax.ShapeDtypeStruct((B, S, D), BF16))
        out_specs.append(tile(D))
    res = pl.pallas_call(
        functools.partial(_out_ln_kernel, n_parts=len(ys), matmul=ws is not None, has_next=nxt is not None),
        grid=(B, S // tm),
        in_specs=in_specs,
        out_specs=out_specs,
        out_shape=out_shape,
        compiler_params=_cparams(("arbitrary", "arbitrary")),
        name="out_ln",
    )(*args)
    return (res[0], res[1]) if nxt is not None else (res[0], None)


def _ffn_ln_kernel(h_ref, w1_ref, w3_ref, w2_ref, x_ref, g_ref, lng_ref, lnb_ref, sc_ref, sh_ref,
                   o_ref, hn_ref, acc_sc):
    j = pl.program_id(2)

    @pl.when(j == 0)
    def _():
        acc_sc[...] = jnp.zeros(acc_sc.shape, F32)

    h = h_ref[0]
    a = jnp.dot(h, w1_ref[...], preferred_element_type=F32)
    b = jnp.dot(h, w3_ref[...], preferred_element_type=F32)
    u = (a * jax.nn.sigmoid(a)) * b
    acc_sc[...] += jnp.dot(u.astype(BF16), w2_ref[...], preferred_element_type=F32)

    @pl.when(j == pl.num_programs(2) - 1)
    def _():
        xn = _deepnorm_ln(x_ref[0], acc_sc[...], g_ref[0], lng_ref[...], lnb_ref[...])
        o_ref[0] = xn
        hn_ref[0] = (xn * (1.0 + sc_ref[0]) + sh_ref[0]).astype(BF16)


def ffn_ln(h, w1, w3, w2, x, gate, ln_g, ln_b, nxt, tm=1024, tf=256):
    B, S, D = x.shape
    F = w1.shape[1]
    tile = pl.BlockSpec((1, tm, D), lambda b, i, j: (b, i, 0))
    vec = pl.BlockSpec((1, 1, D), lambda b, i, j: (b, 0, 0))
    par = pl.BlockSpec((1, D), lambda b, i, j: (0, 0))
    return pl.pallas_call(
        _ffn_ln_kernel,
        grid=(B, S // tm, F // tf),
        in_specs=[tile,
                  pl.BlockSpec((D, tf), lambda b, i, j: (0, j)),
                  pl.BlockSpec((D, tf), lambda b, i, j: (0, j)),
                  pl.BlockSpec((tf, D), lambda b, i, j: (j, 0)),
                  tile, vec, par, par, vec, vec],
        out_specs=[tile, tile],
        out_shape=[jax.ShapeDtypeStruct((B, S, D), F32), jax.ShapeDtypeStruct((B, S, D), BF16)],
        scratch_shapes=[pltpu.VMEM((tm, D), F32)],
        compiler_params=_cparams(("arbitrary", "arbitrary", "arbitrary"), 48),
        name="ffn_ln",
    )(h, w1, w3, w2, x, gate, ln_g, ln_b, nxt[0], nxt[1])


def _compress_kernel(x_ref, pos_ref, w1_ref, w2_ref, o_ref):
    dh = NSA_HEAD_DIM
    x = x_ref[0, 0]
    uv = jnp.dot(x, w1_ref[...], preferred_element_type=F32)
    pc = jnp.dot(pos_ref[...], w1_ref[...], preferred_element_type=F32)
    c = pc[0:1, :dh] + pc[1:2, dh:]
    n = uv.shape[0]
    nxt = pltpu.roll(uv[:, dh:], n - 1, 0)
    hid = uv[:, :dh] + nxt + c
    act = hid * jax.nn.sigmoid(hid)
    o_ref[0, 0] = jnp.dot(act.astype(BF16), w2_ref[...], preferred_element_type=F32).astype(o_ref.dtype)


def nsa_compress(kv, pos2, w1cat, w2):
    B, G, S, dh = kv.shape
    seg = NSA_CMP_STRIDE
    n = S // seg
    x = kv.reshape(B, G, n, seg * dh)
    return pl.pallas_call(
        _compress_kernel,
        grid=(B, G),
        in_specs=[
            pl.BlockSpec((1, 1, n, seg * dh), lambda b, g: (b, g, 0, 0)),
            pl.BlockSpec(pos2.shape, lambda b, g: (0, 0)),
            pl.BlockSpec(w1cat.shape, lambda b, g: (0, 0)),
            pl.BlockSpec(w2.shape, lambda b, g: (0, 0)),
        ],
        out_specs=pl.BlockSpec((1, 1, n, dh), lambda b, g: (b, g, 0, 0)),
        out_shape=jax.ShapeDtypeStruct((B, G, n, dh), BF16),
        compiler_params=_cparams(("arbitrary", "arbitrary")),
        name="nsa_compress",
    )(x, pos2, w1cat, w2)


def _nsa_kernel(q_ref, kc_ref, vc_ref, ks_ref, vs_ref, kw_ref, vw_ref, gt_ref, cb_ref, tb_ref, o_ref, sel_sc,
                *, S, TQ):
    dh = NSA_HEAD_DIM
    HPG = NSA_HPG
    LS = NSA_SLC_BLOCK
    TK = 4 * LS
    R = HPG * TQ
    n_cmp = (S - NSA_CMP_LEN) // NSA_CMP_STRIDE + 1
    n_cp = S // NSA_CMP_STRIDE
    n_slc = S // LS
    n_top = min(NSA_SLC_TOPN, n_slc)
    qi = pl.program_id(2)
    q0 = qi * TQ

    q4 = jnp.concatenate([q_ref[0, :, h * dh:(h + 1) * dh] for h in range(HPG)], axis=0)
    lanes4 = lambda a: jnp.concatenate([a] * HPG, axis=1)

    s = lax.dot_general(kc_ref[0, 0], q4, _NT, preferred_element_type=F32)
    s = s + jnp.concatenate([cb_ref[h] for h in range(HPG)], axis=1)
    n_row = lax.broadcasted_iota(jnp.int32, (n_cp, TQ), 0)
    t_q = q0 + lax.broadcasted_iota(jnp.int32, (n_cp, TQ), 1)
    valid_c = lanes4((n_row * NSA_CMP_STRIDE + (NSA_CMP_LEN - 1) <= t_q) & (n_row < n_cmp))
    s = jnp.where(valid_c, s, NEG_INF)
    m = jnp.max(s, axis=0, keepdims=True)
    m = jnp.where(m == NEG_INF, 0.0, m)
    e = jnp.exp(s - m)
    p_c = e / jnp.maximum(jnp.sum(e, axis=0, keepdims=True), jnp.finfo(F32).tiny)
    o_c = lax.dot_general(vc_ref[0, 0], p_c.astype(BF16), _TN, preferred_element_type=F32)

    p_sum = p_c[:, 0:TQ]
    for h in range(1, HPG):
        p_sum = p_sum + p_c[:, h * TQ:(h + 1) * TQ]
    om = lax.broadcasted_iota(jnp.int32, (n_slc, n_cp), 0)
    c_lo = lax.broadcasted_iota(jnp.int32, (n_slc, n_cp), 1) * NSA_CMP_STRIDE
    ov = jnp.maximum(jnp.minimum(c_lo + NSA_CMP_LEN, om * LS + LS) - jnp.maximum(c_lo, om * LS), 0)
    overlap_t = ov.astype(F32) / NSA_CMP_LEN
    imp_t = jnp.dot(overlap_t, p_sum, precision=lax.Precision.HIGHEST, preferred_element_type=F32)
    blk = lax.broadcasted_iota(jnp.int32, (n_slc, TQ), 0)
    tq_lane = q0 + lax.broadcasted_iota(jnp.int32, (n_slc, TQ), 1)
    tb = lax.shift_right_logical(tq_lane, LS.bit_length() - 1)
    forced = (blk == 0) | (blk == tb) | (blk == tb - 1)
    key = jnp.where(blk > tb, NEG_INF, jnp.where(forced, jnp.inf, imp_t))
    rank = jnp.zeros((n_slc, TQ), jnp.int32)
    for mm in range(n_slc):
        km = key[mm:mm + 1, :]
        beats = (km > key) | ((km == key) & (mm < blk))
        rank = rank + beats.astype(jnp.int32)
    sel_sc[...] = jnp.where((rank < n_top) & (blk <= tb), 1.0, 0.0)

    def toep_tile(k0, n_keys):
        return jnp.concatenate(
            [jnp.concatenate([tb_ref[h, :, pl.ds(_toep_col(q0, k0 + r * BIAS_ROWS), TQ)] for h in range(HPG)], axis=1)
             for r in range(n_keys // BIAS_ROWS)], axis=0)

    init = (jnp.full((1, R), NEG_INF, F32), jnp.zeros((1, R), F32), jnp.zeros((dh, R), F32))

    krow_s = lax.broadcasted_iota(jnp.int32, (TK, TQ), 0)
    tq_s = q0 + lax.broadcasted_iota(jnp.int32, (TK, TQ), 1)

    def slc_body(kt, carry):
        k0 = pl.multiple_of(kt * TK, TK)
        flags = jnp.concatenate(
            [jnp.broadcast_to(sel_sc[pl.ds(kt * (TK // LS) + j, 1), :], (LS, TQ)) for j in range(TK // LS)], axis=0)
        mask = lanes4((flags > 0.5) & (k0 + krow_s <= tq_s))
        s = lax.dot_general(ks_ref[0, 0, pl.ds(k0, TK), :], q4, _NT, preferred_element_type=F32) + toep_tile(k0, TK)
        return _softmax_step(carry, jnp.where(mask, s, NEG_INF), vs_ref[0, 0, pl.ds(k0, TK), :])

    n_kt = (q0 + TQ - 1) // TK + 1
    _, l_s, acc_s = lax.fori_loop(0, n_kt, slc_body, init)
    o_s = acc_s / l_s

    krow_w = lax.broadcasted_iota(jnp.int32, (TQ, TQ), 0)
    tq_w = q0 + lax.broadcasted_iota(jnp.int32, (TQ, TQ), 1)

    def win_body(j, carry):
        k0 = pl.multiple_of(q0 - j * TQ, TQ)
        dist = tq_w - (k0 + krow_w)
        mask = lanes4((dist >= 0) & (dist < NSA_WINDOW))
        s = lax.dot_general(kw_ref[0, 0, pl.ds(k0, TQ), :], q4, _NT, preferred_element_type=F32) + toep_tile(k0, TQ)
        return _softmax_step(carry, jnp.where(mask, s, NEG_INF), vw_ref[0, 0, pl.ds(k0, TQ), :])

    n_wt = jnp.minimum(qi, NSA_WINDOW // TQ) + 1
    _, l_w, acc_w = lax.fori_loop(0, n_wt, win_body, init)
    o_w = acc_w / l_w

    g_t = jax.nn.sigmoid(gt_ref[0, 0]).T
    outs = []
    for h in range(HPG):
        cols = slice(h * TQ, (h + 1) * TQ)
        o = (g_t[h:h + 1] * o_c[:, cols] + g_t[HPG + h:HPG + h + 1] * o_s[:, cols]
             + g_t[2 * HPG + h:2 * HPG + h + 1] * o_w[:, cols])
        outs.append(o.T)
    o_ref[0] = jnp.concatenate(outs, axis=1).astype(o_ref.dtype)


def nsa_attention(q, kc, vc, ks, vs, kw, vw, gates, cmpb, toep, TQ=256):
    B, S, _ = q.shape
    G, HPG, dh = NSA_KV_GROUPS, NSA_HPG, NSA_HEAD_DIM
    n_cp = S // NSA_CMP_STRIDE
    full_kv = pl.BlockSpec((1, 1, S, dh), lambda b, g, i: (b, g, 0, 0))
    cmp_kv = pl.BlockSpec((1, 1, n_cp, dh), lambda b, g, i: (b, g, 0, 0))
    return pl.pallas_call(
        functools.partial(_nsa_kernel, S=S, TQ=TQ),
        grid=(B, G, S // TQ),
        in_specs=[
            pl.BlockSpec((1, TQ, HPG * dh), lambda b, g, i: (b, i, g)),
            cmp_kv, cmp_kv, full_kv, full_kv, full_kv, full_kv,
            pl.BlockSpec((1, 1, TQ, LANE), lambda b, g, i: (b, g, i, 0)),
            pl.BlockSpec((HPG, n_cp, TQ), lambda b, g, i: (g, 0, i)),
            pl.BlockSpec((HPG, BIAS_ROWS, _toep_width(S)), lambda b, g, i: (g, 0, 0)),
        ],
        out_specs=pl.BlockSpec((1, TQ, HPG * dh), lambda b, g, i: (b, i, g)),
        out_shape=jax.ShapeDtypeStruct((B, S, NSA_Q_WIDTH), BF16),
        scratch_shapes=[pltpu.VMEM((S // NSA_SLC_BLOCK, TQ), F32)],
        compiler_params=_cparams(("arbitrary", "arbitrary", "arbitrary"), 48),
        name="nsa_attn",
    )(q, kc, vc, ks, vs, kw, vw, gates, cmpb, toep)


def _router_kernel(h_ref, wr_ref, e_ref, w_ref):
    logits = jnp.dot(h_ref[...].astype(F32), wr_ref[...], precision=lax.Precision.HIGHEST,
                     preferred_element_type=F32)
    lane = lax.broadcasted_iota(jnp.int32, logits.shape, 1)
    logits = jnp.where(lane < N_EXPERTS, logits, NEG_INF)
    l1 = jnp.max(logits, axis=-1, keepdims=True)
    i1 = jnp.min(jnp.where(logits == l1, lane, LANE), axis=-1, keepdims=True)
    rest = jnp.where(lane == i1, NEG_INF, logits)
    l2 = jnp.max(rest, axis=-1, keepdims=True)
    i2 = jnp.min(jnp.where(rest == l2, lane, LANE), axis=-1, keepdims=True)
    e2 = jnp.exp(l2 - l1)
    den = 1.0 + e2
    e_ref[...] = jnp.where(lane == 0, i1, jnp.where(lane == 1, i2, 0))
    w_ref[...] = jnp.where(lane == 0, 1.0 / den, jnp.where(lane == 1, e2 / den, 0.0))


def moe_route(h2, wr_pad, tm=1024):
    N, D = h2.shape
    return pl.pallas_call(
        _router_kernel,
        grid=(N // tm,),
        in_specs=[pl.BlockSpec((tm, D), lambda i: (i, 0)), pl.BlockSpec(wr_pad.shape, lambda i: (0, 0))],
        out_specs=[pl.BlockSpec((tm, LANE), lambda i: (i, 0)), pl.BlockSpec((tm, LANE), lambda i: (i, 0))],
        out_shape=[jax.ShapeDtypeStruct((N, LANE), jnp.int32), jax.ShapeDtypeStruct((N, LANE), F32)],
        compiler_params=_cparams(("arbitrary",)),
        name="moe_router",
    )(h2, wr_pad)


def _moe_ffn_kernel(be_ref, nu_ref, x_ref, rw_ref, w1_ref, w3_ref, w2_ref, o_ref, acc_sc):
    i = pl.program_id(0)
    j = pl.program_id(1)
    last = pl.num_programs(1) - 1
    used = i < nu_ref[0]

    @pl.when(j == 0)
    def _():
        acc_sc[...] = jnp.zeros(acc_sc.shape, F32)

    @pl.when(used)
    def _():
        x = x_ref[...]
        a = jnp.dot(x, w1_ref[0], preferred_element_type=F32)
        b = jnp.dot(x, w3_ref[0], preferred_element_type=F32)
        u = (a * jax.nn.sigmoid(a)) * b
        acc_sc[...] += jnp.dot(u.astype(BF16), w2_ref[0], preferred_element_type=F32)

    @pl.when(j == last)
    def _():
        o_ref[...] = acc_sc[...] * rw_ref[0]


def moe_ffn(xs, row_w, blk_e, n_used, w1, w3, w2, tf=256):
    n_rows, D = xs.shape
    RB = MOE_ROW_BLOCK
    n_blocks = n_rows // RB
    F = w1.shape[2]
    nf = F // tf

    def wcol(i, j, be, nu):
        return (be[i], 0, jnp.where(i < nu[0], j, nf - 1))

    def wrow(i, j, be, nu):
        return (be[i], jnp.where(i < nu[0], j, nf - 1), 0)

    grid_spec = pltpu.PrefetchScalarGridSpec(
        num_scalar_prefetch=2,
        grid=(n_blocks, nf),
        in_specs=[
            pl.BlockSpec((RB, D), lambda i, j, be, nu: (i, 0)),
            pl.BlockSpec((1, RB, 1), lambda i, j, be, nu: (i, 0, 0)),
            pl.BlockSpec((1, D, tf), wcol),
            pl.BlockSpec((1, D, tf), wcol),
            pl.BlockSpec((1, tf, D), wrow),
        ],
        out_specs=pl.BlockSpec((RB, D), lambda i, j, be, nu: (i, 0)),
        scratch_shapes=[pltpu.VMEM((RB, D), F32)],
    )
    return pl.pallas_call(
        _moe_ffn_kernel,
        grid_spec=grid_spec,
        out_shape=jax.ShapeDtypeStruct((n_rows, D), F32),
        compiler_params=_cparams(("arbitrary", "arbitrary")),
        name="moe_ffn",
    )(blk_e, n_used, xs, row_w, w1, w3, w2)


def moe_dispatch(e_idx, w_top, n_tok):
    RB = MOE_ROW_BLOCK
    n_assign = n_tok * TOP_K
    n_blocks = (n_assign + N_EXPERTS * (RB - 1) + RB - 1) // RB
    n_rows = n_blocks * RB
    flat_e = e_idx[:, :TOP_K].reshape(-1)
    flat_w = w_top[:, :TOP_K].reshape(-1)
    onehot = (flat_e[:, None] == jnp.arange(N_EXPERTS, dtype=jnp.int32)[None, :]).astype(jnp.int32)
    csum = jnp.cumsum(onehot, axis=0)
    counts = csum[-1]
    pos_in_e = jnp.sum(csum * onehot, axis=1) - 1
    padded = (counts + RB - 1) // RB * RB
    pad_end = jnp.cumsum(padded)
    pad_start = pad_end - padded
    dest = (jnp.sum(pad_start[None, :] * onehot, axis=1) + pos_in_e).astype(jnp.int32)
    flat_tok = jnp.arange(n_assign, dtype=jnp.int32) // TOP_K
    row_tok = jnp.zeros((n_rows,), jnp.int32).at[dest].set(flat_tok)
    row_w = jnp.zeros((n_rows,), F32).at[dest].set(flat_w)
    blk_start = jnp.arange(n_blocks, dtype=jnp.int32) * RB
    blk_e = jnp.minimum(jnp.sum((blk_start[:, None] >= pad_end[None, :]).astype(jnp.int32), axis=1), N_EXPERTS - 1)
    n_used = (pad_end[-1] // RB).astype(jnp.int32).reshape(1)
    return dest.reshape(n_tok, TOP_K), row_tok, row_w.reshape(n_blocks, RB, 1), blk_e.astype(jnp.int32), n_used


def _pad_cols(w, width):
    return jnp.pad(w, ((0, 0), (0, width - w.shape[1])))


def kernel(x, c, rel_bias, ada_w, ada_b, ln_g, ln_b, ab_w_in, gla_wg2, gla_bg2, gla_norm_g, ab_w_out,
           ffn_w1, ffn_w3, ffn_w2, nsa_w_in, nsa_cmp_pos, nsa_cmp_w1k, nsa_cmp_w2k, nsa_cmp_w1v, nsa_cmp_w2v,
           nsa_w_out, moe_router, moe_w1, moe_w3, moe_w2):
    B, S, D = x.shape
    n_tok = B * S
    mod = adaln_mod(c, ada_w, ada_b)
    mods = [[m.reshape(B, 1, D) for m in jnp.split(mod[i], 6, axis=-1)] for i in range(DEPTH)]
    toep, cmpb = build_bias_tables(rel_bias, S)

    sh1, sc1 = mods[0][0], mods[0][1]
    h = modulate(x, sc1, sh1)
    for i in range(DEPTH):
        j = i // 2
        _, _, g1, sh2, sc2, g2 = mods[i]
        nxt2 = (mods[i + 1][1], mods[i + 1][0]) if i + 1 < DEPTH else None
        lng = ln_g[i].reshape(2, 1, D)
        lnb = ln_b[i].reshape(2, 1, D)
        if i % 2 == 0:
            w = ab_w_in[j]
            cuts = np.cumsum((MOBA_WIDTH, MOBA_WIDTH, MOBA_WIDTH, GLA_QK_WIDTH, GLA_QK_WIDTH, GLA_V_WIDTH,
                              GLA_V_WIDTH))
            parts = jnp.split(w, cuts.tolist(), axis=1)
            parts[-1] = _pad_cols(parts[-1], LANE)
            w_cat = jnp.concatenate(parts, axis=1).astype(BF16)
            widths = [p.shape[1] for p in parts]
            offs = np.concatenate([[0], np.cumsum(widths)[:-1]]).tolist()
            scales = [MOBA_HEAD_DIM ** -0.5] + [1.0] * 7
            segs = tuple((o, wd, "flat", s) for o, wd, s in zip(offs, widths, scales))
            dts = [BF16, BF16, BF16, F32, F32, F32, F32, F32]
            mq, mk, mv, gq, gk, gv, gr, glr = project(h, w_cat, segs, dts)
            y_moba = moba_attention(mq, mk, mv, toep)
            wg2_pad = jnp.pad(gla_wg2[j], ((0, LANE - GLA_GATE_RANK), (0, 0)))
            y_gla = gla_mixer(gq, gk, gv, gr, glr, wg2_pad, gla_bg2[j].reshape(1, -1), gla_norm_g[j].reshape(1, -1))
            w_out = ab_w_out[j].astype(BF16)
            x, h = out_ln([y_moba, y_gla], [w_out[:MOBA_WIDTH], w_out[MOBA_WIDTH:]], x, g1, lng[0], lnb[0],
                          (sc2, sh2))
            x, h = ffn_ln(h, ffn_w1[j].astype(BF16), ffn_w3[j].astype(BF16), ffn_w2[j].astype(BF16), x, g2,
                          lng[1], lnb[1], nxt2)
        else:
            G, HPG, dh = NSA_KV_GROUPS, NSA_HPG, NSA_HEAD_DIM
            w = nsa_w_in[j]
            wq = w[:, :NSA_Q_WIDTH]
            wkv = w[:, NSA_Q_WIDTH:NSA_Q_WIDTH + 6 * NSA_KV_WIDTH]
            wg = w[:, NSA_Q_WIDTH + 6 * NSA_KV_WIDTH:].reshape(D, G, HPG, 3)
            wg = _pad_cols(wg.transpose(0, 1, 3, 2).reshape(D * G, 3 * HPG), LANE).reshape(D, G * LANE)
            w_cat = jnp.concatenate([wq, wkv, wg], axis=1).astype(BF16)
            segs = [(0, NSA_Q_WIDTH, "flat", dh ** -0.5)]
            segs += [(NSA_Q_WIDTH + n * NSA_KV_WIDTH, NSA_KV_WIDTH, "group", 1.0) for n in range(6)]
            segs += [(NSA_Q_WIDTH + 6 * NSA_KV_WIDTH, G * LANE, "group", 1.0)]
            dts = [BF16] * 7 + [F32]
            q, kc, vc, ks, vs, kw, vw, gates = project(h, w_cat, tuple(segs), dts)
            half = NSA_CMP_STRIDE * dh
            pos2 = jnp.pad(nsa_cmp_pos[j].reshape(2, half), ((0, 6), (0, 0))).astype(BF16)
            w1k = jnp.concatenate([nsa_cmp_w1k[j][:half], nsa_cmp_w1k[j][half:]], axis=1).astype(BF16)
            w1v = jnp.concatenate([nsa_cmp_w1v[j][:half], nsa_cmp_w1v[j][half:]], axis=1).astype(BF16)
            k_cmp = nsa_compress(kc, pos2, w1k, nsa_cmp_w2k[j].astype(BF16))
            v_cmp = nsa_compress(vc, pos2, w1v, nsa_cmp_w2v[j].astype(BF16))
            y = nsa_attention(q, k_cmp, v_cmp, ks, vs, kw, vw, gates, cmpb, toep)
            x, h = out_ln([y], [nsa_w_out[j].astype(BF16)], x, g1, lng[0], lnb[0], (sc2, sh2))
            h2 = h.reshape(n_tok, D)
            e_idx, w_top = moe_route(h2, _pad_cols(moe_router[j], LANE))
            dest, row_tok, row_w, blk_e, n_used = moe_dispatch(e_idx, w_top, n_tok)
            xs = jnp.take(h2, row_tok, axis=0)
            ys = moe_ffn(xs, row_w, blk_e, n_used, moe_w1[j].astype(BF16), moe_w3[j].astype(BF16),
                         moe_w2[j].astype(BF16))
            y0 = jnp.take(ys, dest[:, 0], axis=0).reshape(B, S, D)
            y1 = jnp.take(ys, dest[:, 1], axis=0).reshape(B, S, D)
            x, h = out_ln([y0, y1], None, x, g2, lng[1], lnb[1], nxt2)
    return x
```

```python
import functools
import math

import jax
import jax.numpy as jnp
import numpy as np
from jax import lax
from jax.experimental import pallas as pl
from jax.experimental.pallas import tpu as pltpu

F32 = jnp.float32
BF16 = jnp.bfloat16
NEG_INF = float("-inf")

DEPTH = 4
ALPHA_DN = (2 * DEPTH) ** 0.25
LN_EPS = 1e-5
RMS_EPS = 1e-6

N_BUCKETS = 32
MAX_DISTANCE = 1024

MOBA_HEADS = 8
MOBA_HEAD_DIM = 64
MOBA_BLOCK = 256
MOBA_TOPK = 3
MOBA_WIDTH = MOBA_HEADS * MOBA_HEAD_DIM

GLA_HEADS = 4
GLA_DK = 64
GLA_DV = 128
GLA_GATE_RANK = 16
GLA_TAU = 16.0
GLA_CHUNK = 64
GLA_QK_WIDTH = GLA_HEADS * GLA_DK
GLA_V_WIDTH = GLA_HEADS * GLA_DV

NSA_HEADS = 8
NSA_KV_GROUPS = 2
NSA_HPG = NSA_HEADS // NSA_KV_GROUPS
NSA_HEAD_DIM = 128
NSA_CMP_LEN = 32
NSA_CMP_STRIDE = 16
NSA_SLC_BLOCK = 64
NSA_SLC_TOPN = 16
NSA_WINDOW = 512
NSA_Q_WIDTH = NSA_HEADS * NSA_HEAD_DIM
NSA_KV_WIDTH = NSA_KV_GROUPS * NSA_HEAD_DIM

N_EXPERTS = 8
TOP_K = 2
MOE_ROW_BLOCK = 512

LANE = 128
BIAS_ROWS = 128


def _cparams(sem, vmem_mib=40):
    return pltpu.CompilerParams(dimension_semantics=sem, vmem_limit_bytes=vmem_mib << 20)


def _adaln_kernel(c_ref, w_ref, b_ref, o_ref):
    c = c_ref[...]
    cond = c * jax.nn.sigmoid(c)
    o_ref[0] = jnp.dot(cond, w_ref[0], preferred_element_type=F32) + b_ref[0]


def adaln_mod(c, ada_w, ada_b):
    B, D = c.shape
    L, _, N = ada_w.shape
    rows = 8
    cp = jnp.zeros((rows, D), F32).at[:B].set(c)
    tn = N // 4
    out = pl.pallas_call(
        _adaln_kernel,
        grid=(L, N // tn),
        in_specs=[
            pl.BlockSpec((rows, D), lambda l, j: (0, 0)),
            pl.BlockSpec((1, D, tn), lambda l, j: (l, 0, j)),
            pl.BlockSpec((1, 1, tn), lambda l, j: (l, 0, j)),
        ],
        out_specs=pl.BlockSpec((1, rows, tn), lambda l, j: (l, 0, j)),
        out_shape=jax.ShapeDtypeStruct((L, rows, N), F32),
        compiler_params=_cparams(("arbitrary", "arbitrary")),
        name="adaln_mod",
    )(cp, ada_w, ada_b.reshape(L, 1, N))
    return out[:, :B]


def _modulate_kernel(x_ref, sc_ref, sh_ref, o_ref):
    o_ref[0] = (x_ref[0] * (1.0 + sc_ref[0]) + sh_ref[0]).astype(o_ref.dtype)


def modulate(x, sc, sh, tm=512):
    B, S, D = x.shape
    return pl.pallas_call(
        _modulate_kernel,
        grid=(B, S // tm),
        in_specs=[
            pl.BlockSpec((1, tm, D), lambda b, i: (b, i, 0)),
            pl.BlockSpec((1, 1, D), lambda b, i: (b, 0, 0)),
            pl.BlockSpec((1, 1, D), lambda b, i: (b, 0, 0)),
        ],
        out_specs=pl.BlockSpec((1, tm, D), lambda b, i: (b, i, 0)),
        out_shape=jax.ShapeDtypeStruct((B, S, D), BF16),
        compiler_params=_cparams(("arbitrary", "arbitrary")),
        name="modulate",
    )(x, sc, sh)


def _t5_bucket(dist):
    max_exact = N_BUCKETS // 2
    n = jnp.maximum(dist, 0)
    nf = jnp.maximum(n, 1).astype(jnp.float32)
    large = max_exact + (jnp.log(nf / max_exact) / math.log(MAX_DISTANCE / max_exact) * (N_BUCKETS - max_exact)).astype(jnp.int32)
    large = jnp.minimum(large, N_BUCKETS - 1)
    return jnp.where(n < max_exact, n, large)


def _bias_table_kernel(rb_ref, bm_ref, o_ref, *, rows_per_step):
    h = pl.program_id(0)
    n_steps = bm_ref.shape[0] // rows_per_step

    def body(i, carry):
        r0 = pl.multiple_of(i * rows_per_step, rows_per_step)
        bm = bm_ref[pl.ds(r0, rows_per_step), :]
        acc = jnp.zeros(bm.shape, F32)
        for b in range(N_BUCKETS):
            acc = jnp.where(bm == b, rb_ref[b, h], acc)
        o_ref[0, pl.ds(r0, rows_per_step), :] = acc
        return carry

    lax.fori_loop(0, n_steps, body, 0)


def bias_table(rel_bias, bucket_map, rows_per_step):
    R, W = bucket_map.shape
    H = rel_bias.shape[1]
    return pl.pallas_call(
        functools.partial(_bias_table_kernel, rows_per_step=rows_per_step),
        grid=(H,),
        in_specs=[
            pl.BlockSpec(memory_space=pltpu.SMEM),
            pl.BlockSpec((R, W), lambda h: (0, 0)),
        ],
        out_specs=pl.BlockSpec((1, R, W), lambda h: (h, 0, 0)),
        out_shape=jax.ShapeDtypeStruct((H, R, W), F32),
        compiler_params=_cparams(("arbitrary",)),
        name="bias_table",
    )(rel_bias, bucket_map)


def _toep_width(S):
    return S + BIAS_ROWS


def _toep_col(q0, k0):
    return pl.multiple_of(q0 - k0 + BIAS_ROWS, LANE)


def build_bias_tables(rel_bias, S):
    j = jnp.arange(BIAS_ROWS, dtype=jnp.int32)[:, None]
    c = jnp.arange(_toep_width(S), dtype=jnp.int32)[None, :]
    toep_map = _t5_bucket(c - j - BIAS_ROWS)
    n_cmp_pad = S // NSA_CMP_STRIDE
    t = jnp.arange(S, dtype=jnp.int32)[None, :]
    cmp_end = jnp.arange(n_cmp_pad, dtype=jnp.int32)[:, None] * NSA_CMP_STRIDE + NSA_CMP_LEN - 1
    cmp_map = _t5_bucket(t - cmp_end)
    return bias_table(rel_bias, toep_map, 8), bias_table(rel_bias, cmp_map, 8)


def _proj_kernel(h_ref, w_ref, *o_refs, segs):
    h = h_ref[0]
    for (off, width, kind, scale), o_ref in zip(segs, o_refs):
        acc = jnp.dot(h, w_ref[:, off:off + width], preferred_element_type=F32)
        if scale != 1.0:
            acc = acc * scale
        if kind == "flat":
            o_ref[0] = acc.astype(o_ref.dtype)
        else:
            for g in range(width // LANE):
                o_ref[0, g] = acc[:, g * LANE:(g + 1) * LANE].astype(o_ref.dtype)


def project(h, w_cat, segs, out_dtypes, tm=512):
    B, S, D = h.shape
    out_shapes, out_specs = [], []
    for (off, width, kind, scale), dt in zip(segs, out_dtypes):
        if kind == "flat":
            out_shapes.append(jax.ShapeDtypeStruct((B, S, width), dt))
            out_specs.append(pl.BlockSpec((1, tm, width), lambda b, i: (b, i, 0)))
        else:
            G = width // LANE
            out_shapes.append(jax.ShapeDtypeStruct((B, G, S, LANE), dt))
            out_specs.append(pl.BlockSpec((1, G, tm, LANE), lambda b, i: (b, 0, i, 0)))
    return pl.pallas_call(
        functools.partial(_proj_kernel, segs=segs),
        grid=(B, S // tm),
        in_specs=[
            pl.BlockSpec((1, tm, D), lambda b, i: (b, i, 0)),
            pl.BlockSpec(w_cat.shape, lambda b, i: (0, 0)),
        ],
        out_specs=out_specs,
        out_shape=out_shapes,
        compiler_params=_cparams(("arbitrary", "arbitrary"), 48),
        name="in_proj",
    )(h, w_cat)


_NT = (((1,), (1,)), ((), ()))
_TN = (((0,), (0,)), ((), ()))


def _softmax_step(carry, s, v_tile):
    m, l, acc = carry
    m_new = jnp.maximum(m, jnp.max(s, axis=0, keepdims=True))
    alpha = jnp.exp(m - m_new)
    p = jnp.exp(s - m_new)
    l_new = alpha * l + jnp.sum(p, axis=0, keepdims=True)
    acc_new = alpha * acc + lax.dot_general(v_tile, p.astype(BF16), _TN, preferred_element_type=F32)
    return m_new, l_new, acc_new


def _moba_kernel(q_ref, k_ref, v_ref, tb_ref, o_ref, kmean_sc, sel_sc, *, S, heads_per_step):
    L = MOBA_BLOCK
    dh = MOBA_HEAD_DIM
    nb = S // L
    qi = pl.program_id(2)
    q0 = qi * L

    @pl.when(qi == 0)
    def _():
        kf = k_ref[0].astype(F32)
        kmean_sc[...] = jnp.mean(kf.reshape(nb, L, kf.shape[-1]), axis=1)

    q2 = q_ref[0]
    lane_q = lax.broadcasted_iota(jnp.int32, q2.shape, 1)
    lane_m = lax.broadcasted_iota(jnp.int32, kmean_sc.shape, 1)
    blk = lax.broadcasted_iota(jnp.int32, (nb, L), 0)
    qz = []
    for hh in range(heads_per_step):
        in_head = (lane_q >= hh * dh) & (lane_q < (hh + 1) * dh)
        qz.append(jnp.where(in_head, q2, jnp.zeros_like(q2)))
        km = jnp.where((lane_m >= hh * dh) & (lane_m < (hh + 1) * dh), kmean_sc[...], 0.0)
        gate = lax.dot_general(km, q2.astype(F32), _NT, precision=lax.Precision.HIGHEST,
                               preferred_element_type=F32)
        valid = blk < qi
        gm = jnp.where(valid, gate, NEG_INF)
        rank = jnp.zeros((nb, L), jnp.int32)
        for m in range(nb):
            gmm = gm[m:m + 1, :]
            beats = (gmm > gm) | ((gmm == gm) & (m < blk))
            rank = rank + beats.astype(jnp.int32)
        sel_sc[hh] = jnp.where(valid & (rank < MOBA_TOPK), 1.0, 0.0)

    def bias_tile(hh, k0, n_keys):
        return jnp.concatenate([tb_ref[hh, :, pl.ds(_toep_col(q0, k0 + r * BIAS_ROWS), L)]
                                for r in range(n_keys // BIAS_ROWS)], axis=0)

    krow = lax.broadcasted_iota(jnp.int32, (L, L), 0)
    qcol = lax.broadcasted_iota(jnp.int32, (L, L), 1)
    k_own = k_ref[0, pl.ds(pl.multiple_of(q0, L), L), :]
    v_own = v_ref[0, pl.ds(pl.multiple_of(q0, L), L), :]
    carry = []
    for hh in range(heads_per_step):
        s = lax.dot_general(k_own, qz[hh], _NT, preferred_element_type=F32) + bias_tile(hh, q0, L)
        s = jnp.where(krow <= qcol, s, NEG_INF)
        m0 = jnp.max(s, axis=0, keepdims=True)
        p = jnp.exp(s - m0)
        l0 = jnp.sum(p, axis=0, keepdims=True)
        acc0 = lax.dot_general(v_own, p.astype(BF16), _TN, preferred_element_type=F32)
        carry += [m0, l0, acc0[hh * dh:(hh + 1) * dh]]

    def body(pp, carry):
        k0 = pl.multiple_of(pp * 2 * L, 2 * L)
        kt = k_ref[0, pl.ds(k0, 2 * L), :]
        vt = v_ref[0, pl.ds(k0, 2 * L), :]
        out = []
        for hh in range(heads_per_step):
            m, l, acc = carry[3 * hh:3 * hh + 3]
            s = lax.dot_general(kt, qz[hh], _NT, preferred_element_type=F32) + bias_tile(hh, k0, 2 * L)
            flags = jnp.concatenate(
                [jnp.broadcast_to(sel_sc[hh, pl.ds(2 * pp + j, 1), :], (L, L)) for j in range(2)], axis=0)
            s = jnp.where(flags > 0.5, s, NEG_INF)
            m_new = jnp.maximum(m, jnp.max(s, axis=0, keepdims=True))
            alpha = jnp.exp(m - m_new)
            p = jnp.exp(s - m_new)
            l_new = alpha * l + jnp.sum(p, axis=0, keepdims=True)
            pv = lax.dot_general(vt, p.astype(BF16), _TN, preferred_element_type=F32)
            out += [m_new, l_new, alpha * acc + pv[hh * dh:(hh + 1) * dh]]
        return tuple(out)

    carry = lax.fori_loop(0, (qi + 1) // 2, body, tuple(carry))
    o_t = jnp.concatenate([carry[3 * hh + 2] / carry[3 * hh + 1] for hh in range(heads_per_step)], axis=0)
    o_ref[0] = o_t.T.astype(o_ref.dtype)


def moba_attention(q, k, v, toep):
    B, S, _ = q.shape
    L = MOBA_BLOCK
    hps = 4
    wl = hps * MOBA_HEAD_DIM
    n_hp = MOBA_HEADS // hps
    nb = S // L
    return pl.pallas_call(
        functools.partial(_moba_kernel, S=S, heads_per_step=hps),
        grid=(n_hp, B, nb),
        in_specs=[
            pl.BlockSpec((1, L, wl), lambda hp, b, i: (b, i, hp)),
            pl.BlockSpec((1, S, wl), lambda hp, b, i: (b, 0, hp)),
            pl.BlockSpec((1, S, wl), lambda hp, b, i: (b, 0, hp)),
            pl.BlockSpec((hps, BIAS_ROWS, _toep_width(S)), lambda hp, b, i: (hp, 0, 0)),
        ],
        out_specs=pl.BlockSpec((1, L, wl), lambda hp, b, i: (b, i, hp)),
        out_shape=jax.ShapeDtypeStruct((B, S, MOBA_WIDTH), BF16),
        scratch_shapes=[pltpu.VMEM((nb, wl), F32), pltpu.VMEM((hps, nb, L), F32)],
        compiler_params=_cparams(("arbitrary", "arbitrary", "arbitrary"), 40),
        name="moba_attn",
    )(q, k, v, toep)


def _log_sigmoid(z):
    return jnp.minimum(z, 0.0) - jnp.log1p(jnp.exp(-jnp.abs(z)))


def _gla_kernel(q_ref, k_ref, v_ref, r_ref, lr_ref, wg2_ref, bg2_ref, ng_ref, o_ref, st_sc, *, batch):
    C = GLA_CHUNK
    dk, dv = GLA_DK, GLA_DV

    @pl.when(pl.program_id(0) == 0)
    def _():
        st_sc[...] = jnp.zeros(st_sc.shape, F32)

    row = lax.broadcasted_iota(jnp.int32, (C, C), 0)
    col = lax.broadcasted_iota(jnp.int32, (C, C), 1)
    causal = col <= row
    tri = jnp.where(causal, 1.0, 0.0)
    for b in range(batch):
        z = jnp.dot(lr_ref[b], wg2_ref[...], preferred_element_type=F32) + bg2_ref[...]
        la = _log_sigmoid(z) / GLA_TAU
        bc = jnp.dot(tri, la, precision=lax.Precision.HIGHEST, preferred_element_type=F32)
        b_last = bc[C - 1:C, :]
        qd = q_ref[b] * (dk ** -0.5) * jnp.exp(bc)
        kd = k_ref[b] * jnp.exp(-bc)
        kl = k_ref[b] * jnp.exp(b_last - bc)
        e_last = jnp.exp(b_last)
        v = v_ref[b]
        r = r_ref[b]
        outs = []
        for h in range(GLA_HEADS):
            ks = slice(h * dk, (h + 1) * dk)
            vs = slice(h * dv, (h + 1) * dv)
            qh = qd[:, ks].astype(BF16)
            kh = kd[:, ks].astype(BF16)
            vh = v[:, vs].astype(BF16)
            attn = lax.dot_general(qh, kh, (((1,), (1,)), ((), ())), preferred_element_type=F32)
            attn = jnp.where(causal, attn, 0.0)
            st = st_sc[b, h]
            o = jnp.dot(attn.astype(BF16), vh, preferred_element_type=F32)
            o = o + lax.dot_general(qh, st.astype(BF16), (((1,), (1,)), ((), ())), preferred_element_type=F32)
            upd = lax.dot_general(vh, kl[:, ks].astype(BF16), (((0,), (0,)), ((), ())), preferred_element_type=F32)
            st_sc[b, h] = st * e_last[:, ks] + upd
            o = o * lax.rsqrt(jnp.mean(o * o, axis=-1, keepdims=True) + RMS_EPS)
            rg = r[:, vs]
            outs.append(o * ng_ref[:, vs] * (rg * jax.nn.sigmoid(rg)))
        o_ref[b] = jnp.concatenate(outs, axis=-1).astype(o_ref.dtype)


def gla_mixer(q, k, v, r, lr, wg2_pad, bg2, norm_g):
    B, S, _ = q.shape
    C = GLA_CHUNK
    nc = S // C
    blk = lambda w: pl.BlockSpec((B, C, w), lambda c: (0, c, 0))
    full = lambda a: pl.BlockSpec(a.shape, lambda c: (0,) * a.ndim)
    return pl.pallas_call(
        functools.partial(_gla_kernel, batch=B),
        grid=(nc,),
        in_specs=[blk(GLA_QK_WIDTH), blk(GLA_QK_WIDTH), blk(GLA_V_WIDTH), blk(GLA_V_WIDTH), blk(LANE),
                  full(wg2_pad), full(bg2), full(norm_g)],
        out_specs=blk(GLA_V_WIDTH),
        out_shape=jax.ShapeDtypeStruct((B, S, GLA_V_WIDTH), BF16),
        scratch_shapes=[pltpu.VMEM((B, GLA_HEADS, GLA_DV, GLA_DK), F32)],
        compiler_params=_cparams(("arbitrary",)),
        name="gla",
    )(q, k, v, r, lr, wg2_pad, bg2, norm_g)


def _deepnorm_ln(x, y, gate, ln_g, ln_b):
    z = ALPHA_DN * x + (1.0 + gate) * y
    mu = jnp.mean(z, axis=-1, keepdims=True)
    zc = z - mu
    var = jnp.mean(zc * zc, axis=-1, keepdims=True)
    return zc * lax.rsqrt(var + LN_EPS) * ln_g + ln_b


def _out_ln_kernel(*refs, n_parts, matmul, has_next):
    y_refs = refs[:n_parts]
    pos = n_parts
    if matmul:
        w_refs = refs[pos:pos + n_parts]
        pos += n_parts
    x_ref, g_ref, lng_ref, lnb_ref = refs[pos:pos + 4]
    pos += 4
    if has_next:
        sc_ref, sh_ref = refs[pos:pos + 2]
        pos += 2
    o_ref = refs[pos]
    y = None
    for i in range(n_parts):
        part = jnp.dot(y_refs[i][0], w_refs[i][...], preferred_element_type=F32) if matmul else y_refs[i][0]
        y = part if y is None else y + part
    xn = _deepnorm_ln(x_ref[0], y, g_ref[0], lng_ref[...], lnb_ref[...])
    o_ref[0] = xn
    if has_next:
        refs[pos + 1][0] = (xn * (1.0 + sc_ref[0]) + sh_ref[0]).astype(refs[pos + 1].dtype)


def out_ln(ys, ws, x, gate, ln_g, ln_b, nxt, h_dtype=BF16, tm=512):
    B, S, D = x.shape
    tile = lambda w: pl.BlockSpec((1, tm, w), lambda b, i: (b, i, 0))
    vec = pl.BlockSpec((1, 1, D), lambda b, i: (b, 0, 0))
    par = pl.BlockSpec((1, D), lambda b, i: (0, 0))
    args = list(ys)
    in_specs = [tile(y.shape[-1]) for y in ys]
    if ws is not None:
        args += list(ws)
        in_specs += [pl.BlockSpec(w.shape, lambda b, i: (0, 0)) for w in ws]
    args += [x, gate, ln_g, ln_b]
    in_specs += [tile(D), vec, par, par]
    out_shape = [jax.ShapeDtypeStruct((B, S, D), F32)]
    out_specs = [tile(D)]
    if nxt is not None:
        args += list(nxt)
        in_specs += [vec, vec]
        out_shape.append(jax.ShapeDtypeStruct((B, S, D), h_dtype))
        out_specs.append(tile(D))
    res = pl.pallas_call(
        functools.partial(_out_ln_kernel, n_parts=len(ys), matmul=ws is not None, has_next=nxt is not None),
        grid=(B, S // tm),
        in_specs=in_specs,
        out_specs=out_specs,
        out_shape=out_shape,
        compiler_params=_cparams(("arbitrary", "arbitrary")),
        name="out_ln",
    )(*args)
    return (res[0], res[1]) if nxt is not None else (res[0], None)


def _ffn_ln_kernel(h_ref, w1_ref, w3_ref, w2_ref, x_ref, g_ref, lng_ref, lnb_ref, sc_ref, sh_ref,
                   o_ref, hn_ref, acc_sc):
    j = pl.program_id(2)

    @pl.when(j == 0)
    def _():
        acc_sc[...] = jnp.zeros(acc_sc.shape, F32)

    h = h_ref[0]
    a = jnp.dot(h, w1_ref[...], preferred_element_type=F32)
    b = jnp.dot(h, w3_ref[...], preferred_element_type=F32)
    u = (a * jax.nn.sigmoid(a)) * b
    acc_sc[...] += jnp.dot(u.astype(BF16), w2_ref[...], preferred_element_type=F32)

    @pl.when(j == pl.num_programs(2) - 1)
    def _():
        xn = _deepnorm_ln(x_ref[0], acc_sc[...], g_ref[0], lng_ref[...], lnb_ref[...])
        o_ref[0] = xn
        hn_ref[0] = (xn * (1.0 + sc_ref[0]) + sh_ref[0]).astype(BF16)


def ffn_ln(h, w1, w3, w2, x, gate, ln_g, ln_b, nxt, tm=1024, tf=256):
    B, S, D = x.shape
    F = w1.shape[1]
    tile = pl.BlockSpec((1, tm, D), lambda b, i, j: (b, i, 0))
    vec = pl.BlockSpec((1, 1, D), lambda b, i, j: (b, 0, 0))
    par = pl.BlockSpec((1, D), lambda b, i, j: (0, 0))
    return pl.pallas_call(
        _ffn_ln_kernel,
        grid=(B, S // tm, F // tf),
        in_specs=[tile,
                  pl.BlockSpec((D, tf), lambda b, i, j: (0, j)),
                  pl.BlockSpec((D, tf), lambda b, i, j: (0, j)),
                  pl.BlockSpec((tf, D), lambda b, i, j: (j, 0)),
                  tile, vec, par, par, vec, vec],
        out_specs=[tile, tile],
        out_shape=[jax.ShapeDtypeStruct((B, S, D), F32), jax.ShapeDtypeStruct((B, S, D), BF16)],
        scratch_shapes=[pltpu.VMEM((tm, D), F32)],
        compiler_params=_cparams(("arbitrary", "arbitrary", "arbitrary"), 48),
        name="ffn_ln",
    )(h, w1, w3, w2, x, gate, ln_g, ln_b, nxt[0], nxt[1])


def _compress_kernel(x_ref, pos_ref, w1_ref, w2_ref, o_ref):
    dh = NSA_HEAD_DIM
    x = x_ref[0, 0]
    uv = jnp.dot(x, w1_ref[...], preferred_element_type=F32)
    pc = jnp.dot(pos_ref[...], w1_ref[...], preferred_element_type=F32)
    c = pc[0:1, :dh] + pc[1:2, dh:]
    n = uv.shape[0]
    nxt = pltpu.roll(uv[:, dh:], n - 1, 0)
    hid = uv[:, :dh] + nxt + c
    act = hid * jax.nn.sigmoid(hid)
    o_ref[0, 0] = jnp.dot(act.astype(BF16), w2_ref[...], preferred_element_type=F32).astype(o_ref.dtype)


def nsa_compress(kv, pos2, w1cat, w2):
    B, G, S, dh = kv.shape
    seg = NSA_CMP_STRIDE
    n = S // seg
    x = kv.reshape(B, G, n, seg * dh)
    return pl.pallas_call(
        _compress_kernel,
        grid=(B, G),
        in_specs=[
            pl.BlockSpec((1, 1, n, seg * dh), lambda b, g: (b, g, 0, 0)),
            pl.BlockSpec(pos2.shape, lambda b, g: (0, 0)),
            pl.BlockSpec(w1cat.shape, lambda b, g: (0, 0)),
            pl.BlockSpec(w2.shape, lambda b, g: (0, 0)),
        ],
        out_specs=pl.BlockSpec((1, 1, n, dh), lambda b, g: (b, g, 0, 0)),
        out_shape=jax.ShapeDtypeStruct((B, G, n, dh), BF16),
        compiler_params=_cparams(("arbitrary", "arbitrary")),
        name="nsa_compress",
    )(x, pos2, w1cat, w2)


def _nsa_kernel(q_ref, kc_ref, vc_ref, ks_ref, vs_ref, kw_ref, vw_ref, gt_ref, cb_ref, tb_ref, o_ref, sel_sc,
                *, S, TQ):
    dh = NSA_HEAD_DIM
    HPG = NSA_HPG
    LS = NSA_SLC_BLOCK
    TK = 4 * LS
    R = HPG * TQ
    n_cmp = (S - NSA_CMP_LEN) // NSA_CMP_STRIDE + 1
    n_cp = S // NSA_CMP_STRIDE
    n_slc = S // LS
    n_top = min(NSA_SLC_TOPN, n_slc)
    qi = pl.program_id(2)
    q0 = qi * TQ

    q4 = jnp.concatenate([q_ref[0, :, h * dh:(h + 1) * dh] for h in range(HPG)], axis=0)
    lanes4 = lambda a: jnp.concatenate([a] * HPG, axis=1)

    s = lax.dot_general(kc_ref[0, 0], q4, _NT, preferred_element_type=F32)
    s = s + jnp.concatenate([cb_ref[h] for h in range(HPG)], axis=1)
    n_row = lax.broadcasted_iota(jnp.int32, (n_cp, TQ), 0)
    t_q = q0 + lax.broadcasted_iota(jnp.int32, (n_cp, TQ), 1)
    valid_c = lanes4((n_row * NSA_CMP_STRIDE + (NSA_CMP_LEN - 1) <= t_q) & (n_row < n_cmp))
    s = jnp.where(valid_c, s, NEG_INF)
    m = jnp.max(s, axis=0, keepdims=True)
    m = jnp.where(m == NEG_INF, 0.0, m)
    e = jnp.exp(s - m)
    p_c = e / jnp.maximum(jnp.sum(e, axis=0, keepdims=True), jnp.finfo(F32).tiny)
    o_c = lax.dot_general(vc_ref[0, 0], p_c.astype(BF16), _TN, preferred_element_type=F32)

    p_sum = p_c[:, 0:TQ]
    for h in range(1, HPG):
        p_sum = p_sum + p_c[:, h * TQ:(h + 1) * TQ]
    om = lax.broadcasted_iota(jnp.int32, (n_slc, n_cp), 0)
    c_lo = lax.broadcasted_iota(jnp.int32, (n_slc, n_cp), 1) * NSA_CMP_STRIDE
    ov = jnp.maximum(jnp.minimum(c_lo + NSA_CMP_LEN, om * LS + LS) - jnp.maximum(c_lo, om * LS), 0)
    overlap_t = ov.astype(F32) / NSA_CMP_LEN
    imp_t = jnp.dot(overlap_t, p_sum, precision=lax.Precision.HIGHEST, preferred_element_type=F32)
    blk = lax.broadcasted_iota(jnp.int32, (n_slc, TQ), 0)
    tq_lane = q0 + lax.broadcasted_iota(jnp.int32, (n_slc, TQ), 1)
    tb = lax.shift_right_logical(tq_lane, LS.bit_length() - 1)
    forced = (blk == 0) | (blk == tb) | (blk == tb - 1)
    key = jnp.where(blk > tb, NEG_INF, jnp.where(forced, jnp.inf, imp_t))
    rank = jnp.zeros((n_slc, TQ), jnp.int32)
    for mm in range(n_slc):
        km = key[mm:mm + 1, :]
        beats = (km > key) | ((km == key) & (mm < blk))
        rank = rank + beats.astype(jnp.int32)
    sel_sc[...] = jnp.where((rank < n_top) & (blk <= tb), 1.0, 0.0)

    def toep_tile(k0, n_keys):
        return jnp.concatenate(
            [jnp.concatenate([tb_ref[h, :, pl.ds(_toep_col(q0, k0 + r * BIAS_ROWS), TQ)] for h in range(HPG)], axis=1)
             for r in range(n_keys // BIAS_ROWS)], axis=0)

    init = (jnp.full((1, R), NEG_INF, F32), jnp.zeros((1, R), F32), jnp.zeros((dh, R), F32))

    krow_s = lax.broadcasted_iota(jnp.int32, (TK, TQ), 0)
    tq_s = q0 + lax.broadcasted_iota(jnp.int32, (TK, TQ), 1)

    def slc_body(kt, carry):
        k0 = pl.multiple_of(kt * TK, TK)
        flags = jnp.concatenate(
            [jnp.broadcast_to(sel_sc[pl.ds(kt * (TK // LS) + j, 1), :], (LS, TQ)) for j in range(TK // LS)], axis=0)
        mask = lanes4((flags > 0.5) & (k0 + krow_s <= tq_s))
        s = lax.dot_general(ks_ref[0, 0, pl.ds(k0, TK), :], q4, _NT, preferred_element_type=F32) + toep_tile(k0, TK)
        return _softmax_step(carry, jnp.where(mask, s, NEG_INF), vs_ref[0, 0, pl.ds(k0, TK), :])

    n_kt = (q0 + TQ - 1) // TK + 1
    _, l_s, acc_s = lax.fori_loop(0, n_kt, slc_body, init)
    o_s = acc_s / l_s

    krow_w = lax.broadcasted_iota(jnp.int32, (TQ, TQ), 0)
    tq_w = q0 + lax.broadcasted_iota(jnp.int32, (TQ, TQ), 1)

    def win_body(j, carry):
        k0 = pl.multiple_of(q0 - j * TQ, TQ)
        dist = tq_w - (k0 + krow_w)
        mask = lanes4((dist >= 0) & (dist < NSA_WINDOW))
        s = lax.dot_general(kw_ref[0, 0, pl.ds(k0, TQ), :], q4, _NT, preferred_element_type=F32) + toep_tile(k0, TQ)
        return _softmax_step(carry, jnp.where(mask, s, NEG_INF), vw_ref[0, 0, pl.ds(k0, TQ), :])

    n_wt = jnp.minimum(qi, NSA_WINDOW // TQ) + 1
    _, l_w, acc_w = lax.fori_loop(0, n_wt, win_body, init)
    o_w = acc_w / l_w

    g_t = jax.nn.sigmoid(gt_ref[0, 0]).T
    outs = []
    for h in range(HPG):
        cols = slice(h * TQ, (h + 1) * TQ)
        o = (g_t[h:h + 1] * o_c[:, cols] + g_t[HPG + h:HPG + h + 1] * o_s[:, cols]
             + g_t[2 * HPG + h:2 * HPG + h + 1] * o_w[:, cols])
        outs.append(o.T)
    o_ref[0] = jnp.concatenate(outs, axis=1).astype(o_ref.dtype)


def nsa_attention(q, kc, vc, ks, vs, kw, vw, gates, cmpb, toep, TQ=256):
    B, S, _ = q.shape
    G, HPG, dh = NSA_KV_GROUPS, NSA_HPG, NSA_HEAD_DIM
    n_cp = S // NSA_CMP_STRIDE
    full_kv = pl.BlockSpec((1, 1, S, dh), lambda b, g, i: (b, g, 0, 0))
    cmp_kv = pl.BlockSpec((1, 1, n_cp, dh), lambda b, g, i: (b, g, 0, 0))
    return pl.pallas_call(
        functools.partial(_nsa_kernel, S=S, TQ=TQ),
        grid=(B, G, S // TQ),
        in_specs=[
            pl.BlockSpec((1, TQ, HPG * dh), lambda b, g, i: (b, i, g)),
            cmp_kv, cmp_kv, full_kv, full_kv, full_kv, full_kv,
            pl.BlockSpec((1, 1, TQ, LANE), lambda b, g, i: (b, g, i, 0)),
            pl.BlockSpec((HPG, n_cp, TQ), lambda b, g, i: (g, 0, i)),
            pl.BlockSpec((HPG, BIAS_ROWS, _toep_width(S)), lambda b, g, i: (g, 0, 0)),
        ],
        out_specs=pl.BlockSpec((1, TQ, HPG * dh), lambda b, g, i: (b, i, g)),
        out_shape=jax.ShapeDtypeStruct((B, S, NSA_Q_WIDTH), BF16),
        scratch_shapes=[pltpu.VMEM((S // NSA_SLC_BLOCK, TQ), F32)],
        compiler_params=_cparams(("arbitrary", "arbitrary", "arbitrary"), 48),
        name="nsa_attn",
    )(q, kc, vc, ks, vs, kw, vw, gates, cmpb, toep)


def _router_kernel(h_ref, wr_ref, e_ref, w_ref, cnt_ref):
    @pl.when(pl.program_id(0) == 0)
    def _():
        cnt_ref[...] = jnp.zeros(cnt_ref.shape, F32)

    logits = jnp.dot(h_ref[...], wr_ref[...], precision=lax.Precision.HIGHEST, preferred_element_type=F32)
    tm = logits.shape[0]
    lane = lax.broadcasted_iota(jnp.int32, logits.shape, 1)
    logits = jnp.where(lane < N_EXPERTS, logits, NEG_INF)
    l1 = jnp.max(logits, axis=-1, keepdims=True)
    i1 = jnp.min(jnp.where(logits == l1, lane, LANE), axis=-1, keepdims=True)
    rest = jnp.where(lane == i1, NEG_INF, logits)
    l2 = jnp.max(rest, axis=-1, keepdims=True)
    i2 = jnp.min(jnp.where(rest == l2, lane, LANE), axis=-1, keepdims=True)
    e2 = jnp.exp(l2 - l1)
    den = 1.0 + e2
    w_ref[...] = jnp.where(lane == 0, 1.0 / den, jnp.where(lane == 1, e2 / den, 0.0))

    oh1 = lane == i1
    oh2 = lane == i2
    both = jnp.where(oh1 | oh2, 1.0, 0.0)
    earlier = lax.broadcasted_iota(jnp.int32, (tm, tm), 1) < lax.broadcasted_iota(jnp.int32, (tm, tm), 0)
    prefix = jnp.dot(jnp.where(earlier, 1.0, 0.0).astype(BF16), both.astype(BF16),
                     preferred_element_type=F32)
    seen = prefix + cnt_ref[0:1, :]
    p1 = jnp.sum(jnp.where(oh1, seen, 0.0), axis=-1, keepdims=True).astype(jnp.int32)
    p2 = jnp.sum(jnp.where(oh2, seen, 0.0), axis=-1, keepdims=True).astype(jnp.int32)
    e_ref[...] = jnp.where(lane == 0, i1, jnp.where(lane == 1, i2, jnp.where(lane == 2, p1,
                                                                              jnp.where(lane == 3, p2, 0))))
    cnt_ref[...] = cnt_ref[...] + jnp.sum(both, axis=0, keepdims=True)


def moe_route(h2, wr_pad, tm=512):
    N, D = h2.shape
    return pl.pallas_call(
        _router_kernel,
        grid=(N // tm,),
        in_specs=[pl.BlockSpec((tm, D), lambda i: (i, 0)), pl.BlockSpec(wr_pad.shape, lambda i: (0, 0))],
        out_specs=[pl.BlockSpec((tm, LANE), lambda i: (i, 0)), pl.BlockSpec((tm, LANE), lambda i: (i, 0)),
                   pl.BlockSpec((8, LANE), lambda i: (0, 0))],
        out_shape=[jax.ShapeDtypeStruct((N, LANE), jnp.int32), jax.ShapeDtypeStruct((N, LANE), F32),
                   jax.ShapeDtypeStruct((8, LANE), F32)],
        compiler_params=_cparams(("arbitrary",)),
        name="moe_router",
    )(h2, wr_pad)


def _dispatch_kernel(dest_ref, h_ref, xs_in_ref, xs_ref, sem):
    del xs_in_ref
    tm = h_ref.shape[0]
    base = pl.program_id(0) * tm * TOP_K

    def issue(r, carry):
        for k in range(TOP_K):
            d = dest_ref[base + r * TOP_K + k]
            pltpu.make_async_copy(h_ref.at[r], xs_ref.at[d], sem).start()
        return carry

    lax.fori_loop(0, tm, issue, 0)
    for k in range(TOP_K):
        pltpu.make_async_copy(h_ref, xs_ref.at[pl.ds(0, tm)], sem).wait()


def moe_dispatch_rows(h2, dest_flat, n_rows, tm=512):
    N, D = h2.shape
    xs0 = jnp.zeros((n_rows, D), h2.dtype)
    grid_spec = pltpu.PrefetchScalarGridSpec(
        num_scalar_prefetch=1,
        grid=(N // tm,),
        in_specs=[pl.BlockSpec((tm, D), lambda i, d: (i, 0)), pl.BlockSpec(memory_space=pl.ANY)],
        out_specs=pl.BlockSpec(memory_space=pl.ANY),
        scratch_shapes=[pltpu.SemaphoreType.DMA],
    )
    return pl.pallas_call(
        _dispatch_kernel,
        grid_spec=grid_spec,
        out_shape=jax.ShapeDtypeStruct((n_rows, D), h2.dtype),
        input_output_aliases={2: 0},
        compiler_params=_cparams(("arbitrary",)),
        name="moe_dispatch",
    )(dest_flat, h2, xs0)


def _combine_ln_kernel(dest_ref, w_ref, ys_ref, x_ref, g_ref, lng_ref, lnb_ref, *rest, has_next):
    if has_next:
        sc_ref, sh_ref, o_ref, hn_ref, buf, sem = rest
    else:
        o_ref, buf, sem = rest
    tm = x_ref.shape[0]
    base = pl.program_id(0) * tm * TOP_K

    def issue(r, carry):
        for k in range(TOP_K):
            d = dest_ref[base + r * TOP_K + k]
            pltpu.make_async_copy(ys_ref.at[d], buf.at[k, r], sem).start()
        return carry

    lax.fori_loop(0, tm, issue, 0)
    for k in range(TOP_K):
        pltpu.make_async_copy(ys_ref.at[pl.ds(0, tm)], buf.at[k], sem).wait()
    w = w_ref[...]
    y = w[:, 0:1] * buf[0] + w[:, 1:2] * buf[1]
    xn = _deepnorm_ln(x_ref[...], y, g_ref[0], lng_ref[...], lnb_ref[...])
    o_ref[...] = xn
    if has_next:
        hn_ref[...] = (xn * (1.0 + sc_ref[0]) + sh_ref[0]).astype(hn_ref.dtype)


def moe_combine_ln(ys, dest_flat, w_top, x, gate, ln_g, ln_b, nxt, tm=512):
    B, S, D = x.shape
    N = B * S
    per_b = S // tm
    tile = pl.BlockSpec((tm, D), lambda i, d: (i, 0))
    vec = pl.BlockSpec((1, 1, D), lambda i, d: (i // per_b, 0, 0))
    par = pl.BlockSpec((1, D), lambda i, d: (0, 0))
    in_specs = [pl.BlockSpec((tm, LANE), lambda i, d: (i, 0)), pl.BlockSpec(memory_space=pl.ANY), tile, vec, par, par]
    args = [w_top, ys, x.reshape(N, D), gate, ln_g, ln_b]
    out_shape = [jax.ShapeDtypeStruct((N, D), F32)]
    out_specs = [tile]
    if nxt is not None:
        in_specs += [vec, vec]
        args += list(nxt)
        out_shape.append(jax.ShapeDtypeStruct((N, D), BF16))
        out_specs.append(tile)
    grid_spec = pltpu.PrefetchScalarGridSpec(
        num_scalar_prefetch=1,
        grid=(N // tm,),
        in_specs=in_specs,
        out_specs=out_specs,
        scratch_shapes=[pltpu.VMEM((TOP_K, tm, D), F32), pltpu.SemaphoreType.DMA],
    )
    res = pl.pallas_call(
        functools.partial(_combine_ln_kernel, has_next=nxt is not None),
        grid_spec=grid_spec,
        out_shape=out_shape,
        compiler_params=_cparams(("arbitrary",)),
        name="moe_combine_ln",
    )(dest_flat, *args)
    x_new = res[0].reshape(B, S, D)
    return (x_new, res[1].reshape(B, S, D)) if nxt is not None else (x_new, None)


def _moe_ffn_kernel(be_ref, nu_ref, x_ref, w1_ref, w3_ref, w2_ref, o_ref, acc_sc, xb_sc):
    i = pl.program_id(0)
    j = pl.program_id(1)
    last = pl.num_programs(1) - 1
    used = i < nu_ref[0]

    @pl.when(j == 0)
    def _():
        acc_sc[...] = jnp.zeros(acc_sc.shape, F32)
        xb_sc[...] = x_ref[...].astype(BF16)

    @pl.when(used)
    def _():
        x = xb_sc[...]
        a = jnp.dot(x, w1_ref[0], preferred_element_type=F32)
        b = jnp.dot(x, w3_ref[0], preferred_element_type=F32)
        u = (a * jax.nn.sigmoid(a)) * b
        acc_sc[...] += jnp.dot(u.astype(BF16), w2_ref[0], preferred_element_type=F32)

    @pl.when(j == last)
    def _():
        o_ref[...] = acc_sc[...]


def moe_ffn(xs, blk_e, n_used, w1, w3, w2, tf=256):
    n_rows, D = xs.shape
    RB = MOE_ROW_BLOCK
    n_blocks = n_rows // RB
    F = w1.shape[2]
    nf = F // tf

    def wcol(i, j, be, nu):
        return (be[i], 0, jnp.where(i < nu[0], j, nf - 1))

    def wrow(i, j, be, nu):
        return (be[i], jnp.where(i < nu[0], j, nf - 1), 0)

    grid_spec = pltpu.PrefetchScalarGridSpec(
        num_scalar_prefetch=2,
        grid=(n_blocks, nf),
        in_specs=[
            pl.BlockSpec((RB, D), lambda i, j, be, nu: (i, 0)),
            pl.BlockSpec((1, D, tf), wcol),
            pl.BlockSpec((1, D, tf), wcol),
            pl.BlockSpec((1, tf, D), wrow),
        ],
        out_specs=pl.BlockSpec((RB, D), lambda i, j, be, nu: (i, 0)),
        scratch_shapes=[pltpu.VMEM((RB, D), F32), pltpu.VMEM((RB, D), BF16)],
    )
    return pl.pallas_call(
        _moe_ffn_kernel,
        grid_spec=grid_spec,
        out_shape=jax.ShapeDtypeStruct((n_rows, D), F32),
        compiler_params=_cparams(("arbitrary", "arbitrary")),
        name="moe_ffn",
    )(blk_e, n_used, xs, w1, w3, w2)


def moe_layout(e_pos, counts_f, n_tok):
    RB = MOE_ROW_BLOCK
    n_assign = n_tok * TOP_K
    n_blocks = (n_assign + N_EXPERTS * (RB - 1) + RB - 1) // RB
    counts = counts_f[0, :N_EXPERTS].astype(jnp.int32)
    padded = (counts + RB - 1) // RB * RB
    pad_end = jnp.cumsum(padded)
    pad_start = pad_end - padded
    e = e_pos[:, 0:TOP_K]
    pos = e_pos[:, TOP_K:2 * TOP_K]
    start = jnp.zeros_like(e)
    for x in range(N_EXPERTS):
        start = jnp.where(e == x, pad_start[x], start)
    dest_flat = (start + pos).reshape(-1)
    blk_start = jnp.arange(n_blocks, dtype=jnp.int32) * RB
    blk_e = jnp.minimum(jnp.sum((blk_start[:, None] >= pad_end[None, :]).astype(jnp.int32), axis=1), N_EXPERTS - 1)
    n_used = (pad_end[-1] // RB).astype(jnp.int32).reshape(1)
    return dest_flat, blk_e.astype(jnp.int32), n_used, n_blocks * RB


def _pad_cols(w, width):
    return jnp.pad(w, ((0, 0), (0, width - w.shape[1])))


def kernel(x, c, rel_bias, ada_w, ada_b, ln_g, ln_b, ab_w_in, gla_wg2, gla_bg2, gla_norm_g, ab_w_out,
           ffn_w1, ffn_w3, ffn_w2, nsa_w_in, nsa_cmp_pos, nsa_cmp_w1k, nsa_cmp_w2k, nsa_cmp_w1v, nsa_cmp_w2v,
           nsa_w_out, moe_router, moe_w1, moe_w3, moe_w2):
    B, S, D = x.shape
    n_tok = B * S
    mod = adaln_mod(c, ada_w, ada_b)
    mods = [[m.reshape(B, 1, D) for m in jnp.split(mod[i], 6, axis=-1)] for i in range(DEPTH)]
    toep, cmpb = build_bias_tables(rel_bias, S)

    sh1, sc1 = mods[0][0], mods[0][1]
    h = modulate(x, sc1, sh1)
    for i in range(DEPTH):
        j = i // 2
        _, _, g1, sh2, sc2, g2 = mods[i]
        nxt2 = (mods[i + 1][1], mods[i + 1][0]) if i + 1 < DEPTH else None
        lng = ln_g[i].reshape(2, 1, D)
        lnb = ln_b[i].reshape(2, 1, D)
        if i % 2 == 0:
            w = ab_w_in[j]
            cuts = np.cumsum((MOBA_WIDTH, MOBA_WIDTH, MOBA_WIDTH, GLA_QK_WIDTH, GLA_QK_WIDTH, GLA_V_WIDTH,
                              GLA_V_WIDTH))
            parts = jnp.split(w, cuts.tolist(), axis=1)
            parts[-1] = _pad_cols(parts[-1], LANE)
            w_cat = jnp.concatenate(parts, axis=1).astype(BF16)
            widths = [p.shape[1] for p in parts]
            offs = np.concatenate([[0], np.cumsum(widths)[:-1]]).tolist()
            scales = [MOBA_HEAD_DIM ** -0.5] + [1.0] * 7
            segs = tuple((o, wd, "flat", s) for o, wd, s in zip(offs, widths, scales))
            dts = [BF16, BF16, BF16, F32, F32, F32, F32, F32]
            mq, mk, mv, gq, gk, gv, gr, glr = project(h, w_cat, segs, dts)
            y_moba = moba_attention(mq, mk, mv, toep)
            wg2_pad = jnp.pad(gla_wg2[j], ((0, LANE - GLA_GATE_RANK), (0, 0)))
            y_gla = gla_mixer(gq, gk, gv, gr, glr, wg2_pad, gla_bg2[j].reshape(1, -1), gla_norm_g[j].reshape(1, -1))
            w_out = ab_w_out[j].astype(BF16)
            x, h = out_ln([y_moba, y_gla], [w_out[:MOBA_WIDTH], w_out[MOBA_WIDTH:]], x, g1, lng[0], lnb[0],
                          (sc2, sh2))
            x, h = ffn_ln(h, ffn_w1[j].astype(BF16), ffn_w3[j].astype(BF16), ffn_w2[j].astype(BF16), x, g2,
                          lng[1], lnb[1], nxt2)
        else:
            G, HPG, dh = NSA_KV_GROUPS, NSA_HPG, NSA_HEAD_DIM
            w = nsa_w_in[j]
            wq = w[:, :NSA_Q_WIDTH]
            wkv = w[:, NSA_Q_WIDTH:NSA_Q_WIDTH + 6 * NSA_KV_WIDTH]
            wg = w[:, NSA_Q_WIDTH + 6 * NSA_KV_WIDTH:].reshape(D, G, HPG, 3)
            wg = _pad_cols(wg.transpose(0, 1, 3, 2).reshape(D * G, 3 * HPG), LANE).reshape(D, G * LANE)
            w_cat = jnp.concatenate([wq, wkv, wg], axis=1).astype(BF16)
            segs = [(0, NSA_Q_WIDTH, "flat", dh ** -0.5)]
            segs += [(NSA_Q_WIDTH + n * NSA_KV_WIDTH, NSA_KV_WIDTH, "group", 1.0) for n in range(6)]
            segs += [(NSA_Q_WIDTH + 6 * NSA_KV_WIDTH, G * LANE, "group", 1.0)]
            dts = [BF16] * 7 + [F32]
            q, kc, vc, ks, vs, kw, vw, gates = project(h, w_cat, tuple(segs), dts)
            half = NSA_CMP_STRIDE * dh
            pos2 = jnp.pad(nsa_cmp_pos[j].reshape(2, half), ((0, 6), (0, 0))).astype(BF16)
            w1k = jnp.concatenate([nsa_cmp_w1k[j][:half], nsa_cmp_w1k[j][half:]], axis=1).astype(BF16)
            w1v = jnp.concatenate([nsa_cmp_w1v[j][:half], nsa_cmp_w1v[j][half:]], axis=1).astype(BF16)
            k_cmp = nsa_compress(kc, pos2, w1k, nsa_cmp_w2k[j].astype(BF16))
            v_cmp = nsa_compress(vc, pos2, w1v, nsa_cmp_w2v[j].astype(BF16))
            y = nsa_attention(q, k_cmp, v_cmp, ks, vs, kw, vw, gates, cmpb, toep)
            x, h = out_ln([y], [nsa_w_out[j].astype(BF16)], x, g1, lng[0], lnb[0], (sc2, sh2), h_dtype=F32)
            h2 = h.reshape(n_tok, D)
            e_pos, w_top, counts = moe_route(h2, _pad_cols(moe_router[j], LANE))
            dest_flat, blk_e, n_used, n_rows = moe_layout(e_pos, counts, n_tok)
            xs = moe_dispatch_rows(h2, dest_flat, n_rows)
            ys = moe_ffn(xs, blk_e, n_used, moe_w1[j].astype(BF16), moe_w3[j].astype(BF16), moe_w2[j].astype(BF16))
            x, h = moe_combine_ln(ys, dest_flat, w_top, x, g2, lng[1], lnb[1], nxt2)
    return x
```

```python
import functools
import math

import jax
import jax.numpy as jnp
import numpy as np
from jax import lax
from jax.experimental import pallas as pl
from jax.experimental.pallas import tpu as pltpu

F32 = jnp.float32
BF16 = jnp.bfloat16
NEG_INF = float("-inf")

DEPTH = 4
ALPHA_DN = (2 * DEPTH) ** 0.25
LN_EPS = 1e-5
RMS_EPS = 1e-6

N_BUCKETS = 32
MAX_DISTANCE = 1024

MOBA_HEADS = 8
MOBA_HEAD_DIM = 64
MOBA_BLOCK = 256
MOBA_TOPK = 3
MOBA_WIDTH = MOBA_HEADS * MOBA_HEAD_DIM

GLA_HEADS = 4
GLA_DK = 64
GLA_DV = 128
GLA_GATE_RANK = 16
GLA_TAU = 16.0
GLA_CHUNK = 64
GLA_QK_WIDTH = GLA_HEADS * GLA_DK
GLA_V_WIDTH = GLA_HEADS * GLA_DV

NSA_HEADS = 8
NSA_KV_GROUPS = 2
NSA_HPG = NSA_HEADS // NSA_KV_GROUPS
NSA_HEAD_DIM = 128
NSA_CMP_LEN = 32
NSA_CMP_STRIDE = 16
NSA_SLC_BLOCK = 64
NSA_SLC_TOPN = 16
NSA_WINDOW = 512
NSA_Q_WIDTH = NSA_HEADS * NSA_HEAD_DIM
NSA_KV_WIDTH = NSA_KV_GROUPS * NSA_HEAD_DIM

N_EXPERTS = 8
TOP_K = 2
MOE_ROW_BLOCK = 512

LOG2E = math.log2(math.e)
LANE = 128
BIAS_ROWS = 128


def _cparams(sem, vmem_mib=40):
    return pltpu.CompilerParams(dimension_semantics=sem, vmem_limit_bytes=vmem_mib << 20)


def _adaln_kernel(c_ref, w_ref, b_ref, o_ref):
    c = c_ref[...]
    cond = c * jax.nn.sigmoid(c)
    o_ref[0] = jnp.dot(cond, w_ref[0], preferred_element_type=F32) + b_ref[0]


def adaln_mod(c, ada_w, ada_b):
    B, D = c.shape
    L, _, N = ada_w.shape
    rows = 8
    cp = jnp.zeros((rows, D), F32).at[:B].set(c)
    tn = N // 4
    out = pl.pallas_call(
        _adaln_kernel,
        grid=(L, N // tn),
        in_specs=[
            pl.BlockSpec((rows, D), lambda l, j: (0, 0)),
            pl.BlockSpec((1, D, tn), lambda l, j: (l, 0, j)),
            pl.BlockSpec((1, 1, tn), lambda l, j: (l, 0, j)),
        ],
        out_specs=pl.BlockSpec((1, rows, tn), lambda l, j: (l, 0, j)),
        out_shape=jax.ShapeDtypeStruct((L, rows, N), F32),
        compiler_params=_cparams(("arbitrary", "arbitrary")),
        name="adaln_mod",
    )(cp, ada_w, ada_b.reshape(L, 1, N))
    return out[:, :B]


def _modulate_kernel(x_ref, sc_ref, sh_ref, o_ref):
    o_ref[0] = (x_ref[0] * (1.0 + sc_ref[0]) + sh_ref[0]).astype(o_ref.dtype)


def modulate(x, sc, sh, tm=512):
    B, S, D = x.shape
    return pl.pallas_call(
        _modulate_kernel,
        grid=(B, S // tm),
        in_specs=[
            pl.BlockSpec((1, tm, D), lambda b, i: (b, i, 0)),
            pl.BlockSpec((1, 1, D), lambda b, i: (b, 0, 0)),
            pl.BlockSpec((1, 1, D), lambda b, i: (b, 0, 0)),
        ],
        out_specs=pl.BlockSpec((1, tm, D), lambda b, i: (b, i, 0)),
        out_shape=jax.ShapeDtypeStruct((B, S, D), BF16),
        compiler_params=_cparams(("arbitrary", "arbitrary")),
        name="modulate",
    )(x, sc, sh)


def _t5_bucket(dist):
    max_exact = N_BUCKETS // 2
    n = jnp.maximum(dist, 0)
    nf = jnp.maximum(n, 1).astype(jnp.float32)
    large = max_exact + (jnp.log(nf / max_exact) / math.log(MAX_DISTANCE / max_exact) * (N_BUCKETS - max_exact)).astype(jnp.int32)
    large = jnp.minimum(large, N_BUCKETS - 1)
    return jnp.where(n < max_exact, n, large)


def _bias_table_kernel(rb_ref, bm_ref, o_ref, *, rows_per_step):
    h = pl.program_id(0)
    n_steps = bm_ref.shape[0] // rows_per_step

    def body(i, carry):
        r0 = pl.multiple_of(i * rows_per_step, rows_per_step)
        bm = bm_ref[pl.ds(r0, rows_per_step), :]
        acc = jnp.zeros(bm.shape, F32)
        for b in range(N_BUCKETS):
            acc = jnp.where(bm == b, rb_ref[b, h] * LOG2E, acc)
        o_ref[0, pl.ds(r0, rows_per_step), :] = acc
        return carry

    lax.fori_loop(0, n_steps, body, 0)


def bias_table(rel_bias, bucket_map, rows_per_step):
    R, W = bucket_map.shape
    H = rel_bias.shape[1]
    return pl.pallas_call(
        functools.partial(_bias_table_kernel, rows_per_step=rows_per_step),
        grid=(H,),
        in_specs=[
            pl.BlockSpec(memory_space=pltpu.SMEM),
            pl.BlockSpec((R, W), lambda h: (0, 0)),
        ],
        out_specs=pl.BlockSpec((1, R, W), lambda h: (h, 0, 0)),
        out_shape=jax.ShapeDtypeStruct((H, R, W), F32),
        compiler_params=_cparams(("arbitrary",)),
        name="bias_table",
    )(rel_bias, bucket_map)


def _toep_width(S):
    return S + BIAS_ROWS


def _toep_col(q0, k0):
    return pl.multiple_of(q0 - k0 + BIAS_ROWS, LANE)


def build_bias_tables(rel_bias, S):
    j = jnp.arange(BIAS_ROWS, dtype=jnp.int32)[:, None]
    c = jnp.arange(_toep_width(S), dtype=jnp.int32)[None, :]
    toep_map = _t5_bucket(c - j - BIAS_ROWS)
    n_cmp_pad = S // NSA_CMP_STRIDE
    t = jnp.arange(S, dtype=jnp.int32)[None, :]
    cmp_end = jnp.arange(n_cmp_pad, dtype=jnp.int32)[:, None] * NSA_CMP_STRIDE + NSA_CMP_LEN - 1
    cmp_map = _t5_bucket(t - cmp_end)
    return bias_table(rel_bias, toep_map, 8), bias_table(rel_bias, cmp_map, 8)


def _proj_kernel(h_ref, w_ref, *o_refs, segs):
    h = h_ref[0]
    for (off, width, kind, scale), o_ref in zip(segs, o_refs):
        acc = jnp.dot(h, w_ref[:, off:off + width], preferred_element_type=F32)
        if scale != 1.0:
            acc = acc * scale
        if kind == "flat":
            o_ref[0] = acc.astype(o_ref.dtype)
        else:
            for g in range(width // LANE):
                o_ref[0, g] = acc[:, g * LANE:(g + 1) * LANE].astype(o_ref.dtype)


def project(h, w_cat, segs, out_dtypes, tm=512):
    B, S, D = h.shape
    out_shapes, out_specs = [], []
    for (off, width, kind, scale), dt in zip(segs, out_dtypes):
        if kind == "flat":
            out_shapes.append(jax.ShapeDtypeStruct((B, S, width), dt))
            out_specs.append(pl.BlockSpec((1, tm, width), lambda b, i: (b, i, 0)))
        else:
            G = width // LANE
            out_shapes.append(jax.ShapeDtypeStruct((B, G, S, LANE), dt))
            out_specs.append(pl.BlockSpec((1, G, tm, LANE), lambda b, i: (b, 0, i, 0)))
    return pl.pallas_call(
        functools.partial(_proj_kernel, segs=segs),
        grid=(B, S // tm),
        in_specs=[
            pl.BlockSpec((1, tm, D), lambda b, i: (b, i, 0)),
            pl.BlockSpec(w_cat.shape, lambda b, i: (0, 0)),
        ],
        out_specs=out_specs,
        out_shape=out_shapes,
        compiler_params=_cparams(("arbitrary", "arbitrary"), 48),
        name="in_proj",
    )(h, w_cat)


_NT = (((1,), (1,)), ((), ()))
_TN = (((0,), (0,)), ((), ()))


def _softmax_step(carry, s, v_tile):
    m, l, acc = carry
    m_new = jnp.maximum(m, jnp.max(s, axis=0, keepdims=True))
    alpha = jnp.exp2(m - m_new)
    p = jnp.exp2(s - m_new)
    l_new = alpha * l + jnp.sum(p, axis=0, keepdims=True)
    acc_new = alpha * acc + lax.dot_general(v_tile, p.astype(BF16), _TN, preferred_element_type=F32)
    return m_new, l_new, acc_new


def _moba_kernel(q_ref, k_ref, v_ref, tb_ref, o_ref, kmean_sc, sel_sc, *, S, heads_per_step):
    L = MOBA_BLOCK
    dh = MOBA_HEAD_DIM
    nb = S // L
    qi = pl.program_id(2)
    q0 = qi * L

    @pl.when(qi == 0)
    def _():
        kf = k_ref[0].astype(F32)
        kmean_sc[...] = jnp.mean(kf.reshape(nb, L, kf.shape[-1]), axis=1)

    q2 = q_ref[0]
    lane_q = lax.broadcasted_iota(jnp.int32, q2.shape, 1)
    lane_m = lax.broadcasted_iota(jnp.int32, kmean_sc.shape, 1)
    blk = lax.broadcasted_iota(jnp.int32, (nb, L), 0)
    qz = []
    for hh in range(heads_per_step):
        in_head = (lane_q >= hh * dh) & (lane_q < (hh + 1) * dh)
        qz.append(jnp.where(in_head, q2, jnp.zeros_like(q2)))
        km = jnp.where((lane_m >= hh * dh) & (lane_m < (hh + 1) * dh), kmean_sc[...], 0.0)
        gate = lax.dot_general(km, q2.astype(F32), _NT, precision=lax.Precision.HIGHEST,
                               preferred_element_type=F32)
        valid = blk < qi
        gm = jnp.where(valid, gate, NEG_INF)
        rank = jnp.zeros((nb, L), jnp.int32)
        for m in range(nb):
            gmm = gm[m:m + 1, :]
            beats = (gmm > gm) | ((gmm == gm) & (m < blk))
            rank = rank + beats.astype(jnp.int32)
        sel_sc[hh] = jnp.where(valid & (rank < MOBA_TOPK), 1.0, 0.0)

    def bias_tile(hh, k0, n_keys):
        return jnp.concatenate([tb_ref[hh, :, pl.ds(_toep_col(q0, k0 + r * BIAS_ROWS), L)]
                                for r in range(n_keys // BIAS_ROWS)], axis=0)

    krow = lax.broadcasted_iota(jnp.int32, (L, L), 0)
    qcol = lax.broadcasted_iota(jnp.int32, (L, L), 1)
    k_own = k_ref[0, pl.ds(pl.multiple_of(q0, L), L), :]
    v_own = v_ref[0, pl.ds(pl.multiple_of(q0, L), L), :]
    carry = []
    for hh in range(heads_per_step):
        s = lax.dot_general(k_own, qz[hh], _NT, preferred_element_type=F32) + bias_tile(hh, q0, L)
        s = jnp.where(krow <= qcol, s, NEG_INF)
        m0 = jnp.max(s, axis=0, keepdims=True)
        p = jnp.exp2(s - m0)
        l0 = jnp.sum(p, axis=0, keepdims=True)
        acc0 = lax.dot_general(v_own, p.astype(BF16), _TN, preferred_element_type=F32)
        carry += [m0, l0, acc0[hh * dh:(hh + 1) * dh]]

    def body(pp, carry):
        k0 = pl.multiple_of(pp * 2 * L, 2 * L)
        kt = k_ref[0, pl.ds(k0, 2 * L), :]
        vt = v_ref[0, pl.ds(k0, 2 * L), :]
        out = []
        for hh in range(heads_per_step):
            m, l, acc = carry[3 * hh:3 * hh + 3]
            s = lax.dot_general(kt, qz[hh], _NT, preferred_element_type=F32) + bias_tile(hh, k0, 2 * L)
            flags = jnp.concatenate(
                [jnp.broadcast_to(sel_sc[hh, pl.ds(2 * pp + j, 1), :], (L, L)) for j in range(2)], axis=0)
            s = jnp.where(flags > 0.5, s, NEG_INF)
            m_new = jnp.maximum(m, jnp.max(s, axis=0, keepdims=True))
            alpha = jnp.exp2(m - m_new)
            p = jnp.exp2(s - m_new)
            l_new = alpha * l + jnp.sum(p, axis=0, keepdims=True)
            pv = lax.dot_general(vt, p.astype(BF16), _TN, preferred_element_type=F32)
            out += [m_new, l_new, alpha * acc + pv[hh * dh:(hh + 1) * dh]]
        return tuple(out)

    carry = lax.fori_loop(0, (qi + 1) // 2, body, tuple(carry))
    o_t = jnp.concatenate([carry[3 * hh + 2] * (1.0 / carry[3 * hh + 1]) for hh in range(heads_per_step)], axis=0)
    o_ref[0] = o_t.T.astype(o_ref.dtype)


def moba_attention(q, k, v, toep):
    B, S, _ = q.shape
    L = MOBA_BLOCK
    hps = 4
    wl = hps * MOBA_HEAD_DIM
    n_hp = MOBA_HEADS // hps
    nb = S // L
    return pl.pallas_call(
        functools.partial(_moba_kernel, S=S, heads_per_step=hps),
        grid=(n_hp, B, nb),
        in_specs=[
            pl.BlockSpec((1, L, wl), lambda hp, b, i: (b, i, hp)),
            pl.BlockSpec((1, S, wl), lambda hp, b, i: (b, 0, hp)),
            pl.BlockSpec((1, S, wl), lambda hp, b, i: (b, 0, hp)),
            pl.BlockSpec((hps, BIAS_ROWS, _toep_width(S)), lambda hp, b, i: (hp, 0, 0)),
        ],
        out_specs=pl.BlockSpec((1, L, wl), lambda hp, b, i: (b, i, hp)),
        out_shape=jax.ShapeDtypeStruct((B, S, MOBA_WIDTH), BF16),
        scratch_shapes=[pltpu.VMEM((nb, wl), F32), pltpu.VMEM((hps, nb, L), F32)],
        compiler_params=_cparams(("arbitrary", "arbitrary", "arbitrary"), 40),
        name="moba_attn",
    )(q, k, v, toep)


def _log_sigmoid(z):
    return jnp.minimum(z, 0.0) - jnp.log1p(jnp.exp(-jnp.abs(z)))


def _gla_kernel(q_ref, k_ref, v_ref, r_ref, lr_ref, wg2_ref, bg2_ref, ng_ref, o_ref, st_sc, *, batch):
    C = GLA_CHUNK
    dk, dv = GLA_DK, GLA_DV

    @pl.when(pl.program_id(0) == 0)
    def _():
        st_sc[...] = jnp.zeros(st_sc.shape, F32)

    row = lax.broadcasted_iota(jnp.int32, (C, C), 0)
    col = lax.broadcasted_iota(jnp.int32, (C, C), 1)
    causal = col <= row
    tri = jnp.where(causal, 1.0, 0.0)
    for b in range(batch):
        z = jnp.dot(lr_ref[b], wg2_ref[...], preferred_element_type=F32) + bg2_ref[...]
        la = _log_sigmoid(z) / GLA_TAU
        bc = jnp.dot(tri, la, precision=lax.Precision.HIGHEST, preferred_element_type=F32)
        b_last = bc[C - 1:C, :]
        qd = q_ref[b] * (dk ** -0.5) * jnp.exp(bc)
        kd = k_ref[b] * jnp.exp(-bc)
        kl = k_ref[b] * jnp.exp(b_last - bc)
        e_last = jnp.exp(b_last)
        v = v_ref[b]
        r = r_ref[b]
        outs = []
        for h in range(GLA_HEADS):
            ks = slice(h * dk, (h + 1) * dk)
            vs = slice(h * dv, (h + 1) * dv)
            qh = qd[:, ks].astype(BF16)
            kh = kd[:, ks].astype(BF16)
            vh = v[:, vs].astype(BF16)
            attn = lax.dot_general(qh, kh, (((1,), (1,)), ((), ())), preferred_element_type=F32)
            attn = jnp.where(causal, attn, 0.0)
            st = st_sc[b, h]
            o = jnp.dot(attn.astype(BF16), vh, preferred_element_type=F32)
            o = o + lax.dot_general(qh, st.astype(BF16), (((1,), (1,)), ((), ())), preferred_element_type=F32)
            upd = lax.dot_general(vh, kl[:, ks].astype(BF16), (((0,), (0,)), ((), ())), preferred_element_type=F32)
            st_sc[b, h] = st * e_last[:, ks] + upd
            o = o * lax.rsqrt(jnp.mean(o * o, axis=-1, keepdims=True) + RMS_EPS)
            rg = r[:, vs]
            outs.append(o * ng_ref[:, vs] * (rg * jax.nn.sigmoid(rg)))
        o_ref[b] = jnp.concatenate(outs, axis=-1).astype(o_ref.dtype)


def gla_mixer(q, k, v, r, lr, wg2_pad, bg2, norm_g):
    B, S, _ = q.shape
    C = GLA_CHUNK
    nc = S // C
    blk = lambda w: pl.BlockSpec((B, C, w), lambda c: (0, c, 0))
    full = lambda a: pl.BlockSpec(a.shape, lambda c: (0,) * a.ndim)
    return pl.pallas_call(
        functools.partial(_gla_kernel, batch=B),
        grid=(nc,),
        in_specs=[blk(GLA_QK_WIDTH), blk(GLA_QK_WIDTH), blk(GLA_V_WIDTH), blk(GLA_V_WIDTH), blk(LANE),
                  full(wg2_pad), full(bg2), full(norm_g)],
        out_specs=blk(GLA_V_WIDTH),
        out_shape=jax.ShapeDtypeStruct((B, S, GLA_V_WIDTH), BF16),
        scratch_shapes=[pltpu.VMEM((B, GLA_HEADS, GLA_DV, GLA_DK), F32)],
        compiler_params=_cparams(("arbitrary",)),
        name="gla",
    )(q, k, v, r, lr, wg2_pad, bg2, norm_g)


def _deepnorm_ln(x, y, gate, ln_g, ln_b):
    z = ALPHA_DN * x + (1.0 + gate) * y
    mu = jnp.mean(z, axis=-1, keepdims=True)
    zc = z - mu
    var = jnp.mean(zc * zc, axis=-1, keepdims=True)
    return zc * lax.rsqrt(var + LN_EPS) * ln_g + ln_b


def _out_ln_kernel(*refs, n_parts, matmul, has_next):
    y_refs = refs[:n_parts]
    pos = n_parts
    if matmul:
        w_refs = refs[pos:pos + n_parts]
        pos += n_parts
    x_ref, g_ref, lng_ref, lnb_ref = refs[pos:pos + 4]
    pos += 4
    if has_next:
        sc_ref, sh_ref = refs[pos:pos + 2]
        pos += 2
    o_ref = refs[pos]
    y = None
    for i in range(n_parts):
        part = jnp.dot(y_refs[i][0], w_refs[i][...], preferred_element_type=F32) if matmul else y_refs[i][0]
        y = part if y is None else y + part
    xn = _deepnorm_ln(x_ref[0], y, g_ref[0], lng_ref[...], lnb_ref[...])
    o_ref[0] = xn
    if has_next:
        refs[pos + 1][0] = (xn * (1.0 + sc_ref[0]) + sh_ref[0]).astype(refs[pos + 1].dtype)


def out_ln(ys, ws, x, gate, ln_g, ln_b, nxt, h_dtype=BF16, tm=512):
    B, S, D = x.shape
    tile = lambda w: pl.BlockSpec((1, tm, w), lambda b, i: (b, i, 0))
    vec = pl.BlockSpec((1, 1, D), lambda b, i: (b, 0, 0))
    par = pl.BlockSpec((1, D), lambda b, i: (0, 0))
    args = list(ys)
    in_specs = [tile(y.shape[-1]) for y in ys]
    if ws is not None:
        args += list(ws)
        in_specs += [pl.BlockSpec(w.shape, lambda b, i: (0, 0)) for w in ws]
    args += [x, gate, ln_g, ln_b]
    in_specs += [tile(D), vec, par, par]
    out_shape = [jax.ShapeDtypeStruct((B, S, D), F32)]
    out_specs = [tile(D)]
    if nxt is not None:
        args += list(nxt)
        in_specs += [vec, vec]
        out_shape.append(jax.ShapeDtypeStruct((B, S, D), h_dtype))
        out_specs.append(tile(D))
    res = pl.pallas_call(
        functools.partial(_out_ln_kernel, n_parts=len(ys), matmul=ws is not None, has_next=nxt is not None),
        grid=(B, S // tm),
        in_specs=in_specs,
        out_specs=out_specs,
        out_shape=out_shape,
        compiler_params=_cparams(("arbitrary", "arbitrary")),
        name="out_ln",
    )(*args)
    return (res[0], res[1]) if nxt is not None else (res[0], None)


def _ffn_ln_kernel(h_ref, w1_ref, w3_ref, w2_ref, x_ref, g_ref, lng_ref, lnb_ref, sc_ref, sh_ref,
                   o_ref, hn_ref, acc_sc):
    j = pl.program_id(2)

    @pl.when(j == 0)
    def _():
        acc_sc[...] = jnp.zeros(acc_sc.shape, F32)

    h = h_ref[0]
    a = jnp.dot(h, w1_ref[...], preferred_element_type=F32)
    b = jnp.dot(h, w3_ref[...], preferred_element_type=F32)
    u = (a * jax.nn.sigmoid(a)) * b
    acc_sc[...] += jnp.dot(u.astype(BF16), w2_ref[...], preferred_element_type=F32)

    @pl.when(j == pl.num_programs(2) - 1)
    def _():
        xn = _deepnorm_ln(x_ref[0], acc_sc[...], g_ref[0], lng_ref[...], lnb_ref[...])
        o_ref[0] = xn
        hn_ref[0] = (xn * (1.0 + sc_ref[0]) + sh_ref[0]).astype(BF16)


def ffn_ln(h, w1, w3, w2, x, gate, ln_g, ln_b, nxt, tm=1024, tf=256):
    B, S, D = x.shape
    F = w1.shape[1]
    tile = pl.BlockSpec((1, tm, D), lambda b, i, j: (b, i, 0))
    vec = pl.BlockSpec((1, 1, D), lambda b, i, j: (b, 0, 0))
    par = pl.BlockSpec((1, D), lambda b, i, j: (0, 0))
    return pl.pallas_call(
        _ffn_ln_kernel,
        grid=(B, S // tm, F // tf),
        in_specs=[tile,
                  pl.BlockSpec((D, tf), lambda b, i, j: (0, j)),
                  pl.BlockSpec((D, tf), lambda b, i, j: (0, j)),
                  pl.BlockSpec((tf, D), lambda b, i, j: (j, 0)),
                  tile, vec, par, par, vec, vec],
        out_specs=[tile, tile],
        out_shape=[jax.ShapeDtypeStruct((B, S, D), F32), jax.ShapeDtypeStruct((B, S, D), BF16)],
        scratch_shapes=[pltpu.VMEM((tm, D), F32)],
        compiler_params=_cparams(("arbitrary", "arbitrary", "arbitrary"), 48),
        name="ffn_ln",
    )(h, w1, w3, w2, x, gate, ln_g, ln_b, nxt[0], nxt[1])


def _compress_kernel(x_ref, pos_ref, w1_ref, w2_ref, o_ref):
    dh = NSA_HEAD_DIM
    x = x_ref[0, 0]
    uv = jnp.dot(x, w1_ref[...], preferred_element_type=F32)
    pc = jnp.dot(pos_ref[...], w1_ref[...], preferred_element_type=F32)
    c = pc[0:1, :dh] + pc[1:2, dh:]
    n = uv.shape[0]
    nxt = pltpu.roll(uv[:, dh:], n - 1, 0)
    hid = uv[:, :dh] + nxt + c
    act = hid * jax.nn.sigmoid(hid)
    o_ref[0, 0] = jnp.dot(act.astype(BF16), w2_ref[...], preferred_element_type=F32).astype(o_ref.dtype)


def nsa_compress(kv, pos2, w1cat, w2):
    B, G, S, dh = kv.shape
    seg = NSA_CMP_STRIDE
    n = S // seg
    x = kv.reshape(B, G, n, seg * dh)
    return pl.pallas_call(
        _compress_kernel,
        grid=(B, G),
        in_specs=[
            pl.BlockSpec((1, 1, n, seg * dh), lambda b, g: (b, g, 0, 0)),
            pl.BlockSpec(pos2.shape, lambda b, g: (0, 0)),
            pl.BlockSpec(w1cat.shape, lambda b, g: (0, 0)),
            pl.BlockSpec(w2.shape, lambda b, g: (0, 0)),
        ],
        out_specs=pl.BlockSpec((1, 1, n, dh), lambda b, g: (b, g, 0, 0)),
        out_shape=jax.ShapeDtypeStruct((B, G, n, dh), BF16),
        compiler_params=_cparams(("arbitrary", "arbitrary")),
        name="nsa_compress",
    )(x, pos2, w1cat, w2)


def _nsa_kernel(q_ref, kc_ref, vc_ref, ks_ref, vs_ref, kw_ref, vw_ref, gt_ref, cb_ref, tb_ref, o_ref, sel_sc,
                *, S, TQ):
    dh = NSA_HEAD_DIM
    HPG = NSA_HPG
    LS = NSA_SLC_BLOCK
    TK = 4 * LS
    R = HPG * TQ
    n_cmp = (S - NSA_CMP_LEN) // NSA_CMP_STRIDE + 1
    n_cp = S // NSA_CMP_STRIDE
    n_slc = S // LS
    n_top = min(NSA_SLC_TOPN, n_slc)
    qi = pl.program_id(2)
    q0 = qi * TQ

    q4 = jnp.concatenate([q_ref[0, :, h * dh:(h + 1) * dh] for h in range(HPG)], axis=0)
    lanes4 = lambda a: jnp.concatenate([a] * HPG, axis=1)

    s = lax.dot_general(kc_ref[0, 0], q4, _NT, preferred_element_type=F32)
    s = s + jnp.concatenate([cb_ref[h] for h in range(HPG)], axis=1)
    n_row = lax.broadcasted_iota(jnp.int32, (n_cp, TQ), 0)
    t_q = q0 + lax.broadcasted_iota(jnp.int32, (n_cp, TQ), 1)
    valid_c = lanes4((n_row * NSA_CMP_STRIDE + (NSA_CMP_LEN - 1) <= t_q) & (n_row < n_cmp))
    s = jnp.where(valid_c, s, NEG_INF)
    m = jnp.max(s, axis=0, keepdims=True)
    m = jnp.where(m == NEG_INF, 0.0, m)
    e = jnp.exp2(s - m)
    p_c = e * (1.0 / jnp.maximum(jnp.sum(e, axis=0, keepdims=True), jnp.finfo(F32).tiny))
    o_c = lax.dot_general(vc_ref[0, 0], p_c.astype(BF16), _TN, preferred_element_type=F32)

    p_sum = p_c[:, 0:TQ]
    for h in range(1, HPG):
        p_sum = p_sum + p_c[:, h * TQ:(h + 1) * TQ]
    om = lax.broadcasted_iota(jnp.int32, (n_slc, n_cp), 0)
    c_lo = lax.broadcasted_iota(jnp.int32, (n_slc, n_cp), 1) * NSA_CMP_STRIDE
    ov = jnp.maximum(jnp.minimum(c_lo + NSA_CMP_LEN, om * LS + LS) - jnp.maximum(c_lo, om * LS), 0)
    overlap_t = ov.astype(F32) / NSA_CMP_LEN
    imp_t = jnp.dot(overlap_t, p_sum, precision=lax.Precision.HIGHEST, preferred_element_type=F32)
    blk = lax.broadcasted_iota(jnp.int32, (n_slc, TQ), 0)
    tq_lane = q0 + lax.broadcasted_iota(jnp.int32, (n_slc, TQ), 1)
    tb = lax.shift_right_logical(tq_lane, LS.bit_length() - 1)
    forced = (blk == 0) | (blk == tb) | (blk == tb - 1)
    key = jnp.where(blk > tb, NEG_INF, jnp.where(forced, jnp.inf, imp_t))
    rank = jnp.zeros((n_slc, TQ), jnp.int32)
    for mm in range(n_slc):
        km = key[mm:mm + 1, :]
        beats = (km > key) | ((km == key) & (mm < blk))
        rank = rank + beats.astype(jnp.int32)
    sel_sc[...] = jnp.where((rank < n_top) & (blk <= tb), 1.0, 0.0)

    def toep_tile(k0, n_keys):
        return jnp.concatenate(
            [jnp.concatenate([tb_ref[h, :, pl.ds(_toep_col(q0, k0 + r * BIAS_ROWS), TQ)] for h in range(HPG)], axis=1)
             for r in range(n_keys // BIAS_ROWS)], axis=0)

    init = (jnp.full((1, R), NEG_INF, F32), jnp.zeros((1, R), F32), jnp.zeros((dh, R), F32))

    krow_s = lax.broadcasted_iota(jnp.int32, (TK, TQ), 0)
    tq_s = q0 + lax.broadcasted_iota(jnp.int32, (TK, TQ), 1)

    def slc_body(kt, carry):
        k0 = pl.multiple_of(kt * TK, TK)
        flags = jnp.concatenate(
            [jnp.broadcast_to(sel_sc[pl.ds(kt * (TK // LS) + j, 1), :], (LS, TQ)) for j in range(TK // LS)], axis=0)
        mask = lanes4((flags > 0.5) & (k0 + krow_s <= tq_s))
        s = lax.dot_general(ks_ref[0, 0, pl.ds(k0, TK), :], q4, _NT, preferred_element_type=F32) + toep_tile(k0, TK)
        return _softmax_step(carry, jnp.where(mask, s, NEG_INF), vs_ref[0, 0, pl.ds(k0, TK), :])

    n_kt = (q0 + TQ - 1) // TK + 1
    _, l_s, acc_s = lax.fori_loop(0, n_kt, slc_body, init)
    o_s = acc_s * (1.0 / l_s)

    krow_w = lax.broadcasted_iota(jnp.int32, (TQ, TQ), 0)
    tq_w = q0 + lax.broadcasted_iota(jnp.int32, (TQ, TQ), 1)

    def win_body(j, carry):
        k0 = pl.multiple_of(q0 - j * TQ, TQ)
        dist = tq_w - (k0 + krow_w)
        mask = lanes4((dist >= 0) & (dist < NSA_WINDOW))
        s = lax.dot_general(kw_ref[0, 0, pl.ds(k0, TQ), :], q4, _NT, preferred_element_type=F32) + toep_tile(k0, TQ)
        return _softmax_step(carry, jnp.where(mask, s, NEG_INF), vw_ref[0, 0, pl.ds(k0, TQ), :])

    n_wt = jnp.minimum(qi, NSA_WINDOW // TQ) + 1
    _, l_w, acc_w = lax.fori_loop(0, n_wt, win_body, init)
    o_w = acc_w * (1.0 / l_w)

    g_t = jax.nn.sigmoid(gt_ref[0, 0]).T
    outs = []
    for h in range(HPG):
        cols = slice(h * TQ, (h + 1) * TQ)
        o = (g_t[h:h + 1] * o_c[:, cols] + g_t[HPG + h:HPG + h + 1] * o_s[:, cols]
             + g_t[2 * HPG + h:2 * HPG + h + 1] * o_w[:, cols])
        outs.append(o.T)
    o_ref[0] = jnp.concatenate(outs, axis=1).astype(o_ref.dtype)


def nsa_attention(q, kc, vc, ks, vs, kw, vw, gates, cmpb, toep, TQ=256):
    B, S, _ = q.shape
    G, HPG, dh = NSA_KV_GROUPS, NSA_HPG, NSA_HEAD_DIM
    n_cp = S // NSA_CMP_STRIDE
    full_kv = pl.BlockSpec((1, 1, S, dh), lambda b, g, i: (b, g, 0, 0))
    cmp_kv = pl.BlockSpec((1, 1, n_cp, dh), lambda b, g, i: (b, g, 0, 0))
    return pl.pallas_call(
        functools.partial(_nsa_kernel, S=S, TQ=TQ),
        grid=(B, G, S // TQ),
        in_specs=[
            pl.BlockSpec((1, TQ, HPG * dh), lambda b, g, i: (b, i, g)),
            cmp_kv, cmp_kv, full_kv, full_kv, full_kv, full_kv,
            pl.BlockSpec((1, 1, TQ, LANE), lambda b, g, i: (b, g, i, 0)),
            pl.BlockSpec((HPG, n_cp, TQ), lambda b, g, i: (g, 0, i)),
            pl.BlockSpec((HPG, BIAS_ROWS, _toep_width(S)), lambda b, g, i: (g, 0, 0)),
        ],
        out_specs=pl.BlockSpec((1, TQ, HPG * dh), lambda b, g, i: (b, i, g)),
        out_shape=jax.ShapeDtypeStruct((B, S, NSA_Q_WIDTH), BF16),
        scratch_shapes=[pltpu.VMEM((S // NSA_SLC_BLOCK, TQ), F32)],
        compiler_params=_cparams(("arbitrary", "arbitrary", "arbitrary"), 48),
        name="nsa_attn",
    )(q, kc, vc, ks, vs, kw, vw, gates, cmpb, toep)


def _router_kernel(h_ref, wr_ref, e_ref, w_ref, cnt_ref):
    @pl.when(pl.program_id(0) == 0)
    def _():
        cnt_ref[...] = jnp.zeros(cnt_ref.shape, F32)

    logits = jnp.dot(h_ref[...], wr_ref[...], precision=lax.Precision.HIGHEST, preferred_element_type=F32)
    tm = logits.shape[0]
    lane = lax.broadcasted_iota(jnp.int32, logits.shape, 1)
    logits = jnp.where(lane < N_EXPERTS, logits, NEG_INF)
    l1 = jnp.max(logits, axis=-1, keepdims=True)
    i1 = jnp.min(jnp.where(logits == l1, lane, LANE), axis=-1, keepdims=True)
    rest = jnp.where(lane == i1, NEG_INF, logits)
    l2 = jnp.max(rest, axis=-1, keepdims=True)
    i2 = jnp.min(jnp.where(rest == l2, lane, LANE), axis=-1, keepdims=True)
    e2 = jnp.exp(l2 - l1)
    den = 1.0 + e2
    w_ref[...] = jnp.where(lane == 0, 1.0 / den, jnp.where(lane == 1, e2 / den, 0.0))

    oh1 = lane == i1
    oh2 = lane == i2
    both = jnp.where(oh1 | oh2, 1.0, 0.0)
    earlier = lax.broadcasted_iota(jnp.int32, (tm, tm), 1) < lax.broadcasted_iota(jnp.int32, (tm, tm), 0)
    prefix = jnp.dot(jnp.where(earlier, 1.0, 0.0).astype(BF16), both.astype(BF16),
                     preferred_element_type=F32)
    seen = prefix + cnt_ref[0:1, :]
    p1 = jnp.sum(jnp.where(oh1, seen, 0.0), axis=-1, keepdims=True).astype(jnp.int32)
    p2 = jnp.sum(jnp.where(oh2, seen, 0.0), axis=-1, keepdims=True).astype(jnp.int32)
    e_ref[...] = jnp.where(lane == 0, i1, jnp.where(lane == 1, i2, jnp.where(lane == 2, p1,
                                                                              jnp.where(lane == 3, p2, 0))))
    cnt_ref[...] = cnt_ref[...] + jnp.sum(both, axis=0, keepdims=True)


def moe_route(h2, wr_pad, tm=512):
    N, D = h2.shape
    return pl.pallas_call(
        _router_kernel,
        grid=(N // tm,),
        in_specs=[pl.BlockSpec((tm, D), lambda i: (i, 0)), pl.BlockSpec(wr_pad.shape, lambda i: (0, 0))],
        out_specs=[pl.BlockSpec((tm, LANE), lambda i: (i, 0)), pl.BlockSpec((tm, LANE), lambda i: (i, 0)),
                   pl.BlockSpec((8, LANE), lambda i: (0, 0))],
        out_shape=[jax.ShapeDtypeStruct((N, LANE), jnp.int32), jax.ShapeDtypeStruct((N, LANE), F32),
                   jax.ShapeDtypeStruct((8, LANE), F32)],
        compiler_params=_cparams(("arbitrary",)),
        name="moe_router",
    )(h2, wr_pad)


def _dispatch_kernel(dest_ref, h_ref, xs_in_ref, xs_ref, sem):
    del xs_in_ref
    tm = h_ref.shape[0]
    base = pl.program_id(0) * tm * TOP_K

    def issue(r, carry):
        for k in range(TOP_K):
            d = dest_ref[base + r * TOP_K + k]
            pltpu.make_async_copy(h_ref.at[r], xs_ref.at[d], sem).start()
        return carry

    lax.fori_loop(0, tm, issue, 0)
    for k in range(TOP_K):
        pltpu.make_async_copy(h_ref, xs_ref.at[pl.ds(0, tm)], sem).wait()


def moe_dispatch_rows(h2, dest_flat, n_rows, tm=512):
    N, D = h2.shape
    xs0 = jnp.zeros((n_rows, D), h2.dtype)
    grid_spec = pltpu.PrefetchScalarGridSpec(
        num_scalar_prefetch=1,
        grid=(N // tm,),
        in_specs=[pl.BlockSpec((tm, D), lambda i, d: (i, 0)), pl.BlockSpec(memory_space=pl.ANY)],
        out_specs=pl.BlockSpec(memory_space=pl.ANY),
        scratch_shapes=[pltpu.SemaphoreType.DMA],
    )
    return pl.pallas_call(
        _dispatch_kernel,
        grid_spec=grid_spec,
        out_shape=jax.ShapeDtypeStruct((n_rows, D), h2.dtype),
        input_output_aliases={2: 0},
        compiler_params=_cparams(("arbitrary",)),
        name="moe_dispatch",
    )(dest_flat, h2, xs0)


def _combine_ln_kernel(dest_ref, w_ref, ys_ref, x_ref, g_ref, lng_ref, lnb_ref, *rest, has_next):
    if has_next:
        sc_ref, sh_ref, o_ref, hn_ref, buf, sem = rest
    else:
        o_ref, buf, sem = rest
    tm = x_ref.shape[0]
    base = pl.program_id(0) * tm * TOP_K

    def issue(r, carry):
        for k in range(TOP_K):
            d = dest_ref[base + r * TOP_K + k]
            pltpu.make_async_copy(ys_ref.at[d], buf.at[k, r], sem).start()
        return carry

    lax.fori_loop(0, tm, issue, 0)
    for k in range(TOP_K):
        pltpu.make_async_copy(ys_ref.at[pl.ds(0, tm)], buf.at[k], sem).wait()
    w = w_ref[...]
    y = w[:, 0:1] * buf[0] + w[:, 1:2] * buf[1]
    xn = _deepnorm_ln(x_ref[...], y, g_ref[0], lng_ref[...], lnb_ref[...])
    o_ref[...] = xn
    if has_next:
        hn_ref[...] = (xn * (1.0 + sc_ref[0]) + sh_ref[0]).astype(hn_ref.dtype)


def moe_combine_ln(ys, dest_flat, w_top, x, gate, ln_g, ln_b, nxt, tm=512):
    B, S, D = x.shape
    N = B * S
    per_b = S // tm
    tile = pl.BlockSpec((tm, D), lambda i, d: (i, 0))
    vec = pl.BlockSpec((1, 1, D), lambda i, d: (i // per_b, 0, 0))
    par = pl.BlockSpec((1, D), lambda i, d: (0, 0))
    in_specs = [pl.BlockSpec((tm, LANE), lambda i, d: (i, 0)), pl.BlockSpec(memory_space=pl.ANY), tile, vec, par, par]
    args = [w_top, ys, x.reshape(N, D), gate, ln_g, ln_b]
    out_shape = [jax.ShapeDtypeStruct((N, D), F32)]
    out_specs = [tile]
    if nxt is not None:
        in_specs += [vec, vec]
        args += list(nxt)
        out_shape.append(jax.ShapeDtypeStruct((N, D), BF16))
        out_specs.append(tile)
    grid_spec = pltpu.PrefetchScalarGridSpec(
        num_scalar_prefetch=1,
        grid=(N // tm,),
        in_specs=in_specs,
        out_specs=out_specs,
        scratch_shapes=[pltpu.VMEM((TOP_K, tm, D), F32), pltpu.SemaphoreType.DMA],
    )
    res = pl.pallas_call(
        functools.partial(_combine_ln_kernel, has_next=nxt is not None),
        grid_spec=grid_spec,
        out_shape=out_shape,
        compiler_params=_cparams(("arbitrary",)),
        name="moe_combine_ln",
    )(dest_flat, *args)
    x_new = res[0].reshape(B, S, D)
    return (x_new, res[1].reshape(B, S, D)) if nxt is not None else (x_new, None)


def _moe_ffn_kernel(be_ref, nu_ref, x_ref, w1_ref, w3_ref, w2_ref, o_ref, acc_sc, xb_sc):
    i = pl.program_id(0)
    j = pl.program_id(1)
    last = pl.num_programs(1) - 1
    used = i < nu_ref[0]

    @pl.when(j == 0)
    def _():
        acc_sc[...] = jnp.zeros(acc_sc.shape, F32)
        xb_sc[...] = x_ref[...].astype(BF16)

    @pl.when(used)
    def _():
        x = xb_sc[...]
        a = jnp.dot(x, w1_ref[0], preferred_element_type=F32)
        b = jnp.dot(x, w3_ref[0], preferred_element_type=F32)
        u = (a * jax.nn.sigmoid(a)) * b
        acc_sc[...] += jnp.dot(u.astype(BF16), w2_ref[0], preferred_element_type=F32)

    @pl.when(j == last)
    def _():
        o_ref[...] = acc_sc[...]


def moe_ffn(xs, blk_e, n_used, w1, w3, w2, tf=512):
    n_rows, D = xs.shape
    RB = MOE_ROW_BLOCK
    n_blocks = n_rows // RB
    F = w1.shape[2]
    nf = F // tf

    def wcol(i, j, be, nu):
        return (be[i], 0, jnp.where(i < nu[0], j, nf - 1))

    def wrow(i, j, be, nu):
        return (be[i], jnp.where(i < nu[0], j, nf - 1), 0)

    grid_spec = pltpu.PrefetchScalarGridSpec(
        num_scalar_prefetch=2,
        grid=(n_blocks, nf),
        in_specs=[
            pl.BlockSpec((RB, D), lambda i, j, be, nu: (i, 0)),
            pl.BlockSpec((1, D, tf), wcol),
            pl.BlockSpec((1, D, tf), wcol),
            pl.BlockSpec((1, tf, D), wrow),
        ],
        out_specs=pl.BlockSpec((RB, D), lambda i, j, be, nu: (i, 0)),
        scratch_shapes=[pltpu.VMEM((RB, D), F32), pltpu.VMEM((RB, D), BF16)],
    )
    return pl.pallas_call(
        _moe_ffn_kernel,
        grid_spec=grid_spec,
        out_shape=jax.ShapeDtypeStruct((n_rows, D), F32),
        compiler_params=_cparams(("arbitrary", "arbitrary")),
        name="moe_ffn",
    )(blk_e, n_used, xs, w1, w3, w2)


def moe_layout(e_pos, counts_f, n_tok):
    RB = MOE_ROW_BLOCK
    n_assign = n_tok * TOP_K
    n_blocks = (n_assign + N_EXPERTS * (RB - 1) + RB - 1) // RB
    counts = counts_f[0, :N_EXPERTS].astype(jnp.int32)
    padded = (counts + RB - 1) // RB * RB
    pad_end = jnp.cumsum(padded)
    pad_start = pad_end - padded
    e = e_pos[:, 0:TOP_K]
    pos = e_pos[:, TOP_K:2 * TOP_K]
    start = jnp.zeros_like(e)
    for x in range(N_EXPERTS):
        start = jnp.where(e == x, pad_start[x], start)
    dest_flat = (start + pos).reshape(-1)
    blk_start = jnp.arange(n_blocks, dtype=jnp.int32) * RB
    blk_e = jnp.minimum(jnp.sum((blk_start[:, None] >= pad_end[None, :]).astype(jnp.int32), axis=1), N_EXPERTS - 1)
    n_used = (pad_end[-1] // RB).astype(jnp.int32).reshape(1)
    return dest_flat, blk_e.astype(jnp.int32), n_used, n_blocks * RB


def _pad_cols(w, width):
    return jnp.pad(w, ((0, 0), (0, width - w.shape[1])))


def kernel(x, c, rel_bias, ada_w, ada_b, ln_g, ln_b, ab_w_in, gla_wg2, gla_bg2, gla_norm_g, ab_w_out,
           ffn_w1, ffn_w3, ffn_w2, nsa_w_in, nsa_cmp_pos, nsa_cmp_w1k, nsa_cmp_w2k, nsa_cmp_w1v, nsa_cmp_w2v,
           nsa_w_out, moe_router, moe_w1, moe_w3, moe_w2):
    B, S, D = x.shape
    n_tok = B * S
    mod = adaln_mod(c, ada_w, ada_b)
    mods = [[m.reshape(B, 1, D) for m in jnp.split(mod[i], 6, axis=-1)] for i in range(DEPTH)]
    toep, cmpb = build_bias_tables(rel_bias, S)

    sh1, sc1 = mods[0][0], mods[0][1]
    h = modulate(x, sc1, sh1)
    for i in range(DEPTH):
        j = i // 2
        _, _, g1, sh2, sc2, g2 = mods[i]
        nxt2 = (mods[i + 1][1], mods[i + 1][0]) if i + 1 < DEPTH else None
        lng = ln_g[i].reshape(2, 1, D)
        lnb = ln_b[i].reshape(2, 1, D)
        if i % 2 == 0:
            w = ab_w_in[j]
            cuts = np.cumsum((MOBA_WIDTH, MOBA_WIDTH, MOBA_WIDTH, GLA_QK_WIDTH, GLA_QK_WIDTH, GLA_V_WIDTH,
                              GLA_V_WIDTH))
            parts = jnp.split(w, cuts.tolist(), axis=1)
            parts[-1] = _pad_cols(parts[-1], LANE)
            w_cat = jnp.concatenate(parts, axis=1).astype(BF16)
            widths = [p.shape[1] for p in parts]
            offs = np.concatenate([[0], np.cumsum(widths)[:-1]]).tolist()
            scales = [MOBA_HEAD_DIM ** -0.5 * LOG2E] + [1.0] * 7
            segs = tuple((o, wd, "flat", s) for o, wd, s in zip(offs, widths, scales))
            dts = [BF16, BF16, BF16, F32, F32, F32, F32, F32]
            mq, mk, mv, gq, gk, gv, gr, glr = project(h, w_cat, segs, dts)
            y_moba = moba_attention(mq, mk, mv, toep)
            wg2_pad = jnp.pad(gla_wg2[j], ((0, LANE - GLA_GATE_RANK), (0, 0)))
            y_gla = gla_mixer(gq, gk, gv, gr, glr, wg2_pad, gla_bg2[j].reshape(1, -1), gla_norm_g[j].reshape(1, -1))
            w_out = ab_w_out[j].astype(BF16)
            x, h = out_ln([y_moba, y_gla], [w_out[:MOBA_WIDTH], w_out[MOBA_WIDTH:]], x, g1, lng[0], lnb[0],
                          (sc2, sh2))
            x, h = ffn_ln(h, ffn_w1[j].astype(BF16), ffn_w3[j].astype(BF16), ffn_w2[j].astype(BF16), x, g2,
                          lng[1], lnb[1], nxt2)
        else:
            G, HPG, dh = NSA_KV_GROUPS, NSA_HPG, NSA_HEAD_DIM
            w = nsa_w_in[j]
            wq = w[:, :NSA_Q_WIDTH]
            wkv = w[:, NSA_Q_WIDTH:NSA_Q_WIDTH + 6 * NSA_KV_WIDTH]
            wg = w[:, NSA_Q_WIDTH + 6 * NSA_KV_WIDTH:].reshape(D, G, HPG, 3)
            wg = _pad_cols(wg.transpose(0, 1, 3, 2).reshape(D * G, 3 * HPG), LANE).reshape(D, G * LANE)
            w_cat = jnp.concatenate([wq, wkv, wg], axis=1).astype(BF16)
            segs = [(0, NSA_Q_WIDTH, "flat", dh ** -0.5 * LOG2E)]
            segs += [(NSA_Q_WIDTH + n * NSA_KV_WIDTH, NSA_KV_WIDTH, "group", 1.0) for n in range(6)]
            segs += [(NSA_Q_WIDTH + 6 * NSA_KV_WIDTH, G * LANE, "group", 1.0)]
            dts = [BF16] * 7 + [F32]
            q, kc, vc, ks, vs, kw, vw, gates = project(h, w_cat, tuple(segs), dts)
            half = NSA_CMP_STRIDE * dh
            pos2 = jnp.pad(nsa_cmp_pos[j].reshape(2, half), ((0, 6), (0, 0))).astype(BF16)
            w1k = jnp.concatenate([nsa_cmp_w1k[j][:half], nsa_cmp_w1k[j][half:]], axis=1).astype(BF16)
            w1v = jnp.concatenate([nsa_cmp_w1v[j][:half], nsa_cmp_w1v[j][half:]], axis=1).astype(BF16)
            k_cmp = nsa_compress(kc, pos2, w1k, nsa_cmp_w2k[j].astype(BF16))
            v_cmp = nsa_compress(vc, pos2, w1v, nsa_cmp_w2v[j].astype(BF16))
            y = nsa_attention(q, k_cmp, v_cmp, ks, vs, kw, vw, gates, cmpb, toep)
            x, h = out_ln([y], [nsa_w_out[j].astype(BF16)], x, g1, lng[0], lnb[0], (sc2, sh2), h_dtype=F32)
            h2 = h.reshape(n_tok, D)
            e_pos, w_top, counts = moe_route(h2, _pad_cols(moe_router[j], LANE))
            dest_flat, blk_e, n_used, n_rows = moe_layout(e_pos, counts, n_tok)
            xs = moe_dispatch_rows(h2, dest_flat, n_rows)
            ys = moe_ffn(xs, blk_e, n_used, moe_w1[j].astype(BF16), moe_w3[j].astype(BF16), moe_w2[j].astype(BF16))
            x, h = moe_combine_ln(ys, dest_flat, w_top, x, g2, lng[1], lnb[1], nxt2)
    return x
```

```python
import functools
import math

import jax
import jax.numpy as jnp
import numpy as np
from jax import lax
from jax.experimental import pallas as pl
from jax.experimental.pallas import tpu as pltpu

F32 = jnp.float32
BF16 = jnp.bfloat16
NEG_INF = float("-inf")

DEPTH = 4
ALPHA_DN = (2 * DEPTH) ** 0.25
LN_EPS = 1e-5
RMS_EPS = 1e-6

N_BUCKETS = 32
MAX_DISTANCE = 1024

MOBA_HEADS = 8
MOBA_HEAD_DIM = 64
MOBA_BLOCK = 256
MOBA_TOPK = 3
MOBA_WIDTH = MOBA_HEADS * MOBA_HEAD_DIM

GLA_HEADS = 4
GLA_DK = 64
GLA_DV = 128
GLA_GATE_RANK = 16
GLA_TAU = 16.0
GLA_CHUNK = 64
GLA_QK_WIDTH = GLA_HEADS * GLA_DK
GLA_V_WIDTH = GLA_HEADS * GLA_DV

NSA_HEADS = 8
NSA_KV_GROUPS = 2
NSA_HPG = NSA_HEADS // NSA_KV_GROUPS
NSA_HEAD_DIM = 128
NSA_CMP_LEN = 32
NSA_CMP_STRIDE = 16
NSA_SLC_BLOCK = 64
NSA_SLC_TOPN = 16
NSA_WINDOW = 512
NSA_Q_WIDTH = NSA_HEADS * NSA_HEAD_DIM
NSA_KV_WIDTH = NSA_KV_GROUPS * NSA_HEAD_DIM

N_EXPERTS = 8
TOP_K = 2
MOE_BLOCKS_PER_EXPERT = 4


def _moe_row_block(n_tok):
    balanced = n_tok * TOP_K // N_EXPERTS
    rows = balanced // MOE_BLOCKS_PER_EXPERT + balanced // 128
    return -(-rows // 16) * 16

LOG2E = math.log2(math.e)
LANE = 128
BIAS_ROWS = 128


def _cparams(sem, vmem_mib=40, **extra):
    return pltpu.CompilerParams(dimension_semantics=sem, vmem_limit_bytes=vmem_mib << 20, **extra)


def _adaln_kernel(c_ref, w_ref, b_ref, o_ref):
    c = c_ref[...]
    cond = c * jax.nn.sigmoid(c)
    o_ref[0] = jnp.dot(cond, w_ref[0], preferred_element_type=F32) + b_ref[0]


def adaln_mod(c, ada_w, ada_b):
    B, D = c.shape
    L, _, N = ada_w.shape
    rows = 8
    cp = jnp.zeros((rows, D), F32).at[:B].set(c)
    tn = N // 4
    out = pl.pallas_call(
        _adaln_kernel,
        grid=(L, N // tn),
        in_specs=[
            pl.BlockSpec((rows, D), lambda l, j: (0, 0)),
            pl.BlockSpec((1, D, tn), lambda l, j: (l, 0, j)),
            pl.BlockSpec((1, 1, tn), lambda l, j: (l, 0, j)),
        ],
        out_specs=pl.BlockSpec((1, rows, tn), lambda l, j: (l, 0, j)),
        out_shape=jax.ShapeDtypeStruct((L, rows, N), F32),
        compiler_params=_cparams(("arbitrary", "arbitrary")),
        name="adaln_mod",
    )(cp, ada_w, ada_b.reshape(L, 1, N))
    return out[:, :B]


def _modulate_kernel(x_ref, sc_ref, sh_ref, o_ref):
    o_ref[0] = (x_ref[0] * (1.0 + sc_ref[0]) + sh_ref[0]).astype(o_ref.dtype)


def modulate(x, sc, sh, tm=512):
    B, S, D = x.shape
    return pl.pallas_call(
        _modulate_kernel,
        grid=(B, S // tm),
        in_specs=[
            pl.BlockSpec((1, tm, D), lambda b, i: (b, i, 0)),
            pl.BlockSpec((1, 1, D), lambda b, i: (b, 0, 0)),
            pl.BlockSpec((1, 1, D), lambda b, i: (b, 0, 0)),
        ],
        out_specs=pl.BlockSpec((1, tm, D), lambda b, i: (b, i, 0)),
        out_shape=jax.ShapeDtypeStruct((B, S, D), BF16),
        compiler_params=_cparams(("arbitrary", "arbitrary")),
        name="modulate",
    )(x, sc, sh)


def _t5_bucket(dist):
    max_exact = N_BUCKETS // 2
    n = jnp.maximum(dist, 0)
    nf = jnp.maximum(n, 1).astype(jnp.float32)
    large = max_exact + (jnp.log(nf / max_exact) / math.log(MAX_DISTANCE / max_exact) * (N_BUCKETS - max_exact)).astype(jnp.int32)
    large = jnp.minimum(large, N_BUCKETS - 1)
    return jnp.where(n < max_exact, n, large)


def _bias_table_kernel(rb_ref, bm_ref, o_ref, *, rows_per_step):
    h = pl.program_id(0)
    n_steps = bm_ref.shape[0] // rows_per_step

    def body(i, carry):
        r0 = pl.multiple_of(i * rows_per_step, rows_per_step)
        bm = bm_ref[pl.ds(r0, rows_per_step), :]
        acc = jnp.zeros(bm.shape, F32)
        for b in range(N_BUCKETS):
            acc = jnp.where(bm == b, rb_ref[b, h] * LOG2E, acc)
        o_ref[0, pl.ds(r0, rows_per_step), :] = acc
        return carry

    lax.fori_loop(0, n_steps, body, 0)


def bias_table(rel_bias, bucket_map, rows_per_step):
    R, W = bucket_map.shape
    H = rel_bias.shape[1]
    return pl.pallas_call(
        functools.partial(_bias_table_kernel, rows_per_step=rows_per_step),
        grid=(H,),
        in_specs=[
            pl.BlockSpec(memory_space=pltpu.SMEM),
            pl.BlockSpec((R, W), lambda h: (0, 0)),
        ],
        out_specs=pl.BlockSpec((1, R, W), lambda h: (h, 0, 0)),
        out_shape=jax.ShapeDtypeStruct((H, R, W), F32),
        compiler_params=_cparams(("arbitrary",)),
        name="bias_table",
    )(rel_bias, bucket_map)


def _toep_width(S):
    return S + BIAS_ROWS


def _toep_col(q0, k0):
    return pl.multiple_of(q0 - k0 + BIAS_ROWS, LANE)


def build_bias_tables(rel_bias, S):
    j = jnp.arange(BIAS_ROWS, dtype=jnp.int32)[:, None]
    c = jnp.arange(_toep_width(S), dtype=jnp.int32)[None, :]
    toep_map = _t5_bucket(c - j - BIAS_ROWS)
    n_cmp_pad = S // NSA_CMP_STRIDE
    t = jnp.arange(S, dtype=jnp.int32)[None, :]
    cmp_end = jnp.arange(n_cmp_pad, dtype=jnp.int32)[:, None] * NSA_CMP_STRIDE + NSA_CMP_LEN - 1
    cmp_map = _t5_bucket(t - cmp_end)
    return bias_table(rel_bias, toep_map, 8), bias_table(rel_bias, cmp_map, 8)


def _proj_kernel(h_ref, w_ref, wt_ref, *o_refs, segs):
    h = h_ref[0]
    for (off, width, kind, scale), o_ref in zip(segs, o_refs):
        if kind == "flat_t":
            acc = lax.dot_general(wt_ref[off:off + width, :], h, _NT, preferred_element_type=F32)
        else:
            acc = jnp.dot(h, w_ref[:, off:off + width], preferred_element_type=F32)
        if scale != 1.0:
            acc = acc * scale
        if kind in ("flat", "flat_t"):
            o_ref[0] = acc.astype(o_ref.dtype)
        else:
            for g in range(width // LANE):
                o_ref[0, g] = acc[:, g * LANE:(g + 1) * LANE].astype(o_ref.dtype)


def project(h, w_cat, segs, out_dtypes, wt=None, tm=512):
    B, S, D = h.shape
    if wt is None:
        wt = jnp.zeros((8, D), w_cat.dtype)
    out_shapes, out_specs = [], []
    for (off, width, kind, scale), dt in zip(segs, out_dtypes):
        if kind == "flat":
            out_shapes.append(jax.ShapeDtypeStruct((B, S, width), dt))
            out_specs.append(pl.BlockSpec((1, tm, width), lambda b, i: (b, i, 0)))
        elif kind == "flat_t":
            out_shapes.append(jax.ShapeDtypeStruct((B, width, S), dt))
            out_specs.append(pl.BlockSpec((1, width, tm), lambda b, i: (b, 0, i)))
        else:
            G = width // LANE
            out_shapes.append(jax.ShapeDtypeStruct((B, G, S, LANE), dt))
            out_specs.append(pl.BlockSpec((1, G, tm, LANE), lambda b, i: (b, 0, i, 0)))
    return pl.pallas_call(
        functools.partial(_proj_kernel, segs=segs),
        grid=(B, S // tm),
        in_specs=[
            pl.BlockSpec((1, tm, D), lambda b, i: (b, i, 0)),
            pl.BlockSpec(w_cat.shape, lambda b, i: (0, 0)),
            pl.BlockSpec(wt.shape, lambda b, i: (0, 0)),
        ],
        out_specs=out_specs,
        out_shape=out_shapes,
        compiler_params=_cparams(("arbitrary", "arbitrary"), 48),
        name="in_proj",
    )(h, w_cat, wt)


_NT = (((1,), (1,)), ((), ()))
_TN = (((0,), (0,)), ((), ()))


def _softmax_step(carry, s, v_tile):
    m, l, acc = carry
    m_new = jnp.maximum(m, jnp.max(s, axis=0, keepdims=True))
    alpha = jnp.exp2(m - m_new)
    p = jnp.exp2(s - m_new)
    l_new = alpha * l + jnp.sum(p, axis=0, keepdims=True)
    acc_new = alpha * acc + lax.dot_general(v_tile, p.astype(BF16), _TN, preferred_element_type=F32)
    return m_new, l_new, acc_new


def _moba_kernel(q_ref, k_ref, v_ref, tb_ref, o_ref, kmean_sc, sel_sc, s0_sc, s1_sc, p0_sc, p1_sc,
                 *, S, heads_per_step):
    L = MOBA_BLOCK
    dh = MOBA_HEAD_DIM
    nb = S // L
    qi = pl.program_id(2)
    q0 = qi * L

    @pl.when(qi == 0)
    def _():
        kf = k_ref[0].astype(F32)
        kmean_sc[...] = jnp.mean(kf.reshape(nb, L, kf.shape[-1]), axis=1)

    q2 = q_ref[0]
    lane_q = lax.broadcasted_iota(jnp.int32, q2.shape, 1)
    lane_m = lax.broadcasted_iota(jnp.int32, kmean_sc.shape, 1)
    blk = lax.broadcasted_iota(jnp.int32, (nb, L), 0)
    qz = []
    for hh in range(heads_per_step):
        in_head = (lane_q >= hh * dh) & (lane_q < (hh + 1) * dh)
        qz.append(jnp.where(in_head, q2, jnp.zeros_like(q2)))
        km = jnp.where((lane_m >= hh * dh) & (lane_m < (hh + 1) * dh), kmean_sc[...], 0.0)
        gate = lax.dot_general(km, q2.astype(F32), _NT, precision=lax.Precision.HIGHEST,
                               preferred_element_type=F32)
        valid = blk < qi
        gm = jnp.where(valid, gate, NEG_INF)
        rank = jnp.zeros((nb, L), jnp.int32)
        for m in range(nb):
            gmm = gm[m:m + 1, :]
            beats = (gmm > gm) | ((gmm == gm) & (m < blk))
            rank = rank + beats.astype(jnp.int32)
        sel_sc[hh] = jnp.where(valid & (rank < MOBA_TOPK), 1.0, 0.0)

    def bias_tile(hh, k0, n_keys):
        return jnp.concatenate([tb_ref[hh, :, pl.ds(_toep_col(q0, k0 + r * BIAS_ROWS), L)]
                                for r in range(n_keys // BIAS_ROWS)], axis=0)

    krow = lax.broadcasted_iota(jnp.int32, (L, L), 0)
    qcol = lax.broadcasted_iota(jnp.int32, (L, L), 1)
    k_own = k_ref[0, pl.ds(pl.multiple_of(q0, L), L), :]
    v_tile = lambda hh, k0, n_keys: v_ref[0, hh * dh:(hh + 1) * dh, pl.ds(k0, n_keys)]
    carry = []
    for hh in range(heads_per_step):
        s = lax.dot_general(k_own, qz[hh], _NT, preferred_element_type=F32) + bias_tile(hh, q0, L)
        s = jnp.where(krow <= qcol, s, NEG_INF)
        m0 = jnp.max(s, axis=0, keepdims=True)
        p = jnp.exp2(s - m0)
        l0 = jnp.sum(p, axis=0, keepdims=True)
        acc0 = jnp.dot(v_tile(hh, pl.multiple_of(q0, L), L), p.astype(BF16), preferred_element_type=F32)
        carry += [m0, l0, acc0, jnp.ones((1, L), F32)]

    n = (qi + 1) // 2
    s_bufs = (s0_sc, s1_sc)
    p_bufs = (p0_sc, p1_sc)
    tile_k0 = lambda i: pl.multiple_of(jnp.clip(i, 0, jnp.maximum(n - 1, 0)) * 2 * L, 2 * L)

    def put_scores(i, buf):
        kt = k_ref[0, pl.ds(tile_k0(i), 2 * L), :]
        for hh in range(heads_per_step):
            buf[hh] = lax.dot_general(kt, qz[hh], _NT, preferred_element_type=F32)

    put_scores(0, s_bufs[0])
    p_bufs[1][...] = jnp.zeros(p_bufs[1].shape, BF16)

    def half(i, cur, carry):
        put_scores(i + 1, s_bufs[1 - cur])
        k0 = tile_k0(i)
        blk0 = lax.shift_right_logical(k0, L.bit_length() - 1)
        thresh = jnp.where(i < n, 0.5, 2.0)
        out = []
        for hh in range(heads_per_step):
            m, l, acc, alpha_prev = carry[4 * hh:4 * hh + 4]
            flags = jnp.concatenate(
                [jnp.broadcast_to(sel_sc[hh, pl.ds(blk0 + j, 1), :], (L, L)) for j in range(2)], axis=0)
            s = jnp.where(flags > thresh, s_bufs[cur][hh] + bias_tile(hh, k0, 2 * L), NEG_INF)
            m_new = jnp.maximum(m, jnp.max(s, axis=0, keepdims=True))
            alpha = jnp.exp2(m - m_new)
            p = jnp.exp2(s - m_new)
            l_new = alpha * l + jnp.sum(p, axis=0, keepdims=True)
            p_bufs[cur][hh] = p.astype(BF16)
            pv = jnp.dot(v_tile(hh, tile_k0(i - 1), 2 * L), p_bufs[1 - cur][hh], preferred_element_type=F32)
            out += [m_new, l_new, alpha_prev * acc + pv, alpha]
        return tuple(out)

    def body(jj, carry):
        return half(2 * jj + 1, 1, half(2 * jj, 0, carry))

    n_pairs = (n + 1) // 2
    carry = lax.fori_loop(0, n_pairs, body, tuple(carry))
    outs = []
    for hh in range(heads_per_step):
        _, l, acc, alpha = carry[4 * hh:4 * hh + 4]
        pv = jnp.dot(v_tile(hh, tile_k0(2 * n_pairs - 1), 2 * L), p_bufs[1][hh], preferred_element_type=F32)
        outs.append((alpha * acc + pv) * (1.0 / l))
    o_ref[0] = jnp.concatenate(outs, axis=0).T.astype(o_ref.dtype)


def moba_attention(q, k, v, toep):
    B, S, _ = q.shape
    L = MOBA_BLOCK
    hps = 4
    wl = hps * MOBA_HEAD_DIM
    n_hp = MOBA_HEADS // hps
    nb = S // L
    return pl.pallas_call(
        functools.partial(_moba_kernel, S=S, heads_per_step=hps),
        grid=(n_hp, B, nb),
        in_specs=[
            pl.BlockSpec((1, L, wl), lambda hp, b, i: (b, i, hp)),
            pl.BlockSpec((1, S, wl), lambda hp, b, i: (b, 0, hp)),
            pl.BlockSpec((1, wl, S), lambda hp, b, i: (b, hp, 0)),
            pl.BlockSpec((hps, BIAS_ROWS, _toep_width(S)), lambda hp, b, i: (hp, 0, 0)),
        ],
        out_specs=pl.BlockSpec((1, L, wl), lambda hp, b, i: (b, i, hp)),
        out_shape=jax.ShapeDtypeStruct((B, S, MOBA_WIDTH), BF16),
        scratch_shapes=[pltpu.VMEM((nb, wl), F32), pltpu.VMEM((hps, nb, L), F32)]
        + [pltpu.VMEM((hps, 2 * L, L), F32)] * 2 + [pltpu.VMEM((hps, 2 * L, L), BF16)] * 2,
        compiler_params=_cparams(("arbitrary", "arbitrary", "arbitrary"), 48),
        name="moba_attn",
    )(q, k, v, toep)


def _log_sigmoid(z):
    return jnp.minimum(z, 0.0) - jnp.log1p(jnp.exp(-jnp.abs(z)))


def _gla_kernel(q_ref, k_ref, v_ref, r_ref, lr_ref, wg2_ref, bg2_ref, ng_ref, o_ref, st_sc, *, batch):
    C = GLA_CHUNK
    dk, dv = GLA_DK, GLA_DV

    @pl.when(pl.program_id(0) == 0)
    def _():
        st_sc[...] = jnp.zeros(st_sc.shape, F32)

    row = lax.broadcasted_iota(jnp.int32, (C, C), 0)
    col = lax.broadcasted_iota(jnp.int32, (C, C), 1)
    causal = col <= row
    tri = jnp.where(causal, 1.0, 0.0)
    for b in range(batch):
        z = jnp.dot(lr_ref[b], wg2_ref[...], preferred_element_type=F32) + bg2_ref[...]
        la = _log_sigmoid(z) / GLA_TAU
        bc = jnp.dot(tri, la, precision=lax.Precision.HIGHEST, preferred_element_type=F32)
        b_last = bc[C - 1:C, :]
        qd = q_ref[b] * (dk ** -0.5) * jnp.exp(bc)
        kd = k_ref[b] * jnp.exp(-bc)
        kl = k_ref[b] * jnp.exp(b_last - bc)
        e_last = jnp.exp(b_last)
        v = v_ref[b]
        r = r_ref[b]
        outs = []
        for h in range(GLA_HEADS):
            ks = slice(h * dk, (h + 1) * dk)
            vs = slice(h * dv, (h + 1) * dv)
            qh = qd[:, ks].astype(BF16)
            kh = kd[:, ks].astype(BF16)
            vh = v[:, vs].astype(BF16)
            attn = lax.dot_general(qh, kh, (((1,), (1,)), ((), ())), preferred_element_type=F32)
            attn = jnp.where(causal, attn, 0.0)
            st = st_sc[b, h]
            o = jnp.dot(attn.astype(BF16), vh, preferred_element_type=F32)
            o = o + lax.dot_general(qh, st.astype(BF16), (((1,), (1,)), ((), ())), preferred_element_type=F32)
            upd = lax.dot_general(vh, kl[:, ks].astype(BF16), (((0,), (0,)), ((), ())), preferred_element_type=F32)
            st_sc[b, h] = st * e_last[:, ks] + upd
            o = o * lax.rsqrt(jnp.mean(o * o, axis=-1, keepdims=True) + RMS_EPS)
            rg = r[:, vs]
            outs.append(o * ng_ref[:, vs] * (rg * jax.nn.sigmoid(rg)))
        o_ref[b] = jnp.concatenate(outs, axis=-1).astype(o_ref.dtype)


def gla_mixer(q, k, v, r, lr, wg2_pad, bg2, norm_g):
    B, S, _ = q.shape
    C = GLA_CHUNK
    nc = S // C
    blk = lambda w: pl.BlockSpec((B, C, w), lambda c: (0, c, 0))
    full = lambda a: pl.BlockSpec(a.shape, lambda c: (0,) * a.ndim)
    return pl.pallas_call(
        functools.partial(_gla_kernel, batch=B),
        grid=(nc,),
        in_specs=[blk(GLA_QK_WIDTH), blk(GLA_QK_WIDTH), blk(GLA_V_WIDTH), blk(GLA_V_WIDTH), blk(LANE),
                  full(wg2_pad), full(bg2), full(norm_g)],
        out_specs=blk(GLA_V_WIDTH),
        out_shape=jax.ShapeDtypeStruct((B, S, GLA_V_WIDTH), BF16),
        scratch_shapes=[pltpu.VMEM((B, GLA_HEADS, GLA_DV, GLA_DK), F32)],
        compiler_params=_cparams(("arbitrary",)),
        name="gla",
    )(q, k, v, r, lr, wg2_pad, bg2, norm_g)


def _deepnorm_ln(x, y, gate, ln_g, ln_b):
    z = ALPHA_DN * x + (1.0 + gate) * y
    mu = jnp.mean(z, axis=-1, keepdims=True)
    zc = z - mu
    var = jnp.mean(zc * zc, axis=-1, keepdims=True)
    return zc * lax.rsqrt(var + LN_EPS) * ln_g + ln_b


def _out_ln_kernel(*refs, n_parts, matmul, has_next):
    y_refs = refs[:n_parts]
    pos = n_parts
    if matmul:
        w_refs = refs[pos:pos + n_parts]
        pos += n_parts
    x_ref, g_ref, lng_ref, lnb_ref = refs[pos:pos + 4]
    pos += 4
    if has_next:
        sc_ref, sh_ref = refs[pos:pos + 2]
        pos += 2
    o_ref = refs[pos]
    y = None
    for i in range(n_parts):
        part = jnp.dot(y_refs[i][0], w_refs[i][...], preferred_element_type=F32) if matmul else y_refs[i][0]
        y = part if y is None else y + part
    xn = _deepnorm_ln(x_ref[0], y, g_ref[0], lng_ref[...], lnb_ref[...])
    o_ref[0] = xn
    if has_next:
        refs[pos + 1][0] = (xn * (1.0 + sc_ref[0]) + sh_ref[0]).astype(refs[pos + 1].dtype)


def out_ln(ys, ws, x, gate, ln_g, ln_b, nxt, h_dtype=BF16, tm=512):
    B, S, D = x.shape
    tile = lambda w: pl.BlockSpec((1, tm, w), lambda b, i: (b, i, 0))
    vec = pl.BlockSpec((1, 1, D), lambda b, i: (b, 0, 0))
    par = pl.BlockSpec((1, D), lambda b, i: (0, 0))
    args = list(ys)
    in_specs = [tile(y.shape[-1]) for y in ys]
    if ws is not None:
        args += list(ws)
        in_specs += [pl.BlockSpec(w.shape, lambda b, i: (0, 0)) for w in ws]
    args += [x, gate, ln_g, ln_b]
    in_specs += [tile(D), vec, par, par]
    out_shape = [jax.ShapeDtypeStruct((B, S, D), F32)]
    out_specs = [tile(D)]
    if nxt is not None:
        args += list(nxt)
        in_specs += [vec, vec]
        out_shape.append(jax.ShapeDtypeStruct((B, S, D), h_dtype))
        out_specs.append(tile(D))
    res = pl.pallas_call(
        functools.partial(_out_ln_kernel, n_parts=len(ys), matmul=ws is not None, has_next=nxt is not None),
        grid=(B, S // tm),
        in_specs=in_specs,
        out_specs=out_specs,
        out_shape=out_shape,
        compiler_params=_cparams(("arbitrary", "arbitrary")),
        name="out_ln",
    )(*args)
    return (res[0], res[1]) if nxt is not None else (res[0], None)


def _ffn_ln_kernel(h_ref, w1_ref, w3_ref, w2_ref, x_ref, g_ref, lng_ref, lnb_ref, sc_ref, sh_ref,
                   o_ref, hn_ref, acc_sc):
    j = pl.program_id(2)

    @pl.when(j == 0)
    def _():
        acc_sc[...] = jnp.zeros(acc_sc.shape, F32)

    h = h_ref[0]
    a = jnp.dot(h, w1_ref[...], preferred_element_type=F32)
    b = jnp.dot(h, w3_ref[...], preferred_element_type=F32)
    u = (a * jax.nn.sigmoid(a)) * b
    acc_sc[...] += jnp.dot(u.astype(BF16), w2_ref[...], preferred_element_type=F32)

    @pl.when(j == pl.num_programs(2) - 1)
    def _():
        xn = _deepnorm_ln(x_ref[0], acc_sc[...], g_ref[0], lng_ref[...], lnb_ref[...])
        o_ref[0] = xn
        hn_ref[0] = (xn * (1.0 + sc_ref[0]) + sh_ref[0]).astype(BF16)


def ffn_ln(h, w1, w3, w2, x, gate, ln_g, ln_b, nxt, tm=512):
    B, S, D = x.shape
    F = w1.shape[1]
    tf = F // 2
    tile = pl.BlockSpec((1, tm, D), lambda b, i, j: (b, i, 0))
    vec = pl.BlockSpec((1, 1, D), lambda b, i, j: (b, 0, 0))
    par = pl.BlockSpec((1, D), lambda b, i, j: (0, 0))
    return pl.pallas_call(
        _ffn_ln_kernel,
        grid=(B, S // tm, F // tf),
        in_specs=[tile,
                  pl.BlockSpec((D, tf), lambda b, i, j: (0, j)),
                  pl.BlockSpec((D, tf), lambda b, i, j: (0, j)),
                  pl.BlockSpec((tf, D), lambda b, i, j: (j, 0)),
                  tile, vec, par, par, vec, vec],
        out_specs=[tile, tile],
        out_shape=[jax.ShapeDtypeStruct((B, S, D), F32), jax.ShapeDtypeStruct((B, S, D), BF16)],
        scratch_shapes=[pltpu.VMEM((tm, D), F32)],
        compiler_params=_cparams(("arbitrary", "arbitrary", "arbitrary"), 48),
        name="ffn_ln",
    )(h, w1, w3, w2, x, gate, ln_g, ln_b, nxt[0], nxt[1])


def _compress_kernel(x_ref, pos_ref, w1_ref, w2_ref, o_ref):
    dh = NSA_HEAD_DIM
    x = x_ref[0, 0]
    uv = jnp.dot(x, w1_ref[...], preferred_element_type=F32)
    pc = jnp.dot(pos_ref[...], w1_ref[...], preferred_element_type=F32)
    c = pc[0:1, :dh] + pc[1:2, dh:]
    n = uv.shape[0]
    nxt = pltpu.roll(uv[:, dh:], n - 1, 0)
    hid = uv[:, :dh] + nxt + c
    act = hid * jax.nn.sigmoid(hid)
    o_ref[0, 0] = jnp.dot(act.astype(BF16), w2_ref[...], preferred_element_type=F32).astype(o_ref.dtype)


def nsa_compress(kv, pos2, w1cat, w2):
    B, G, S, dh = kv.shape
    seg = NSA_CMP_STRIDE
    n = S // seg
    x = kv.reshape(B, G, n, seg * dh)
    return pl.pallas_call(
        _compress_kernel,
        grid=(B, G),
        in_specs=[
            pl.BlockSpec((1, 1, n, seg * dh), lambda b, g: (b, g, 0, 0)),
            pl.BlockSpec(pos2.shape, lambda b, g: (0, 0)),
            pl.BlockSpec(w1cat.shape, lambda b, g: (0, 0)),
            pl.BlockSpec(w2.shape, lambda b, g: (0, 0)),
        ],
        out_specs=pl.BlockSpec((1, 1, n, dh), lambda b, g: (b, g, 0, 0)),
        out_shape=jax.ShapeDtypeStruct((B, G, n, dh), BF16),
        compiler_params=_cparams(("arbitrary", "arbitrary")),
        name="nsa_compress",
    )(x, pos2, w1cat, w2)


def _nsa_kernel(q_ref, kc_ref, vc_ref, ks_ref, vs_ref, kw_ref, vw_ref, gt_ref, cb_ref, tb_ref, o_ref, sel_sc,
                s0_sc, s1_sc, p0_sc, p1_sc, *, S, TQ):
    dh = NSA_HEAD_DIM
    HPG = NSA_HPG
    LS = NSA_SLC_BLOCK
    TK = 4 * LS
    R = HPG * TQ
    n_cmp = (S - NSA_CMP_LEN) // NSA_CMP_STRIDE + 1
    n_cp = S // NSA_CMP_STRIDE
    n_slc = S // LS
    n_top = min(NSA_SLC_TOPN, n_slc)
    qi = pl.program_id(2)
    q0 = qi * TQ

    q4 = jnp.concatenate([q_ref[0, :, h * dh:(h + 1) * dh] for h in range(HPG)], axis=0)
    lanes4 = lambda a: jnp.concatenate([a] * HPG, axis=1)

    s = lax.dot_general(kc_ref[0, 0], q4, _NT, preferred_element_type=F32)
    s = s + jnp.concatenate([cb_ref[h] for h in range(HPG)], axis=1)
    n_row = lax.broadcasted_iota(jnp.int32, (n_cp, TQ), 0)
    t_q = q0 + lax.broadcasted_iota(jnp.int32, (n_cp, TQ), 1)
    valid_c = lanes4((n_row * NSA_CMP_STRIDE + (NSA_CMP_LEN - 1) <= t_q) & (n_row < n_cmp))
    s = jnp.where(valid_c, s, NEG_INF)
    m = jnp.max(s, axis=0, keepdims=True)
    m = jnp.where(m == NEG_INF, 0.0, m)
    e = jnp.exp2(s - m)
    p_c = e * (1.0 / jnp.maximum(jnp.sum(e, axis=0, keepdims=True), jnp.finfo(F32).tiny))
    o_c = lax.dot_general(vc_ref[0, 0], p_c.astype(BF16), _TN, preferred_element_type=F32)

    p_sum = p_c[:, 0:TQ]
    for h in range(1, HPG):
        p_sum = p_sum + p_c[:, h * TQ:(h + 1) * TQ]
    om = lax.broadcasted_iota(jnp.int32, (n_slc, n_cp), 0)
    c_lo = lax.broadcasted_iota(jnp.int32, (n_slc, n_cp), 1) * NSA_CMP_STRIDE
    ov = jnp.maximum(jnp.minimum(c_lo + NSA_CMP_LEN, om * LS + LS) - jnp.maximum(c_lo, om * LS), 0)
    overlap_t = ov.astype(F32) / NSA_CMP_LEN
    imp_t = jnp.dot(overlap_t, p_sum, precision=lax.Precision.HIGHEST, preferred_element_type=F32)
    blk = lax.broadcasted_iota(jnp.int32, (n_slc, TQ), 0)
    tq_lane = q0 + lax.broadcasted_iota(jnp.int32, (n_slc, TQ), 1)
    tb = lax.shift_right_logical(tq_lane, LS.bit_length() - 1)
    forced = (blk == 0) | (blk == tb) | (blk == tb - 1)
    key = jnp.where(blk > tb, NEG_INF, jnp.where(forced, jnp.inf, imp_t))
    rank = jnp.zeros((n_slc, TQ), jnp.int32)
    for mm in range(n_slc):
        km = key[mm:mm + 1, :]
        beats = (km > key) | ((km == key) & (mm < blk))
        rank = rank + beats.astype(jnp.int32)
    sel_sc[...] = jnp.where((rank < n_top) & (blk <= tb), 1.0, 0.0)

    def toep_tile(k0, n_keys):
        return jnp.concatenate(
            [jnp.concatenate([tb_ref[h, :, pl.ds(_toep_col(q0, k0 + r * BIAS_ROWS), TQ)] for h in range(HPG)], axis=1)
             for r in range(n_keys // BIAS_ROWS)], axis=0)

    def pipelined_attention(n, scores_fn, post_fn, v_fn):
        s_bufs = (s0_sc, s1_sc)
        p_bufs = (p0_sc, p1_sc)
        s_bufs[0][...] = scores_fn(0)
        p_bufs[1][...] = jnp.zeros(p_bufs[1].shape, BF16)

        def half(i, cur, carry):
            m, l, acc, alpha_prev = carry
            s_bufs[1 - cur][...] = scores_fn(jnp.minimum(i + 1, n - 1))
            s = post_fn(i, s_bufs[cur][...])
            m_new = jnp.maximum(m, jnp.max(s, axis=0, keepdims=True))
            alpha = jnp.exp2(m - m_new)
            p = jnp.exp2(s - m_new)
            l_new = alpha * l + jnp.sum(p, axis=0, keepdims=True)
            p_bufs[cur][...] = p.astype(BF16)
            pv = lax.dot_general(v_fn(jnp.maximum(i - 1, 0)), p_bufs[1 - cur][...], _TN, preferred_element_type=F32)
            return m_new, l_new, alpha_prev * acc + pv, alpha

        def body(jj, carry):
            return half(2 * jj + 1, 1, half(2 * jj, 0, carry))

        n_pairs = (n + 1) // 2
        init = (jnp.full((1, R), NEG_INF, F32), jnp.zeros((1, R), F32), jnp.zeros((dh, R), F32),
                jnp.ones((1, R), F32))
        _, l, acc, alpha = lax.fori_loop(0, n_pairs, body, init)
        pv = lax.dot_general(v_fn(jnp.minimum(2 * n_pairs - 1, n - 1)), p_bufs[1][...], _TN,
                             preferred_element_type=F32)
        return (alpha * acc + pv) * (1.0 / l)

    krow_s = lax.broadcasted_iota(jnp.int32, (TK, TQ), 0)
    tq_s = q0 + lax.broadcasted_iota(jnp.int32, (TK, TQ), 1)
    n_kt = (q0 + TQ - 1) // TK + 1

    def slc_scores(kt):
        return lax.dot_general(ks_ref[0, 0, pl.ds(pl.multiple_of(kt * TK, TK), TK), :], q4, _NT,
                               preferred_element_type=F32)

    def slc_post(kt, s):
        live = kt < n_kt
        ktc = jnp.minimum(kt, n_kt - 1)
        k0 = pl.multiple_of(ktc * TK, TK)
        flags = jnp.concatenate(
            [jnp.broadcast_to(sel_sc[pl.ds(ktc * (TK // LS) + j, 1), :], (LS, TQ)) for j in range(TK // LS)], axis=0)
        mask = lanes4((flags > jnp.where(live, 0.5, 2.0)) & (k0 + krow_s <= tq_s))
        return jnp.where(mask, s + toep_tile(k0, TK), NEG_INF)

    o_s = pipelined_attention(n_kt, slc_scores, slc_post,
                              lambda kt: vs_ref[0, 0, pl.ds(pl.multiple_of(kt * TK, TK), TK), :])

    krow_w = lax.broadcasted_iota(jnp.int32, (TQ, TQ), 0)
    tq_w = q0 + lax.broadcasted_iota(jnp.int32, (TQ, TQ), 1)
    n_wt = jnp.minimum(qi, NSA_WINDOW // TQ) + 1

    def win_body(j, carry):
        k0 = pl.multiple_of(q0 - j * TQ, TQ)
        dist = tq_w - (k0 + krow_w)
        mask = lanes4((dist >= 0) & (dist < NSA_WINDOW))
        s = lax.dot_general(kw_ref[0, 0, pl.ds(k0, TQ), :], q4, _NT, preferred_element_type=F32) + toep_tile(k0, TQ)
        return _softmax_step(carry, jnp.where(mask, s, NEG_INF), vw_ref[0, 0, pl.ds(k0, TQ), :])

    init_w = (jnp.full((1, R), NEG_INF, F32), jnp.zeros((1, R), F32), jnp.zeros((dh, R), F32))
    _, l_w, acc_w = lax.fori_loop(0, n_wt, win_body, init_w)
    o_w = acc_w * (1.0 / l_w)

    g_t = jax.nn.sigmoid(gt_ref[0, 0]).T
    outs = []
    for h in range(HPG):
        cols = slice(h * TQ, (h + 1) * TQ)
        o = (g_t[h:h + 1] * o_c[:, cols] + g_t[HPG + h:HPG + h + 1] * o_s[:, cols]
             + g_t[2 * HPG + h:2 * HPG + h + 1] * o_w[:, cols])
        outs.append(o.T)
    o_ref[0] = jnp.concatenate(outs, axis=1).astype(o_ref.dtype)


def nsa_attention(q, kc, vc, ks, vs, kw, vw, gates, cmpb, toep, TQ=256):
    B, S, _ = q.shape
    G, HPG, dh = NSA_KV_GROUPS, NSA_HPG, NSA_HEAD_DIM
    n_cp = S // NSA_CMP_STRIDE
    full_kv = pl.BlockSpec((1, 1, S, dh), lambda b, g, i: (b, g, 0, 0))
    cmp_kv = pl.BlockSpec((1, 1, n_cp, dh), lambda b, g, i: (b, g, 0, 0))
    return pl.pallas_call(
        functools.partial(_nsa_kernel, S=S, TQ=TQ),
        grid=(B, G, S // TQ),
        in_specs=[
            pl.BlockSpec((1, TQ, HPG * dh), lambda b, g, i: (b, i, g)),
            cmp_kv, cmp_kv, full_kv, full_kv, full_kv, full_kv,
            pl.BlockSpec((1, 1, TQ, LANE), lambda b, g, i: (b, g, i, 0)),
            pl.BlockSpec((HPG, n_cp, TQ), lambda b, g, i: (g, 0, i)),
            pl.BlockSpec((HPG, BIAS_ROWS, _toep_width(S)), lambda b, g, i: (g, 0, 0)),
        ],
        out_specs=pl.BlockSpec((1, TQ, HPG * dh), lambda b, g, i: (b, i, g)),
        out_shape=jax.ShapeDtypeStruct((B, S, NSA_Q_WIDTH), BF16),
        scratch_shapes=[pltpu.VMEM((S // NSA_SLC_BLOCK, TQ), F32)]
        + [pltpu.VMEM((TQ, HPG * TQ), F32)] * 2 + [pltpu.VMEM((TQ, HPG * TQ), BF16)] * 2,
        compiler_params=_cparams(("arbitrary", "arbitrary", "arbitrary"), 48),
        name="nsa_attn",
    )(q, kc, vc, ks, vs, kw, vw, gates, cmpb, toep)


def _router_kernel(h_ref, wr_ref, e_ref, w_ref, cnt_ref):
    @pl.when(pl.program_id(0) == 0)
    def _():
        cnt_ref[...] = jnp.zeros(cnt_ref.shape, F32)

    logits = jnp.dot(h_ref[...], wr_ref[...], precision=lax.Precision.HIGHEST, preferred_element_type=F32)
    tm = logits.shape[0]
    lane = lax.broadcasted_iota(jnp.int32, logits.shape, 1)
    logits = jnp.where(lane < N_EXPERTS, logits, NEG_INF)
    l1 = jnp.max(logits, axis=-1, keepdims=True)
    i1 = jnp.min(jnp.where(logits == l1, lane, LANE), axis=-1, keepdims=True)
    rest = jnp.where(lane == i1, NEG_INF, logits)
    l2 = jnp.max(rest, axis=-1, keepdims=True)
    i2 = jnp.min(jnp.where(rest == l2, lane, LANE), axis=-1, keepdims=True)
    e2 = jnp.exp(l2 - l1)
    den = 1.0 + e2
    w_ref[...] = jnp.where(lane == 0, 1.0 / den, jnp.where(lane == 1, e2 / den, 0.0))

    oh1 = lane == i1
    oh2 = lane == i2
    both = jnp.where(oh1 | oh2, 1.0, 0.0)
    earlier = lax.broadcasted_iota(jnp.int32, (tm, tm), 1) < lax.broadcasted_iota(jnp.int32, (tm, tm), 0)
    prefix = jnp.dot(jnp.where(earlier, 1.0, 0.0).astype(BF16), both.astype(BF16),
                     preferred_element_type=F32)
    seen = prefix + cnt_ref[0:1, :]
    p1 = jnp.sum(jnp.where(oh1, seen, 0.0), axis=-1, keepdims=True).astype(jnp.int32)
    p2 = jnp.sum(jnp.where(oh2, seen, 0.0), axis=-1, keepdims=True).astype(jnp.int32)
    e_ref[...] = jnp.where(lane == 0, i1, jnp.where(lane == 1, i2, jnp.where(lane == 2, p1,
                                                                              jnp.where(lane == 3, p2, 0))))
    cnt_ref[...] = cnt_ref[...] + jnp.sum(both, axis=0, keepdims=True)


def moe_route(h2, wr_pad, tm=512):
    N, D = h2.shape
    return pl.pallas_call(
        _router_kernel,
        grid=(N // tm,),
        in_specs=[pl.BlockSpec((tm, D), lambda i: (i, 0)), pl.BlockSpec(wr_pad.shape, lambda i: (0, 0))],
        out_specs=[pl.BlockSpec((tm, LANE), lambda i: (i, 0)), pl.BlockSpec((tm, LANE), lambda i: (i, 0)),
                   pl.BlockSpec((8, LANE), lambda i: (0, 0))],
        out_shape=[jax.ShapeDtypeStruct((N, LANE), jnp.int32), jax.ShapeDtypeStruct((N, LANE), F32),
                   jax.ShapeDtypeStruct((8, LANE), F32)],
        compiler_params=_cparams(("arbitrary",)),
        name="moe_router",
    )(h2, wr_pad)


def _dispatch_kernel(dest_ref, h_ref, xs_in_ref, xs_ref, sem):
    del xs_in_ref
    tm = h_ref.shape[0]
    base = pl.program_id(0) * tm * TOP_K

    def issue(r, carry):
        for k in range(TOP_K):
            d = dest_ref[base + r * TOP_K + k]
            pltpu.make_async_copy(h_ref.at[r], xs_ref.at[d], sem).start()
        return carry

    lax.fori_loop(0, tm, issue, 0, unroll=16)
    for k in range(TOP_K):
        pltpu.make_async_copy(h_ref, xs_ref.at[pl.ds(0, tm)], sem).wait()


def moe_dispatch_rows(h2, dest_flat, n_rows, tm=512):
    N, D = h2.shape
    xs0 = jnp.zeros((n_rows, D), h2.dtype)
    grid_spec = pltpu.PrefetchScalarGridSpec(
        num_scalar_prefetch=1,
        grid=(N // tm,),
        in_specs=[pl.BlockSpec((tm, D), lambda i, d: (i, 0)), pl.BlockSpec(memory_space=pl.ANY)],
        out_specs=pl.BlockSpec(memory_space=pl.ANY),
        scratch_shapes=[pltpu.SemaphoreType.DMA],
    )
    return pl.pallas_call(
        _dispatch_kernel,
        grid_spec=grid_spec,
        out_shape=jax.ShapeDtypeStruct((n_rows, D), h2.dtype),
        input_output_aliases={2: 0},
        compiler_params=_cparams(("arbitrary",)),
        name="moe_dispatch",
    )(dest_flat, h2, xs0)


def _combine_ln_kernel(dest_ref, w_ref, ys_ref, x_ref, g_ref, lng_ref, lnb_ref, *rest, has_next):
    if has_next:
        sc_ref, sh_ref, o_ref, hn_ref, buf, sem = rest
    else:
        o_ref, buf, sem = rest
    tm = x_ref.shape[0]
    base = pl.program_id(0) * tm * TOP_K

    def issue(r, carry):
        for k in range(TOP_K):
            d = dest_ref[base + r * TOP_K + k]
            pltpu.make_async_copy(ys_ref.at[d], buf.at[k, r], sem).start()
        return carry

    lax.fori_loop(0, tm, issue, 0, unroll=16)
    for k in range(TOP_K):
        pltpu.make_async_copy(ys_ref.at[pl.ds(0, tm)], buf.at[k], sem).wait()
    w = w_ref[...]
    y = w[:, 0:1] * buf[0] + w[:, 1:2] * buf[1]
    xn = _deepnorm_ln(x_ref[...], y, g_ref[0], lng_ref[...], lnb_ref[...])
    o_ref[...] = xn
    if has_next:
        hn_ref[...] = (xn * (1.0 + sc_ref[0]) + sh_ref[0]).astype(hn_ref.dtype)


def moe_combine_ln(ys, dest_flat, w_top, x, gate, ln_g, ln_b, nxt, tm=512):
    B, S, D = x.shape
    N = B * S
    per_b = S // tm
    tile = pl.BlockSpec((tm, D), lambda i, d: (i, 0))
    vec = pl.BlockSpec((1, 1, D), lambda i, d: (i // per_b, 0, 0))
    par = pl.BlockSpec((1, D), lambda i, d: (0, 0))
    in_specs = [pl.BlockSpec((tm, LANE), lambda i, d: (i, 0)), pl.BlockSpec(memory_space=pl.ANY), tile, vec, par, par]
    args = [w_top, ys, x.reshape(N, D), gate, ln_g, ln_b]
    out_shape = [jax.ShapeDtypeStruct((N, D), F32)]
    out_specs = [tile]
    if nxt is not None:
        in_specs += [vec, vec]
        args += list(nxt)
        out_shape.append(jax.ShapeDtypeStruct((N, D), BF16))
        out_specs.append(tile)
    grid_spec = pltpu.PrefetchScalarGridSpec(
        num_scalar_prefetch=1,
        grid=(N // tm,),
        in_specs=in_specs,
        out_specs=out_specs,
        scratch_shapes=[pltpu.VMEM((TOP_K, tm, D), F32), pltpu.SemaphoreType.DMA],
    )
    res = pl.pallas_call(
        functools.partial(_combine_ln_kernel, has_next=nxt is not None),
        grid_spec=grid_spec,
        out_shape=out_shape,
        compiler_params=_cparams(("arbitrary",)),
        name="moe_combine_ln",
    )(dest_flat, *args)
    x_new = res[0].reshape(B, S, D)
    return (x_new, res[1].reshape(B, S, D)) if nxt is not None else (x_new, None)


def _moe_ffn_kernel(be_ref, nu_ref, x_ref, w1_ref, w3_ref, w2_ref, o_ref, acc_sc, xb_sc):
    i = pl.program_id(0)
    j = pl.program_id(1)
    last = pl.num_programs(1) - 1
    used = i < nu_ref[0]

    @pl.when(j == 0)
    def _():
        acc_sc[...] = jnp.zeros(acc_sc.shape, F32)
        xb_sc[...] = x_ref[...].astype(BF16)

    @pl.when(used)
    def _():
        x = xb_sc[...]
        a = jnp.dot(x, w1_ref[0].astype(BF16), preferred_element_type=F32)
        b = jnp.dot(x, w3_ref[0].astype(BF16), preferred_element_type=F32)
        u = (a * jax.nn.sigmoid(a)) * b
        acc_sc[...] += jnp.dot(u.astype(BF16), w2_ref[0].astype(BF16), preferred_element_type=F32)

    @pl.when(j == last)
    def _():
        o_ref[...] = acc_sc[...]


def moe_ffn(xs, blk_e, n_used, w1, w3, w2, tf=512):
    n_rows, D = xs.shape
    n_blocks = blk_e.shape[0]
    RB = n_rows // n_blocks
    F = w1.shape[2]
    nf = F // tf

    def wcol(i, j, be, nu):
        return (be[i], 0, jnp.where(i < nu[0], j, nf - 1))

    def wrow(i, j, be, nu):
        return (be[i], jnp.where(i < nu[0], j, nf - 1), 0)

    grid_spec = pltpu.PrefetchScalarGridSpec(
        num_scalar_prefetch=2,
        grid=(n_blocks, nf),
        in_specs=[
            pl.BlockSpec((RB, D), lambda i, j, be, nu: (i, 0)),
            pl.BlockSpec((1, D, tf), wcol),
            pl.BlockSpec((1, D, tf), wcol),
            pl.BlockSpec((1, tf, D), wrow),
        ],
        out_specs=pl.BlockSpec((RB, D), lambda i, j, be, nu: (i, 0)),
        scratch_shapes=[pltpu.VMEM((RB, D), F32), pltpu.VMEM((RB, D), BF16)],
    )
    return pl.pallas_call(
        _moe_ffn_kernel,
        grid_spec=grid_spec,
        out_shape=jax.ShapeDtypeStruct((n_rows, D), F32),
        compiler_params=_cparams(("arbitrary", "arbitrary"), 56),
        name="moe_ffn",
    )(blk_e, n_used, xs, w1, w3, w2)


def moe_layout(e_pos, counts_f, n_tok):
    RB = _moe_row_block(n_tok)
    n_assign = n_tok * TOP_K
    n_blocks = (n_assign + N_EXPERTS * (RB - 1) + RB - 1) // RB
    counts = counts_f[0, :N_EXPERTS].astype(jnp.int32)
    padded = (counts + RB - 1) // RB * RB
    pad_end = jnp.cumsum(padded)
    pad_start = pad_end - padded
    e = e_pos[:, 0:TOP_K]
    pos = e_pos[:, TOP_K:2 * TOP_K]
    start = jnp.zeros_like(e)
    for x in range(N_EXPERTS):
        start = jnp.where(e == x, pad_start[x], start)
    dest_flat = (start + pos).reshape(-1)
    blk_start = jnp.arange(n_blocks, dtype=jnp.int32) * RB
    blk_e = jnp.minimum(jnp.sum((blk_start[:, None] >= pad_end[None, :]).astype(jnp.int32), axis=1), N_EXPERTS - 1)
    n_used = (pad_end[-1] // RB).astype(jnp.int32).reshape(1)
    return dest_flat, blk_e.astype(jnp.int32), n_used, n_blocks * RB


def _pad_cols(w, width):
    return jnp.pad(w, ((0, 0), (0, width - w.shape[1])))


def kernel(x, c, rel_bias, ada_w, ada_b, ln_g, ln_b, ab_w_in, gla_wg2, gla_bg2, gla_norm_g, ab_w_out,
           ffn_w1, ffn_w3, ffn_w2, nsa_w_in, nsa_cmp_pos, nsa_cmp_w1k, nsa_cmp_w2k, nsa_cmp_w1v, nsa_cmp_w2v,
           nsa_w_out, moe_router, moe_w1, moe_w3, moe_w2):
    B, S, D = x.shape
    n_tok = B * S
    mod = adaln_mod(c, ada_w, ada_b)
    mods = [[m.reshape(B, 1, D) for m in jnp.split(mod[i], 6, axis=-1)] for i in range(DEPTH)]
    toep, cmpb = build_bias_tables(rel_bias, S)

    sh1, sc1 = mods[0][0], mods[0][1]
    h = modulate(x, sc1, sh1)
    for i in range(DEPTH):
        j = i // 2
        _, _, g1, sh2, sc2, g2 = mods[i]
        nxt2 = (mods[i + 1][1], mods[i + 1][0]) if i + 1 < DEPTH else None
        lng = ln_g[i].reshape(2, 1, D)
        lnb = ln_b[i].reshape(2, 1, D)
        if i % 2 == 0:
            w = ab_w_in[j]
            cuts = np.cumsum((MOBA_WIDTH, MOBA_WIDTH, MOBA_WIDTH, GLA_QK_WIDTH, GLA_QK_WIDTH, GLA_V_WIDTH,
                              GLA_V_WIDTH))
            parts = jnp.split(w, cuts.tolist(), axis=1)
            parts[-1] = _pad_cols(parts[-1], LANE)
            w_cat = jnp.concatenate(parts, axis=1).astype(BF16)
            widths = [p.shape[1] for p in parts]
            offs = np.concatenate([[0], np.cumsum(widths)[:-1]]).tolist()
            scales = [MOBA_HEAD_DIM ** -0.5 * LOG2E] + [1.0] * 7
            kinds = ["flat", "flat", "flat_t"] + ["flat"] * 5
            offs[2] = 0
            segs = tuple((o, wd, kd, s) for o, wd, kd, s in zip(offs, widths, kinds, scales))
            dts = [BF16, BF16, BF16, F32, F32, F32, F32, F32]
            mq, mk, mv, gq, gk, gv, gr, glr = project(h, w_cat, segs, dts, wt=parts[2].T.astype(BF16))
            y_moba = moba_attention(mq, mk, mv, toep)
            wg2_pad = jnp.pad(gla_wg2[j], ((0, LANE - GLA_GATE_RANK), (0, 0)))
            y_gla = gla_mixer(gq, gk, gv, gr, glr, wg2_pad, gla_bg2[j].reshape(1, -1), gla_norm_g[j].reshape(1, -1))
            w_out = ab_w_out[j].astype(BF16)
            x, h = out_ln([y_moba, y_gla], [w_out[:MOBA_WIDTH], w_out[MOBA_WIDTH:]], x, g1, lng[0], lnb[0],
                          (sc2, sh2))
            x, h = ffn_ln(h, ffn_w1[j].astype(BF16), ffn_w3[j].astype(BF16), ffn_w2[j].astype(BF16), x, g2,
                          lng[1], lnb[1], nxt2)
        else:
            G, HPG, dh = NSA_KV_GROUPS, NSA_HPG, NSA_HEAD_DIM
            w = nsa_w_in[j]
            wq = w[:, :NSA_Q_WIDTH]
            wkv = w[:, NSA_Q_WIDTH:NSA_Q_WIDTH + 6 * NSA_KV_WIDTH]
            wg = w[:, NSA_Q_WIDTH + 6 * NSA_KV_WIDTH:].reshape(D, G, HPG, 3)
            wg = _pad_cols(wg.transpose(0, 1, 3, 2).reshape(D * G, 3 * HPG), LANE).reshape(D, G * LANE)
            w_cat = jnp.concatenate([wq, wkv, wg], axis=1).astype(BF16)
            segs = [(0, NSA_Q_WIDTH, "flat", dh ** -0.5 * LOG2E)]
            segs += [(NSA_Q_WIDTH + n * NSA_KV_WIDTH, NSA_KV_WIDTH, "group", 1.0) for n in range(6)]
            segs += [(NSA_Q_WIDTH + 6 * NSA_KV_WIDTH, G * LANE, "group", 1.0)]
            dts = [BF16] * 7 + [F32]
            q, kc, vc, ks, vs, kw, vw, gates = project(h, w_cat, tuple(segs), dts)
            half = NSA_CMP_STRIDE * dh
            pos2 = jnp.pad(nsa_cmp_pos[j].reshape(2, half), ((0, 6), (0, 0))).astype(BF16)
            w1k = jnp.concatenate([nsa_cmp_w1k[j][:half], nsa_cmp_w1k[j][half:]], axis=1).astype(BF16)
            w1v = jnp.concatenate([nsa_cmp_w1v[j][:half], nsa_cmp_w1v[j][half:]], axis=1).astype(BF16)
            k_cmp = nsa_compress(kc, pos2, w1k, nsa_cmp_w2k[j].astype(BF16))
            v_cmp = nsa_compress(vc, pos2, w1v, nsa_cmp_w2v[j].astype(BF16))
            y = nsa_attention(q, k_cmp, v_cmp, ks, vs, kw, vw, gates, cmpb, toep)
            x, h = out_ln([y], [nsa_w_out[j].astype(BF16)], x, g1, lng[0], lnb[0], (sc2, sh2), h_dtype=F32)
            h2 = h.reshape(n_tok, D)
            e_pos, w_top, counts = moe_route(h2, _pad_cols(moe_router[j], LANE))
            dest_flat, blk_e, n_used, n_rows = moe_layout(e_pos, counts, n_tok)
            xs = moe_dispatch_rows(h2, dest_flat, n_rows)
            stack = lambda w: w.reshape((-1,) + w.shape[2:])
            ys = moe_ffn(xs, blk_e + j * N_EXPERTS, n_used, stack(moe_w1), stack(moe_w3), stack(moe_w2))
            x, h = moe_combine_ln(ys, dest_flat, w_top, x, g2, lng[1], lnb[1], nxt2)
    return x
```

```python
import functools
import math

import jax
import jax.numpy as jnp
import numpy as np
from jax import lax
from jax.experimental import pallas as pl
from jax.experimental.pallas import tpu as pltpu

F32 = jnp.float32
BF16 = jnp.bfloat16
NEG_INF = float("-inf")

DEPTH = 4
ALPHA_DN = (2 * DEPTH) ** 0.25
LN_EPS = 1e-5
RMS_EPS = 1e-6

N_BUCKETS = 32
MAX_DISTANCE = 1024

MOBA_HEADS = 8
MOBA_HEAD_DIM = 64
MOBA_BLOCK = 256
MOBA_TOPK = 3
MOBA_WIDTH = MOBA_HEADS * MOBA_HEAD_DIM

GLA_HEADS = 4
GLA_DK = 64
GLA_DV = 128
GLA_GATE_RANK = 16
GLA_TAU = 16.0
GLA_CHUNK = 64
GLA_QK_WIDTH = GLA_HEADS * GLA_DK
GLA_V_WIDTH = GLA_HEADS * GLA_DV

NSA_HEADS = 8
NSA_KV_GROUPS = 2
NSA_HPG = NSA_HEADS // NSA_KV_GROUPS
NSA_HEAD_DIM = 128
NSA_CMP_LEN = 32
NSA_CMP_STRIDE = 16
NSA_SLC_BLOCK = 64
NSA_SLC_TOPN = 16
NSA_WINDOW = 512
NSA_Q_WIDTH = NSA_HEADS * NSA_HEAD_DIM
NSA_KV_WIDTH = NSA_KV_GROUPS * NSA_HEAD_DIM

N_EXPERTS = 8
TOP_K = 2
MOE_BLOCKS_PER_EXPERT = 4


def _moe_row_block(n_tok):
    balanced = n_tok * TOP_K // N_EXPERTS
    rows = balanced // MOE_BLOCKS_PER_EXPERT + balanced // 128
    return -(-rows // 16) * 16

LOG2E = math.log2(math.e)
LANE = 128
BIAS_ROWS = 128


def _cparams(sem, vmem_mib=40, **extra):
    return pltpu.CompilerParams(dimension_semantics=sem, vmem_limit_bytes=vmem_mib << 20, **extra)


def _adaln_kernel(c_ref, w_ref, b_ref, o_ref):
    c = c_ref[...]
    cond = c * jax.nn.sigmoid(c)
    o_ref[0] = jnp.dot(cond, w_ref[0], preferred_element_type=F32) + b_ref[0]


def adaln_mod(c, ada_w, ada_b):
    B, D = c.shape
    L, _, N = ada_w.shape
    rows = 8
    cp = jnp.zeros((rows, D), F32).at[:B].set(c)
    tn = N // 4
    out = pl.pallas_call(
        _adaln_kernel,
        grid=(L, N // tn),
        in_specs=[
            pl.BlockSpec((rows, D), lambda l, j: (0, 0)),
            pl.BlockSpec((1, D, tn), lambda l, j: (l, 0, j)),
            pl.BlockSpec((1, 1, tn), lambda l, j: (l, 0, j)),
        ],
        out_specs=pl.BlockSpec((1, rows, tn), lambda l, j: (l, 0, j)),
        out_shape=jax.ShapeDtypeStruct((L, rows, N), F32),
        compiler_params=_cparams(("arbitrary", "arbitrary")),
        name="adaln_mod",
    )(cp, ada_w, ada_b.reshape(L, 1, N))
    return out[:, :B]


def _modulate_kernel(x_ref, sc_ref, sh_ref, o_ref):
    o_ref[0] = (x_ref[0] * (1.0 + sc_ref[0]) + sh_ref[0]).astype(o_ref.dtype)


def modulate(x, sc, sh, tm=512):
    B, S, D = x.shape
    return pl.pallas_call(
        _modulate_kernel,
        grid=(B, S // tm),
        in_specs=[
            pl.BlockSpec((1, tm, D), lambda b, i: (b, i, 0)),
            pl.BlockSpec((1, 1, D), lambda b, i: (b, 0, 0)),
            pl.BlockSpec((1, 1, D), lambda b, i: (b, 0, 0)),
        ],
        out_specs=pl.BlockSpec((1, tm, D), lambda b, i: (b, i, 0)),
        out_shape=jax.ShapeDtypeStruct((B, S, D), BF16),
        compiler_params=_cparams(("arbitrary", "arbitrary")),
        name="modulate",
    )(x, sc, sh)


def _t5_bucket(dist):
    max_exact = N_BUCKETS // 2
    n = jnp.maximum(dist, 0)
    nf = jnp.maximum(n, 1).astype(jnp.float32)
    large = max_exact + (jnp.log(nf / max_exact) / math.log(MAX_DISTANCE / max_exact) * (N_BUCKETS - max_exact)).astype(jnp.int32)
    large = jnp.minimum(large, N_BUCKETS - 1)
    return jnp.where(n < max_exact, n, large)


def _bias_table_kernel(rb_ref, bm_ref, o_ref, *, rows_per_step):
    h = pl.program_id(0)
    n_steps = bm_ref.shape[0] // rows_per_step

    def body(i, carry):
        r0 = pl.multiple_of(i * rows_per_step, rows_per_step)
        bm = bm_ref[pl.ds(r0, rows_per_step), :]
        acc = jnp.zeros(bm.shape, F32)
        for b in range(N_BUCKETS):
            acc = jnp.where(bm == b, rb_ref[b, h] * LOG2E, acc)
        o_ref[0, pl.ds(r0, rows_per_step), :] = acc
        return carry

    lax.fori_loop(0, n_steps, body, 0)


def bias_table(rel_bias, bucket_map, rows_per_step):
    R, W = bucket_map.shape
    H = rel_bias.shape[1]
    return pl.pallas_call(
        functools.partial(_bias_table_kernel, rows_per_step=rows_per_step),
        grid=(H,),
        in_specs=[
            pl.BlockSpec(memory_space=pltpu.SMEM),
            pl.BlockSpec((R, W), lambda h: (0, 0)),
        ],
        out_specs=pl.BlockSpec((1, R, W), lambda h: (h, 0, 0)),
        out_shape=jax.ShapeDtypeStruct((H, R, W), F32),
        compiler_params=_cparams(("arbitrary",)),
        name="bias_table",
    )(rel_bias, bucket_map)


def _toep_width(S):
    return S + BIAS_ROWS


def _toep_col(q0, k0):
    return pl.multiple_of(q0 - k0 + BIAS_ROWS, LANE)


def build_bias_tables(rel_bias, S):
    j = jnp.arange(BIAS_ROWS, dtype=jnp.int32)[:, None]
    c = jnp.arange(_toep_width(S), dtype=jnp.int32)[None, :]
    toep_map = _t5_bucket(c - j - BIAS_ROWS)
    n_cmp_pad = S // NSA_CMP_STRIDE
    t = jnp.arange(S, dtype=jnp.int32)[None, :]
    cmp_end = jnp.arange(n_cmp_pad, dtype=jnp.int32)[:, None] * NSA_CMP_STRIDE + NSA_CMP_LEN - 1
    cmp_map = _t5_bucket(t - cmp_end)
    return bias_table(rel_bias, toep_map, 8), bias_table(rel_bias, cmp_map, 8)


def _proj_kernel(h_ref, w_ref, wt_ref, *o_refs, segs):
    h = h_ref[0]
    for (off, width, kind, scale), o_ref in zip(segs, o_refs):
        if kind == "flat_t":
            acc = lax.dot_general(wt_ref[off:off + width, :], h, _NT, preferred_element_type=F32)
        else:
            acc = jnp.dot(h, w_ref[:, off:off + width], preferred_element_type=F32)
        if scale != 1.0:
            acc = acc * scale
        if kind in ("flat", "flat_t"):
            o_ref[0] = acc.astype(o_ref.dtype)
        else:
            for g in range(width // LANE):
                o_ref[0, g] = acc[:, g * LANE:(g + 1) * LANE].astype(o_ref.dtype)


def project(h, w_cat, segs, out_dtypes, wt=None, tm=512):
    B, S, D = h.shape
    if wt is None:
        wt = jnp.zeros((8, D), w_cat.dtype)
    out_shapes, out_specs = [], []
    for (off, width, kind, scale), dt in zip(segs, out_dtypes):
        if kind == "flat":
            out_shapes.append(jax.ShapeDtypeStruct((B, S, width), dt))
            out_specs.append(pl.BlockSpec((1, tm, width), lambda b, i: (b, i, 0)))
        elif kind == "flat_t":
            out_shapes.append(jax.ShapeDtypeStruct((B, width, S), dt))
            out_specs.append(pl.BlockSpec((1, width, tm), lambda b, i: (b, 0, i)))
        else:
            G = width // LANE
            out_shapes.append(jax.ShapeDtypeStruct((B, G, S, LANE), dt))
            out_specs.append(pl.BlockSpec((1, G, tm, LANE), lambda b, i: (b, 0, i, 0)))
    return pl.pallas_call(
        functools.partial(_proj_kernel, segs=segs),
        grid=(B, S // tm),
        in_specs=[
            pl.BlockSpec((1, tm, D), lambda b, i: (b, i, 0)),
            pl.BlockSpec(w_cat.shape, lambda b, i: (0, 0)),
            pl.BlockSpec(wt.shape, lambda b, i: (0, 0)),
        ],
        out_specs=out_specs,
        out_shape=out_shapes,
        compiler_params=_cparams(("arbitrary", "arbitrary"), 48),
        name="in_proj",
    )(h, w_cat, wt)


_NT = (((1,), (1,)), ((), ()))
_TN = (((0,), (0,)), ((), ()))


def _softmax_step(carry, s, v_tile):
    m, l, acc = carry
    m_new = jnp.maximum(m, jnp.max(s, axis=0, keepdims=True))
    alpha = jnp.exp2(m - m_new)
    p = jnp.exp2(s - m_new)
    l_new = alpha * l + jnp.sum(p, axis=0, keepdims=True)
    acc_new = alpha * acc + lax.dot_general(v_tile, p.astype(BF16), _TN, preferred_element_type=F32)
    return m_new, l_new, acc_new


def _moba_kernel(q_ref, k_ref, v_ref, tb_ref, o_ref, kmean_sc, sel_sc, s0_sc, s1_sc, p0_sc, p1_sc,
                 *, S, heads_per_step):
    L = MOBA_BLOCK
    dh = MOBA_HEAD_DIM
    nb = S // L
    qi = pl.program_id(2)
    q0 = qi * L

    @pl.when(qi == 0)
    def _():
        kf = k_ref[0].astype(F32)
        kmean_sc[...] = jnp.mean(kf.reshape(nb, L, kf.shape[-1]), axis=1)

    q2 = q_ref[0]
    lane_q = lax.broadcasted_iota(jnp.int32, q2.shape, 1)
    lane_m = lax.broadcasted_iota(jnp.int32, kmean_sc.shape, 1)
    blk = lax.broadcasted_iota(jnp.int32, (nb, L), 0)
    qz = []
    for hh in range(heads_per_step):
        in_head = (lane_q >= hh * dh) & (lane_q < (hh + 1) * dh)
        qz.append(jnp.where(in_head, q2, jnp.zeros_like(q2)))
        km = jnp.where((lane_m >= hh * dh) & (lane_m < (hh + 1) * dh), kmean_sc[...], 0.0)
        gate = lax.dot_general(km, q2.astype(F32), _NT, precision=lax.Precision.HIGHEST,
                               preferred_element_type=F32)
        valid = blk < qi
        gm = jnp.where(valid, gate, NEG_INF)
        rank = jnp.zeros((nb, L), jnp.int32)
        for m in range(nb):
            gmm = gm[m:m + 1, :]
            beats = (gmm > gm) | ((gmm == gm) & (m < blk))
            rank = rank + beats.astype(jnp.int32)
        sel_sc[hh] = jnp.where(valid & (rank < MOBA_TOPK), 1.0, 0.0)

    def bias_tile(hh, k0, n_keys):
        return jnp.concatenate([tb_ref[hh, :, pl.ds(_toep_col(q0, k0 + r * BIAS_ROWS), L)]
                                for r in range(n_keys // BIAS_ROWS)], axis=0)

    krow = lax.broadcasted_iota(jnp.int32, (L, L), 0)
    qcol = lax.broadcasted_iota(jnp.int32, (L, L), 1)
    k_own = k_ref[0, pl.ds(pl.multiple_of(q0, L), L), :]
    v_tile = lambda hh, k0, n_keys: v_ref[0, hh * dh:(hh + 1) * dh, pl.ds(k0, n_keys)]
    carry = []
    for hh in range(heads_per_step):
        s = lax.dot_general(k_own, qz[hh], _NT, preferred_element_type=F32) + bias_tile(hh, q0, L)
        s = jnp.where(krow <= qcol, s, NEG_INF)
        m0 = jnp.max(s, axis=0, keepdims=True)
        p = jnp.exp2(s - m0)
        l0 = jnp.sum(p, axis=0, keepdims=True)
        acc0 = jnp.dot(v_tile(hh, pl.multiple_of(q0, L), L), p.astype(BF16), preferred_element_type=F32)
        carry += [m0, l0, acc0, jnp.ones((1, L), F32)]

    n = (qi + 1) // 2
    s_bufs = (s0_sc, s1_sc)
    p_bufs = (p0_sc, p1_sc)
    tile_k0 = lambda i: pl.multiple_of(jnp.clip(i, 0, jnp.maximum(n - 1, 0)) * 2 * L, 2 * L)

    def put_scores(i, buf):
        kt = k_ref[0, pl.ds(tile_k0(i), 2 * L), :]
        for hh in range(heads_per_step):
            buf[hh] = lax.dot_general(kt, qz[hh], _NT, preferred_element_type=F32)

    put_scores(0, s_bufs[0])
    p_bufs[1][...] = jnp.zeros(p_bufs[1].shape, BF16)

    def half(i, cur, carry):
        put_scores(i + 1, s_bufs[1 - cur])
        k0 = tile_k0(i)
        blk0 = lax.shift_right_logical(k0, L.bit_length() - 1)
        thresh = jnp.where(i < n, 0.5, 2.0)
        out = []
        for hh in range(heads_per_step):
            m, l, acc, alpha_prev = carry[4 * hh:4 * hh + 4]
            flags = jnp.concatenate(
                [jnp.broadcast_to(sel_sc[hh, pl.ds(blk0 + j, 1), :], (L, L)) for j in range(2)], axis=0)
            s = jnp.where(flags > thresh, s_bufs[cur][hh] + bias_tile(hh, k0, 2 * L), NEG_INF)
            m_new = jnp.maximum(m, jnp.max(s, axis=0, keepdims=True))
            alpha = jnp.exp2(m - m_new)
            p = jnp.exp2(s - m_new)
            l_new = alpha * l + jnp.sum(p, axis=0, keepdims=True)
            p_bufs[cur][hh] = p.astype(BF16)
            pv = jnp.dot(v_tile(hh, tile_k0(i - 1), 2 * L), p_bufs[1 - cur][hh], preferred_element_type=F32)
            out += [m_new, l_new, alpha_prev * acc + pv, alpha]
        return tuple(out)

    def body(jj, carry):
        return half(2 * jj + 1, 1, half(2 * jj, 0, carry))

    n_pairs = (n + 1) // 2
    carry = lax.fori_loop(0, n_pairs, body, tuple(carry))
    outs = []
    for hh in range(heads_per_step):
        _, l, acc, alpha = carry[4 * hh:4 * hh + 4]
        pv = jnp.dot(v_tile(hh, tile_k0(2 * n_pairs - 1), 2 * L), p_bufs[1][hh], preferred_element_type=F32)
        outs.append((alpha * acc + pv) * (1.0 / l))
    o_ref[0] = jnp.concatenate(outs, axis=0).T.astype(o_ref.dtype)


def moba_attention(q, k, v, toep):
    B, S, _ = q.shape
    L = MOBA_BLOCK
    hps = 4
    wl = hps * MOBA_HEAD_DIM
    n_hp = MOBA_HEADS // hps
    nb = S // L
    return pl.pallas_call(
        functools.partial(_moba_kernel, S=S, heads_per_step=hps),
        grid=(n_hp, B, nb),
        in_specs=[
            pl.BlockSpec((1, L, wl), lambda hp, b, i: (b, i, hp)),
            pl.BlockSpec((1, S, wl), lambda hp, b, i: (b, 0, hp)),
            pl.BlockSpec((1, wl, S), lambda hp, b, i: (b, hp, 0)),
            pl.BlockSpec((hps, BIAS_ROWS, _toep_width(S)), lambda hp, b, i: (hp, 0, 0)),
        ],
        out_specs=pl.BlockSpec((1, L, wl), lambda hp, b, i: (b, i, hp)),
        out_shape=jax.ShapeDtypeStruct((B, S, MOBA_WIDTH), BF16),
        scratch_shapes=[pltpu.VMEM((nb, wl), F32), pltpu.VMEM((hps, nb, L), F32)]
        + [pltpu.VMEM((hps, 2 * L, L), F32)] * 2 + [pltpu.VMEM((hps, 2 * L, L), BF16)] * 2,
        compiler_params=_cparams(("arbitrary", "arbitrary", "arbitrary"), 48),
        name="moba_attn",
    )(q, k, v, toep)


def _log_sigmoid(z):
    return jnp.minimum(z, 0.0) - jnp.log1p(jnp.exp(-jnp.abs(z)))


def _gla_kernel(q_ref, k_ref, v_ref, r_ref, lr_ref, wg2_ref, bg2_ref, ng_ref, o_ref, st_sc, *, batch):
    C = GLA_CHUNK
    dk, dv = GLA_DK, GLA_DV

    @pl.when(pl.program_id(0) == 0)
    def _():
        st_sc[...] = jnp.zeros(st_sc.shape, F32)

    row = lax.broadcasted_iota(jnp.int32, (C, C), 0)
    col = lax.broadcasted_iota(jnp.int32, (C, C), 1)
    causal = col <= row
    tri = jnp.where(causal, 1.0, 0.0)
    for b in range(batch):
        z = jnp.dot(lr_ref[b], wg2_ref[...], preferred_element_type=F32) + bg2_ref[...]
        la = _log_sigmoid(z) / GLA_TAU
        bc = jnp.dot(tri, la, precision=lax.Precision.HIGHEST, preferred_element_type=F32)
        b_last = bc[C - 1:C, :]
        qd = q_ref[b] * (dk ** -0.5) * jnp.exp(bc)
        kd = k_ref[b] * jnp.exp(-bc)
        kl = k_ref[b] * jnp.exp(b_last - bc)
        e_last = jnp.exp(b_last)
        v = v_ref[b]
        r = r_ref[b]
        outs = []
        for h in range(GLA_HEADS):
            ks = slice(h * dk, (h + 1) * dk)
            vs = slice(h * dv, (h + 1) * dv)
            qh = qd[:, ks].astype(BF16)
            kh = kd[:, ks].astype(BF16)
            vh = v[:, vs].astype(BF16)
            attn = lax.dot_general(qh, kh, (((1,), (1,)), ((), ())), preferred_element_type=F32)
            attn = jnp.where(causal, attn, 0.0)
            st = st_sc[b, h]
            o = jnp.dot(attn.astype(BF16), vh, preferred_element_type=F32)
            o = o + lax.dot_general(qh, st.astype(BF16), (((1,), (1,)), ((), ())), preferred_element_type=F32)
            upd = lax.dot_general(vh, kl[:, ks].astype(BF16), (((0,), (0,)), ((), ())), preferred_element_type=F32)
            st_sc[b, h] = st * e_last[:, ks] + upd
            o = o * lax.rsqrt(jnp.mean(o * o, axis=-1, keepdims=True) + RMS_EPS)
            rg = r[:, vs]
            outs.append(o * ng_ref[:, vs] * (rg * jax.nn.sigmoid(rg)))
        o_ref[b] = jnp.concatenate(outs, axis=-1).astype(o_ref.dtype)


def gla_mixer(q, k, v, r, lr, wg2_pad, bg2, norm_g):
    B, S, _ = q.shape
    C = GLA_CHUNK
    nc = S // C
    blk = lambda w: pl.BlockSpec((B, C, w), lambda c: (0, c, 0))
    full = lambda a: pl.BlockSpec(a.shape, lambda c: (0,) * a.ndim)
    return pl.pallas_call(
        functools.partial(_gla_kernel, batch=B),
        grid=(nc,),
        in_specs=[blk(GLA_QK_WIDTH), blk(GLA_QK_WIDTH), blk(GLA_V_WIDTH), blk(GLA_V_WIDTH), blk(LANE),
                  full(wg2_pad), full(bg2), full(norm_g)],
        out_specs=blk(GLA_V_WIDTH),
        out_shape=jax.ShapeDtypeStruct((B, S, GLA_V_WIDTH), BF16),
        scratch_shapes=[pltpu.VMEM((B, GLA_HEADS, GLA_DV, GLA_DK), F32)],
        compiler_params=_cparams(("arbitrary",)),
        name="gla",
    )(q, k, v, r, lr, wg2_pad, bg2, norm_g)


def _deepnorm_ln(x, y, gate, ln_g, ln_b):
    z = ALPHA_DN * x + (1.0 + gate) * y
    mu = jnp.mean(z, axis=-1, keepdims=True)
    zc = z - mu
    var = jnp.mean(zc * zc, axis=-1, keepdims=True)
    return zc * lax.rsqrt(var + LN_EPS) * ln_g + ln_b


def _out_ln_kernel(*refs, n_parts, matmul, has_next):
    y_refs = refs[:n_parts]
    pos = n_parts
    if matmul:
        w_refs = refs[pos:pos + n_parts]
        pos += n_parts
    x_ref, g_ref, lng_ref, lnb_ref = refs[pos:pos + 4]
    pos += 4
    if has_next:
        sc_ref, sh_ref = refs[pos:pos + 2]
        pos += 2
    o_ref = refs[pos]
    y = None
    for i in range(n_parts):
        part = jnp.dot(y_refs[i][0], w_refs[i][...], preferred_element_type=F32) if matmul else y_refs[i][0]
        y = part if y is None else y + part
    xn = _deepnorm_ln(x_ref[0], y, g_ref[0], lng_ref[...], lnb_ref[...])
    o_ref[0] = xn
    if has_next:
        refs[pos + 1][0] = (xn * (1.0 + sc_ref[0]) + sh_ref[0]).astype(refs[pos + 1].dtype)


def out_ln(ys, ws, x, gate, ln_g, ln_b, nxt, h_dtype=BF16, tm=512):
    B, S, D = x.shape
    tile = lambda w: pl.BlockSpec((1, tm, w), lambda b, i: (b, i, 0))
    vec = pl.BlockSpec((1, 1, D), lambda b, i: (b, 0, 0))
    par = pl.BlockSpec((1, D), lambda b, i: (0, 0))
    args = list(ys)
    in_specs = [tile(y.shape[-1]) for y in ys]
    if ws is not None:
        args += list(ws)
        in_specs += [pl.BlockSpec(w.shape, lambda b, i: (0, 0)) for w in ws]
    args += [x, gate, ln_g, ln_b]
    in_specs += [tile(D), vec, par, par]
    out_shape = [jax.ShapeDtypeStruct((B, S, D), F32)]
    out_specs = [tile(D)]
    if nxt is not None:
        args += list(nxt)
        in_specs += [vec, vec]
        out_shape.append(jax.ShapeDtypeStruct((B, S, D), h_dtype))
        out_specs.append(tile(D))
    res = pl.pallas_call(
        functools.partial(_out_ln_kernel, n_parts=len(ys), matmul=ws is not None, has_next=nxt is not None),
        grid=(B, S // tm),
        in_specs=in_specs,
        out_specs=out_specs,
        out_shape=out_shape,
        compiler_params=_cparams(("arbitrary", "arbitrary")),
        name="out_ln",
    )(*args)
    return (res[0], res[1]) if nxt is not None else (res[0], None)


def _ffn_ln_kernel(h_ref, w1_ref, w3_ref, w2_ref, x_ref, g_ref, lng_ref, lnb_ref, sc_ref, sh_ref,
                   o_ref, hn_ref, acc_sc):
    j = pl.program_id(2)

    @pl.when(j == 0)
    def _():
        acc_sc[...] = jnp.zeros(acc_sc.shape, F32)

    h = h_ref[0]
    a = jnp.dot(h, w1_ref[...], preferred_element_type=F32)
    b = jnp.dot(h, w3_ref[...], preferred_element_type=F32)
    u = (a * jax.nn.sigmoid(a)) * b
    acc_sc[...] += jnp.dot(u.astype(BF16), w2_ref[...], preferred_element_type=F32)

    @pl.when(j == pl.num_programs(2) - 1)
    def _():
        xn = _deepnorm_ln(x_ref[0], acc_sc[...], g_ref[0], lng_ref[...], lnb_ref[...])
        o_ref[0] = xn
        hn_ref[0] = (xn * (1.0 + sc_ref[0]) + sh_ref[0]).astype(BF16)


def ffn_ln(h, w1, w3, w2, x, gate, ln_g, ln_b, nxt, tm=512):
    B, S, D = x.shape
    F = w1.shape[1]
    tf = F // 2
    tile = pl.BlockSpec((1, tm, D), lambda b, i, j: (b, i, 0))
    vec = pl.BlockSpec((1, 1, D), lambda b, i, j: (b, 0, 0))
    par = pl.BlockSpec((1, D), lambda b, i, j: (0, 0))
    return pl.pallas_call(
        _ffn_ln_kernel,
        grid=(B, S // tm, F // tf),
        in_specs=[tile,
                  pl.BlockSpec((D, tf), lambda b, i, j: (0, j)),
                  pl.BlockSpec((D, tf), lambda b, i, j: (0, j)),
                  pl.BlockSpec((tf, D), lambda b, i, j: (j, 0)),
                  tile, vec, par, par, vec, vec],
        out_specs=[tile, tile],
        out_shape=[jax.ShapeDtypeStruct((B, S, D), F32), jax.ShapeDtypeStruct((B, S, D), BF16)],
        scratch_shapes=[pltpu.VMEM((tm, D), F32)],
        compiler_params=_cparams(("arbitrary", "arbitrary", "arbitrary"), 48),
        name="ffn_ln",
    )(h, w1, w3, w2, x, gate, ln_g, ln_b, nxt[0], nxt[1])


def _compress_kernel(x_ref, pos_ref, w1_ref, w2_ref, o_ref):
    dh = NSA_HEAD_DIM
    x = x_ref[0, 0]
    uv = jnp.dot(x, w1_ref[...], preferred_element_type=F32)
    pc = jnp.dot(pos_ref[...], w1_ref[...], preferred_element_type=F32)
    c = pc[0:1, :dh] + pc[1:2, dh:]
    n = uv.shape[0]
    nxt = pltpu.roll(uv[:, dh:], n - 1, 0)
    hid = uv[:, :dh] + nxt + c
    act = hid * jax.nn.sigmoid(hid)
    o_ref[0, 0] = jnp.dot(act.astype(BF16), w2_ref[...], preferred_element_type=F32).astype(o_ref.dtype)


def nsa_compress(kv, pos2, w1cat, w2):
    B, G, S, dh = kv.shape
    seg = NSA_CMP_STRIDE
    n = S // seg
    x = kv.reshape(B, G, n, seg * dh)
    return pl.pallas_call(
        _compress_kernel,
        grid=(B, G),
        in_specs=[
            pl.BlockSpec((1, 1, n, seg * dh), lambda b, g: (b, g, 0, 0)),
            pl.BlockSpec(pos2.shape, lambda b, g: (0, 0)),
            pl.BlockSpec(w1cat.shape, lambda b, g: (0, 0)),
            pl.BlockSpec(w2.shape, lambda b, g: (0, 0)),
        ],
        out_specs=pl.BlockSpec((1, 1, n, dh), lambda b, g: (b, g, 0, 0)),
        out_shape=jax.ShapeDtypeStruct((B, G, n, dh), BF16),
        compiler_params=_cparams(("arbitrary", "arbitrary")),
        name="nsa_compress",
    )(x, pos2, w1cat, w2)


def _nsa_kernel(q_ref, kc_ref, vc_ref, ks_ref, vs_ref, kw_ref, vw_ref, gt_ref, cb_ref, tb_ref, o_ref, sel_sc,
                s0_sc, s1_sc, p0_sc, p1_sc, *, S, TQ):
    dh = NSA_HEAD_DIM
    HPG = NSA_HPG
    LS = NSA_SLC_BLOCK
    TK = 4 * LS
    R = HPG * TQ
    n_cmp = (S - NSA_CMP_LEN) // NSA_CMP_STRIDE + 1
    n_cp = S // NSA_CMP_STRIDE
    n_slc = S // LS
    n_top = min(NSA_SLC_TOPN, n_slc)
    qi = pl.program_id(2)
    q0 = qi * TQ

    q4 = jnp.concatenate([q_ref[0, :, h * dh:(h + 1) * dh] for h in range(HPG)], axis=0)
    lanes4 = lambda a: jnp.concatenate([a] * HPG, axis=1)

    s = lax.dot_general(kc_ref[0, 0], q4, _NT, preferred_element_type=F32)
    s = s + jnp.concatenate([cb_ref[h] for h in range(HPG)], axis=1)
    n_row = lax.broadcasted_iota(jnp.int32, (n_cp, TQ), 0)
    t_q = q0 + lax.broadcasted_iota(jnp.int32, (n_cp, TQ), 1)
    valid_c = lanes4((n_row * NSA_CMP_STRIDE + (NSA_CMP_LEN - 1) <= t_q) & (n_row < n_cmp))
    s = jnp.where(valid_c, s, NEG_INF)
    m = jnp.max(s, axis=0, keepdims=True)
    m = jnp.where(m == NEG_INF, 0.0, m)
    e = jnp.exp2(s - m)
    p_c = e * (1.0 / jnp.maximum(jnp.sum(e, axis=0, keepdims=True), jnp.finfo(F32).tiny))
    o_c = lax.dot_general(vc_ref[0, 0], p_c.astype(BF16), _TN, preferred_element_type=F32)

    p_sum = p_c[:, 0:TQ]
    for h in range(1, HPG):
        p_sum = p_sum + p_c[:, h * TQ:(h + 1) * TQ]
    om = lax.broadcasted_iota(jnp.int32, (n_slc, n_cp), 0)
    c_lo = lax.broadcasted_iota(jnp.int32, (n_slc, n_cp), 1) * NSA_CMP_STRIDE
    ov = jnp.maximum(jnp.minimum(c_lo + NSA_CMP_LEN, om * LS + LS) - jnp.maximum(c_lo, om * LS), 0)
    overlap_t = ov.astype(F32) / NSA_CMP_LEN
    imp_t = jnp.dot(overlap_t, p_sum, precision=lax.Precision.HIGHEST, preferred_element_type=F32)
    blk = lax.broadcasted_iota(jnp.int32, (n_slc, TQ), 0)
    tq_lane = q0 + lax.broadcasted_iota(jnp.int32, (n_slc, TQ), 1)
    tb = lax.shift_right_logical(tq_lane, LS.bit_length() - 1)
    forced = (blk == 0) | (blk == tb) | (blk == tb - 1)
    key = jnp.where(blk > tb, NEG_INF, jnp.where(forced, jnp.inf, imp_t))
    rank = jnp.zeros((n_slc, TQ), jnp.int32)
    for mm in range(n_slc):
        km = key[mm:mm + 1, :]
        beats = (km > key) | ((km == key) & (mm < blk))
        rank = rank + beats.astype(jnp.int32)
    sel_sc[...] = jnp.where((rank < n_top) & (blk <= tb), 1.0, 0.0)

    def toep_tile(k0, n_keys):
        return jnp.concatenate(
            [jnp.concatenate([tb_ref[h, :, pl.ds(_toep_col(q0, k0 + r * BIAS_ROWS), TQ)] for h in range(HPG)], axis=1)
             for r in range(n_keys // BIAS_ROWS)], axis=0)

    def pipelined_attention(n, scores_fn, post_fn, v_fn):
        s_bufs = (s0_sc, s1_sc)
        p_bufs = (p0_sc, p1_sc)
        s_bufs[0][...] = scores_fn(0)
        p_bufs[1][...] = jnp.zeros(p_bufs[1].shape, BF16)

        def half(i, cur, carry):
            m, l, acc, alpha_prev = carry
            s_bufs[1 - cur][...] = scores_fn(jnp.minimum(i + 1, n - 1))
            s = post_fn(i, s_bufs[cur][...])
            m_new = jnp.maximum(m, jnp.max(s, axis=0, keepdims=True))
            alpha = jnp.exp2(m - m_new)
            p = jnp.exp2(s - m_new)
            l_new = alpha * l + jnp.sum(p, axis=0, keepdims=True)
            p_bufs[cur][...] = p.astype(BF16)
            pv = lax.dot_general(v_fn(jnp.maximum(i - 1, 0)), p_bufs[1 - cur][...], _TN, preferred_element_type=F32)
            return m_new, l_new, alpha_prev * acc + pv, alpha

        def body(jj, carry):
            return half(2 * jj + 1, 1, half(2 * jj, 0, carry))

        n_pairs = (n + 1) // 2
        init = (jnp.full((1, R), NEG_INF, F32), jnp.zeros((1, R), F32), jnp.zeros((dh, R), F32),
                jnp.ones((1, R), F32))
        _, l, acc, alpha = lax.fori_loop(0, n_pairs, body, init)
        pv = lax.dot_general(v_fn(jnp.minimum(2 * n_pairs - 1, n - 1)), p_bufs[1][...], _TN,
                             preferred_element_type=F32)
        return (alpha * acc + pv) * (1.0 / l)

    krow_s = lax.broadcasted_iota(jnp.int32, (TK, TQ), 0)
    tq_s = q0 + lax.broadcasted_iota(jnp.int32, (TK, TQ), 1)
    n_kt = (q0 + TQ - 1) // TK + 1

    def slc_scores(kt):
        return lax.dot_general(ks_ref[0, 0, pl.ds(pl.multiple_of(kt * TK, TK), TK), :], q4, _NT,
                               preferred_element_type=F32)

    def slc_post(kt, s):
        live = kt < n_kt
        ktc = jnp.minimum(kt, n_kt - 1)
        k0 = pl.multiple_of(ktc * TK, TK)
        flags = jnp.concatenate(
            [jnp.broadcast_to(sel_sc[pl.ds(ktc * (TK // LS) + j, 1), :], (LS, TQ)) for j in range(TK // LS)], axis=0)
        mask = lanes4((flags > jnp.where(live, 0.5, 2.0)) & (k0 + krow_s <= tq_s))
        return jnp.where(mask, s + toep_tile(k0, TK), NEG_INF)

    o_s = pipelined_attention(n_kt, slc_scores, slc_post,
                              lambda kt: vs_ref[0, 0, pl.ds(pl.multiple_of(kt * TK, TK), TK), :])

    krow_w = lax.broadcasted_iota(jnp.int32, (TQ, TQ), 0)
    tq_w = q0 + lax.broadcasted_iota(jnp.int32, (TQ, TQ), 1)
    n_wt = jnp.minimum(qi, NSA_WINDOW // TQ) + 1

    def win_body(j, carry):
        k0 = pl.multiple_of(q0 - j * TQ, TQ)
        dist = tq_w - (k0 + krow_w)
        mask = lanes4((dist >= 0) & (dist < NSA_WINDOW))
        s = lax.dot_general(kw_ref[0, 0, pl.ds(k0, TQ), :], q4, _NT, preferred_element_type=F32) + toep_tile(k0, TQ)
        return _softmax_step(carry, jnp.where(mask, s, NEG_INF), vw_ref[0, 0, pl.ds(k0, TQ), :])

    init_w = (jnp.full((1, R), NEG_INF, F32), jnp.zeros((1, R), F32), jnp.zeros((dh, R), F32))
    _, l_w, acc_w = lax.fori_loop(0, n_wt, win_body, init_w)
    o_w = acc_w * (1.0 / l_w)

    g_t = jax.nn.sigmoid(gt_ref[0, 0]).T
    outs = []
    for h in range(HPG):
        cols = slice(h * TQ, (h + 1) * TQ)
        o = (g_t[h:h + 1] * o_c[:, cols] + g_t[HPG + h:HPG + h + 1] * o_s[:, cols]
             + g_t[2 * HPG + h:2 * HPG + h + 1] * o_w[:, cols])
        outs.append(o.T)
    o_ref[0] = jnp.concatenate(outs, axis=1).astype(o_ref.dtype)


def nsa_attention(q, kc, vc, ks, vs, kw, vw, gates, cmpb, toep, TQ=256):
    B, S, _ = q.shape
    G, HPG, dh = NSA_KV_GROUPS, NSA_HPG, NSA_HEAD_DIM
    n_cp = S // NSA_CMP_STRIDE
    full_kv = pl.BlockSpec((1, 1, S, dh), lambda b, g, i: (b, g, 0, 0))
    cmp_kv = pl.BlockSpec((1, 1, n_cp, dh), lambda b, g, i: (b, g, 0, 0))
    return pl.pallas_call(
        functools.partial(_nsa_kernel, S=S, TQ=TQ),
        grid=(B, G, S // TQ),
        in_specs=[
            pl.BlockSpec((1, TQ, HPG * dh), lambda b, g, i: (b, i, g)),
            cmp_kv, cmp_kv, full_kv, full_kv, full_kv, full_kv,
            pl.BlockSpec((1, 1, TQ, LANE), lambda b, g, i: (b, g, i, 0)),
            pl.BlockSpec((HPG, n_cp, TQ), lambda b, g, i: (g, 0, i)),
            pl.BlockSpec((HPG, BIAS_ROWS, _toep_width(S)), lambda b, g, i: (g, 0, 0)),
        ],
        out_specs=pl.BlockSpec((1, TQ, HPG * dh), lambda b, g, i: (b, i, g)),
        out_shape=jax.ShapeDtypeStruct((B, S, NSA_Q_WIDTH), BF16),
        scratch_shapes=[pltpu.VMEM((S // NSA_SLC_BLOCK, TQ), F32)]
        + [pltpu.VMEM((TQ, HPG * TQ), F32)] * 2 + [pltpu.VMEM((TQ, HPG * TQ), BF16)] * 2,
        compiler_params=_cparams(("arbitrary", "arbitrary", "arbitrary"), 48),
        name="nsa_attn",
    )(q, kc, vc, ks, vs, kw, vw, gates, cmpb, toep)


def _router_kernel(h_ref, wr_ref, e_ref, w_ref, cnt_ref):
    @pl.when(pl.program_id(0) == 0)
    def _():
        cnt_ref[...] = jnp.zeros(cnt_ref.shape, F32)

    logits = jnp.dot(h_ref[...], wr_ref[...], precision=lax.Precision.HIGHEST, preferred_element_type=F32)
    tm = logits.shape[0]
    lane = lax.broadcasted_iota(jnp.int32, logits.shape, 1)
    logits = jnp.where(lane < N_EXPERTS, logits, NEG_INF)
    l1 = jnp.max(logits, axis=-1, keepdims=True)
    i1 = jnp.min(jnp.where(logits == l1, lane, LANE), axis=-1, keepdims=True)
    rest = jnp.where(lane == i1, NEG_INF, logits)
    l2 = jnp.max(rest, axis=-1, keepdims=True)
    i2 = jnp.min(jnp.where(rest == l2, lane, LANE), axis=-1, keepdims=True)
    e2 = jnp.exp(l2 - l1)
    den = 1.0 + e2
    w_ref[...] = jnp.where(lane == 0, 1.0 / den, jnp.where(lane == 1, e2 / den, 0.0))

    oh1 = lane == i1
    oh2 = lane == i2
    both = jnp.where(oh1 | oh2, 1.0, 0.0)
    earlier = lax.broadcasted_iota(jnp.int32, (tm, tm), 1) < lax.broadcasted_iota(jnp.int32, (tm, tm), 0)
    prefix = jnp.dot(jnp.where(earlier, 1.0, 0.0).astype(BF16), both.astype(BF16),
                     preferred_element_type=F32)
    seen = prefix + cnt_ref[0:1, :]
    p1 = jnp.sum(jnp.where(oh1, seen, 0.0), axis=-1, keepdims=True).astype(jnp.int32)
    p2 = jnp.sum(jnp.where(oh2, seen, 0.0), axis=-1, keepdims=True).astype(jnp.int32)
    e_ref[...] = jnp.where(lane == 0, i1, jnp.where(lane == 1, i2, jnp.where(lane == 2, p1,
                                                                              jnp.where(lane == 3, p2, 0))))
    cnt_ref[...] = cnt_ref[...] + jnp.sum(both, axis=0, keepdims=True)


def moe_route(h2, wr_pad, tm=512):
    N, D = h2.shape
    return pl.pallas_call(
        _router_kernel,
        grid=(N // tm,),
        in_specs=[pl.BlockSpec((tm, D), lambda i: (i, 0)), pl.BlockSpec(wr_pad.shape, lambda i: (0, 0))],
        out_specs=[pl.BlockSpec((tm, LANE), lambda i: (i, 0)), pl.BlockSpec((tm, LANE), lambda i: (i, 0)),
                   pl.BlockSpec((8, LANE), lambda i: (0, 0))],
        out_shape=[jax.ShapeDtypeStruct((N, LANE), jnp.int32), jax.ShapeDtypeStruct((N, LANE), F32),
                   jax.ShapeDtypeStruct((8, LANE), F32)],
        compiler_params=_cparams(("arbitrary",)),
        name="moe_router",
    )(h2, wr_pad)


def _dispatch_kernel(dest_ref, h_ref, xs_in_ref, xs_ref, sem):
    del xs_in_ref
    tm = h_ref.shape[0]
    base = pl.program_id(0) * tm * TOP_K

    def issue(r, carry):
        for k in range(TOP_K):
            d = dest_ref[base + r * TOP_K + k]
            pltpu.make_async_copy(h_ref.at[r], xs_ref.at[d], sem).start()
        return carry

    lax.fori_loop(0, tm, issue, 0, unroll=16)
    for k in range(TOP_K):
        pltpu.make_async_copy(h_ref, xs_ref.at[pl.ds(0, tm)], sem).wait()


def moe_dispatch_rows(h2, dest_flat, n_rows, recycled=None, tm=512):
    N, D = h2.shape
    xs0 = jnp.zeros((n_rows, D), h2.dtype) if recycled is None else recycled
    grid_spec = pltpu.PrefetchScalarGridSpec(
        num_scalar_prefetch=1,
        grid=(N // tm,),
        in_specs=[pl.BlockSpec((tm, D), lambda i, d: (i, 0)), pl.BlockSpec(memory_space=pl.ANY)],
        out_specs=pl.BlockSpec(memory_space=pl.ANY),
        scratch_shapes=[pltpu.SemaphoreType.DMA],
    )
    return pl.pallas_call(
        _dispatch_kernel,
        grid_spec=grid_spec,
        out_shape=jax.ShapeDtypeStruct((n_rows, D), h2.dtype),
        input_output_aliases={2: 0},
        compiler_params=_cparams(("arbitrary",)),
        name="moe_dispatch",
    )(dest_flat, h2, xs0)


def _combine_ln_kernel(dest_ref, w_ref, ys_ref, x_ref, g_ref, lng_ref, lnb_ref, *rest, has_next):
    if has_next:
        sc_ref, sh_ref, o_ref, hn_ref, buf, sem = rest
    else:
        o_ref, buf, sem = rest
    tm = x_ref.shape[0]
    i = pl.program_id(0)
    n = pl.num_programs(0)
    slot = i & 1
    rows = 64

    def issue(step, to_slot, r0):
        base = step * tm * TOP_K
        for r in range(rows):
            for k in range(TOP_K):
                d = dest_ref[base + (r0 + r) * TOP_K + k]
                pltpu.make_async_copy(ys_ref.at[d], buf.at[to_slot, k, r0 + r], sem.at[to_slot]).start()

    def wait_slot(s):
        for k in range(TOP_K):
            pltpu.make_async_copy(ys_ref.at[pl.ds(0, tm)], buf.at[s, k], sem.at[s]).wait()

    @pl.when(i == 0)
    def _():
        lax.fori_loop(0, tm // rows, lambda c, z: (issue(0, 0, c * rows), z)[1], 0)

    wait_slot(slot)
    nxt_step = jnp.where(i + 1 < n, i + 1, 0)
    gate = g_ref[0]
    lng = lng_ref[...]
    lnb = lnb_ref[...]

    def chunk(c, carry):
        r0 = pl.multiple_of(c * rows, rows)
        issue(nxt_step, 1 - slot, r0)
        rs = pl.ds(r0, rows)
        w = w_ref[rs, :]
        y = w[:, 0:1] * buf[slot, 0, rs, :] + w[:, 1:2] * buf[slot, 1, rs, :]
        xn = _deepnorm_ln(x_ref[rs, :], y, gate, lng, lnb)
        o_ref[rs, :] = xn
        if has_next:
            hn_ref[rs, :] = (xn * (1.0 + sc_ref[0]) + sh_ref[0]).astype(hn_ref.dtype)
        return carry

    lax.fori_loop(0, tm // rows, chunk, 0)

    @pl.when(i == n - 1)
    def _():
        wait_slot(1 - slot)


def moe_combine_ln(ys, dest_flat, w_top, x, gate, ln_g, ln_b, nxt, tm=512):
    B, S, D = x.shape
    N = B * S
    per_b = S // tm
    tile = pl.BlockSpec((tm, D), lambda i, d: (i, 0))
    vec = pl.BlockSpec((1, 1, D), lambda i, d: (i // per_b, 0, 0))
    par = pl.BlockSpec((1, D), lambda i, d: (0, 0))
    in_specs = [pl.BlockSpec((tm, LANE), lambda i, d: (i, 0)), pl.BlockSpec(memory_space=pl.ANY), tile, vec, par, par]
    args = [w_top, ys, x.reshape(N, D), gate, ln_g, ln_b]
    out_shape = [jax.ShapeDtypeStruct((N, D), F32)]
    out_specs = [tile]
    if nxt is not None:
        in_specs += [vec, vec]
        args += list(nxt)
        out_shape.append(jax.ShapeDtypeStruct((N, D), BF16))
        out_specs.append(tile)
    grid_spec = pltpu.PrefetchScalarGridSpec(
        num_scalar_prefetch=1,
        grid=(N // tm,),
        in_specs=in_specs,
        out_specs=out_specs,
        scratch_shapes=[pltpu.VMEM((2, TOP_K, tm, D), F32), pltpu.SemaphoreType.DMA((2,))],
    )
    res = pl.pallas_call(
        functools.partial(_combine_ln_kernel, has_next=nxt is not None),
        grid_spec=grid_spec,
        out_shape=out_shape,
        compiler_params=_cparams(("arbitrary",)),
        name="moe_combine_ln",
    )(dest_flat, *args)
    x_new = res[0].reshape(B, S, D)
    return (x_new, res[1].reshape(B, S, D)) if nxt is not None else (x_new, None)


def _moe_ffn_kernel(be_ref, nu_ref, x_ref, w1_ref, w3_ref, w2_ref, o_ref, acc_sc, xb_sc):
    i = pl.program_id(0)
    j = pl.program_id(1)
    last = pl.num_programs(1) - 1
    used = i < nu_ref[0]

    @pl.when(j == 0)
    def _():
        acc_sc[...] = jnp.zeros(acc_sc.shape, F32)
        xb_sc[...] = x_ref[...].astype(BF16)

    @pl.when(used)
    def _():
        x = xb_sc[...]
        a = jnp.dot(x, w1_ref[0].astype(BF16), preferred_element_type=F32)
        b = jnp.dot(x, w3_ref[0].astype(BF16), preferred_element_type=F32)
        u = (a * jax.nn.sigmoid(a)) * b
        acc_sc[...] += jnp.dot(u.astype(BF16), w2_ref[0].astype(BF16), preferred_element_type=F32)

    @pl.when(j == last)
    def _():
        o_ref[...] = acc_sc[...]


def moe_ffn(xs, blk_e, n_used, w1, w3, w2, tf=512):
    n_rows, D = xs.shape
    n_blocks = blk_e.shape[0]
    RB = n_rows // n_blocks
    F = w1.shape[2]
    nf = F // tf

    def wcol(i, j, be, nu):
        return (be[i], 0, jnp.where(i < nu[0], j, nf - 1))

    def wrow(i, j, be, nu):
        return (be[i], jnp.where(i < nu[0], j, nf - 1), 0)

    grid_spec = pltpu.PrefetchScalarGridSpec(
        num_scalar_prefetch=2,
        grid=(n_blocks, nf),
        in_specs=[
            pl.BlockSpec((RB, D), lambda i, j, be, nu: (i, 0)),
            pl.BlockSpec((1, D, tf), wcol),
            pl.BlockSpec((1, D, tf), wcol),
            pl.BlockSpec((1, tf, D), wrow),
        ],
        out_specs=pl.BlockSpec((RB, D), lambda i, j, be, nu: (i, 0)),
        scratch_shapes=[pltpu.VMEM((RB, D), F32), pltpu.VMEM((RB, D), BF16)],
    )
    return pl.pallas_call(
        _moe_ffn_kernel,
        grid_spec=grid_spec,
        out_shape=jax.ShapeDtypeStruct((n_rows, D), F32),
        compiler_params=_cparams(("arbitrary", "arbitrary"), 56),
        name="moe_ffn",
    )(blk_e, n_used, xs, w1, w3, w2)


def moe_layout(e_pos, counts_f, n_tok):
    RB = _moe_row_block(n_tok)
    n_assign = n_tok * TOP_K
    n_blocks = (n_assign + N_EXPERTS * (RB - 1) + RB - 1) // RB
    counts = counts_f[0, :N_EXPERTS].astype(jnp.int32)
    padded = (counts + RB - 1) // RB * RB
    pad_end = jnp.cumsum(padded)
    pad_start = pad_end - padded
    e = e_pos[:, 0:TOP_K]
    pos = e_pos[:, TOP_K:2 * TOP_K]
    start = jnp.zeros_like(e)
    for x in range(N_EXPERTS):
        start = jnp.where(e == x, pad_start[x], start)
    dest_flat = (start + pos).reshape(-1)
    blk_start = jnp.arange(n_blocks, dtype=jnp.int32) * RB
    blk_e = jnp.minimum(jnp.sum((blk_start[:, None] >= pad_end[None, :]).astype(jnp.int32), axis=1), N_EXPERTS - 1)
    n_used = (pad_end[-1] // RB).astype(jnp.int32).reshape(1)
    return dest_flat, blk_e.astype(jnp.int32), n_used, n_blocks * RB


def _pad_cols(w, width):
    return jnp.pad(w, ((0, 0), (0, width - w.shape[1])))


def kernel(x, c, rel_bias, ada_w, ada_b, ln_g, ln_b, ab_w_in, gla_wg2, gla_bg2, gla_norm_g, ab_w_out,
           ffn_w1, ffn_w3, ffn_w2, nsa_w_in, nsa_cmp_pos, nsa_cmp_w1k, nsa_cmp_w2k, nsa_cmp_w1v, nsa_cmp_w2v,
           nsa_w_out, moe_router, moe_w1, moe_w3, moe_w2):
    B, S, D = x.shape
    n_tok = B * S
    mod = adaln_mod(c, ada_w, ada_b)
    mods = [[m.reshape(B, 1, D) for m in jnp.split(mod[i], 6, axis=-1)] for i in range(DEPTH)]
    toep, cmpb = build_bias_tables(rel_bias, S)

    xs_dead = None
    sh1, sc1 = mods[0][0], mods[0][1]
    h = modulate(x, sc1, sh1)
    for i in range(DEPTH):
        j = i // 2
        _, _, g1, sh2, sc2, g2 = mods[i]
        nxt2 = (mods[i + 1][1], mods[i + 1][0]) if i + 1 < DEPTH else None
        lng = ln_g[i].reshape(2, 1, D)
        lnb = ln_b[i].reshape(2, 1, D)
        if i % 2 == 0:
            w = ab_w_in[j]
            cuts = np.cumsum((MOBA_WIDTH, MOBA_WIDTH, MOBA_WIDTH, GLA_QK_WIDTH, GLA_QK_WIDTH, GLA_V_WIDTH,
                              GLA_V_WIDTH))
            parts = jnp.split(w, cuts.tolist(), axis=1)
            parts[-1] = _pad_cols(parts[-1], LANE)
            w_cat = jnp.concatenate(parts, axis=1).astype(BF16)
            widths = [p.shape[1] for p in parts]
            offs = np.concatenate([[0], np.cumsum(widths)[:-1]]).tolist()
            scales = [MOBA_HEAD_DIM ** -0.5 * LOG2E] + [1.0] * 7
            kinds = ["flat", "flat", "flat_t"] + ["flat"] * 5
            offs[2] = 0
            segs = tuple((o, wd, kd, s) for o, wd, kd, s in zip(offs, widths, kinds, scales))
            dts = [BF16, BF16, BF16, F32, F32, F32, F32, F32]
            mq, mk, mv, gq, gk, gv, gr, glr = project(h, w_cat, segs, dts, wt=parts[2].T.astype(BF16))
            y_moba = moba_attention(mq, mk, mv, toep)
            wg2_pad = jnp.pad(gla_wg2[j], ((0, LANE - GLA_GATE_RANK), (0, 0)))
            y_gla = gla_mixer(gq, gk, gv, gr, glr, wg2_pad, gla_bg2[j].reshape(1, -1), gla_norm_g[j].reshape(1, -1))
            w_out = ab_w_out[j].astype(BF16)
            x, h = out_ln([y_moba, y_gla], [w_out[:MOBA_WIDTH], w_out[MOBA_WIDTH:]], x, g1, lng[0], lnb[0],
                          (sc2, sh2))
            x, h = ffn_ln(h, ffn_w1[j].astype(BF16), ffn_w3[j].astype(BF16), ffn_w2[j].astype(BF16), x, g2,
                          lng[1], lnb[1], nxt2)
        else:
            G, HPG, dh = NSA_KV_GROUPS, NSA_HPG, NSA_HEAD_DIM
            w = nsa_w_in[j]
            wq = w[:, :NSA_Q_WIDTH]
            wkv = w[:, NSA_Q_WIDTH:NSA_Q_WIDTH + 6 * NSA_KV_WIDTH]
            wg = w[:, NSA_Q_WIDTH + 6 * NSA_KV_WIDTH:].reshape(D, G, HPG, 3)
            wg = _pad_cols(wg.transpose(0, 1, 3, 2).reshape(D * G, 3 * HPG), LANE).reshape(D, G * LANE)
            w_cat = jnp.concatenate([wq, wkv, wg], axis=1).astype(BF16)
            segs = [(0, NSA_Q_WIDTH, "flat", dh ** -0.5 * LOG2E)]
            segs += [(NSA_Q_WIDTH + n * NSA_KV_WIDTH, NSA_KV_WIDTH, "group", 1.0) for n in range(6)]
            segs += [(NSA_Q_WIDTH + 6 * NSA_KV_WIDTH, G * LANE, "group", 1.0)]
            dts = [BF16] * 7 + [F32]
            q, kc, vc, ks, vs, kw, vw, gates = project(h, w_cat, tuple(segs), dts)
            half = NSA_CMP_STRIDE * dh
            pos2 = jnp.pad(nsa_cmp_pos[j].reshape(2, half), ((0, 6), (0, 0))).astype(BF16)
            w1k = jnp.concatenate([nsa_cmp_w1k[j][:half], nsa_cmp_w1k[j][half:]], axis=1).astype(BF16)
            w1v = jnp.concatenate([nsa_cmp_w1v[j][:half], nsa_cmp_w1v[j][half:]], axis=1).astype(BF16)
            k_cmp = nsa_compress(kc, pos2, w1k, nsa_cmp_w2k[j].astype(BF16))
            v_cmp = nsa_compress(vc, pos2, w1v, nsa_cmp_w2v[j].astype(BF16))
            y = nsa_attention(q, k_cmp, v_cmp, ks, vs, kw, vw, gates, cmpb, toep)
            x, h = out_ln([y], [nsa_w_out[j].astype(BF16)], x, g1, lng[0], lnb[0], (sc2, sh2), h_dtype=F32)
            h2 = h.reshape(n_tok, D)
            e_pos, w_top, counts = moe_route(h2, _pad_cols(moe_router[j], LANE))
            dest_flat, blk_e, n_used, n_rows = moe_layout(e_pos, counts, n_tok)
            xs = moe_dispatch_rows(h2, dest_flat, n_rows, recycled=xs_dead)
            xs_dead = xs
            stack = lambda w: w.reshape((-1,) + w.shape[2:])
            ys = moe_ffn(xs, blk_e + j * N_EXPERTS, n_used, stack(moe_w1), stack(moe_w3), stack(moe_w2))
            x, h = moe_combine_ln(ys, dest_flat, w_top, x, g2, lng[1], lnb[1], nxt2)
    return x
```

```python
import functools
import math

import jax
import jax.numpy as jnp
import numpy as np
from jax import lax
from jax.experimental import pallas as pl
from jax.experimental.pallas import tpu as pltpu

F32 = jnp.float32
BF16 = jnp.bfloat16
NEG_INF = float("-inf")

DEPTH = 4
ALPHA_DN = (2 * DEPTH) ** 0.25
LN_EPS = 1e-5
RMS_EPS = 1e-6

N_BUCKETS = 32
MAX_DISTANCE = 1024

MOBA_HEADS = 8
MOBA_HEAD_DIM = 64
MOBA_BLOCK = 256
MOBA_TOPK = 3
MOBA_WIDTH = MOBA_HEADS * MOBA_HEAD_DIM

GLA_HEADS = 4
GLA_DK = 64
GLA_DV = 128
GLA_GATE_RANK = 16
GLA_TAU = 16.0
GLA_CHUNK = 64
GLA_QK_WIDTH = GLA_HEADS * GLA_DK
GLA_V_WIDTH = GLA_HEADS * GLA_DV

NSA_HEADS = 8
NSA_KV_GROUPS = 2
NSA_HPG = NSA_HEADS // NSA_KV_GROUPS
NSA_HEAD_DIM = 128
NSA_CMP_LEN = 32
NSA_CMP_STRIDE = 16
NSA_SLC_BLOCK = 64
NSA_SLC_TOPN = 16
NSA_WINDOW = 512
NSA_Q_WIDTH = NSA_HEADS * NSA_HEAD_DIM
NSA_KV_WIDTH = NSA_KV_GROUPS * NSA_HEAD_DIM

N_EXPERTS = 8
TOP_K = 2
MOE_BLOCKS_PER_EXPERT = 4


def _moe_row_block(n_tok):
    balanced = n_tok * TOP_K // N_EXPERTS
    rows = balanced // MOE_BLOCKS_PER_EXPERT + balanced // 128
    return -(-rows // 16) * 16

LOG2E = math.log2(math.e)
LANE = 128
BIAS_ROWS = 128


def _cparams(sem, vmem_mib=40, **extra):
    return pltpu.CompilerParams(dimension_semantics=sem, vmem_limit_bytes=vmem_mib << 20, **extra)


def _adaln_kernel(c_ref, w_ref, b_ref, o_ref):
    c = c_ref[...]
    cond = c * jax.nn.sigmoid(c)
    o_ref[0] = jnp.dot(cond, w_ref[0], preferred_element_type=F32) + b_ref[0]


def adaln_mod(c, ada_w, ada_b):
    B, D = c.shape
    L, _, N = ada_w.shape
    rows = 8
    cp = jnp.zeros((rows, D), F32).at[:B].set(c)
    tn = N // 4
    out = pl.pallas_call(
        _adaln_kernel,
        grid=(L, N // tn),
        in_specs=[
            pl.BlockSpec((rows, D), lambda l, j: (0, 0)),
            pl.BlockSpec((1, D, tn), lambda l, j: (l, 0, j)),
            pl.BlockSpec((1, 1, tn), lambda l, j: (l, 0, j)),
        ],
        out_specs=pl.BlockSpec((1, rows, tn), lambda l, j: (l, 0, j)),
        out_shape=jax.ShapeDtypeStruct((L, rows, N), F32),
        compiler_params=_cparams(("arbitrary", "arbitrary")),
        name="adaln_mod",
    )(cp, ada_w, ada_b.reshape(L, 1, N))
    return out[:, :B]


def _modulate_kernel(x_ref, sc_ref, sh_ref, o_ref):
    o_ref[0] = (x_ref[0] * (1.0 + sc_ref[0]) + sh_ref[0]).astype(o_ref.dtype)


def modulate(x, sc, sh, tm=512):
    B, S, D = x.shape
    return pl.pallas_call(
        _modulate_kernel,
        grid=(B, S // tm),
        in_specs=[
            pl.BlockSpec((1, tm, D), lambda b, i: (b, i, 0)),
            pl.BlockSpec((1, 1, D), lambda b, i: (b, 0, 0)),
            pl.BlockSpec((1, 1, D), lambda b, i: (b, 0, 0)),
        ],
        out_specs=pl.BlockSpec((1, tm, D), lambda b, i: (b, i, 0)),
        out_shape=jax.ShapeDtypeStruct((B, S, D), BF16),
        compiler_params=_cparams(("arbitrary", "arbitrary")),
        name="modulate",
    )(x, sc, sh)


def _t5_bucket(dist):
    max_exact = N_BUCKETS // 2
    n = jnp.maximum(dist, 0)
    nf = jnp.maximum(n, 1).astype(jnp.float32)
    large = max_exact + (jnp.log(nf / max_exact) / math.log(MAX_DISTANCE / max_exact) * (N_BUCKETS - max_exact)).astype(jnp.int32)
    large = jnp.minimum(large, N_BUCKETS - 1)
    return jnp.where(n < max_exact, n, large)


def _bias_table_kernel(rb_ref, bm_ref, o_ref, *, rows_per_step):
    h = pl.program_id(0)
    n_steps = bm_ref.shape[0] // rows_per_step

    def body(i, carry):
        r0 = pl.multiple_of(i * rows_per_step, rows_per_step)
        bm = bm_ref[pl.ds(r0, rows_per_step), :]
        acc = jnp.zeros(bm.shape, F32)
        for b in range(N_BUCKETS):
            acc = jnp.where(bm == b, rb_ref[b, h] * LOG2E, acc)
        o_ref[0, pl.ds(r0, rows_per_step), :] = acc
        return carry

    lax.fori_loop(0, n_steps, body, 0)


def bias_table(rel_bias, bucket_map, rows_per_step):
    R, W = bucket_map.shape
    H = rel_bias.shape[1]
    return pl.pallas_call(
        functools.partial(_bias_table_kernel, rows_per_step=rows_per_step),
        grid=(H,),
        in_specs=[
            pl.BlockSpec(memory_space=pltpu.SMEM),
            pl.BlockSpec((R, W), lambda h: (0, 0)),
        ],
        out_specs=pl.BlockSpec((1, R, W), lambda h: (h, 0, 0)),
        out_shape=jax.ShapeDtypeStruct((H, R, W), F32),
        compiler_params=_cparams(("arbitrary",)),
        name="bias_table",
    )(rel_bias, bucket_map)


def _toep_width(S):
    return S + BIAS_ROWS


def _toep_col(q0, k0):
    return pl.multiple_of(q0 - k0 + BIAS_ROWS, LANE)


def build_bias_tables(rel_bias, S):
    j = jnp.arange(BIAS_ROWS, dtype=jnp.int32)[:, None]
    c = jnp.arange(_toep_width(S), dtype=jnp.int32)[None, :]
    toep_map = _t5_bucket(c - j - BIAS_ROWS)
    n_cmp_pad = S // NSA_CMP_STRIDE
    t = jnp.arange(S, dtype=jnp.int32)[None, :]
    cmp_end = jnp.arange(n_cmp_pad, dtype=jnp.int32)[:, None] * NSA_CMP_STRIDE + NSA_CMP_LEN - 1
    cmp_map = _t5_bucket(t - cmp_end)
    return bias_table(rel_bias, toep_map, 8), bias_table(rel_bias, cmp_map, 8)


def _proj_kernel(h_ref, w_ref, wt_ref, *o_refs, segs):
    h = h_ref[0]
    for (off, width, kind, scale), o_ref in zip(segs, o_refs):
        if kind == "flat_t":
            acc = lax.dot_general(wt_ref[off:off + width, :], h, _NT, preferred_element_type=F32)
        else:
            acc = jnp.dot(h, w_ref[:, off:off + width], preferred_element_type=F32)
        if scale != 1.0:
            acc = acc * scale
        if kind in ("flat", "flat_t"):
            o_ref[0] = acc.astype(o_ref.dtype)
        else:
            for g in range(width // LANE):
                o_ref[0, g] = acc[:, g * LANE:(g + 1) * LANE].astype(o_ref.dtype)


def project(h, w_cat, segs, out_dtypes, wt=None, tm=512):
    B, S, D = h.shape
    if wt is None:
        wt = jnp.zeros((8, D), w_cat.dtype)
    out_shapes, out_specs = [], []
    for (off, width, kind, scale), dt in zip(segs, out_dtypes):
        if kind == "flat":
            out_shapes.append(jax.ShapeDtypeStruct((B, S, width), dt))
            out_specs.append(pl.BlockSpec((1, tm, width), lambda b, i: (b, i, 0)))
        elif kind == "flat_t":
            out_shapes.append(jax.ShapeDtypeStruct((B, width, S), dt))
            out_specs.append(pl.BlockSpec((1, width, tm), lambda b, i: (b, 0, i)))
        else:
            G = width // LANE
            out_shapes.append(jax.ShapeDtypeStruct((B, G, S, LANE), dt))
            out_specs.append(pl.BlockSpec((1, G, tm, LANE), lambda b, i: (b, 0, i, 0)))
    return pl.pallas_call(
        functools.partial(_proj_kernel, segs=segs),
        grid=(B, S // tm),
        in_specs=[
            pl.BlockSpec((1, tm, D), lambda b, i: (b, i, 0)),
            pl.BlockSpec(w_cat.shape, lambda b, i: (0, 0)),
            pl.BlockSpec(wt.shape, lambda b, i: (0, 0)),
        ],
        out_specs=out_specs,
        out_shape=out_shapes,
        compiler_params=_cparams(("arbitrary", "arbitrary"), 48),
        name="in_proj",
    )(h, w_cat, wt)


_NT = (((1,), (1,)), ((), ()))
_TN = (((0,), (0,)), ((), ()))


def _softmax_step(carry, s, v_tile):
    m, l, acc = carry
    m_new = jnp.maximum(m, jnp.max(s, axis=0, keepdims=True))
    alpha = jnp.exp2(m - m_new)
    p = jnp.exp2(s - m_new)
    l_new = alpha * l + jnp.sum(p, axis=0, keepdims=True)
    acc_new = alpha * acc + lax.dot_general(v_tile, p.astype(BF16), _TN, preferred_element_type=F32)
    return m_new, l_new, acc_new


def _moba_kernel(q_ref, k_ref, v_ref, tb_ref, o_ref, kmean_sc, sel_sc, s0_sc, s1_sc, p0_sc, p1_sc,
                 *, S, heads_per_step):
    L = MOBA_BLOCK
    dh = MOBA_HEAD_DIM
    nb = S // L
    qi = pl.program_id(2)
    q0 = qi * L

    @pl.when(qi == 0)
    def _():
        kf = k_ref[0].astype(F32)
        kmean_sc[...] = jnp.mean(kf.reshape(nb, L, kf.shape[-1]), axis=1)

    q2 = q_ref[0]
    lane_q = lax.broadcasted_iota(jnp.int32, q2.shape, 1)
    lane_m = lax.broadcasted_iota(jnp.int32, kmean_sc.shape, 1)
    blk = lax.broadcasted_iota(jnp.int32, (nb, L), 0)
    qz = []
    for hh in range(heads_per_step):
        in_head = (lane_q >= hh * dh) & (lane_q < (hh + 1) * dh)
        qz.append(jnp.where(in_head, q2, jnp.zeros_like(q2)))
        km = jnp.where((lane_m >= hh * dh) & (lane_m < (hh + 1) * dh), kmean_sc[...], 0.0)
        gate = lax.dot_general(km, q2.astype(F32), _NT, precision=lax.Precision.HIGHEST,
                               preferred_element_type=F32)
        valid = blk < qi
        gm = jnp.where(valid, gate, NEG_INF)
        rank = jnp.zeros((nb, L), jnp.int32)
        for m in range(nb):
            gmm = gm[m:m + 1, :]
            beats = (gmm > gm) | ((gmm == gm) & (m < blk))
            rank = rank + beats.astype(jnp.int32)
        sel_sc[hh] = jnp.where(valid & (rank < MOBA_TOPK), 1.0, 0.0)

    def bias_tile(hh, k0, n_keys):
        return jnp.concatenate([tb_ref[hh, :, pl.ds(_toep_col(q0, k0 + r * BIAS_ROWS), L)]
                                for r in range(n_keys // BIAS_ROWS)], axis=0)

    krow = lax.broadcasted_iota(jnp.int32, (L, L), 0)
    qcol = lax.broadcasted_iota(jnp.int32, (L, L), 1)
    k_own = k_ref[0, pl.ds(pl.multiple_of(q0, L), L), :]
    v_tile = lambda hh, k0, n_keys: v_ref[0, hh * dh:(hh + 1) * dh, pl.ds(k0, n_keys)]
    carry = []
    for hh in range(heads_per_step):
        s = lax.dot_general(k_own, qz[hh], _NT, preferred_element_type=F32) + bias_tile(hh, q0, L)
        s = jnp.where(krow <= qcol, s, NEG_INF)
        m0 = jnp.max(s, axis=0, keepdims=True)
        p = jnp.exp2(s - m0)
        l0 = jnp.sum(p, axis=0, keepdims=True)
        acc0 = jnp.dot(v_tile(hh, pl.multiple_of(q0, L), L), p.astype(BF16), preferred_element_type=F32)
        carry += [m0, l0, acc0, jnp.ones((1, L), F32)]

    n = (qi + 1) // 2
    s_bufs = (s0_sc, s1_sc)
    p_bufs = (p0_sc, p1_sc)
    tile_k0 = lambda i: pl.multiple_of(jnp.clip(i, 0, jnp.maximum(n - 1, 0)) * 2 * L, 2 * L)

    def put_scores(i, buf):
        kt = k_ref[0, pl.ds(tile_k0(i), 2 * L), :]
        for hh in range(heads_per_step):
            buf[hh] = lax.dot_general(kt, qz[hh], _NT, preferred_element_type=F32)

    put_scores(0, s_bufs[0])
    p_bufs[1][...] = jnp.zeros(p_bufs[1].shape, BF16)

    def half(i, cur, carry):
        put_scores(i + 1, s_bufs[1 - cur])
        k0 = tile_k0(i)
        blk0 = lax.shift_right_logical(k0, L.bit_length() - 1)
        thresh = jnp.where(i < n, 0.5, 2.0)
        out = []
        for hh in range(heads_per_step):
            m, l, acc, alpha_prev = carry[4 * hh:4 * hh + 4]
            flags = jnp.concatenate(
                [jnp.broadcast_to(sel_sc[hh, pl.ds(blk0 + j, 1), :], (L, L)) for j in range(2)], axis=0)
            s = jnp.where(flags > thresh, s_bufs[cur][hh] + bias_tile(hh, k0, 2 * L), NEG_INF)
            m_new = jnp.maximum(m, jnp.max(s, axis=0, keepdims=True))
            alpha = jnp.exp2(m - m_new)
            p = jnp.exp2(s - m_new)
            l_new = alpha * l + jnp.sum(p, axis=0, keepdims=True)
            p_bufs[cur][hh] = p.astype(BF16)
            pv = jnp.dot(v_tile(hh, tile_k0(i - 1), 2 * L), p_bufs[1 - cur][hh], preferred_element_type=F32)
            out += [m_new, l_new, alpha_prev * acc + pv, alpha]
        return tuple(out)

    def body(jj, carry):
        return half(2 * jj + 1, 1, half(2 * jj, 0, carry))

    n_pairs = (n + 1) // 2
    carry = lax.fori_loop(0, n_pairs, body, tuple(carry))
    outs = []
    for hh in range(heads_per_step):
        _, l, acc, alpha = carry[4 * hh:4 * hh + 4]
        pv = jnp.dot(v_tile(hh, tile_k0(2 * n_pairs - 1), 2 * L), p_bufs[1][hh], preferred_element_type=F32)
        outs.append((alpha * acc + pv) * (1.0 / l))
    o_ref[0] = jnp.concatenate(outs, axis=0).T.astype(o_ref.dtype)


def moba_attention(q, k, v, toep):
    B, S, _ = q.shape
    L = MOBA_BLOCK
    hps = 4
    wl = hps * MOBA_HEAD_DIM
    n_hp = MOBA_HEADS // hps
    nb = S // L
    return pl.pallas_call(
        functools.partial(_moba_kernel, S=S, heads_per_step=hps),
        grid=(n_hp, B, nb),
        in_specs=[
            pl.BlockSpec((1, L, wl), lambda hp, b, i: (b, i, hp)),
            pl.BlockSpec((1, S, wl), lambda hp, b, i: (b, 0, hp)),
            pl.BlockSpec((1, wl, S), lambda hp, b, i: (b, hp, 0)),
            pl.BlockSpec((hps, BIAS_ROWS, _toep_width(S)), lambda hp, b, i: (hp, 0, 0)),
        ],
        out_specs=pl.BlockSpec((1, L, wl), lambda hp, b, i: (b, i, hp)),
        out_shape=jax.ShapeDtypeStruct((B, S, MOBA_WIDTH), BF16),
        scratch_shapes=[pltpu.VMEM((nb, wl), F32), pltpu.VMEM((hps, nb, L), F32)]
        + [pltpu.VMEM((hps, 2 * L, L), F32)] * 2 + [pltpu.VMEM((hps, 2 * L, L), BF16)] * 2,
        compiler_params=_cparams(("arbitrary", "arbitrary", "arbitrary"), 48),
        name="moba_attn",
    )(q, k, v, toep)


def _log_sigmoid(z):
    return jnp.minimum(z, 0.0) - jnp.log1p(jnp.exp(-jnp.abs(z)))


def _gla_kernel(q_ref, k_ref, v_ref, r_ref, lr_ref, wg2_ref, bg2_ref, ng_ref, o_ref, st_sc, *, batch):
    C = GLA_CHUNK
    dk, dv = GLA_DK, GLA_DV

    @pl.when(pl.program_id(0) == 0)
    def _():
        st_sc[...] = jnp.zeros(st_sc.shape, F32)

    row = lax.broadcasted_iota(jnp.int32, (C, C), 0)
    col = lax.broadcasted_iota(jnp.int32, (C, C), 1)
    causal = col <= row
    tri = jnp.where(causal, 1.0, 0.0)
    for b in range(batch):
        z = jnp.dot(lr_ref[b], wg2_ref[...], preferred_element_type=F32) + bg2_ref[...]
        la = _log_sigmoid(z) / GLA_TAU
        bc = jnp.dot(tri, la, precision=lax.Precision.HIGHEST, preferred_element_type=F32)
        b_last = bc[C - 1:C, :]
        qd = q_ref[b] * (dk ** -0.5) * jnp.exp(bc)
        kd = k_ref[b] * jnp.exp(-bc)
        kl = k_ref[b] * jnp.exp(b_last - bc)
        e_last = jnp.exp(b_last)
        v = v_ref[b]
        r = r_ref[b]
        outs = []
        for h in range(GLA_HEADS):
            ks = slice(h * dk, (h + 1) * dk)
            vs = slice(h * dv, (h + 1) * dv)
            qh = qd[:, ks].astype(BF16)
            kh = kd[:, ks].astype(BF16)
            vh = v[:, vs].astype(BF16)
            attn = lax.dot_general(qh, kh, (((1,), (1,)), ((), ())), preferred_element_type=F32)
            attn = jnp.where(causal, attn, 0.0)
            st = st_sc[b, h]
            o = jnp.dot(attn.astype(BF16), vh, preferred_element_type=F32)
            o = o + lax.dot_general(qh, st.astype(BF16), (((1,), (1,)), ((), ())), preferred_element_type=F32)
            upd = lax.dot_general(vh, kl[:, ks].astype(BF16), (((0,), (0,)), ((), ())), preferred_element_type=F32)
            st_sc[b, h] = st * e_last[:, ks] + upd
            o = o * lax.rsqrt(jnp.mean(o * o, axis=-1, keepdims=True) + RMS_EPS)
            rg = r[:, vs]
            outs.append(o * ng_ref[:, vs] * (rg * jax.nn.sigmoid(rg)))
        o_ref[b] = jnp.concatenate(outs, axis=-1).astype(o_ref.dtype)


def gla_mixer(q, k, v, r, lr, wg2_pad, bg2, norm_g):
    B, S, _ = q.shape
    C = GLA_CHUNK
    nc = S // C
    blk = lambda w: pl.BlockSpec((B, C, w), lambda c: (0, c, 0))
    full = lambda a: pl.BlockSpec(a.shape, lambda c: (0,) * a.ndim)
    return pl.pallas_call(
        functools.partial(_gla_kernel, batch=B),
        grid=(nc,),
        in_specs=[blk(GLA_QK_WIDTH), blk(GLA_QK_WIDTH), blk(GLA_V_WIDTH), blk(GLA_V_WIDTH), blk(LANE),
                  full(wg2_pad), full(bg2), full(norm_g)],
        out_specs=blk(GLA_V_WIDTH),
        out_shape=jax.ShapeDtypeStruct((B, S, GLA_V_WIDTH), BF16),
        scratch_shapes=[pltpu.VMEM((B, GLA_HEADS, GLA_DV, GLA_DK), F32)],
        compiler_params=_cparams(("arbitrary",)),
        name="gla",
    )(q, k, v, r, lr, wg2_pad, bg2, norm_g)


def _deepnorm_ln(x, y, gate, ln_g, ln_b):
    z = ALPHA_DN * x + (1.0 + gate) * y
    mu = jnp.mean(z, axis=-1, keepdims=True)
    zc = z - mu
    var = jnp.mean(zc * zc, axis=-1, keepdims=True)
    return zc * lax.rsqrt(var + LN_EPS) * ln_g + ln_b


def _out_ln_kernel(*refs, n_parts, matmul, has_next):
    y_refs = refs[:n_parts]
    pos = n_parts
    if matmul:
        w_refs = refs[pos:pos + n_parts]
        pos += n_parts
    x_ref, g_ref, lng_ref, lnb_ref = refs[pos:pos + 4]
    pos += 4
    if has_next:
        sc_ref, sh_ref = refs[pos:pos + 2]
        pos += 2
    o_ref = refs[pos]
    y = None
    for i in range(n_parts):
        part = jnp.dot(y_refs[i][0], w_refs[i][...], preferred_element_type=F32) if matmul else y_refs[i][0]
        y = part if y is None else y + part
    xn = _deepnorm_ln(x_ref[0], y, g_ref[0], lng_ref[...], lnb_ref[...])
    o_ref[0] = xn
    if has_next:
        refs[pos + 1][0] = (xn * (1.0 + sc_ref[0]) + sh_ref[0]).astype(refs[pos + 1].dtype)


def out_ln(ys, ws, x, gate, ln_g, ln_b, nxt, h_dtype=BF16, tm=512):
    B, S, D = x.shape
    tile = lambda w: pl.BlockSpec((1, tm, w), lambda b, i: (b, i, 0))
    vec = pl.BlockSpec((1, 1, D), lambda b, i: (b, 0, 0))
    par = pl.BlockSpec((1, D), lambda b, i: (0, 0))
    args = list(ys)
    in_specs = [tile(y.shape[-1]) for y in ys]
    if ws is not None:
        args += list(ws)
        in_specs += [pl.BlockSpec(w.shape, lambda b, i: (0, 0)) for w in ws]
    args += [x, gate, ln_g, ln_b]
    in_specs += [tile(D), vec, par, par]
    out_shape = [jax.ShapeDtypeStruct((B, S, D), F32)]
    out_specs = [tile(D)]
    if nxt is not None:
        args += list(nxt)
        in_specs += [vec, vec]
        out_shape.append(jax.ShapeDtypeStruct((B, S, D), h_dtype))
        out_specs.append(tile(D))
    res = pl.pallas_call(
        functools.partial(_out_ln_kernel, n_parts=len(ys), matmul=ws is not None, has_next=nxt is not None),
        grid=(B, S // tm),
        in_specs=in_specs,
        out_specs=out_specs,
        out_shape=out_shape,
        compiler_params=_cparams(("arbitrary", "arbitrary")),
        name="out_ln",
    )(*args)
    return (res[0], res[1]) if nxt is not None else (res[0], None)


def _ffn_ln_kernel(h_ref, w1_ref, w3_ref, w2_ref, x_ref, g_ref, lng_ref, lnb_ref, sc_ref, sh_ref,
                   o_ref, hn_ref, acc_sc):
    j = pl.program_id(2)

    @pl.when(j == 0)
    def _():
        acc_sc[...] = jnp.zeros(acc_sc.shape, F32)

    h = h_ref[0]
    a = jnp.dot(h, w1_ref[...], preferred_element_type=F32)
    b = jnp.dot(h, w3_ref[...], preferred_element_type=F32)
    u = (a * jax.nn.sigmoid(a)) * b
    acc_sc[...] += jnp.dot(u.astype(BF16), w2_ref[...], preferred_element_type=F32)

    @pl.when(j == pl.num_programs(2) - 1)
    def _():
        xn = _deepnorm_ln(x_ref[0], acc_sc[...], g_ref[0], lng_ref[...], lnb_ref[...])
        o_ref[0] = xn
        hn_ref[0] = (xn * (1.0 + sc_ref[0]) + sh_ref[0]).astype(BF16)


def ffn_ln(h, w1, w3, w2, x, gate, ln_g, ln_b, nxt, tm=512):
    B, S, D = x.shape
    F = w1.shape[1]
    tf = F // 2
    tile = pl.BlockSpec((1, tm, D), lambda b, i, j: (b, i, 0))
    vec = pl.BlockSpec((1, 1, D), lambda b, i, j: (b, 0, 0))
    par = pl.BlockSpec((1, D), lambda b, i, j: (0, 0))
    return pl.pallas_call(
        _ffn_ln_kernel,
        grid=(B, S // tm, F // tf),
        in_specs=[tile,
                  pl.BlockSpec((D, tf), lambda b, i, j: (0, j)),
                  pl.BlockSpec((D, tf), lambda b, i, j: (0, j)),
                  pl.BlockSpec((tf, D), lambda b, i, j: (j, 0)),
                  tile, vec, par, par, vec, vec],
        out_specs=[tile, tile],
        out_shape=[jax.ShapeDtypeStruct((B, S, D), F32), jax.ShapeDtypeStruct((B, S, D), BF16)],
        scratch_shapes=[pltpu.VMEM((tm, D), F32)],
        compiler_params=_cparams(("arbitrary", "arbitrary", "arbitrary"), 48),
        name="ffn_ln",
    )(h, w1, w3, w2, x, gate, ln_g, ln_b, nxt[0], nxt[1])


def _compress_kernel(x_ref, pos_ref, w1_ref, w2_ref, o_ref):
    dh = NSA_HEAD_DIM
    x = x_ref[0, 0]
    uv = jnp.dot(x, w1_ref[...], preferred_element_type=F32)
    pc = jnp.dot(pos_ref[...], w1_ref[...], preferred_element_type=F32)
    c = pc[0:1, :dh] + pc[1:2, dh:]
    n = uv.shape[0]
    nxt = pltpu.roll(uv[:, dh:], n - 1, 0)
    hid = uv[:, :dh] + nxt + c
    act = hid * jax.nn.sigmoid(hid)
    o_ref[0, 0] = jnp.dot(act.astype(BF16), w2_ref[...], preferred_element_type=F32).astype(o_ref.dtype)


def nsa_compress(kv, pos2, w1cat, w2):
    B, G, S, dh = kv.shape
    seg = NSA_CMP_STRIDE
    n = S // seg
    x = kv.reshape(B, G, n, seg * dh)
    return pl.pallas_call(
        _compress_kernel,
        grid=(B, G),
        in_specs=[
            pl.BlockSpec((1, 1, n, seg * dh), lambda b, g: (b, g, 0, 0)),
            pl.BlockSpec(pos2.shape, lambda b, g: (0, 0)),
            pl.BlockSpec(w1cat.shape, lambda b, g: (0, 0)),
            pl.BlockSpec(w2.shape, lambda b, g: (0, 0)),
        ],
        out_specs=pl.BlockSpec((1, 1, n, dh), lambda b, g: (b, g, 0, 0)),
        out_shape=jax.ShapeDtypeStruct((B, G, n, dh), BF16),
        compiler_params=_cparams(("arbitrary", "arbitrary")),
        name="nsa_compress",
    )(x, pos2, w1cat, w2)


def _nsa_kernel(q_ref, kc_ref, vc_ref, ks_ref, vs_ref, kw_ref, vw_ref, gt_ref, cb_ref, tb_ref, o_ref, sel_sc,
                s0_sc, s1_sc, p0_sc, p1_sc, *, S, TQ):
    dh = NSA_HEAD_DIM
    HPG = NSA_HPG
    LS = NSA_SLC_BLOCK
    TK = 4 * LS
    R = HPG * TQ
    n_cmp = (S - NSA_CMP_LEN) // NSA_CMP_STRIDE + 1
    n_cp = S // NSA_CMP_STRIDE
    n_slc = S // LS
    n_top = min(NSA_SLC_TOPN, n_slc)
    qi = pl.program_id(2)
    q0 = qi * TQ

    q4 = jnp.concatenate([q_ref[0, :, h * dh:(h + 1) * dh] for h in range(HPG)], axis=0)
    lanes4 = lambda a: jnp.concatenate([a] * HPG, axis=1)

    s = lax.dot_general(kc_ref[0, 0], q4, _NT, preferred_element_type=F32)
    s = s + jnp.concatenate([cb_ref[h] for h in range(HPG)], axis=1)
    n_row = lax.broadcasted_iota(jnp.int32, (n_cp, TQ), 0)
    t_q = q0 + lax.broadcasted_iota(jnp.int32, (n_cp, TQ), 1)
    valid_c = lanes4((n_row * NSA_CMP_STRIDE + (NSA_CMP_LEN - 1) <= t_q) & (n_row < n_cmp))
    s = jnp.where(valid_c, s, NEG_INF)
    m = jnp.max(s, axis=0, keepdims=True)
    m = jnp.where(m == NEG_INF, 0.0, m)
    e = jnp.exp2(s - m)
    p_c = e * (1.0 / jnp.maximum(jnp.sum(e, axis=0, keepdims=True), jnp.finfo(F32).tiny))
    o_c = lax.dot_general(vc_ref[0, 0], p_c.astype(BF16), _TN, preferred_element_type=F32)

    p_sum = p_c[:, 0:TQ]
    for h in range(1, HPG):
        p_sum = p_sum + p_c[:, h * TQ:(h + 1) * TQ]
    om = lax.broadcasted_iota(jnp.int32, (n_slc, n_cp), 0)
    c_lo = lax.broadcasted_iota(jnp.int32, (n_slc, n_cp), 1) * NSA_CMP_STRIDE
    ov = jnp.maximum(jnp.minimum(c_lo + NSA_CMP_LEN, om * LS + LS) - jnp.maximum(c_lo, om * LS), 0)
    overlap_t = ov.astype(F32) / NSA_CMP_LEN
    imp_t = jnp.dot(overlap_t, p_sum, precision=lax.Precision.HIGHEST, preferred_element_type=F32)
    blk = lax.broadcasted_iota(jnp.int32, (n_slc, TQ), 0)
    tq_lane = q0 + lax.broadcasted_iota(jnp.int32, (n_slc, TQ), 1)
    tb = lax.shift_right_logical(tq_lane, LS.bit_length() - 1)
    forced = (blk == 0) | (blk == tb) | (blk == tb - 1)
    key = jnp.where(blk > tb, NEG_INF, jnp.where(forced, jnp.inf, imp_t))
    rank = jnp.zeros((n_slc, TQ), jnp.int32)
    for mm in range(n_slc):
        km = key[mm:mm + 1, :]
        beats = (km > key) | ((km == key) & (mm < blk))
        rank = rank + beats.astype(jnp.int32)
    sel_sc[...] = jnp.where((rank < n_top) & (blk <= tb), 1.0, 0.0)

    def toep_tile(k0, n_keys):
        return jnp.concatenate(
            [jnp.concatenate([tb_ref[h, :, pl.ds(_toep_col(q0, k0 + r * BIAS_ROWS), TQ)] for h in range(HPG)], axis=1)
             for r in range(n_keys // BIAS_ROWS)], axis=0)

    def pipelined_attention(n, scores_fn, post_fn, v_fn):
        s_bufs = (s0_sc, s1_sc)
        p_bufs = (p0_sc, p1_sc)
        s_bufs[0][...] = scores_fn(0)
        p_bufs[1][...] = jnp.zeros(p_bufs[1].shape, BF16)

        def half(i, cur, carry):
            m, l, acc, alpha_prev = carry
            s_bufs[1 - cur][...] = scores_fn(jnp.minimum(i + 1, n - 1))
            s = post_fn(i, s_bufs[cur][...])
            m_new = jnp.maximum(m, jnp.max(s, axis=0, keepdims=True))
            alpha = jnp.exp2(m - m_new)
            p = jnp.exp2(s - m_new)
            l_new = alpha * l + jnp.sum(p, axis=0, keepdims=True)
            p_bufs[cur][...] = p.astype(BF16)
            pv = lax.dot_general(v_fn(jnp.maximum(i - 1, 0)), p_bufs[1 - cur][...], _TN, preferred_element_type=F32)
            return m_new, l_new, alpha_prev * acc + pv, alpha

        def body(jj, carry):
            return half(2 * jj + 1, 1, half(2 * jj, 0, carry))

        n_pairs = (n + 1) // 2
        init = (jnp.full((1, R), NEG_INF, F32), jnp.zeros((1, R), F32), jnp.zeros((dh, R), F32),
                jnp.ones((1, R), F32))
        _, l, acc, alpha = lax.fori_loop(0, n_pairs, body, init)
        pv = lax.dot_general(v_fn(jnp.minimum(2 * n_pairs - 1, n - 1)), p_bufs[1][...], _TN,
                             preferred_element_type=F32)
        return (alpha * acc + pv) * (1.0 / l)

    krow_s = lax.broadcasted_iota(jnp.int32, (TK, TQ), 0)
    tq_s = q0 + lax.broadcasted_iota(jnp.int32, (TK, TQ), 1)
    n_kt = (q0 + TQ - 1) // TK + 1

    def slc_scores(kt):
        return lax.dot_general(ks_ref[0, 0, pl.ds(pl.multiple_of(kt * TK, TK), TK), :], q4, _NT,
                               preferred_element_type=F32)

    def slc_post(kt, s):
        live = kt < n_kt
        ktc = jnp.minimum(kt, n_kt - 1)
        k0 = pl.multiple_of(ktc * TK, TK)
        flags = jnp.concatenate(
            [jnp.broadcast_to(sel_sc[pl.ds(ktc * (TK // LS) + j, 1), :], (LS, TQ)) for j in range(TK // LS)], axis=0)
        mask = lanes4((flags > jnp.where(live, 0.5, 2.0)) & (k0 + krow_s <= tq_s))
        return jnp.where(mask, s + toep_tile(k0, TK), NEG_INF)

    o_s = pipelined_attention(n_kt, slc_scores, slc_post,
                              lambda kt: vs_ref[0, 0, pl.ds(pl.multiple_of(kt * TK, TK), TK), :])

    krow_w = lax.broadcasted_iota(jnp.int32, (TQ, TQ), 0)
    tq_w = q0 + lax.broadcasted_iota(jnp.int32, (TQ, TQ), 1)
    n_wt = jnp.minimum(qi, NSA_WINDOW // TQ) + 1

    def win_body(j, carry):
        k0 = pl.multiple_of(q0 - j * TQ, TQ)
        dist = tq_w - (k0 + krow_w)
        mask = lanes4((dist >= 0) & (dist < NSA_WINDOW))
        s = lax.dot_general(kw_ref[0, 0, pl.ds(k0, TQ), :], q4, _NT, preferred_element_type=F32) + toep_tile(k0, TQ)
        return _softmax_step(carry, jnp.where(mask, s, NEG_INF), vw_ref[0, 0, pl.ds(k0, TQ), :])

    init_w = (jnp.full((1, R), NEG_INF, F32), jnp.zeros((1, R), F32), jnp.zeros((dh, R), F32))
    _, l_w, acc_w = lax.fori_loop(0, n_wt, win_body, init_w)
    o_w = acc_w * (1.0 / l_w)

    g_t = jax.nn.sigmoid(gt_ref[0, 0]).T
    outs = []
    for h in range(HPG):
        cols = slice(h * TQ, (h + 1) * TQ)
        o = (g_t[h:h + 1] * o_c[:, cols] + g_t[HPG + h:HPG + h + 1] * o_s[:, cols]
             + g_t[2 * HPG + h:2 * HPG + h + 1] * o_w[:, cols])
        outs.append(o.T)
    o_ref[0] = jnp.concatenate(outs, axis=1).astype(o_ref.dtype)


def nsa_attention(q, kc, vc, ks, vs, kw, vw, gates, cmpb, toep, TQ=256):
    B, S, _ = q.shape
    G, HPG, dh = NSA_KV_GROUPS, NSA_HPG, NSA_HEAD_DIM
    n_cp = S // NSA_CMP_STRIDE
    full_kv = pl.BlockSpec((1, 1, S, dh), lambda b, g, i: (b, g, 0, 0))
    cmp_kv = pl.BlockSpec((1, 1, n_cp, dh), lambda b, g, i: (b, g, 0, 0))
    return pl.pallas_call(
        functools.partial(_nsa_kernel, S=S, TQ=TQ),
        grid=(B, G, S // TQ),
        in_specs=[
            pl.BlockSpec((1, TQ, HPG * dh), lambda b, g, i: (b, i, g)),
            cmp_kv, cmp_kv, full_kv, full_kv, full_kv, full_kv,
            pl.BlockSpec((1, 1, TQ, LANE), lambda b, g, i: (b, g, i, 0)),
            pl.BlockSpec((HPG, n_cp, TQ), lambda b, g, i: (g, 0, i)),
            pl.BlockSpec((HPG, BIAS_ROWS, _toep_width(S)), lambda b, g, i: (g, 0, 0)),
        ],
        out_specs=pl.BlockSpec((1, TQ, HPG * dh), lambda b, g, i: (b, i, g)),
        out_shape=jax.ShapeDtypeStruct((B, S, NSA_Q_WIDTH), BF16),
        scratch_shapes=[pltpu.VMEM((S // NSA_SLC_BLOCK, TQ), F32)]
        + [pltpu.VMEM((TQ, HPG * TQ), F32)] * 2 + [pltpu.VMEM((TQ, HPG * TQ), BF16)] * 2,
        compiler_params=_cparams(("arbitrary", "arbitrary", "arbitrary"), 48),
        name="nsa_attn",
    )(q, kc, vc, ks, vs, kw, vw, gates, cmpb, toep)


def _router_kernel(h_ref, wr_ref, e_ref, w_ref, cnt_ref):
    @pl.when(pl.program_id(0) == 0)
    def _():
        cnt_ref[...] = jnp.zeros(cnt_ref.shape, F32)

    logits = jnp.dot(h_ref[...], wr_ref[...], precision=lax.Precision.HIGHEST, preferred_element_type=F32)
    tm = logits.shape[0]
    lane = lax.broadcasted_iota(jnp.int32, logits.shape, 1)
    logits = jnp.where(lane < N_EXPERTS, logits, NEG_INF)
    l1 = jnp.max(logits, axis=-1, keepdims=True)
    i1 = jnp.min(jnp.where(logits == l1, lane, LANE), axis=-1, keepdims=True)
    rest = jnp.where(lane == i1, NEG_INF, logits)
    l2 = jnp.max(rest, axis=-1, keepdims=True)
    i2 = jnp.min(jnp.where(rest == l2, lane, LANE), axis=-1, keepdims=True)
    e2 = jnp.exp(l2 - l1)
    den = 1.0 + e2
    w_ref[...] = jnp.where(lane == 0, 1.0 / den, jnp.where(lane == 1, e2 / den, 0.0))

    oh1 = lane == i1
    oh2 = lane == i2
    both = jnp.where(oh1 | oh2, 1.0, 0.0)
    earlier = lax.broadcasted_iota(jnp.int32, (tm, tm), 1) < lax.broadcasted_iota(jnp.int32, (tm, tm), 0)
    prefix = jnp.dot(jnp.where(earlier, 1.0, 0.0).astype(BF16), both.astype(BF16),
                     preferred_element_type=F32)
    seen = prefix + cnt_ref[0:1, :]
    p1 = jnp.sum(jnp.where(oh1, seen, 0.0), axis=-1, keepdims=True).astype(jnp.int32)
    p2 = jnp.sum(jnp.where(oh2, seen, 0.0), axis=-1, keepdims=True).astype(jnp.int32)
    e_ref[...] = jnp.where(lane == 0, i1, jnp.where(lane == 1, i2, jnp.where(lane == 2, p1,
                                                                              jnp.where(lane == 3, p2, 0))))
    cnt_ref[...] = cnt_ref[...] + jnp.sum(both, axis=0, keepdims=True)


def moe_route(h2, wr_pad, tm=512):
    N, D = h2.shape
    return pl.pallas_call(
        _router_kernel,
        grid=(N // tm,),
        in_specs=[pl.BlockSpec((tm, D), lambda i: (i, 0)), pl.BlockSpec(wr_pad.shape, lambda i: (0, 0))],
        out_specs=[pl.BlockSpec((tm, LANE), lambda i: (i, 0)), pl.BlockSpec((tm, LANE), lambda i: (i, 0)),
                   pl.BlockSpec((8, LANE), lambda i: (0, 0))],
        out_shape=[jax.ShapeDtypeStruct((N, LANE), jnp.int32), jax.ShapeDtypeStruct((N, LANE), F32),
                   jax.ShapeDtypeStruct((8, LANE), F32)],
        compiler_params=_cparams(("arbitrary",)),
        name="moe_router",
    )(h2, wr_pad)


def _dispatch_kernel(dest_ref, h_ref, xs_in_ref, xs_ref, sem):
    del xs_in_ref
    tm = h_ref.shape[0]
    base = pl.program_id(0) * tm * TOP_K

    def issue(r, carry):
        for k in range(TOP_K):
            d = dest_ref[base + r * TOP_K + k]
            pltpu.make_async_copy(h_ref.at[r], xs_ref.at[d], sem).start(priority=k)
        return carry

    lax.fori_loop(0, tm, issue, 0, unroll=16)
    for k in range(TOP_K):
        pltpu.make_async_copy(h_ref, xs_ref.at[pl.ds(0, tm)], sem).wait()


def moe_dispatch_rows(h2, dest_flat, n_rows, recycled=None, tm=512):
    N, D = h2.shape
    xs0 = jnp.zeros((n_rows, D), h2.dtype) if recycled is None else recycled
    grid_spec = pltpu.PrefetchScalarGridSpec(
        num_scalar_prefetch=1,
        grid=(N // tm,),
        in_specs=[pl.BlockSpec((tm, D), lambda i, d: (i, 0)), pl.BlockSpec(memory_space=pl.ANY)],
        out_specs=pl.BlockSpec(memory_space=pl.ANY),
        scratch_shapes=[pltpu.SemaphoreType.DMA],
    )
    return pl.pallas_call(
        _dispatch_kernel,
        grid_spec=grid_spec,
        out_shape=jax.ShapeDtypeStruct((n_rows, D), h2.dtype),
        input_output_aliases={2: 0},
        compiler_params=_cparams(("arbitrary",)),
        name="moe_dispatch",
    )(dest_flat, h2, xs0)


def _combine_ln_kernel(dest_ref, w_ref, ys_ref, x_ref, g_ref, lng_ref, lnb_ref, *rest, has_next):
    if has_next:
        sc_ref, sh_ref, o_ref, hn_ref, buf, sem = rest
    else:
        o_ref, buf, sem = rest
    tm = x_ref.shape[0]
    base = pl.program_id(0) * tm * TOP_K

    def issue(r, carry):
        for k in range(TOP_K):
            d = dest_ref[base + r * TOP_K + k]
            pltpu.make_async_copy(ys_ref.at[d], buf.at[k, r], sem).start(priority=k)
        return carry

    lax.fori_loop(0, tm, issue, 0, unroll=16)
    for k in range(TOP_K):
        pltpu.make_async_copy(ys_ref.at[pl.ds(0, tm)], buf.at[k], sem).wait()
    w = w_ref[...]
    y = w[:, 0:1] * buf[0] + w[:, 1:2] * buf[1]
    xn = _deepnorm_ln(x_ref[...], y, g_ref[0], lng_ref[...], lnb_ref[...])
    o_ref[...] = xn
    if has_next:
        hn_ref[...] = (xn * (1.0 + sc_ref[0]) + sh_ref[0]).astype(hn_ref.dtype)


def moe_combine_ln(ys, dest_flat, w_top, x, gate, ln_g, ln_b, nxt, tm=512):
    B, S, D = x.shape
    N = B * S
    per_b = S // tm
    tile = pl.BlockSpec((tm, D), lambda i, d: (i, 0))
    vec = pl.BlockSpec((1, 1, D), lambda i, d: (i // per_b, 0, 0))
    par = pl.BlockSpec((1, D), lambda i, d: (0, 0))
    in_specs = [pl.BlockSpec((tm, LANE), lambda i, d: (i, 0)), pl.BlockSpec(memory_space=pl.ANY), tile, vec, par, par]
    args = [w_top, ys, x.reshape(N, D), gate, ln_g, ln_b]
    out_shape = [jax.ShapeDtypeStruct((N, D), F32)]
    out_specs = [tile]
    if nxt is not None:
        in_specs += [vec, vec]
        args += list(nxt)
        out_shape.append(jax.ShapeDtypeStruct((N, D), BF16))
        out_specs.append(tile)
    grid_spec = pltpu.PrefetchScalarGridSpec(
        num_scalar_prefetch=1,
        grid=(N // tm,),
        in_specs=in_specs,
        out_specs=out_specs,
        scratch_shapes=[pltpu.VMEM((TOP_K, tm, D), F32), pltpu.SemaphoreType.DMA],
    )
    res = pl.pallas_call(
        functools.partial(_combine_ln_kernel, has_next=nxt is not None),
        grid_spec=grid_spec,
        out_shape=out_shape,
        compiler_params=_cparams(("arbitrary",)),
        name="moe_combine_ln",
    )(dest_flat, *args)
    x_new = res[0].reshape(B, S, D)
    return (x_new, res[1].reshape(B, S, D)) if nxt is not None else (x_new, None)


def _moe_ffn_kernel(be_ref, nu_ref, x_ref, w1_ref, w3_ref, w2_ref, o_ref, acc_sc, xb_sc):
    i = pl.program_id(0)
    j = pl.program_id(1)
    last = pl.num_programs(1) - 1
    used = i < nu_ref[0]

    @pl.when(j == 0)
    def _():
        acc_sc[...] = jnp.zeros(acc_sc.shape, F32)
        xb_sc[...] = x_ref[...].astype(BF16)

    @pl.when(used)
    def _():
        x = xb_sc[...]
        a = jnp.dot(x, w1_ref[0].astype(BF16), preferred_element_type=F32)
        b = jnp.dot(x, w3_ref[0].astype(BF16), preferred_element_type=F32)
        u = (a * jax.nn.sigmoid(a)) * b
        acc_sc[...] += jnp.dot(u.astype(BF16), w2_ref[0].astype(BF16), preferred_element_type=F32)

    @pl.when(j == last)
    def _():
        o_ref[...] = acc_sc[...]


def moe_ffn(xs, blk_e, n_used, w1, w3, w2, tf=512):
    n_rows, D = xs.shape
    n_blocks = blk_e.shape[0]
    RB = n_rows // n_blocks
    F = w1.shape[2]
    nf = F // tf

    def wcol(i, j, be, nu):
        return (be[i], 0, jnp.where(i < nu[0], j, nf - 1))

    def wrow(i, j, be, nu):
        return (be[i], jnp.where(i < nu[0], j, nf - 1), 0)

    grid_spec = pltpu.PrefetchScalarGridSpec(
        num_scalar_prefetch=2,
        grid=(n_blocks, nf),
        in_specs=[
            pl.BlockSpec((RB, D), lambda i, j, be, nu: (i, 0)),
            pl.BlockSpec((1, D, tf), wcol),
            pl.BlockSpec((1, D, tf), wcol),
            pl.BlockSpec((1, tf, D), wrow),
        ],
        out_specs=pl.BlockSpec((RB, D), lambda i, j, be, nu: (i, 0)),
        scratch_shapes=[pltpu.VMEM((RB, D), F32), pltpu.VMEM((RB, D), BF16)],
    )
    return pl.pallas_call(
        _moe_ffn_kernel,
        grid_spec=grid_spec,
        out_shape=jax.ShapeDtypeStruct((n_rows, D), F32),
        compiler_params=_cparams(("arbitrary", "arbitrary"), 56),
        name="moe_ffn",
    )(blk_e, n_used, xs, w1, w3, w2)


def moe_layout(e_pos, counts_f, n_tok):
    RB = _moe_row_block(n_tok)
    n_assign = n_tok * TOP_K
    n_blocks = (n_assign + N_EXPERTS * (RB - 1) + RB - 1) // RB
    counts = counts_f[0, :N_EXPERTS].astype(jnp.int32)
    padded = (counts + RB - 1) // RB * RB
    pad_end = jnp.cumsum(padded)
    pad_start = pad_end - padded
    e = e_pos[:, 0:TOP_K]
    pos = e_pos[:, TOP_K:2 * TOP_K]
    start = jnp.zeros_like(e)
    for x in range(N_EXPERTS):
        start = jnp.where(e == x, pad_start[x], start)
    dest_flat = (start + pos).reshape(-1)
    blk_start = jnp.arange(n_blocks, dtype=jnp.int32) * RB
    blk_e = jnp.minimum(jnp.sum((blk_start[:, None] >= pad_end[None, :]).astype(jnp.int32), axis=1), N_EXPERTS - 1)
    n_used = (pad_end[-1] // RB).astype(jnp.int32).reshape(1)
    return dest_flat, blk_e.astype(jnp.int32), n_used, n_blocks * RB


def _pad_cols(w, width):
    return jnp.pad(w, ((0, 0), (0, width - w.shape[1])))


def kernel(x, c, rel_bias, ada_w, ada_b, ln_g, ln_b, ab_w_in, gla_wg2, gla_bg2, gla_norm_g, ab_w_out,
           ffn_w1, ffn_w3, ffn_w2, nsa_w_in, nsa_cmp_pos, nsa_cmp_w1k, nsa_cmp_w2k, nsa_cmp_w1v, nsa_cmp_w2v,
           nsa_w_out, moe_router, moe_w1, moe_w3, moe_w2):
    B, S, D = x.shape
    n_tok = B * S
    mod = adaln_mod(c, ada_w, ada_b)
    mods = [[m.reshape(B, 1, D) for m in jnp.split(mod[i], 6, axis=-1)] for i in range(DEPTH)]
    toep, cmpb = build_bias_tables(rel_bias, S)

    xs_dead = None
    sh1, sc1 = mods[0][0], mods[0][1]
    h = modulate(x, sc1, sh1)
    for i in range(DEPTH):
        j = i // 2
        _, _, g1, sh2, sc2, g2 = mods[i]
        nxt2 = (mods[i + 1][1], mods[i + 1][0]) if i + 1 < DEPTH else None
        lng = ln_g[i].reshape(2, 1, D)
        lnb = ln_b[i].reshape(2, 1, D)
        if i % 2 == 0:
            w = ab_w_in[j]
            cuts = np.cumsum((MOBA_WIDTH, MOBA_WIDTH, MOBA_WIDTH, GLA_QK_WIDTH, GLA_QK_WIDTH, GLA_V_WIDTH,
                              GLA_V_WIDTH))
            parts = jnp.split(w, cuts.tolist(), axis=1)
            parts[-1] = _pad_cols(parts[-1], LANE)
            w_cat = jnp.concatenate(parts, axis=1).astype(BF16)
            widths = [p.shape[1] for p in parts]
            offs = np.concatenate([[0], np.cumsum(widths)[:-1]]).tolist()
            scales = [MOBA_HEAD_DIM ** -0.5 * LOG2E] + [1.0] * 7
            kinds = ["flat", "flat", "flat_t"] + ["flat"] * 5
            offs[2] = 0
            segs = tuple((o, wd, kd, s) for o, wd, kd, s in zip(offs, widths, kinds, scales))
            dts = [BF16, BF16, BF16, F32, F32, F32, F32, F32]
            mq, mk, mv, gq, gk, gv, gr, glr = project(h, w_cat, segs, dts, wt=parts[2].T.astype(BF16))
            y_moba = moba_attention(mq, mk, mv, toep)
            wg2_pad = jnp.pad(gla_wg2[j], ((0, LANE - GLA_GATE_RANK), (0, 0)))
            y_gla = gla_mixer(gq, gk, gv, gr, glr, wg2_pad, gla_bg2[j].reshape(1, -1), gla_norm_g[j].reshape(1, -1))
            w_out = ab_w_out[j].astype(BF16)
            x, h = out_ln([y_moba, y_gla], [w_out[:MOBA_WIDTH], w_out[MOBA_WIDTH:]], x, g1, lng[0], lnb[0],
                          (sc2, sh2))
            x, h = ffn_ln(h, ffn_w1[j].astype(BF16), ffn_w3[j].astype(BF16), ffn_w2[j].astype(BF16), x, g2,
                          lng[1], lnb[1], nxt2)
        else:
            G, HPG, dh = NSA_KV_GROUPS, NSA_HPG, NSA_HEAD_DIM
            w = nsa_w_in[j]
            wq = w[:, :NSA_Q_WIDTH]
            wkv = w[:, NSA_Q_WIDTH:NSA_Q_WIDTH + 6 * NSA_KV_WIDTH]
            wg = w[:, NSA_Q_WIDTH + 6 * NSA_KV_WIDTH:].reshape(D, G, HPG, 3)
            wg = _pad_cols(wg.transpose(0, 1, 3, 2).reshape(D * G, 3 * HPG), LANE).reshape(D, G * LANE)
            w_cat = jnp.concatenate([wq, wkv, wg], axis=1).astype(BF16)
            segs = [(0, NSA_Q_WIDTH, "flat", dh ** -0.5 * LOG2E)]
            segs += [(NSA_Q_WIDTH + n * NSA_KV_WIDTH, NSA_KV_WIDTH, "group", 1.0) for n in range(6)]
            segs += [(NSA_Q_WIDTH + 6 * NSA_KV_WIDTH, G * LANE, "group", 1.0)]
            dts = [BF16] * 7 + [F32]
            q, kc, vc, ks, vs, kw, vw, gates = project(h, w_cat, tuple(segs), dts)
            half = NSA_CMP_STRIDE * dh
            pos2 = jnp.pad(nsa_cmp_pos[j].reshape(2, half), ((0, 6), (0, 0))).astype(BF16)
            w1k = jnp.concatenate([nsa_cmp_w1k[j][:half], nsa_cmp_w1k[j][half:]], axis=1).astype(BF16)
            w1v = jnp.concatenate([nsa_cmp_w1v[j][:half], nsa_cmp_w1v[j][half:]], axis=1).astype(BF16)
            k_cmp = nsa_compress(kc, pos2, w1k, nsa_cmp_w2k[j].astype(BF16))
            v_cmp = nsa_compress(vc, pos2, w1v, nsa_cmp_w2v[j].astype(BF16))
            y = nsa_attention(q, k_cmp, v_cmp, ks, vs, kw, vw, gates, cmpb, toep)
            x, h = out_ln([y], [nsa_w_out[j].astype(BF16)], x, g1, lng[0], lnb[0], (sc2, sh2), h_dtype=F32)
            h2 = h.reshape(n_tok, D)
            e_pos, w_top, counts = moe_route(h2, _pad_cols(moe_router[j], LANE))
            dest_flat, blk_e, n_used, n_rows = moe_layout(e_pos, counts, n_tok)
            xs = moe_dispatch_rows(h2, dest_flat, n_rows, recycled=xs_dead)
            xs_dead = xs
            stack = lambda w: w.reshape((-1,) + w.shape[2:])
            ys = moe_ffn(xs, blk_e + j * N_EXPERTS, n_used, stack(moe_w1), stack(moe_w3), stack(moe_w2))
            x, h = moe_combine_ln(ys, dest_flat, w_top, x, g2, lng[1], lnb[1], nxt2)
    return x
```

```python
import functools
import math

import jax
import jax.numpy as jnp
import numpy as np
from jax import lax
from jax.experimental import pallas as pl
from jax.experimental.pallas import tpu as pltpu

F32 = jnp.float32
BF16 = jnp.bfloat16
NEG_INF = float("-inf")

DEPTH = 4
ALPHA_DN = (2 * DEPTH) ** 0.25
LN_EPS = 1e-5
RMS_EPS = 1e-6

N_BUCKETS = 32
MAX_DISTANCE = 1024

MOBA_HEADS = 8
MOBA_HEAD_DIM = 64
MOBA_BLOCK = 256
MOBA_TOPK = 3
MOBA_WIDTH = MOBA_HEADS * MOBA_HEAD_DIM

GLA_HEADS = 4
GLA_DK = 64
GLA_DV = 128
GLA_GATE_RANK = 16
GLA_TAU = 16.0
GLA_CHUNK = 64
GLA_QK_WIDTH = GLA_HEADS * GLA_DK
GLA_V_WIDTH = GLA_HEADS * GLA_DV

NSA_HEADS = 8
NSA_KV_GROUPS = 2
NSA_HPG = NSA_HEADS // NSA_KV_GROUPS
NSA_HEAD_DIM = 128
NSA_CMP_LEN = 32
NSA_CMP_STRIDE = 16
NSA_SLC_BLOCK = 64
NSA_SLC_TOPN = 16
NSA_WINDOW = 512
NSA_Q_WIDTH = NSA_HEADS * NSA_HEAD_DIM
NSA_KV_WIDTH = NSA_KV_GROUPS * NSA_HEAD_DIM

N_EXPERTS = 8
TOP_K = 2
MOE_BLOCKS_PER_EXPERT = 4


def _moe_row_block(n_tok):
    balanced = n_tok * TOP_K // N_EXPERTS
    rows = balanced // MOE_BLOCKS_PER_EXPERT + balanced // 128
    return -(-rows // 16) * 16

LOG2E = math.log2(math.e)
LANE = 128
BIAS_ROWS = 128


def _cparams(sem, vmem_mib=40, **extra):
    return pltpu.CompilerParams(dimension_semantics=sem, vmem_limit_bytes=vmem_mib << 20, **extra)


def _adaln_kernel(c_ref, w_ref, b_ref, o_ref):
    c = c_ref[...]
    cond = c * jax.nn.sigmoid(c)
    o_ref[0] = jnp.dot(cond, w_ref[0], preferred_element_type=F32) + b_ref[0]


def adaln_mod(c, ada_w, ada_b):
    B, D = c.shape
    L, _, N = ada_w.shape
    rows = 8
    cp = jnp.zeros((rows, D), F32).at[:B].set(c)
    tn = N // 4
    out = pl.pallas_call(
        _adaln_kernel,
        grid=(L, N // tn),
        in_specs=[
            pl.BlockSpec((rows, D), lambda l, j: (0, 0)),
            pl.BlockSpec((1, D, tn), lambda l, j: (l, 0, j)),
            pl.BlockSpec((1, 1, tn), lambda l, j: (l, 0, j)),
        ],
        out_specs=pl.BlockSpec((1, rows, tn), lambda l, j: (l, 0, j)),
        out_shape=jax.ShapeDtypeStruct((L, rows, N), F32),
        compiler_params=_cparams(("arbitrary", "arbitrary")),
        name="adaln_mod",
    )(cp, ada_w, ada_b.reshape(L, 1, N))
    return out[:, :B]


def _modulate_kernel(x_ref, sc_ref, sh_ref, o_ref):
    o_ref[0] = (x_ref[0] * (1.0 + sc_ref[0]) + sh_ref[0]).astype(o_ref.dtype)


def modulate(x, sc, sh, tm=512):
    B, S, D = x.shape
    return pl.pallas_call(
        _modulate_kernel,
        grid=(B, S // tm),
        in_specs=[
            pl.BlockSpec((1, tm, D), lambda b, i: (b, i, 0)),
            pl.BlockSpec((1, 1, D), lambda b, i: (b, 0, 0)),
            pl.BlockSpec((1, 1, D), lambda b, i: (b, 0, 0)),
        ],
        out_specs=pl.BlockSpec((1, tm, D), lambda b, i: (b, i, 0)),
        out_shape=jax.ShapeDtypeStruct((B, S, D), BF16),
        compiler_params=_cparams(("arbitrary", "arbitrary")),
        name="modulate",
    )(x, sc, sh)


def _t5_bucket(dist):
    max_exact = N_BUCKETS // 2
    n = jnp.maximum(dist, 0)
    nf = jnp.maximum(n, 1).astype(jnp.float32)
    large = max_exact + (jnp.log(nf / max_exact) / math.log(MAX_DISTANCE / max_exact) * (N_BUCKETS - max_exact)).astype(jnp.int32)
    large = jnp.minimum(large, N_BUCKETS - 1)
    return jnp.where(n < max_exact, n, large)


def _bias_table_kernel(rb_ref, bm_ref, o_ref, *, rows_per_step):
    h = pl.program_id(0)
    n_steps = bm_ref.shape[0] // rows_per_step

    def body(i, carry):
        r0 = pl.multiple_of(i * rows_per_step, rows_per_step)
        bm = bm_ref[pl.ds(r0, rows_per_step), :]
        acc = jnp.zeros(bm.shape, F32)
        for b in range(N_BUCKETS):
            acc = jnp.where(bm == b, rb_ref[b, h] * LOG2E, acc)
        o_ref[0, pl.ds(r0, rows_per_step), :] = acc
        return carry

    lax.fori_loop(0, n_steps, body, 0)


def bias_table(rel_bias, bucket_map, rows_per_step):
    R, W = bucket_map.shape
    H = rel_bias.shape[1]
    return pl.pallas_call(
        functools.partial(_bias_table_kernel, rows_per_step=rows_per_step),
        grid=(H,),
        in_specs=[
            pl.BlockSpec(memory_space=pltpu.SMEM),
            pl.BlockSpec((R, W), lambda h: (0, 0)),
        ],
        out_specs=pl.BlockSpec((1, R, W), lambda h: (h, 0, 0)),
        out_shape=jax.ShapeDtypeStruct((H, R, W), F32),
        compiler_params=_cparams(("arbitrary",)),
        name="bias_table",
    )(rel_bias, bucket_map)


def _toep_width(S):
    return S + BIAS_ROWS


def _toep_col(q0, k0):
    return pl.multiple_of(q0 - k0 + BIAS_ROWS, LANE)


def build_bias_tables(rel_bias, S):
    j = jnp.arange(BIAS_ROWS, dtype=jnp.int32)[:, None]
    c = jnp.arange(_toep_width(S), dtype=jnp.int32)[None, :]
    toep_map = _t5_bucket(c - j - BIAS_ROWS)
    n_cmp_pad = S // NSA_CMP_STRIDE
    t = jnp.arange(S, dtype=jnp.int32)[None, :]
    cmp_end = jnp.arange(n_cmp_pad, dtype=jnp.int32)[:, None] * NSA_CMP_STRIDE + NSA_CMP_LEN - 1
    cmp_map = _t5_bucket(t - cmp_end)
    return bias_table(rel_bias, toep_map, 8), bias_table(rel_bias, cmp_map, 8)


def _proj_kernel(h_ref, w_ref, wt_ref, *o_refs, segs):
    h = h_ref[0]
    for (off, width, kind, scale), o_ref in zip(segs, o_refs):
        if kind == "flat_t":
            acc = lax.dot_general(wt_ref[off:off + width, :], h, _NT, preferred_element_type=F32)
        else:
            acc = jnp.dot(h, w_ref[:, off:off + width], preferred_element_type=F32)
        if scale != 1.0:
            acc = acc * scale
        if kind in ("flat", "flat_t"):
            o_ref[0] = acc.astype(o_ref.dtype)
        else:
            for g in range(width // LANE):
                o_ref[0, g] = acc[:, g * LANE:(g + 1) * LANE].astype(o_ref.dtype)


def project(h, w_cat, segs, out_dtypes, wt=None, tm=512):
    B, S, D = h.shape
    if wt is None:
        wt = jnp.zeros((8, D), w_cat.dtype)
    out_shapes, out_specs = [], []
    for (off, width, kind, scale), dt in zip(segs, out_dtypes):
        if kind == "flat":
            out_shapes.append(jax.ShapeDtypeStruct((B, S, width), dt))
            out_specs.append(pl.BlockSpec((1, tm, width), lambda b, i: (b, i, 0)))
        elif kind == "flat_t":
            out_shapes.append(jax.ShapeDtypeStruct((B, width, S), dt))
            out_specs.append(pl.BlockSpec((1, width, tm), lambda b, i: (b, 0, i)))
        else:
            G = width // LANE
            out_shapes.append(jax.ShapeDtypeStruct((B, G, S, LANE), dt))
            out_specs.append(pl.BlockSpec((1, G, tm, LANE), lambda b, i: (b, 0, i, 0)))
    return pl.pallas_call(
        functools.partial(_proj_kernel, segs=segs),
        grid=(B, S // tm),
        in_specs=[
            pl.BlockSpec((1, tm, D), lambda b, i: (b, i, 0)),
            pl.BlockSpec(w_cat.shape, lambda b, i: (0, 0)),
            pl.BlockSpec(wt.shape, lambda b, i: (0, 0)),
        ],
        out_specs=out_specs,
        out_shape=out_shapes,
        compiler_params=_cparams(("arbitrary", "arbitrary"), 48),
        name="in_proj",
    )(h, w_cat, wt)


_NT = (((1,), (1,)), ((), ()))
_TN = (((0,), (0,)), ((), ()))


def _softmax_step(carry, s, v_tile):
    m, l, acc = carry
    m_new = jnp.maximum(m, jnp.max(s, axis=0, keepdims=True))
    alpha = jnp.exp2(m - m_new)
    p = jnp.exp2(s - m_new)
    l_new = alpha * l + jnp.sum(p, axis=0, keepdims=True)
    acc_new = alpha * acc + lax.dot_general(v_tile, p.astype(BF16), _TN, preferred_element_type=F32)
    return m_new, l_new, acc_new


def _moba_kernel(q_ref, k_ref, v_ref, tb_ref, o_ref, kmean_sc, sel_sc, s0_sc, s1_sc, p0_sc, p1_sc,
                 *, S, heads_per_step):
    L = MOBA_BLOCK
    dh = MOBA_HEAD_DIM
    nb = S // L
    qi = pl.program_id(2)
    q0 = qi * L

    @pl.when(qi == 0)
    def _():
        kf = k_ref[0].astype(F32)
        kmean_sc[...] = jnp.mean(kf.reshape(nb, L, kf.shape[-1]), axis=1)

    q2 = q_ref[0]
    lane_q = lax.broadcasted_iota(jnp.int32, q2.shape, 1)
    lane_m = lax.broadcasted_iota(jnp.int32, kmean_sc.shape, 1)
    blk = lax.broadcasted_iota(jnp.int32, (nb, L), 0)
    qz = []
    for hh in range(heads_per_step):
        in_head = (lane_q >= hh * dh) & (lane_q < (hh + 1) * dh)
        qz.append(jnp.where(in_head, q2, jnp.zeros_like(q2)))
        km = jnp.where((lane_m >= hh * dh) & (lane_m < (hh + 1) * dh), kmean_sc[...], 0.0)
        gate = lax.dot_general(km, q2.astype(F32), _NT, precision=lax.Precision.HIGHEST,
                               preferred_element_type=F32)
        valid = blk < qi
        gm = jnp.where(valid, gate, NEG_INF)
        rank = jnp.zeros((nb, L), jnp.int32)
        for m in range(nb):
            gmm = gm[m:m + 1, :]
            beats = (gmm > gm) | ((gmm == gm) & (m < blk))
            rank = rank + beats.astype(jnp.int32)
        sel_sc[hh] = jnp.where(valid & (rank < MOBA_TOPK), 1.0, 0.0)

    def bias_tile(hh, k0, n_keys):
        return jnp.concatenate([tb_ref[hh, :, pl.ds(_toep_col(q0, k0 + r * BIAS_ROWS), L)]
                                for r in range(n_keys // BIAS_ROWS)], axis=0)

    krow = lax.broadcasted_iota(jnp.int32, (L, L), 0)
    qcol = lax.broadcasted_iota(jnp.int32, (L, L), 1)
    k_own = k_ref[0, pl.ds(pl.multiple_of(q0, L), L), :]
    v_tile = lambda hh, k0, n_keys: v_ref[0, hh * dh:(hh + 1) * dh, pl.ds(k0, n_keys)]
    carry = []
    for hh in range(heads_per_step):
        s = lax.dot_general(k_own, qz[hh], _NT, preferred_element_type=F32) + bias_tile(hh, q0, L)
        s = jnp.where(krow <= qcol, s, NEG_INF)
        m0 = jnp.max(s, axis=0, keepdims=True)
        p = jnp.exp2(s - m0)
        l0 = jnp.sum(p, axis=0, keepdims=True)
        acc0 = jnp.dot(v_tile(hh, pl.multiple_of(q0, L), L), p.astype(BF16), preferred_element_type=F32)
        carry += [m0, l0, acc0, jnp.ones((1, L), F32)]

    n = (qi + 1) // 2
    s_bufs = (s0_sc, s1_sc)
    p_bufs = (p0_sc, p1_sc)
    tile_k0 = lambda i: pl.multiple_of(jnp.clip(i, 0, jnp.maximum(n - 1, 0)) * 2 * L, 2 * L)

    def put_scores(i, buf):
        kt = k_ref[0, pl.ds(tile_k0(i), 2 * L), :]
        for hh in range(heads_per_step):
            buf[hh] = lax.dot_general(kt, qz[hh], _NT, preferred_element_type=F32)

    put_scores(0, s_bufs[0])
    p_bufs[1][...] = jnp.zeros(p_bufs[1].shape, BF16)

    def half(i, cur, carry):
        put_scores(i + 1, s_bufs[1 - cur])
        k0 = tile_k0(i)
        blk0 = lax.shift_right_logical(k0, L.bit_length() - 1)
        thresh = jnp.where(i < n, 0.5, 2.0)
        out = []
        for hh in range(heads_per_step):
            m, l, acc, alpha_prev = carry[4 * hh:4 * hh + 4]
            flags = jnp.concatenate(
                [jnp.broadcast_to(sel_sc[hh, pl.ds(blk0 + j, 1), :], (L, L)) for j in range(2)], axis=0)
            s = jnp.where(flags > thresh, s_bufs[cur][hh] + bias_tile(hh, k0, 2 * L), NEG_INF)
            m_new = jnp.maximum(m, jnp.max(s, axis=0, keepdims=True))
            alpha = jnp.exp2(m - m_new)
            p = jnp.exp2(s - m_new)
            l_new = alpha * l + jnp.sum(p, axis=0, keepdims=True)
            p_bufs[cur][hh] = p.astype(BF16)
            pv = jnp.dot(v_tile(hh, tile_k0(i - 1), 2 * L), p_bufs[1 - cur][hh], preferred_element_type=F32)
            out += [m_new, l_new, alpha_prev * acc + pv, alpha]
        return tuple(out)

    def body(jj, carry):
        return half(2 * jj + 1, 1, half(2 * jj, 0, carry))

    n_pairs = (n + 1) // 2
    carry = lax.fori_loop(0, n_pairs, body, tuple(carry))
    outs = []
    for hh in range(heads_per_step):
        _, l, acc, alpha = carry[4 * hh:4 * hh + 4]
        pv = jnp.dot(v_tile(hh, tile_k0(2 * n_pairs - 1), 2 * L), p_bufs[1][hh], preferred_element_type=F32)
        outs.append((alpha * acc + pv) * (1.0 / l))
    o_ref[0] = jnp.concatenate(outs, axis=0).T.astype(o_ref.dtype)


def moba_attention(q, k, v, toep):
    B, S, _ = q.shape
    L = MOBA_BLOCK
    hps = 2
    wl = hps * MOBA_HEAD_DIM
    n_hp = MOBA_HEADS // hps
    nb = S // L
    return pl.pallas_call(
        functools.partial(_moba_kernel, S=S, heads_per_step=hps),
        grid=(n_hp, B, nb),
        in_specs=[
            pl.BlockSpec((1, L, wl), lambda hp, b, i: (b, i, hp)),
            pl.BlockSpec((1, S, wl), lambda hp, b, i: (b, 0, hp)),
            pl.BlockSpec((1, wl, S), lambda hp, b, i: (b, hp, 0)),
            pl.BlockSpec((hps, BIAS_ROWS, _toep_width(S)), lambda hp, b, i: (hp, 0, 0)),
        ],
        out_specs=pl.BlockSpec((1, L, wl), lambda hp, b, i: (b, i, hp)),
        out_shape=jax.ShapeDtypeStruct((B, S, MOBA_WIDTH), BF16),
        scratch_shapes=[pltpu.VMEM((nb, wl), F32), pltpu.VMEM((hps, nb, L), F32)]
        + [pltpu.VMEM((hps, 2 * L, L), F32)] * 2 + [pltpu.VMEM((hps, 2 * L, L), BF16)] * 2,
        compiler_params=_cparams(("arbitrary", "arbitrary", "arbitrary"), 48),
        name="moba_attn",
    )(q, k, v, toep)


def _log_sigmoid(z):
    return jnp.minimum(z, 0.0) - jnp.log1p(jnp.exp(-jnp.abs(z)))


def _gla_kernel(q_ref, k_ref, v_ref, r_ref, lr_ref, wg2_ref, bg2_ref, ng_ref, o_ref, st_sc, *, batch):
    C = GLA_CHUNK
    dk, dv = GLA_DK, GLA_DV

    @pl.when(pl.program_id(0) == 0)
    def _():
        st_sc[...] = jnp.zeros(st_sc.shape, F32)

    row = lax.broadcasted_iota(jnp.int32, (C, C), 0)
    col = lax.broadcasted_iota(jnp.int32, (C, C), 1)
    causal = col <= row
    tri = jnp.where(causal, 1.0, 0.0)
    for b in range(batch):
        z = jnp.dot(lr_ref[b], wg2_ref[...], preferred_element_type=F32) + bg2_ref[...]
        la = _log_sigmoid(z) / GLA_TAU
        bc = jnp.dot(tri, la, precision=lax.Precision.HIGHEST, preferred_element_type=F32)
        b_last = bc[C - 1:C, :]
        qd = q_ref[b] * (dk ** -0.5) * jnp.exp(bc)
        kd = k_ref[b] * jnp.exp(-bc)
        kl = k_ref[b] * jnp.exp(b_last - bc)
        e_last = jnp.exp(b_last)
        v = v_ref[b]
        r = r_ref[b]
        outs = []
        for h in range(GLA_HEADS):
            ks = slice(h * dk, (h + 1) * dk)
            vs = slice(h * dv, (h + 1) * dv)
            qh = qd[:, ks].astype(BF16)
            kh = kd[:, ks].astype(BF16)
            vh = v[:, vs].astype(BF16)
            attn = lax.dot_general(qh, kh, (((1,), (1,)), ((), ())), preferred_element_type=F32)
            attn = jnp.where(causal, attn, 0.0)
            st = st_sc[b, h]
            o = jnp.dot(attn.astype(BF16), vh, preferred_element_type=F32)
            o = o + lax.dot_general(qh, st.astype(BF16), (((1,), (1,)), ((), ())), preferred_element_type=F32)
            upd = lax.dot_general(vh, kl[:, ks].astype(BF16), (((0,), (0,)), ((), ())), preferred_element_type=F32)
            st_sc[b, h] = st * e_last[:, ks] + upd
            o = o * lax.rsqrt(jnp.mean(o * o, axis=-1, keepdims=True) + RMS_EPS)
            rg = r[:, vs]
            outs.append(o * ng_ref[:, vs] * (rg * jax.nn.sigmoid(rg)))
        o_ref[b] = jnp.concatenate(outs, axis=-1).astype(o_ref.dtype)


def gla_mixer(q, k, v, r, lr, wg2_pad, bg2, norm_g):
    B, S, _ = q.shape
    C = GLA_CHUNK
    nc = S // C
    blk = lambda w: pl.BlockSpec((B, C, w), lambda c: (0, c, 0))
    full = lambda a: pl.BlockSpec(a.shape, lambda c: (0,) * a.ndim)
    return pl.pallas_call(
        functools.partial(_gla_kernel, batch=B),
        grid=(nc,),
        in_specs=[blk(GLA_QK_WIDTH), blk(GLA_QK_WIDTH), blk(GLA_V_WIDTH), blk(GLA_V_WIDTH), blk(LANE),
                  full(wg2_pad), full(bg2), full(norm_g)],
        out_specs=blk(GLA_V_WIDTH),
        out_shape=jax.ShapeDtypeStruct((B, S, GLA_V_WIDTH), BF16),
        scratch_shapes=[pltpu.VMEM((B, GLA_HEADS, GLA_DV, GLA_DK), F32)],
        compiler_params=_cparams(("arbitrary",)),
        name="gla",
    )(q, k, v, r, lr, wg2_pad, bg2, norm_g)


def _deepnorm_ln(x, y, gate, ln_g, ln_b):
    z = ALPHA_DN * x + (1.0 + gate) * y
    mu = jnp.mean(z, axis=-1, keepdims=True)
    zc = z - mu
    var = jnp.mean(zc * zc, axis=-1, keepdims=True)
    return zc * lax.rsqrt(var + LN_EPS) * ln_g + ln_b


def _out_ln_kernel(*refs, n_parts, matmul, has_next):
    y_refs = refs[:n_parts]
    pos = n_parts
    if matmul:
        w_refs = refs[pos:pos + n_parts]
        pos += n_parts
    x_ref, g_ref, lng_ref, lnb_ref = refs[pos:pos + 4]
    pos += 4
    if has_next:
        sc_ref, sh_ref = refs[pos:pos + 2]
        pos += 2
    o_ref = refs[pos]
    y = None
    for i in range(n_parts):
        part = jnp.dot(y_refs[i][0], w_refs[i][...], preferred_element_type=F32) if matmul else y_refs[i][0]
        y = part if y is None else y + part
    xn = _deepnorm_ln(x_ref[0], y, g_ref[0], lng_ref[...], lnb_ref[...])
    o_ref[0] = xn
    if has_next:
        refs[pos + 1][0] = (xn * (1.0 + sc_ref[0]) + sh_ref[0]).astype(refs[pos + 1].dtype)


def out_ln(ys, ws, x, gate, ln_g, ln_b, nxt, h_dtype=BF16, tm=512):
    B, S, D = x.shape
    tile = lambda w: pl.BlockSpec((1, tm, w), lambda b, i: (b, i, 0))
    vec = pl.BlockSpec((1, 1, D), lambda b, i: (b, 0, 0))
    par = pl.BlockSpec((1, D), lambda b, i: (0, 0))
    args = list(ys)
    in_specs = [tile(y.shape[-1]) for y in ys]
    if ws is not None:
        args += list(ws)
        in_specs += [pl.BlockSpec(w.shape, lambda b, i: (0, 0)) for w in ws]
    args += [x, gate, ln_g, ln_b]
    in_specs += [tile(D), vec, par, par]
    out_shape = [jax.ShapeDtypeStruct((B, S, D), F32)]
    out_specs = [tile(D)]
    if nxt is not None:
        args += list(nxt)
        in_specs += [vec, vec]
        out_shape.append(jax.ShapeDtypeStruct((B, S, D), h_dtype))
        out_specs.append(tile(D))
    res = pl.pallas_call(
        functools.partial(_out_ln_kernel, n_parts=len(ys), matmul=ws is not None, has_next=nxt is not None),
        grid=(B, S // tm),
        in_specs=in_specs,
        out_specs=out_specs,
        out_shape=out_shape,
        compiler_params=_cparams(("arbitrary", "arbitrary")),
        name="out_ln",
    )(*args)
    return (res[0], res[1]) if nxt is not None else (res[0], None)


def _ffn_ln_kernel(h_ref, w1_ref, w3_ref, w2_ref, x_ref, g_ref, lng_ref, lnb_ref, sc_ref, sh_ref,
                   o_ref, hn_ref, acc_sc):
    j = pl.program_id(2)

    @pl.when(j == 0)
    def _():
        acc_sc[...] = jnp.zeros(acc_sc.shape, F32)

    h = h_ref[0]
    a = jnp.dot(h, w1_ref[...], preferred_element_type=F32)
    b = jnp.dot(h, w3_ref[...], preferred_element_type=F32)
    u = (a * jax.nn.sigmoid(a)) * b
    acc_sc[...] += jnp.dot(u.astype(BF16), w2_ref[...], preferred_element_type=F32)

    @pl.when(j == pl.num_programs(2) - 1)
    def _():
        xn = _deepnorm_ln(x_ref[0], acc_sc[...], g_ref[0], lng_ref[...], lnb_ref[...])
        o_ref[0] = xn
        hn_ref[0] = (xn * (1.0 + sc_ref[0]) + sh_ref[0]).astype(BF16)


def ffn_ln(h, w1, w3, w2, x, gate, ln_g, ln_b, nxt, tm=512):
    B, S, D = x.shape
    F = w1.shape[1]
    tf = F // 2
    tile = pl.BlockSpec((1, tm, D), lambda b, i, j: (b, i, 0))
    vec = pl.BlockSpec((1, 1, D), lambda b, i, j: (b, 0, 0))
    par = pl.BlockSpec((1, D), lambda b, i, j: (0, 0))
    return pl.pallas_call(
        _ffn_ln_kernel,
        grid=(B, S // tm, F // tf),
        in_specs=[tile,
                  pl.BlockSpec((D, tf), lambda b, i, j: (0, j)),
                  pl.BlockSpec((D, tf), lambda b, i, j: (0, j)),
                  pl.BlockSpec((tf, D), lambda b, i, j: (j, 0)),
                  tile, vec, par, par, vec, vec],
        out_specs=[tile, tile],
        out_shape=[jax.ShapeDtypeStruct((B, S, D), F32), jax.ShapeDtypeStruct((B, S, D), BF16)],
        scratch_shapes=[pltpu.VMEM((tm, D), F32)],
        compiler_params=_cparams(("arbitrary", "arbitrary", "arbitrary"), 48),
        name="ffn_ln",
    )(h, w1, w3, w2, x, gate, ln_g, ln_b, nxt[0], nxt[1])


def _compress_kernel(x_ref, pos_ref, w1_ref, w2_ref, o_ref):
    dh = NSA_HEAD_DIM
    x = x_ref[0, 0]
    uv = jnp.dot(x, w1_ref[...], preferred_element_type=F32)
    pc = jnp.dot(pos_ref[...], w1_ref[...], preferred_element_type=F32)
    c = pc[0:1, :dh] + pc[1:2, dh:]
    n = uv.shape[0]
    nxt = pltpu.roll(uv[:, dh:], n - 1, 0)
    hid = uv[:, :dh] + nxt + c
    act = hid * jax.nn.sigmoid(hid)
    o_ref[0, 0] = jnp.dot(act.astype(BF16), w2_ref[...], preferred_element_type=F32).astype(o_ref.dtype)


def nsa_compress(kv, pos2, w1cat, w2):
    B, G, S, dh = kv.shape
    seg = NSA_CMP_STRIDE
    n = S // seg
    x = kv.reshape(B, G, n, seg * dh)
    return pl.pallas_call(
        _compress_kernel,
        grid=(B, G),
        in_specs=[
            pl.BlockSpec((1, 1, n, seg * dh), lambda b, g: (b, g, 0, 0)),
            pl.BlockSpec(pos2.shape, lambda b, g: (0, 0)),
            pl.BlockSpec(w1cat.shape, lambda b, g: (0, 0)),
            pl.BlockSpec(w2.shape, lambda b, g: (0, 0)),
        ],
        out_specs=pl.BlockSpec((1, 1, n, dh), lambda b, g: (b, g, 0, 0)),
        out_shape=jax.ShapeDtypeStruct((B, G, n, dh), BF16),
        compiler_params=_cparams(("arbitrary", "arbitrary")),
        name="nsa_compress",
    )(x, pos2, w1cat, w2)


def _nsa_kernel(q_ref, kc_ref, vc_ref, ks_ref, vs_ref, kw_ref, vw_ref, gt_ref, cb_ref, tb_ref, o_ref, sel_sc,
                s0_sc, s1_sc, p0_sc, p1_sc, *, S, TQ):
    dh = NSA_HEAD_DIM
    HPG = NSA_HPG
    LS = NSA_SLC_BLOCK
    TK = 4 * LS
    R = HPG * TQ
    n_cmp = (S - NSA_CMP_LEN) // NSA_CMP_STRIDE + 1
    n_cp = S // NSA_CMP_STRIDE
    n_slc = S // LS
    n_top = min(NSA_SLC_TOPN, n_slc)
    qi = pl.program_id(2)
    q0 = qi * TQ

    q4 = jnp.concatenate([q_ref[0, :, h * dh:(h + 1) * dh] for h in range(HPG)], axis=0)
    lanes4 = lambda a: jnp.concatenate([a] * HPG, axis=1)

    s = lax.dot_general(kc_ref[0, 0], q4, _NT, preferred_element_type=F32)
    s = s + jnp.concatenate([cb_ref[h] for h in range(HPG)], axis=1)
    n_row = lax.broadcasted_iota(jnp.int32, (n_cp, TQ), 0)
    t_q = q0 + lax.broadcasted_iota(jnp.int32, (n_cp, TQ), 1)
    valid_c = lanes4((n_row * NSA_CMP_STRIDE + (NSA_CMP_LEN - 1) <= t_q) & (n_row < n_cmp))
    s = jnp.where(valid_c, s, NEG_INF)
    m = jnp.max(s, axis=0, keepdims=True)
    m = jnp.where(m == NEG_INF, 0.0, m)
    e = jnp.exp2(s - m)
    p_c = e * (1.0 / jnp.maximum(jnp.sum(e, axis=0, keepdims=True), jnp.finfo(F32).tiny))
    o_c = lax.dot_general(vc_ref[0, 0], p_c.astype(BF16), _TN, preferred_element_type=F32)

    p_sum = p_c[:, 0:TQ]
    for h in range(1, HPG):
        p_sum = p_sum + p_c[:, h * TQ:(h + 1) * TQ]
    om = lax.broadcasted_iota(jnp.int32, (n_slc, n_cp), 0)
    c_lo = lax.broadcasted_iota(jnp.int32, (n_slc, n_cp), 1) * NSA_CMP_STRIDE
    ov = jnp.maximum(jnp.minimum(c_lo + NSA_CMP_LEN, om * LS + LS) - jnp.maximum(c_lo, om * LS), 0)
    overlap_t = ov.astype(F32) / NSA_CMP_LEN
    imp_t = jnp.dot(overlap_t, p_sum, precision=lax.Precision.HIGHEST, preferred_element_type=F32)
    blk = lax.broadcasted_iota(jnp.int32, (n_slc, TQ), 0)
    tq_lane = q0 + lax.broadcasted_iota(jnp.int32, (n_slc, TQ), 1)
    tb = lax.shift_right_logical(tq_lane, LS.bit_length() - 1)
    forced = (blk == 0) | (blk == tb) | (blk == tb - 1)
    key = jnp.where(blk > tb, NEG_INF, jnp.where(forced, jnp.inf, imp_t))
    rank = jnp.zeros((n_slc, TQ), jnp.int32)
    for mm in range(n_slc):
        km = key[mm:mm + 1, :]
        beats = (km > key) | ((km == key) & (mm < blk))
        rank = rank + beats.astype(jnp.int32)
    sel_sc[...] = jnp.where((rank < n_top) & (blk <= tb), 1.0, 0.0)

    def toep_tile(k0, n_keys):
        return jnp.concatenate(
            [jnp.concatenate([tb_ref[h, :, pl.ds(_toep_col(q0, k0 + r * BIAS_ROWS), TQ)] for h in range(HPG)], axis=1)
             for r in range(n_keys // BIAS_ROWS)], axis=0)

    def pipelined_attention(n, scores_fn, post_fn, v_fn):
        s_bufs = (s0_sc, s1_sc)
        p_bufs = (p0_sc, p1_sc)
        s_bufs[0][...] = scores_fn(0)
        p_bufs[1][...] = jnp.zeros(p_bufs[1].shape, BF16)

        def half(i, cur, carry):
            m, l, acc, alpha_prev = carry
            s_bufs[1 - cur][...] = scores_fn(jnp.minimum(i + 1, n - 1))
            s = post_fn(i, s_bufs[cur][...])
            m_new = jnp.maximum(m, jnp.max(s, axis=0, keepdims=True))
            alpha = jnp.exp2(m - m_new)
            p = jnp.exp2(s - m_new)
            l_new = alpha * l + jnp.sum(p, axis=0, keepdims=True)
            p_bufs[cur][...] = p.astype(BF16)
            pv = lax.dot_general(v_fn(jnp.maximum(i - 1, 0)), p_bufs[1 - cur][...], _TN, preferred_element_type=F32)
            return m_new, l_new, alpha_prev * acc + pv, alpha

        def body(jj, carry):
            return half(2 * jj + 1, 1, half(2 * jj, 0, carry))

        n_pairs = (n + 1) // 2
        init = (jnp.full((1, R), NEG_INF, F32), jnp.zeros((1, R), F32), jnp.zeros((dh, R), F32),
                jnp.ones((1, R), F32))
        _, l, acc, alpha = lax.fori_loop(0, n_pairs, body, init)
        pv = lax.dot_general(v_fn(jnp.minimum(2 * n_pairs - 1, n - 1)), p_bufs[1][...], _TN,
                             preferred_element_type=F32)
        return (alpha * acc + pv) * (1.0 / l)

    krow_s = lax.broadcasted_iota(jnp.int32, (TK, TQ), 0)
    tq_s = q0 + lax.broadcasted_iota(jnp.int32, (TK, TQ), 1)
    n_kt = (q0 + TQ - 1) // TK + 1

    def slc_scores(kt):
        return lax.dot_general(ks_ref[0, 0, pl.ds(pl.multiple_of(kt * TK, TK), TK), :], q4, _NT,
                               preferred_element_type=F32)

    def slc_post(kt, s):
        live = kt < n_kt
        ktc = jnp.minimum(kt, n_kt - 1)
        k0 = pl.multiple_of(ktc * TK, TK)
        flags = jnp.concatenate(
            [jnp.broadcast_to(sel_sc[pl.ds(ktc * (TK // LS) + j, 1), :], (LS, TQ)) for j in range(TK // LS)], axis=0)
        mask = lanes4((flags > jnp.where(live, 0.5, 2.0)) & (k0 + krow_s <= tq_s))
        return jnp.where(mask, s + toep_tile(k0, TK), NEG_INF)

    o_s = pipelined_attention(n_kt, slc_scores, slc_post,
                              lambda kt: vs_ref[0, 0, pl.ds(pl.multiple_of(kt * TK, TK), TK), :])

    krow_w = lax.broadcasted_iota(jnp.int32, (TQ, TQ), 0)
    tq_w = q0 + lax.broadcasted_iota(jnp.int32, (TQ, TQ), 1)
    n_wt = jnp.minimum(qi, NSA_WINDOW // TQ) + 1

    def win_body(j, carry):
        k0 = pl.multiple_of(q0 - j * TQ, TQ)
        dist = tq_w - (k0 + krow_w)
        mask = lanes4((dist >= 0) & (dist < NSA_WINDOW))
        s = lax.dot_general(kw_ref[0, 0, pl.ds(k0, TQ), :], q4, _NT, preferred_element_type=F32) + toep_tile(k0, TQ)
        return _softmax_step(carry, jnp.where(mask, s, NEG_INF), vw_ref[0, 0, pl.ds(k0, TQ), :])

    init_w = (jnp.full((1, R), NEG_INF, F32), jnp.zeros((1, R), F32), jnp.zeros((dh, R), F32))
    _, l_w, acc_w = lax.fori_loop(0, n_wt, win_body, init_w)
    o_w = acc_w * (1.0 / l_w)

    g_t = jax.nn.sigmoid(gt_ref[0, 0]).T
    outs = []
    for h in range(HPG):
        cols = slice(h * TQ, (h + 1) * TQ)
        o = (g_t[h:h + 1] * o_c[:, cols] + g_t[HPG + h:HPG + h + 1] * o_s[:, cols]
             + g_t[2 * HPG + h:2 * HPG + h + 1] * o_w[:, cols])
        outs.append(o.T)
    o_ref[0] = jnp.concatenate(outs, axis=1).astype(o_ref.dtype)


def nsa_attention(q, kc, vc, ks, vs, kw, vw, gates, cmpb, toep, TQ=256):
    B, S, _ = q.shape
    G, HPG, dh = NSA_KV_GROUPS, NSA_HPG, NSA_HEAD_DIM
    n_cp = S // NSA_CMP_STRIDE
    full_kv = pl.BlockSpec((1, 1, S, dh), lambda b, g, i: (b, g, 0, 0))
    cmp_kv = pl.BlockSpec((1, 1, n_cp, dh), lambda b, g, i: (b, g, 0, 0))
    return pl.pallas_call(
        functools.partial(_nsa_kernel, S=S, TQ=TQ),
        grid=(B, G, S // TQ),
        in_specs=[
            pl.BlockSpec((1, TQ, HPG * dh), lambda b, g, i: (b, i, g)),
            cmp_kv, cmp_kv, full_kv, full_kv, full_kv, full_kv,
            pl.BlockSpec((1, 1, TQ, LANE), lambda b, g, i: (b, g, i, 0)),
            pl.BlockSpec((HPG, n_cp, TQ), lambda b, g, i: (g, 0, i)),
            pl.BlockSpec((HPG, BIAS_ROWS, _toep_width(S)), lambda b, g, i: (g, 0, 0)),
        ],
        out_specs=pl.BlockSpec((1, TQ, HPG * dh), lambda b, g, i: (b, i, g)),
        out_shape=jax.ShapeDtypeStruct((B, S, NSA_Q_WIDTH), BF16),
        scratch_shapes=[pltpu.VMEM((S // NSA_SLC_BLOCK, TQ), F32)]
        + [pltpu.VMEM((TQ, HPG * TQ), F32)] * 2 + [pltpu.VMEM((TQ, HPG * TQ), BF16)] * 2,
        compiler_params=_cparams(("arbitrary", "arbitrary", "arbitrary"), 48),
        name="nsa_attn",
    )(q, kc, vc, ks, vs, kw, vw, gates, cmpb, toep)


def _router_kernel(h_ref, wr_ref, e_ref, w_ref, cnt_ref):
    @pl.when(pl.program_id(0) == 0)
    def _():
        cnt_ref[...] = jnp.zeros(cnt_ref.shape, F32)

    logits = jnp.dot(h_ref[...], wr_ref[...], precision=lax.Precision.HIGHEST, preferred_element_type=F32)
    tm = logits.shape[0]
    lane = lax.broadcasted_iota(jnp.int32, logits.shape, 1)
    logits = jnp.where(lane < N_EXPERTS, logits, NEG_INF)
    l1 = jnp.max(logits, axis=-1, keepdims=True)
    i1 = jnp.min(jnp.where(logits == l1, lane, LANE), axis=-1, keepdims=True)
    rest = jnp.where(lane == i1, NEG_INF, logits)
    l2 = jnp.max(rest, axis=-1, keepdims=True)
    i2 = jnp.min(jnp.where(rest == l2, lane, LANE), axis=-1, keepdims=True)
    e2 = jnp.exp(l2 - l1)
    den = 1.0 + e2
    w_ref[...] = jnp.where(lane == 0, 1.0 / den, jnp.where(lane == 1, e2 / den, 0.0))

    oh1 = lane == i1
    oh2 = lane == i2
    both = jnp.where(oh1 | oh2, 1.0, 0.0)
    earlier = lax.broadcasted_iota(jnp.int32, (tm, tm), 1) < lax.broadcasted_iota(jnp.int32, (tm, tm), 0)
    prefix = jnp.dot(jnp.where(earlier, 1.0, 0.0).astype(BF16), both.astype(BF16),
                     preferred_element_type=F32)
    seen = prefix + cnt_ref[0:1, :]
    p1 = jnp.sum(jnp.where(oh1, seen, 0.0), axis=-1, keepdims=True).astype(jnp.int32)
    p2 = jnp.sum(jnp.where(oh2, seen, 0.0), axis=-1, keepdims=True).astype(jnp.int32)
    e_ref[...] = jnp.where(lane == 0, i1, jnp.where(lane == 1, i2, jnp.where(lane == 2, p1,
                                                                              jnp.where(lane == 3, p2, 0))))
    cnt_ref[...] = cnt_ref[...] + jnp.sum(both, axis=0, keepdims=True)


def moe_route(h2, wr_pad, tm=512):
    N, D = h2.shape
    return pl.pallas_call(
        _router_kernel,
        grid=(N // tm,),
        in_specs=[pl.BlockSpec((tm, D), lambda i: (i, 0)), pl.BlockSpec(wr_pad.shape, lambda i: (0, 0))],
        out_specs=[pl.BlockSpec((tm, LANE), lambda i: (i, 0)), pl.BlockSpec((tm, LANE), lambda i: (i, 0)),
                   pl.BlockSpec((8, LANE), lambda i: (0, 0))],
        out_shape=[jax.ShapeDtypeStruct((N, LANE), jnp.int32), jax.ShapeDtypeStruct((N, LANE), F32),
                   jax.ShapeDtypeStruct((8, LANE), F32)],
        compiler_params=_cparams(("arbitrary",)),
        name="moe_router",
    )(h2, wr_pad)


def _dispatch_kernel(dest_ref, h_ref, xs_in_ref, xs_ref, sem):
    del xs_in_ref
    tm = h_ref.shape[0]
    base = pl.program_id(0) * tm * TOP_K

    def issue(r, carry):
        for k in range(TOP_K):
            d = dest_ref[base + r * TOP_K + k]
            pltpu.make_async_copy(h_ref.at[r], xs_ref.at[d], sem).start(priority=k)
        return carry

    lax.fori_loop(0, tm, issue, 0, unroll=16)
    for k in range(TOP_K):
        pltpu.make_async_copy(h_ref, xs_ref.at[pl.ds(0, tm)], sem).wait()


def moe_dispatch_rows(h2, dest_flat, n_rows, recycled=None, tm=512):
    N, D = h2.shape
    xs0 = jnp.zeros((n_rows, D), h2.dtype) if recycled is None else recycled
    grid_spec = pltpu.PrefetchScalarGridSpec(
        num_scalar_prefetch=1,
        grid=(N // tm,),
        in_specs=[pl.BlockSpec((tm, D), lambda i, d: (i, 0)), pl.BlockSpec(memory_space=pl.ANY)],
        out_specs=pl.BlockSpec(memory_space=pl.ANY),
        scratch_shapes=[pltpu.SemaphoreType.DMA],
    )
    return pl.pallas_call(
        _dispatch_kernel,
        grid_spec=grid_spec,
        out_shape=jax.ShapeDtypeStruct((n_rows, D), h2.dtype),
        input_output_aliases={2: 0},
        compiler_params=_cparams(("arbitrary",)),
        name="moe_dispatch",
    )(dest_flat, h2, xs0)


def _combine_ln_kernel(dest_ref, w_ref, ys_ref, x_ref, g_ref, lng_ref, lnb_ref, *rest, has_next):
    if has_next:
        sc_ref, sh_ref, o_ref, hn_ref, buf, sem = rest
    else:
        o_ref, buf, sem = rest
    tm = x_ref.shape[0]
    base = pl.program_id(0) * tm * TOP_K

    def issue(r, carry):
        for k in range(TOP_K):
            d = dest_ref[base + r * TOP_K + k]
            pltpu.make_async_copy(ys_ref.at[d], buf.at[k, r], sem).start(priority=k)
        return carry

    lax.fori_loop(0, tm, issue, 0, unroll=16)
    for k in range(TOP_K):
        pltpu.make_async_copy(ys_ref.at[pl.ds(0, tm)], buf.at[k], sem).wait()
    w = w_ref[...]
    y = w[:, 0:1] * buf[0] + w[:, 1:2] * buf[1]
    xn = _deepnorm_ln(x_ref[...], y, g_ref[0], lng_ref[...], lnb_ref[...])
    o_ref[...] = xn
    if has_next:
        hn_ref[...] = (xn * (1.0 + sc_ref[0]) + sh_ref[0]).astype(hn_ref.dtype)


def moe_combine_ln(ys, dest_flat, w_top, x, gate, ln_g, ln_b, nxt, tm=512):
    B, S, D = x.shape
    N = B * S
    per_b = S // tm
    tile = pl.BlockSpec((tm, D), lambda i, d: (i, 0))
    vec = pl.BlockSpec((1, 1, D), lambda i, d: (i // per_b, 0, 0))
    par = pl.BlockSpec((1, D), lambda i, d: (0, 0))
    in_specs = [pl.BlockSpec((tm, LANE), lambda i, d: (i, 0)), pl.BlockSpec(memory_space=pl.ANY), tile, vec, par, par]
    args = [w_top, ys, x.reshape(N, D), gate, ln_g, ln_b]
    out_shape = [jax.ShapeDtypeStruct((N, D), F32)]
    out_specs = [tile]
    if nxt is not None:
        in_specs += [vec, vec]
        args += list(nxt)
        out_shape.append(jax.ShapeDtypeStruct((N, D), BF16))
        out_specs.append(tile)
    grid_spec = pltpu.PrefetchScalarGridSpec(
        num_scalar_prefetch=1,
        grid=(N // tm,),
        in_specs=in_specs,
        out_specs=out_specs,
        scratch_shapes=[pltpu.VMEM((TOP_K, tm, D), F32), pltpu.SemaphoreType.DMA],
    )
    res = pl.pallas_call(
        functools.partial(_combine_ln_kernel, has_next=nxt is not None),
        grid_spec=grid_spec,
        out_shape=out_shape,
        compiler_params=_cparams(("arbitrary",)),
        name="moe_combine_ln",
    )(dest_flat, *args)
    x_new = res[0].reshape(B, S, D)
    return (x_new, res[1].reshape(B, S, D)) if nxt is not None else (x_new, None)


def _moe_ffn_kernel(be_ref, nu_ref, x_ref, w1_ref, w3_ref, w2_ref, o_ref, acc_sc, xb_sc):
    i = pl.program_id(0)
    j = pl.program_id(1)
    last = pl.num_programs(1) - 1
    used = i < nu_ref[0]

    @pl.when(j == 0)
    def _():
        acc_sc[...] = jnp.zeros(acc_sc.shape, F32)
        xb_sc[...] = x_ref[...].astype(BF16)

    @pl.when(used)
    def _():
        x = xb_sc[...]
        a = jnp.dot(x, w1_ref[0].astype(BF16), preferred_element_type=F32)
        b = jnp.dot(x, w3_ref[0].astype(BF16), preferred_element_type=F32)
        u = (a * jax.nn.sigmoid(a)) * b
        acc_sc[...] += jnp.dot(u.astype(BF16), w2_ref[0].astype(BF16), preferred_element_type=F32)

    @pl.when(j == last)
    def _():
        o_ref[...] = acc_sc[...]


def moe_ffn(xs, blk_e, n_used, w1, w3, w2, tf=512):
    n_rows, D = xs.shape
    n_blocks = blk_e.shape[0]
    RB = n_rows // n_blocks
    F = w1.shape[2]
    nf = F // tf

    def wcol(i, j, be, nu):
        return (be[i], 0, jnp.where(i < nu[0], j, nf - 1))

    def wrow(i, j, be, nu):
        return (be[i], jnp.where(i < nu[0], j, nf - 1), 0)

    grid_spec = pltpu.PrefetchScalarGridSpec(
        num_scalar_prefetch=2,
        grid=(n_blocks, nf),
        in_specs=[
            pl.BlockSpec((RB, D), lambda i, j, be, nu: (i, 0)),
            pl.BlockSpec((1, D, tf), wcol),
            pl.BlockSpec((1, D, tf), wcol),
            pl.BlockSpec((1, tf, D), wrow),
        ],
        out_specs=pl.BlockSpec((RB, D), lambda i, j, be, nu: (i, 0)),
        scratch_shapes=[pltpu.VMEM((RB, D), F32), pltpu.VMEM((RB, D), BF16)],
    )
    return pl.pallas_call(
        _moe_ffn_kernel,
        grid_spec=grid_spec,
        out_shape=jax.ShapeDtypeStruct((n_rows, D), F32),
        compiler_params=_cparams(("arbitrary", "arbitrary"), 56),
        name="moe_ffn",
    )(blk_e, n_used, xs, w1, w3, w2)


def moe_layout(e_pos, counts_f, n_tok):
    RB = _moe_row_block(n_tok)
    n_assign = n_tok * TOP_K
    n_blocks = (n_assign + N_EXPERTS * (RB - 1) + RB - 1) // RB
    counts = counts_f[0, :N_EXPERTS].astype(jnp.int32)
    padded = (counts + RB - 1) // RB * RB
    pad_end = jnp.cumsum(padded)
    pad_start = pad_end - padded
    e = e_pos[:, 0:TOP_K]
    pos = e_pos[:, TOP_K:2 * TOP_K]
    start = jnp.zeros_like(e)
    for x in range(N_EXPERTS):
        start = jnp.where(e == x, pad_start[x], start)
    dest_flat = (start + pos).reshape(-1)
    blk_start = jnp.arange(n_blocks, dtype=jnp.int32) * RB
    blk_e = jnp.minimum(jnp.sum((blk_start[:, None] >= pad_end[None, :]).astype(jnp.int32), axis=1), N_EXPERTS - 1)
    n_used = (pad_end[-1] // RB).astype(jnp.int32).reshape(1)
    return dest_flat, blk_e.astype(jnp.int32), n_used, n_blocks * RB


def _pad_cols(w, width):
    return jnp.pad(w, ((0, 0), (0, width - w.shape[1])))


def kernel(x, c, rel_bias, ada_w, ada_b, ln_g, ln_b, ab_w_in, gla_wg2, gla_bg2, gla_norm_g, ab_w_out,
           ffn_w1, ffn_w3, ffn_w2, nsa_w_in, nsa_cmp_pos, nsa_cmp_w1k, nsa_cmp_w2k, nsa_cmp_w1v, nsa_cmp_w2v,
           nsa_w_out, moe_router, moe_w1, moe_w3, moe_w2):
    B, S, D = x.shape
    n_tok = B * S
    mod = adaln_mod(c, ada_w, ada_b)
    mods = [[m.reshape(B, 1, D) for m in jnp.split(mod[i], 6, axis=-1)] for i in range(DEPTH)]
    toep, cmpb = build_bias_tables(rel_bias, S)

    xs_dead = None
    sh1, sc1 = mods[0][0], mods[0][1]
    h = modulate(x, sc1, sh1)
    for i in range(DEPTH):
        j = i // 2
        _, _, g1, sh2, sc2, g2 = mods[i]
        nxt2 = (mods[i + 1][1], mods[i + 1][0]) if i + 1 < DEPTH else None
        lng = ln_g[i].reshape(2, 1, D)
        lnb = ln_b[i].reshape(2, 1, D)
        if i % 2 == 0:
            w = ab_w_in[j]
            cuts = np.cumsum((MOBA_WIDTH, MOBA_WIDTH, MOBA_WIDTH, GLA_QK_WIDTH, GLA_QK_WIDTH, GLA_V_WIDTH,
                              GLA_V_WIDTH))
            parts = jnp.split(w, cuts.tolist(), axis=1)
            parts[-1] = _pad_cols(parts[-1], LANE)
            w_cat = jnp.concatenate(parts, axis=1).astype(BF16)
            widths = [p.shape[1] for p in parts]
            offs = np.concatenate([[0], np.cumsum(widths)[:-1]]).tolist()
            scales = [MOBA_HEAD_DIM ** -0.5 * LOG2E] + [1.0] * 7
            kinds = ["flat", "flat", "flat_t"] + ["flat"] * 5
            offs[2] = 0
            segs = tuple((o, wd, kd, s) for o, wd, kd, s in zip(offs, widths, kinds, scales))
            dts = [BF16, BF16, BF16, F32, F32, F32, F32, F32]
            mq, mk, mv, gq, gk, gv, gr, glr = project(h, w_cat, segs, dts, wt=parts[2].T.astype(BF16))
            y_moba = moba_attention(mq, mk, mv, toep)
            wg2_pad = jnp.pad(gla_wg2[j], ((0, LANE - GLA_GATE_RANK), (0, 0)))
            y_gla = gla_mixer(gq, gk, gv, gr, glr, wg2_pad, gla_bg2[j].reshape(1, -1), gla_norm_g[j].reshape(1, -1))
            w_out = ab_w_out[j].astype(BF16)
            x, h = out_ln([y_moba, y_gla], [w_out[:MOBA_WIDTH], w_out[MOBA_WIDTH:]], x, g1, lng[0], lnb[0],
                          (sc2, sh2))
            x, h = ffn_ln(h, ffn_w1[j].astype(BF16), ffn_w3[j].astype(BF16), ffn_w2[j].astype(BF16), x, g2,
                          lng[1], lnb[1], nxt2)
        else:
            G, HPG, dh = NSA_KV_GROUPS, NSA_HPG, NSA_HEAD_DIM
            w = nsa_w_in[j]
            wq = w[:, :NSA_Q_WIDTH]
            wkv = w[:, NSA_Q_WIDTH:NSA_Q_WIDTH + 6 * NSA_KV_WIDTH]
            wg = w[:, NSA_Q_WIDTH + 6 * NSA_KV_WIDTH:].reshape(D, G, HPG, 3)
            wg = _pad_cols(wg.transpose(0, 1, 3, 2).reshape(D * G, 3 * HPG), LANE).reshape(D, G * LANE)
            w_cat = jnp.concatenate([wq, wkv, wg], axis=1).astype(BF16)
            segs = [(0, NSA_Q_WIDTH, "flat", dh ** -0.5 * LOG2E)]
            segs += [(NSA_Q_WIDTH + n * NSA_KV_WIDTH, NSA_KV_WIDTH, "group", 1.0) for n in range(6)]
            segs += [(NSA_Q_WIDTH + 6 * NSA_KV_WIDTH, G * LANE, "group", 1.0)]
            dts = [BF16] * 7 + [F32]
            q, kc, vc, ks, vs, kw, vw, gates = project(h, w_cat, tuple(segs), dts)
            half = NSA_CMP_STRIDE * dh
            pos2 = jnp.pad(nsa_cmp_pos[j].reshape(2, half), ((0, 6), (0, 0))).astype(BF16)
            w1k = jnp.concatenate([nsa_cmp_w1k[j][:half], nsa_cmp_w1k[j][half:]], axis=1).astype(BF16)
            w1v = jnp.concatenate([nsa_cmp_w1v[j][:half], nsa_cmp_w1v[j][half:]], axis=1).astype(BF16)
            k_cmp = nsa_compress(kc, pos2, w1k, nsa_cmp_w2k[j].astype(BF16))
            v_cmp = nsa_compress(vc, pos2, w1v, nsa_cmp_w2v[j].astype(BF16))
            y = nsa_attention(q, k_cmp, v_cmp, ks, vs, kw, vw, gates, cmpb, toep)
            x, h = out_ln([y], [nsa_w_out[j].astype(BF16)], x, g1, lng[0], lnb[0], (sc2, sh2), h_dtype=F32)
            h2 = h.reshape(n_tok, D)
            e_pos, w_top, counts = moe_route(h2, _pad_cols(moe_router[j], LANE))
            dest_flat, blk_e, n_used, n_rows = moe_layout(e_pos, counts, n_tok)
            xs = moe_dispatch_rows(h2, dest_flat, n_rows, recycled=xs_dead)
            xs_dead = xs
            stack = lambda w: w.reshape((-1,) + w.shape[2:])
            ys = moe_ffn(xs, blk_e + j * N_EXPERTS, n_used, stack(moe_w1), stack(moe_w3), stack(moe_w2))
            x, h = moe_combine_ln(ys, dest_flat, w_top, x, g2, lng[1], lnb[1], nxt2)
    return x
```

```python
import functools
import math

import jax
import jax.numpy as jnp
import numpy as np
from jax import lax
from jax.experimental import pallas as pl
from jax.experimental.pallas import tpu as pltpu

F32 = jnp.float32
BF16 = jnp.bfloat16
NEG_INF = float("-inf")

DEPTH = 4
ALPHA_DN = (2 * DEPTH) ** 0.25
LN_EPS = 1e-5
RMS_EPS = 1e-6

N_BUCKETS = 32
MAX_DISTANCE = 1024

MOBA_HEADS = 8
MOBA_HEAD_DIM = 64
MOBA_BLOCK = 256
MOBA_TOPK = 3
MOBA_WIDTH = MOBA_HEADS * MOBA_HEAD_DIM

GLA_HEADS = 4
GLA_DK = 64
GLA_DV = 128
GLA_GATE_RANK = 16
GLA_TAU = 16.0
GLA_CHUNK = 64
GLA_QK_WIDTH = GLA_HEADS * GLA_DK
GLA_V_WIDTH = GLA_HEADS * GLA_DV

NSA_HEADS = 8
NSA_KV_GROUPS = 2
NSA_HPG = NSA_HEADS // NSA_KV_GROUPS
NSA_HEAD_DIM = 128
NSA_CMP_LEN = 32
NSA_CMP_STRIDE = 16
NSA_SLC_BLOCK = 64
NSA_SLC_TOPN = 16
NSA_WINDOW = 512
NSA_Q_WIDTH = NSA_HEADS * NSA_HEAD_DIM
NSA_KV_WIDTH = NSA_KV_GROUPS * NSA_HEAD_DIM

N_EXPERTS = 8
TOP_K = 2
MOE_BLOCKS_PER_EXPERT = 4


def _moe_row_block(n_tok):
    balanced = n_tok * TOP_K // N_EXPERTS
    rows = balanced // MOE_BLOCKS_PER_EXPERT + balanced // 128
    return -(-rows // 16) * 16

LOG2E = math.log2(math.e)
LANE = 128
BIAS_ROWS = 128


def _cparams(sem, vmem_mib=40, **extra):
    return pltpu.CompilerParams(dimension_semantics=sem, vmem_limit_bytes=vmem_mib << 20, **extra)


def _adaln_kernel(c_ref, w_ref, b_ref, o_ref):
    c = c_ref[...]
    cond = c * jax.nn.sigmoid(c)
    o_ref[0] = jnp.dot(cond, w_ref[0], preferred_element_type=F32) + b_ref[0]


def adaln_mod(c, ada_w, ada_b):
    B, D = c.shape
    L, _, N = ada_w.shape
    rows = 8
    cp = jnp.zeros((rows, D), F32).at[:B].set(c)
    tn = N // 4
    out = pl.pallas_call(
        _adaln_kernel,
        grid=(L, N // tn),
        in_specs=[
            pl.BlockSpec((rows, D), lambda l, j: (0, 0)),
            pl.BlockSpec((1, D, tn), lambda l, j: (l, 0, j)),
            pl.BlockSpec((1, 1, tn), lambda l, j: (l, 0, j)),
        ],
        out_specs=pl.BlockSpec((1, rows, tn), lambda l, j: (l, 0, j)),
        out_shape=jax.ShapeDtypeStruct((L, rows, N), F32),
        compiler_params=_cparams(("arbitrary", "arbitrary")),
        name="adaln_mod",
    )(cp, ada_w, ada_b.reshape(L, 1, N))
    return out[:, :B]


def _modulate_kernel(x_ref, sc_ref, sh_ref, o_ref):
    o_ref[0] = (x_ref[0] * (1.0 + sc_ref[0]) + sh_ref[0]).astype(o_ref.dtype)


def modulate(x, sc, sh, tm=512):
    B, S, D = x.shape
    return pl.pallas_call(
        _modulate_kernel,
        grid=(B, S // tm),
        in_specs=[
            pl.BlockSpec((1, tm, D), lambda b, i: (b, i, 0)),
            pl.BlockSpec((1, 1, D), lambda b, i: (b, 0, 0)),
            pl.BlockSpec((1, 1, D), lambda b, i: (b, 0, 0)),
        ],
        out_specs=pl.BlockSpec((1, tm, D), lambda b, i: (b, i, 0)),
        out_shape=jax.ShapeDtypeStruct((B, S, D), BF16),
        compiler_params=_cparams(("arbitrary", "arbitrary")),
        name="modulate",
    )(x, sc, sh)


def _t5_bucket(dist):
    max_exact = N_BUCKETS // 2
    n = jnp.maximum(dist, 0)
    nf = jnp.maximum(n, 1).astype(jnp.float32)
    large = max_exact + (jnp.log(nf / max_exact) / math.log(MAX_DISTANCE / max_exact) * (N_BUCKETS - max_exact)).astype(jnp.int32)
    large = jnp.minimum(large, N_BUCKETS - 1)
    return jnp.where(n < max_exact, n, large)


def _bias_table_kernel(rb_ref, bm_ref, o_ref, *, rows_per_step):
    h = pl.program_id(0)
    n_steps = bm_ref.shape[0] // rows_per_step

    def body(i, carry):
        r0 = pl.multiple_of(i * rows_per_step, rows_per_step)
        bm = bm_ref[pl.ds(r0, rows_per_step), :]
        acc = jnp.zeros(bm.shape, F32)
        for b in range(N_BUCKETS):
            acc = jnp.where(bm == b, rb_ref[b, h] * LOG2E, acc)
        o_ref[0, pl.ds(r0, rows_per_step), :] = acc
        return carry

    lax.fori_loop(0, n_steps, body, 0)


def bias_table(rel_bias, bucket_map, rows_per_step):
    R, W = bucket_map.shape
    H = rel_bias.shape[1]
    return pl.pallas_call(
        functools.partial(_bias_table_kernel, rows_per_step=rows_per_step),
        grid=(H,),
        in_specs=[
            pl.BlockSpec(memory_space=pltpu.SMEM),
            pl.BlockSpec((R, W), lambda h: (0, 0)),
        ],
        out_specs=pl.BlockSpec((1, R, W), lambda h: (h, 0, 0)),
        out_shape=jax.ShapeDtypeStruct((H, R, W), F32),
        compiler_params=_cparams(("arbitrary",)),
        name="bias_table",
    )(rel_bias, bucket_map)


def _toep_width(S):
    return S + BIAS_ROWS


def _toep_col(q0, k0):
    return pl.multiple_of(q0 - k0 + BIAS_ROWS, LANE)


def build_bias_tables(rel_bias, S):
    j = jnp.arange(BIAS_ROWS, dtype=jnp.int32)[:, None]
    c = jnp.arange(_toep_width(S), dtype=jnp.int32)[None, :]
    toep_map = _t5_bucket(c - j - BIAS_ROWS)
    n_cmp_pad = S // NSA_CMP_STRIDE
    t = jnp.arange(S, dtype=jnp.int32)[None, :]
    cmp_end = jnp.arange(n_cmp_pad, dtype=jnp.int32)[:, None] * NSA_CMP_STRIDE + NSA_CMP_LEN - 1
    cmp_map = _t5_bucket(t - cmp_end)
    return bias_table(rel_bias, toep_map, 8), bias_table(rel_bias, cmp_map, 8)


def _proj_kernel(h_ref, w_ref, wt_ref, *o_refs, segs):
    h = h_ref[0]
    for (off, width, kind, scale), o_ref in zip(segs, o_refs):
        if kind == "flat_t":
            acc = lax.dot_general(wt_ref[off:off + width, :], h, _NT, preferred_element_type=F32)
        else:
            acc = jnp.dot(h, w_ref[:, off:off + width], preferred_element_type=F32)
        if scale != 1.0:
            acc = acc * scale
        if kind in ("flat", "flat_t"):
            o_ref[0] = acc.astype(o_ref.dtype)
        else:
            for g in range(width // LANE):
                o_ref[0, g] = acc[:, g * LANE:(g + 1) * LANE].astype(o_ref.dtype)


def project(h, w_cat, segs, out_dtypes, wt=None, tm=512):
    B, S, D = h.shape
    if wt is None:
        wt = jnp.zeros((8, D), w_cat.dtype)
    out_shapes, out_specs = [], []
    for (off, width, kind, scale), dt in zip(segs, out_dtypes):
        if kind == "flat":
            out_shapes.append(jax.ShapeDtypeStruct((B, S, width), dt))
            out_specs.append(pl.BlockSpec((1, tm, width), lambda b, i: (b, i, 0)))
        elif kind == "flat_t":
            out_shapes.append(jax.ShapeDtypeStruct((B, width, S), dt))
            out_specs.append(pl.BlockSpec((1, width, tm), lambda b, i: (b, 0, i)))
        else:
            G = width // LANE
            out_shapes.append(jax.ShapeDtypeStruct((B, G, S, LANE), dt))
            out_specs.append(pl.BlockSpec((1, G, tm, LANE), lambda b, i: (b, 0, i, 0)))
    return pl.pallas_call(
        functools.partial(_proj_kernel, segs=segs),
        grid=(B, S // tm),
        in_specs=[
            pl.BlockSpec((1, tm, D), lambda b, i: (b, i, 0)),
            pl.BlockSpec(w_cat.shape, lambda b, i: (0, 0)),
            pl.BlockSpec(wt.shape, lambda b, i: (0, 0)),
        ],
        out_specs=out_specs,
        out_shape=out_shapes,
        compiler_params=_cparams(("arbitrary", "arbitrary"), 48),
        name="in_proj",
    )(h, w_cat, wt)


_NT = (((1,), (1,)), ((), ()))
_TN = (((0,), (0,)), ((), ()))


def _softmax_step(carry, s, v_tile):
    m, l, acc = carry
    m_new = jnp.maximum(m, jnp.max(s, axis=0, keepdims=True))
    alpha = jnp.exp2(m - m_new)
    p = jnp.exp2(s - m_new)
    l_new = alpha * l + jnp.sum(p, axis=0, keepdims=True)
    acc_new = alpha * acc + lax.dot_general(v_tile, p.astype(BF16), _TN, preferred_element_type=F32)
    return m_new, l_new, acc_new


def _moba_kernel(q_ref, k_ref, v_ref, tb_ref, o_ref, kmean_sc, sel_sc, s0_sc, s1_sc, p0_sc, p1_sc,
                 *, S, heads_per_step):
    L = MOBA_BLOCK
    dh = MOBA_HEAD_DIM
    nb = S // L
    qi = pl.program_id(2)
    q0 = qi * L

    @pl.when(qi == 0)
    def _():
        kf = k_ref[0].astype(F32)
        kmean_sc[...] = jnp.mean(kf.reshape(nb, L, kf.shape[-1]), axis=1)

    q2 = q_ref[0]
    lane_q = lax.broadcasted_iota(jnp.int32, q2.shape, 1)
    lane_m = lax.broadcasted_iota(jnp.int32, kmean_sc.shape, 1)
    blk = lax.broadcasted_iota(jnp.int32, (nb, L), 0)
    qz = []
    for hh in range(heads_per_step):
        in_head = (lane_q >= hh * dh) & (lane_q < (hh + 1) * dh)
        qz.append(jnp.where(in_head, q2, jnp.zeros_like(q2)))
        km = jnp.where((lane_m >= hh * dh) & (lane_m < (hh + 1) * dh), kmean_sc[...], 0.0)
        gate = lax.dot_general(km, q2.astype(F32), _NT, precision=lax.Precision.HIGHEST,
                               preferred_element_type=F32)
        valid = blk < qi
        gm = jnp.where(valid, gate, NEG_INF)
        rank = jnp.zeros((nb, L), jnp.int32)
        for m in range(nb):
            gmm = gm[m:m + 1, :]
            beats = (gmm > gm) | ((gmm == gm) & (m < blk))
            rank = rank + beats.astype(jnp.int32)
        sel_sc[hh] = jnp.where(valid & (rank < MOBA_TOPK), 1.0, 0.0)

    def bias_tile(hh, k0, n_keys):
        return jnp.concatenate([tb_ref[hh, :, pl.ds(_toep_col(q0, k0 + r * BIAS_ROWS), L)]
                                for r in range(n_keys // BIAS_ROWS)], axis=0)

    krow = lax.broadcasted_iota(jnp.int32, (L, L), 0)
    qcol = lax.broadcasted_iota(jnp.int32, (L, L), 1)
    k_own = k_ref[0, pl.ds(pl.multiple_of(q0, L), L), :]
    v_tile = lambda hh, k0, n_keys: v_ref[0, hh * dh:(hh + 1) * dh, pl.ds(k0, n_keys)]
    carry = []
    for hh in range(heads_per_step):
        s = lax.dot_general(k_own, qz[hh], _NT, preferred_element_type=F32) + bias_tile(hh, q0, L)
        s = jnp.where(krow <= qcol, s, NEG_INF)
        m0 = jnp.max(s, axis=0, keepdims=True)
        p = jnp.exp2(s - m0)
        l0 = jnp.sum(p, axis=0, keepdims=True)
        acc0 = jnp.dot(v_tile(hh, pl.multiple_of(q0, L), L), p.astype(BF16), preferred_element_type=F32)
        carry += [m0, l0, acc0, jnp.ones((1, L), F32)]

    n = (qi + 1) // 2
    s_bufs = (s0_sc, s1_sc)
    p_bufs = (p0_sc, p1_sc)
    tile_k0 = lambda i: pl.multiple_of(jnp.clip(i, 0, jnp.maximum(n - 1, 0)) * 2 * L, 2 * L)

    def put_scores(i, buf):
        kt = k_ref[0, pl.ds(tile_k0(i), 2 * L), :]
        for hh in range(heads_per_step):
            buf[hh] = lax.dot_general(kt, qz[hh], _NT, preferred_element_type=F32)

    put_scores(0, s_bufs[0])
    p_bufs[1][...] = jnp.zeros(p_bufs[1].shape, BF16)

    def half(i, cur, carry):
        put_scores(i + 1, s_bufs[1 - cur])
        k0 = tile_k0(i)
        blk0 = lax.shift_right_logical(k0, L.bit_length() - 1)
        thresh = jnp.where(i < n, 0.5, 2.0)
        out = []
        for hh in range(heads_per_step):
            m, l, acc, alpha_prev = carry[4 * hh:4 * hh + 4]
            flags = jnp.concatenate(
                [jnp.broadcast_to(sel_sc[hh, pl.ds(blk0 + j, 1), :], (L, L)) for j in range(2)], axis=0)
            s = jnp.where(flags > thresh, s_bufs[cur][hh] + bias_tile(hh, k0, 2 * L), NEG_INF)
            m_new = jnp.maximum(m, jnp.max(s, axis=0, keepdims=True))
            alpha = jnp.exp2(m - m_new)
            p = jnp.exp2(s - m_new)
            l_new = alpha * l + jnp.sum(p, axis=0, keepdims=True)
            p_bufs[cur][hh] = p.astype(BF16)
            pv = jnp.dot(v_tile(hh, tile_k0(i - 1), 2 * L), p_bufs[1 - cur][hh], preferred_element_type=F32)
            out += [m_new, l_new, alpha_prev * acc + pv, alpha]
        return tuple(out)

    def body(jj, carry):
        return half(2 * jj + 1, 1, half(2 * jj, 0, carry))

    n_pairs = (n + 1) // 2
    carry = lax.fori_loop(0, n_pairs, body, tuple(carry))
    outs = []
    for hh in range(heads_per_step):
        _, l, acc, alpha = carry[4 * hh:4 * hh + 4]
        pv = jnp.dot(v_tile(hh, tile_k0(2 * n_pairs - 1), 2 * L), p_bufs[1][hh], preferred_element_type=F32)
        outs.append((alpha * acc + pv) * (1.0 / l))
    o_ref[0] = jnp.concatenate(outs, axis=0).T.astype(o_ref.dtype)


def moba_attention(q, k, v, toep):
    B, S, _ = q.shape
    L = MOBA_BLOCK
    hps = 4
    wl = hps * MOBA_HEAD_DIM
    n_hp = MOBA_HEADS // hps
    nb = S // L
    return pl.pallas_call(
        functools.partial(_moba_kernel, S=S, heads_per_step=hps),
        grid=(n_hp, B, nb),
        in_specs=[
            pl.BlockSpec((1, L, wl), lambda hp, b, i: (b, i, hp)),
            pl.BlockSpec((1, S, wl), lambda hp, b, i: (b, 0, hp)),
            pl.BlockSpec((1, wl, S), lambda hp, b, i: (b, hp, 0)),
            pl.BlockSpec((hps, BIAS_ROWS, _toep_width(S)), lambda hp, b, i: (hp, 0, 0)),
        ],
        out_specs=pl.BlockSpec((1, L, wl), lambda hp, b, i: (b, i, hp)),
        out_shape=jax.ShapeDtypeStruct((B, S, MOBA_WIDTH), BF16),
        scratch_shapes=[pltpu.VMEM((nb, wl), F32), pltpu.VMEM((hps, nb, L), F32)]
        + [pltpu.VMEM((hps, 2 * L, L), F32)] * 2 + [pltpu.VMEM((hps, 2 * L, L), BF16)] * 2,
        compiler_params=_cparams(("arbitrary", "arbitrary", "arbitrary"), 48),
        name="moba_attn",
    )(q, k, v, toep)


def _log_sigmoid(z):
    return jnp.minimum(z, 0.0) - jnp.log1p(jnp.exp(-jnp.abs(z)))


def _gla_kernel(q_ref, k_ref, v_ref, r_ref, lr_ref, wg2_ref, bg2_ref, ng_ref, o_ref, st_sc, *, batch):
    C = GLA_CHUNK
    dk, dv = GLA_DK, GLA_DV

    @pl.when(pl.program_id(0) == 0)
    def _():
        st_sc[...] = jnp.zeros(st_sc.shape, F32)

    row = lax.broadcasted_iota(jnp.int32, (C, C), 0)
    col = lax.broadcasted_iota(jnp.int32, (C, C), 1)
    causal = col <= row
    tri = jnp.where(causal, 1.0, 0.0)
    for b in range(batch):
        z = jnp.dot(lr_ref[b], wg2_ref[...], preferred_element_type=F32) + bg2_ref[...]
        la = _log_sigmoid(z) / GLA_TAU
        bc = jnp.dot(tri, la, precision=lax.Precision.HIGHEST, preferred_element_type=F32)
        b_last = bc[C - 1:C, :]
        qd = q_ref[b] * (dk ** -0.5) * jnp.exp(bc)
        kd = k_ref[b] * jnp.exp(-bc)
        kl = k_ref[b] * jnp.exp(b_last - bc)
        e_last = jnp.exp(b_last)
        v = v_ref[b]
        r = r_ref[b]
        outs = []
        for h in range(GLA_HEADS):
            ks = slice(h * dk, (h + 1) * dk)
            vs = slice(h * dv, (h + 1) * dv)
            qh = qd[:, ks].astype(BF16)
            kh = kd[:, ks].astype(BF16)
            vh = v[:, vs].astype(BF16)
            attn = lax.dot_general(qh, kh, (((1,), (1,)), ((), ())), preferred_element_type=F32)
            attn = jnp.where(causal, attn, 0.0)
            st = st_sc[b, h]
            o = jnp.dot(attn.astype(BF16), vh, preferred_element_type=F32)
            o = o + lax.dot_general(qh, st.astype(BF16), (((1,), (1,)), ((), ())), preferred_element_type=F32)
            upd = lax.dot_general(vh, kl[:, ks].astype(BF16), (((0,), (0,)), ((), ())), preferred_element_type=F32)
            st_sc[b, h] = st * e_last[:, ks] + upd
            o = o * lax.rsqrt(jnp.mean(o * o, axis=-1, keepdims=True) + RMS_EPS)
            rg = r[:, vs]
            outs.append(o * ng_ref[:, vs] * (rg * jax.nn.sigmoid(rg)))
        o_ref[b] = jnp.concatenate(outs, axis=-1).astype(o_ref.dtype)


def gla_mixer(q, k, v, r, lr, wg2_pad, bg2, norm_g):
    B, S, _ = q.shape
    C = GLA_CHUNK
    nc = S // C
    blk = lambda w: pl.BlockSpec((B, C, w), lambda c: (0, c, 0))
    full = lambda a: pl.BlockSpec(a.shape, lambda c: (0,) * a.ndim)
    return pl.pallas_call(
        functools.partial(_gla_kernel, batch=B),
        grid=(nc,),
        in_specs=[blk(GLA_QK_WIDTH), blk(GLA_QK_WIDTH), blk(GLA_V_WIDTH), blk(GLA_V_WIDTH), blk(LANE),
                  full(wg2_pad), full(bg2), full(norm_g)],
        out_specs=blk(GLA_V_WIDTH),
        out_shape=jax.ShapeDtypeStruct((B, S, GLA_V_WIDTH), BF16),
        scratch_shapes=[pltpu.VMEM((B, GLA_HEADS, GLA_DV, GLA_DK), F32)],
        compiler_params=_cparams(("arbitrary",)),
        name="gla",
    )(q, k, v, r, lr, wg2_pad, bg2, norm_g)


def _deepnorm_ln(x, y, gate, ln_g, ln_b):
    z = ALPHA_DN * x + (1.0 + gate) * y
    mu = jnp.mean(z, axis=-1, keepdims=True)
    zc = z - mu
    var = jnp.mean(zc * zc, axis=-1, keepdims=True)
    return zc * lax.rsqrt(var + LN_EPS) * ln_g + ln_b


def _out_ln_kernel(*refs, n_parts, matmul, has_next):
    y_refs = refs[:n_parts]
    pos = n_parts
    if matmul:
        w_refs = refs[pos:pos + n_parts]
        pos += n_parts
    x_ref, g_ref, lng_ref, lnb_ref = refs[pos:pos + 4]
    pos += 4
    if has_next:
        sc_ref, sh_ref = refs[pos:pos + 2]
        pos += 2
    o_ref = refs[pos]
    y = None
    for i in range(n_parts):
        part = jnp.dot(y_refs[i][0], w_refs[i][...], preferred_element_type=F32) if matmul else y_refs[i][0]
        y = part if y is None else y + part
    xn = _deepnorm_ln(x_ref[0], y, g_ref[0], lng_ref[...], lnb_ref[...])
    o_ref[0] = xn
    if has_next:
        refs[pos + 1][0] = (xn * (1.0 + sc_ref[0]) + sh_ref[0]).astype(refs[pos + 1].dtype)


def out_ln(ys, ws, x, gate, ln_g, ln_b, nxt, h_dtype=BF16, tm=512):
    B, S, D = x.shape
    tile = lambda w: pl.BlockSpec((1, tm, w), lambda b, i: (b, i, 0))
    vec = pl.BlockSpec((1, 1, D), lambda b, i: (b, 0, 0))
    par = pl.BlockSpec((1, D), lambda b, i: (0, 0))
    args = list(ys)
    in_specs = [tile(y.shape[-1]) for y in ys]
    if ws is not None:
        args += list(ws)
        in_specs += [pl.BlockSpec(w.shape, lambda b, i: (0, 0)) for w in ws]
    args += [x, gate, ln_g, ln_b]
    in_specs += [tile(D), vec, par, par]
    out_shape = [jax.ShapeDtypeStruct((B, S, D), F32)]
    out_specs = [tile(D)]
    if nxt is not None:
        args += list(nxt)
        in_specs += [vec, vec]
        out_shape.append(jax.ShapeDtypeStruct((B, S, D), h_dtype))
        out_specs.append(tile(D))
    res = pl.pallas_call(
        functools.partial(_out_ln_kernel, n_parts=len(ys), matmul=ws is not None, has_next=nxt is not None),
        grid=(B, S // tm),
        in_specs=in_specs,
        out_specs=out_specs,
        out_shape=out_shape,
        compiler_params=_cparams(("arbitrary", "arbitrary")),
        name="out_ln",
    )(*args)
    return (res[0], res[1]) if nxt is not None else (res[0], None)


def _ffn_ln_kernel(h_ref, w1_ref, w3_ref, w2_ref, x_ref, g_ref, lng_ref, lnb_ref, sc_ref, sh_ref,
                   o_ref, hn_ref, acc_sc):
    j = pl.program_id(2)

    @pl.when(j == 0)
    def _():
        acc_sc[...] = jnp.zeros(acc_sc.shape, F32)

    h = h_ref[0]
    a = jnp.dot(h, w1_ref[0], preferred_element_type=F32)
    b = jnp.dot(h, w3_ref[0], preferred_element_type=F32)
    u = (a * jax.nn.sigmoid(a)) * b
    acc_sc[...] += jnp.dot(u.astype(BF16), w2_ref[...], preferred_element_type=F32)

    @pl.when(j == pl.num_programs(2) - 1)
    def _():
        xn = _deepnorm_ln(x_ref[0], acc_sc[...], g_ref[0], lng_ref[...], lnb_ref[...])
        o_ref[0] = xn
        hn_ref[0] = (xn * (1.0 + sc_ref[0]) + sh_ref[0]).astype(BF16)


def ffn_ln(h, w1, w3, w2, x, gate, ln_g, ln_b, nxt, tm=512):
    B, S, D = x.shape
    F = w1.shape[1]
    tf = F // 2
    w1, w3 = (w.reshape(D, F // tf, tf).transpose(1, 0, 2) for w in (w1, w3))
    tile = pl.BlockSpec((1, tm, D), lambda b, i, j: (b, i, 0))
    vec = pl.BlockSpec((1, 1, D), lambda b, i, j: (b, 0, 0))
    par = pl.BlockSpec((1, D), lambda b, i, j: (0, 0))
    return pl.pallas_call(
        _ffn_ln_kernel,
        grid=(B, S // tm, F // tf),
        in_specs=[tile,
                  pl.BlockSpec((1, D, tf), lambda b, i, j: (j, 0, 0)),
                  pl.BlockSpec((1, D, tf), lambda b, i, j: (j, 0, 0)),
                  pl.BlockSpec((tf, D), lambda b, i, j: (j, 0)),
                  tile, vec, par, par, vec, vec],
        out_specs=[tile, tile],
        out_shape=[jax.ShapeDtypeStruct((B, S, D), F32), jax.ShapeDtypeStruct((B, S, D), BF16)],
        scratch_shapes=[pltpu.VMEM((tm, D), F32)],
        compiler_params=_cparams(("arbitrary", "arbitrary", "arbitrary"), 48),
        name="ffn_ln",
    )(h, w1, w3, w2, x, gate, ln_g, ln_b, nxt[0], nxt[1])


def _compress_kernel(x_ref, pos_ref, w1_ref, w2_ref, o_ref):
    dh = NSA_HEAD_DIM
    x = x_ref[0, 0]
    uv = jnp.dot(x, w1_ref[...], preferred_element_type=F32)
    pc = jnp.dot(pos_ref[...], w1_ref[...], preferred_element_type=F32)
    c = pc[0:1, :dh] + pc[1:2, dh:]
    n = uv.shape[0]
    nxt = pltpu.roll(uv[:, dh:], n - 1, 0)
    hid = uv[:, :dh] + nxt + c
    act = hid * jax.nn.sigmoid(hid)
    o_ref[0, 0] = jnp.dot(act.astype(BF16), w2_ref[...], preferred_element_type=F32).astype(o_ref.dtype)


def nsa_compress(kv, pos2, w1cat, w2):
    B, G, S, dh = kv.shape
    seg = NSA_CMP_STRIDE
    n = S // seg
    x = kv.reshape(B, G, n, seg * dh)
    return pl.pallas_call(
        _compress_kernel,
        grid=(B, G),
        in_specs=[
            pl.BlockSpec((1, 1, n, seg * dh), lambda b, g: (b, g, 0, 0)),
            pl.BlockSpec(pos2.shape, lambda b, g: (0, 0)),
            pl.BlockSpec(w1cat.shape, lambda b, g: (0, 0)),
            pl.BlockSpec(w2.shape, lambda b, g: (0, 0)),
        ],
        out_specs=pl.BlockSpec((1, 1, n, dh), lambda b, g: (b, g, 0, 0)),
        out_shape=jax.ShapeDtypeStruct((B, G, n, dh), BF16),
        compiler_params=_cparams(("arbitrary", "arbitrary")),
        name="nsa_compress",
    )(x, pos2, w1cat, w2)


def _nsa_kernel(q_ref, kc_ref, vc_ref, ks_ref, vs_ref, kw_ref, vw_ref, gt_ref, cb_ref, tb_ref, o_ref, sel_sc,
                s0_sc, s1_sc, p0_sc, p1_sc, *, S, TQ):
    dh = NSA_HEAD_DIM
    HPG = NSA_HPG
    LS = NSA_SLC_BLOCK
    TK = 4 * LS
    R = HPG * TQ
    n_cmp = (S - NSA_CMP_LEN) // NSA_CMP_STRIDE + 1
    n_cp = S // NSA_CMP_STRIDE
    n_slc = S // LS
    n_top = min(NSA_SLC_TOPN, n_slc)
    qi = pl.program_id(2)
    q0 = qi * TQ

    q4 = jnp.concatenate([q_ref[0, :, h * dh:(h + 1) * dh] for h in range(HPG)], axis=0)
    lanes4 = lambda a: jnp.concatenate([a] * HPG, axis=1)

    s = lax.dot_general(kc_ref[0, 0], q4, _NT, preferred_element_type=F32)
    s = s + jnp.concatenate([cb_ref[h] for h in range(HPG)], axis=1)
    n_row = lax.broadcasted_iota(jnp.int32, (n_cp, TQ), 0)
    t_q = q0 + lax.broadcasted_iota(jnp.int32, (n_cp, TQ), 1)
    valid_c = lanes4((n_row * NSA_CMP_STRIDE + (NSA_CMP_LEN - 1) <= t_q) & (n_row < n_cmp))
    s = jnp.where(valid_c, s, NEG_INF)
    m = jnp.max(s, axis=0, keepdims=True)
    m = jnp.where(m == NEG_INF, 0.0, m)
    e = jnp.exp2(s - m)
    p_c = e * (1.0 / jnp.maximum(jnp.sum(e, axis=0, keepdims=True), jnp.finfo(F32).tiny))
    o_c = lax.dot_general(vc_ref[0, 0], p_c.astype(BF16), _TN, preferred_element_type=F32)

    p_sum = p_c[:, 0:TQ]
    for h in range(1, HPG):
        p_sum = p_sum + p_c[:, h * TQ:(h + 1) * TQ]
    om = lax.broadcasted_iota(jnp.int32, (n_slc, n_cp), 0)
    c_lo = lax.broadcasted_iota(jnp.int32, (n_slc, n_cp), 1) * NSA_CMP_STRIDE
    ov = jnp.maximum(jnp.minimum(c_lo + NSA_CMP_LEN, om * LS + LS) - jnp.maximum(c_lo, om * LS), 0)
    overlap_t = ov.astype(F32) / NSA_CMP_LEN
    imp_t = jnp.dot(overlap_t, p_sum, precision=lax.Precision.HIGHEST, preferred_element_type=F32)
    blk = lax.broadcasted_iota(jnp.int32, (n_slc, TQ), 0)
    tq_lane = q0 + lax.broadcasted_iota(jnp.int32, (n_slc, TQ), 1)
    tb = lax.shift_right_logical(tq_lane, LS.bit_length() - 1)
    forced = (blk == 0) | (blk == tb) | (blk == tb - 1)
    key = jnp.where(blk > tb, NEG_INF, jnp.where(forced, jnp.inf, imp_t))
    rank = jnp.zeros((n_slc, TQ), jnp.int32)
    for mm in range(n_slc):
        km = key[mm:mm + 1, :]
        beats = (km > key) | ((km == key) & (mm < blk))
        rank = rank + beats.astype(jnp.int32)
    sel_sc[...] = jnp.where((rank < n_top) & (blk <= tb), 1.0, 0.0)

    def toep_tile(k0, n_keys):
        return jnp.concatenate(
            [jnp.concatenate([tb_ref[h, :, pl.ds(_toep_col(q0, k0 + r * BIAS_ROWS), TQ)] for h in range(HPG)], axis=1)
             for r in range(n_keys // BIAS_ROWS)], axis=0)

    def pipelined_attention(n, scores_fn, post_fn, v_fn):
        s_bufs = (s0_sc, s1_sc)
        p_bufs = (p0_sc, p1_sc)
        s_bufs[0][...] = scores_fn(0)
        p_bufs[1][...] = jnp.zeros(p_bufs[1].shape, BF16)

        def half(i, cur, carry):
            m, l, acc, alpha_prev = carry
            s_bufs[1 - cur][...] = scores_fn(jnp.minimum(i + 1, n - 1))
            s = post_fn(i, s_bufs[cur][...])
            m_new = jnp.maximum(m, jnp.max(s, axis=0, keepdims=True))
            alpha = jnp.exp2(m - m_new)
            p = jnp.exp2(s - m_new)
            l_new = alpha * l + jnp.sum(p, axis=0, keepdims=True)
            p_bufs[cur][...] = p.astype(BF16)
            pv = lax.dot_general(v_fn(jnp.maximum(i - 1, 0)), p_bufs[1 - cur][...], _TN, preferred_element_type=F32)
            return m_new, l_new, alpha_prev * acc + pv, alpha

        def body(jj, carry):
            return half(2 * jj + 1, 1, half(2 * jj, 0, carry))

        n_pairs = (n + 1) // 2
        init = (jnp.full((1, R), NEG_INF, F32), jnp.zeros((1, R), F32), jnp.zeros((dh, R), F32),
                jnp.ones((1, R), F32))
        _, l, acc, alpha = lax.fori_loop(0, n_pairs, body, init)
        pv = lax.dot_general(v_fn(jnp.minimum(2 * n_pairs - 1, n - 1)), p_bufs[1][...], _TN,
                             preferred_element_type=F32)
        return (alpha * acc + pv) * (1.0 / l)

    krow_s = lax.broadcasted_iota(jnp.int32, (TK, TQ), 0)
    tq_s = q0 + lax.broadcasted_iota(jnp.int32, (TK, TQ), 1)
    n_kt = (q0 + TQ - 1) // TK + 1

    def slc_scores(kt):
        return lax.dot_general(ks_ref[0, 0, pl.ds(pl.multiple_of(kt * TK, TK), TK), :], q4, _NT,
                               preferred_element_type=F32)

    def slc_post(kt, s):
        live = kt < n_kt
        ktc = jnp.minimum(kt, n_kt - 1)
        k0 = pl.multiple_of(ktc * TK, TK)
        flags = jnp.concatenate(
            [jnp.broadcast_to(sel_sc[pl.ds(ktc * (TK // LS) + j, 1), :], (LS, TQ)) for j in range(TK // LS)], axis=0)
        mask = lanes4((flags > jnp.where(live, 0.5, 2.0)) & (k0 + krow_s <= tq_s))
        return jnp.where(mask, s + toep_tile(k0, TK), NEG_INF)

    o_s = pipelined_attention(n_kt, slc_scores, slc_post,
                              lambda kt: vs_ref[0, 0, pl.ds(pl.multiple_of(kt * TK, TK), TK), :])

    krow_w = lax.broadcasted_iota(jnp.int32, (TQ, TQ), 0)
    tq_w = q0 + lax.broadcasted_iota(jnp.int32, (TQ, TQ), 1)
    n_wt = jnp.minimum(qi, NSA_WINDOW // TQ) + 1

    def win_body(j, carry):
        k0 = pl.multiple_of(q0 - j * TQ, TQ)
        dist = tq_w - (k0 + krow_w)
        mask = lanes4((dist >= 0) & (dist < NSA_WINDOW))
        s = lax.dot_general(kw_ref[0, 0, pl.ds(k0, TQ), :], q4, _NT, preferred_element_type=F32) + toep_tile(k0, TQ)
        return _softmax_step(carry, jnp.where(mask, s, NEG_INF), vw_ref[0, 0, pl.ds(k0, TQ), :])

    init_w = (jnp.full((1, R), NEG_INF, F32), jnp.zeros((1, R), F32), jnp.zeros((dh, R), F32))
    _, l_w, acc_w = lax.fori_loop(0, n_wt, win_body, init_w)
    o_w = acc_w * (1.0 / l_w)

    g_t = jax.nn.sigmoid(gt_ref[0, 0]).T
    outs = []
    for h in range(HPG):
        cols = slice(h * TQ, (h + 1) * TQ)
        o = (g_t[h:h + 1] * o_c[:, cols] + g_t[HPG + h:HPG + h + 1] * o_s[:, cols]
             + g_t[2 * HPG + h:2 * HPG + h + 1] * o_w[:, cols])
        outs.append(o.T)
    o_ref[0] = jnp.concatenate(outs, axis=1).astype(o_ref.dtype)


def nsa_attention(q, kc, vc, ks, vs, kw, vw, gates, cmpb, toep, TQ=256):
    B, S, _ = q.shape
    G, HPG, dh = NSA_KV_GROUPS, NSA_HPG, NSA_HEAD_DIM
    n_cp = S // NSA_CMP_STRIDE
    full_kv = pl.BlockSpec((1, 1, S, dh), lambda b, g, i: (b, g, 0, 0))
    cmp_kv = pl.BlockSpec((1, 1, n_cp, dh), lambda b, g, i: (b, g, 0, 0))
    return pl.pallas_call(
        functools.partial(_nsa_kernel, S=S, TQ=TQ),
        grid=(B, G, S // TQ),
        in_specs=[
            pl.BlockSpec((1, TQ, HPG * dh), lambda b, g, i: (b, i, g)),
            cmp_kv, cmp_kv, full_kv, full_kv, full_kv, full_kv,
            pl.BlockSpec((1, 1, TQ, LANE), lambda b, g, i: (b, g, i, 0)),
            pl.BlockSpec((HPG, n_cp, TQ), lambda b, g, i: (g, 0, i)),
            pl.BlockSpec((HPG, BIAS_ROWS, _toep_width(S)), lambda b, g, i: (g, 0, 0)),
        ],
        out_specs=pl.BlockSpec((1, TQ, HPG * dh), lambda b, g, i: (b, i, g)),
        out_shape=jax.ShapeDtypeStruct((B, S, NSA_Q_WIDTH), BF16),
        scratch_shapes=[pltpu.VMEM((S // NSA_SLC_BLOCK, TQ), F32)]
        + [pltpu.VMEM((TQ, HPG * TQ), F32)] * 2 + [pltpu.VMEM((TQ, HPG * TQ), BF16)] * 2,
        compiler_params=_cparams(("arbitrary", "arbitrary", "arbitrary"), 48),
        name="nsa_attn",
    )(q, kc, vc, ks, vs, kw, vw, gates, cmpb, toep)


def _router_kernel(h_ref, wr_ref, e_ref, w_ref, cnt_ref):
    @pl.when(pl.program_id(0) == 0)
    def _():
        cnt_ref[...] = jnp.zeros(cnt_ref.shape, F32)

    logits = jnp.dot(h_ref[...], wr_ref[...], precision=lax.Precision.HIGHEST, preferred_element_type=F32)
    tm = logits.shape[0]
    lane = lax.broadcasted_iota(jnp.int32, logits.shape, 1)
    logits = jnp.where(lane < N_EXPERTS, logits, NEG_INF)
    l1 = jnp.max(logits, axis=-1, keepdims=True)
    i1 = jnp.min(jnp.where(logits == l1, lane, LANE), axis=-1, keepdims=True)
    rest = jnp.where(lane == i1, NEG_INF, logits)
    l2 = jnp.max(rest, axis=-1, keepdims=True)
    i2 = jnp.min(jnp.where(rest == l2, lane, LANE), axis=-1, keepdims=True)
    e2 = jnp.exp(l2 - l1)
    den = 1.0 + e2
    w_ref[...] = jnp.where(lane == 0, 1.0 / den, jnp.where(lane == 1, e2 / den, 0.0))

    oh1 = lane == i1
    oh2 = lane == i2
    both = jnp.where(oh1 | oh2, 1.0, 0.0)
    earlier = lax.broadcasted_iota(jnp.int32, (tm, tm), 1) < lax.broadcasted_iota(jnp.int32, (tm, tm), 0)
    prefix = jnp.dot(jnp.where(earlier, 1.0, 0.0).astype(BF16), both.astype(BF16),
                     preferred_element_type=F32)
    seen = prefix + cnt_ref[0:1, :]
    p1 = jnp.sum(jnp.where(oh1, seen, 0.0), axis=-1, keepdims=True).astype(jnp.int32)
    p2 = jnp.sum(jnp.where(oh2, seen, 0.0), axis=-1, keepdims=True).astype(jnp.int32)
    e_ref[...] = jnp.where(lane == 0, i1, jnp.where(lane == 1, i2, jnp.where(lane == 2, p1,
                                                                              jnp.where(lane == 3, p2, 0))))
    cnt_ref[...] = cnt_ref[...] + jnp.sum(both, axis=0, keepdims=True)


def moe_route(h2, wr_pad, tm=512):
    N, D = h2.shape
    return pl.pallas_call(
        _router_kernel,
        grid=(N // tm,),
        in_specs=[pl.BlockSpec((tm, D), lambda i: (i, 0)), pl.BlockSpec(wr_pad.shape, lambda i: (0, 0))],
        out_specs=[pl.BlockSpec((tm, LANE), lambda i: (i, 0)), pl.BlockSpec((tm, LANE), lambda i: (i, 0)),
                   pl.BlockSpec((8, LANE), lambda i: (0, 0))],
        out_shape=[jax.ShapeDtypeStruct((N, LANE), jnp.int32), jax.ShapeDtypeStruct((N, LANE), F32),
                   jax.ShapeDtypeStruct((8, LANE), F32)],
        compiler_params=_cparams(("arbitrary",)),
        name="moe_router",
    )(h2, wr_pad)


def _dispatch_kernel(dest_ref, h_ref, xs_in_ref, xs_ref, sem):
    del xs_in_ref
    tm = h_ref.shape[0]
    base = pl.program_id(0) * tm * TOP_K

    def issue(r, carry):
        for k in range(TOP_K):
            d = dest_ref[base + r * TOP_K + k]
            pltpu.make_async_copy(h_ref.at[r], xs_ref.at[d], sem).start(priority=k)
        return carry

    lax.fori_loop(0, tm, issue, 0, unroll=16)
    for k in range(TOP_K):
        pltpu.make_async_copy(h_ref, xs_ref.at[pl.ds(0, tm)], sem).wait()


def moe_dispatch_rows(h2, dest_flat, n_rows, recycled=None, tm=512):
    N, D = h2.shape
    xs0 = jnp.zeros((n_rows, D), h2.dtype) if recycled is None else recycled
    grid_spec = pltpu.PrefetchScalarGridSpec(
        num_scalar_prefetch=1,
        grid=(N // tm,),
        in_specs=[pl.BlockSpec((tm, D), lambda i, d: (i, 0)), pl.BlockSpec(memory_space=pl.ANY)],
        out_specs=pl.BlockSpec(memory_space=pl.ANY),
        scratch_shapes=[pltpu.SemaphoreType.DMA],
    )
    return pl.pallas_call(
        _dispatch_kernel,
        grid_spec=grid_spec,
        out_shape=jax.ShapeDtypeStruct((n_rows, D), h2.dtype),
        input_output_aliases={2: 0},
        compiler_params=_cparams(("arbitrary",)),
        name="moe_dispatch",
    )(dest_flat, h2, xs0)


def _combine_ln_kernel(dest_ref, w_ref, ys_ref, x_ref, g_ref, lng_ref, lnb_ref, *rest, has_next):
    if has_next:
        sc_ref, sh_ref, o_ref, hn_ref, buf, sem = rest
    else:
        o_ref, buf, sem = rest
    tm = x_ref.shape[0]
    base = pl.program_id(0) * tm * TOP_K

    def issue(r, carry):
        for k in range(TOP_K):
            d = dest_ref[base + r * TOP_K + k]
            pltpu.make_async_copy(ys_ref.at[d], buf.at[k, r], sem).start(priority=k)
        return carry

    lax.fori_loop(0, tm, issue, 0, unroll=16)
    for k in range(TOP_K):
        pltpu.make_async_copy(ys_ref.at[pl.ds(0, tm)], buf.at[k], sem).wait()
    w = w_ref[...]
    y = w[:, 0:1] * buf[0] + w[:, 1:2] * buf[1]
    xn = _deepnorm_ln(x_ref[...], y, g_ref[0], lng_ref[...], lnb_ref[...])
    o_ref[...] = xn
    if has_next:
        hn_ref[...] = (xn * (1.0 + sc_ref[0]) + sh_ref[0]).astype(hn_ref.dtype)


def moe_combine_ln(ys, dest_flat, w_top, x, gate, ln_g, ln_b, nxt, tm=512):
    B, S, D = x.shape
    N = B * S
    per_b = S // tm
    tile = pl.BlockSpec((tm, D), lambda i, d: (i, 0))
    vec = pl.BlockSpec((1, 1, D), lambda i, d: (i // per_b, 0, 0))
    par = pl.BlockSpec((1, D), lambda i, d: (0, 0))
    in_specs = [pl.BlockSpec((tm, LANE), lambda i, d: (i, 0)), pl.BlockSpec(memory_space=pl.ANY), tile, vec, par, par]
    args = [w_top, ys, x.reshape(N, D), gate, ln_g, ln_b]
    out_shape = [jax.ShapeDtypeStruct((N, D), F32)]
    out_specs = [tile]
    if nxt is not None:
        in_specs += [vec, vec]
        args += list(nxt)
        out_shape.append(jax.ShapeDtypeStruct((N, D), BF16))
        out_specs.append(tile)
    grid_spec = pltpu.PrefetchScalarGridSpec(
        num_scalar_prefetch=1,
        grid=(N // tm,),
        in_specs=in_specs,
        out_specs=out_specs,
        scratch_shapes=[pltpu.VMEM((TOP_K, tm, D), F32), pltpu.SemaphoreType.DMA],
    )
    res = pl.pallas_call(
        functools.partial(_combine_ln_kernel, has_next=nxt is not None),
        grid_spec=grid_spec,
        out_shape=out_shape,
        compiler_params=_cparams(("arbitrary",)),
        name="moe_combine_ln",
    )(dest_flat, *args)
    x_new = res[0].reshape(B, S, D)
    return (x_new, res[1].reshape(B, S, D)) if nxt is not None else (x_new, None)


def _moe_ffn_kernel(be_ref, nu_ref, x_ref, w1_ref, w3_ref, w2_ref, o_ref, acc_sc, xb_sc):
    i = pl.program_id(0)
    j = pl.program_id(1)
    last = pl.num_programs(1) - 1
    used = i < nu_ref[0]

    @pl.when(j == 0)
    def _():
        acc_sc[...] = jnp.zeros(acc_sc.shape, F32)
        xb_sc[...] = x_ref[...].astype(BF16)

    @pl.when(used)
    def _():
        x = xb_sc[...]
        a = jnp.dot(x, w1_ref[0].astype(BF16), preferred_element_type=F32)
        b = jnp.dot(x, w3_ref[0].astype(BF16), preferred_element_type=F32)
        u = (a * jax.nn.sigmoid(a)) * b
        acc_sc[...] += jnp.dot(u.astype(BF16), w2_ref[0].astype(BF16), preferred_element_type=F32)

    @pl.when(j == last)
    def _():
        o_ref[...] = acc_sc[...]


def moe_ffn(xs, blk_e, n_used, w1, w3, w2, tf=512):
    n_rows, D = xs.shape
    n_blocks = blk_e.shape[0]
    RB = n_rows // n_blocks
    F = w1.shape[2]
    nf = F // tf

    def wcol(i, j, be, nu):
        return (be[i], 0, jnp.where(i < nu[0], j, nf - 1))

    def wrow(i, j, be, nu):
        return (be[i], jnp.where(i < nu[0], j, nf - 1), 0)

    grid_spec = pltpu.PrefetchScalarGridSpec(
        num_scalar_prefetch=2,
        grid=(n_blocks, nf),
        in_specs=[
            pl.BlockSpec((RB, D), lambda i, j, be, nu: (i, 0)),
            pl.BlockSpec((1, D, tf), wcol),
            pl.BlockSpec((1, D, tf), wcol),
            pl.BlockSpec((1, tf, D), wrow),
        ],
        out_specs=pl.BlockSpec((RB, D), lambda i, j, be, nu: (i, 0)),
        scratch_shapes=[pltpu.VMEM((RB, D), F32), pltpu.VMEM((RB, D), BF16)],
    )
    return pl.pallas_call(
        _moe_ffn_kernel,
        grid_spec=grid_spec,
        out_shape=jax.ShapeDtypeStruct((n_rows, D), F32),
        compiler_params=_cparams(("arbitrary", "arbitrary"), 56),
        name="moe_ffn",
    )(blk_e, n_used, xs, w1, w3, w2)


def moe_layout(e_pos, counts_f, n_tok):
    RB = _moe_row_block(n_tok)
    n_assign = n_tok * TOP_K
    n_blocks = (n_assign + N_EXPERTS * (RB - 1) + RB - 1) // RB
    counts = counts_f[0, :N_EXPERTS].astype(jnp.int32)
    padded = (counts + RB - 1) // RB * RB
    pad_end = jnp.cumsum(padded)
    pad_start = pad_end - padded
    e = e_pos[:, 0:TOP_K]
    pos = e_pos[:, TOP_K:2 * TOP_K]
    start = jnp.zeros_like(e)
    for x in range(N_EXPERTS):
        start = jnp.where(e == x, pad_start[x], start)
    dest_flat = (start + pos).reshape(-1)
    blk_start = jnp.arange(n_blocks, dtype=jnp.int32) * RB
    blk_e = jnp.minimum(jnp.sum((blk_start[:, None] >= pad_end[None, :]).astype(jnp.int32), axis=1), N_EXPERTS - 1)
    n_used = (pad_end[-1] // RB).astype(jnp.int32).reshape(1)
    return dest_flat, blk_e.astype(jnp.int32), n_used, n_blocks * RB


def _pad_cols(w, width):
    return jnp.pad(w, ((0, 0), (0, width - w.shape[1])))


def kernel(x, c, rel_bias, ada_w, ada_b, ln_g, ln_b, ab_w_in, gla_wg2, gla_bg2, gla_norm_g, ab_w_out,
           ffn_w1, ffn_w3, ffn_w2, nsa_w_in, nsa_cmp_pos, nsa_cmp_w1k, nsa_cmp_w2k, nsa_cmp_w1v, nsa_cmp_w2v,
           nsa_w_out, moe_router, moe_w1, moe_w3, moe_w2):
    B, S, D = x.shape
    n_tok = B * S
    mod = adaln_mod(c, ada_w, ada_b)
    mods = [[m.reshape(B, 1, D) for m in jnp.split(mod[i], 6, axis=-1)] for i in range(DEPTH)]
    toep, cmpb = build_bias_tables(rel_bias, S)

    xs_dead = None
    sh1, sc1 = mods[0][0], mods[0][1]
    h = modulate(x, sc1, sh1)
    for i in range(DEPTH):
        j = i // 2
        _, _, g1, sh2, sc2, g2 = mods[i]
        nxt2 = (mods[i + 1][1], mods[i + 1][0]) if i + 1 < DEPTH else None
        lng = ln_g[i].reshape(2, 1, D)
        lnb = ln_b[i].reshape(2, 1, D)
        if i % 2 == 0:
            w = ab_w_in[j]
            cuts = np.cumsum((MOBA_WIDTH, MOBA_WIDTH, MOBA_WIDTH, GLA_QK_WIDTH, GLA_QK_WIDTH, GLA_V_WIDTH,
                              GLA_V_WIDTH))
            parts = jnp.split(w, cuts.tolist(), axis=1)
            parts[-1] = _pad_cols(parts[-1], LANE)
            w_cat = jnp.concatenate(parts, axis=1).astype(BF16)
            widths = [p.shape[1] for p in parts]
            offs = np.concatenate([[0], np.cumsum(widths)[:-1]]).tolist()
            scales = [MOBA_HEAD_DIM ** -0.5 * LOG2E] + [1.0] * 7
            kinds = ["flat", "flat", "flat_t"] + ["flat"] * 5
            offs[2] = 0
            segs = tuple((o, wd, kd, s) for o, wd, kd, s in zip(offs, widths, kinds, scales))
            dts = [BF16, BF16, BF16, F32, F32, F32, F32, F32]
            mq, mk, mv, gq, gk, gv, gr, glr = project(h, w_cat, segs, dts, wt=parts[2].T.astype(BF16))
            y_moba = moba_attention(mq, mk, mv, toep)
            wg2_pad = jnp.pad(gla_wg2[j], ((0, LANE - GLA_GATE_RANK), (0, 0)))
            y_gla = gla_mixer(gq, gk, gv, gr, glr, wg2_pad, gla_bg2[j].reshape(1, -1), gla_norm_g[j].reshape(1, -1))
            w_out = ab_w_out[j].astype(BF16)
            x, h = out_ln([y_moba, y_gla], [w_out[:MOBA_WIDTH], w_out[MOBA_WIDTH:]], x, g1, lng[0], lnb[0],
                          (sc2, sh2))
            x, h = ffn_ln(h, ffn_w1[j].astype(BF16), ffn_w3[j].astype(BF16), ffn_w2[j].astype(BF16), x, g2,
                          lng[1], lnb[1], nxt2)
        else:
            G, HPG, dh = NSA_KV_GROUPS, NSA_HPG, NSA_HEAD_DIM
            w = nsa_w_in[j]
            wq = w[:, :NSA_Q_WIDTH]
            wkv = w[:, NSA_Q_WIDTH:NSA_Q_WIDTH + 6 * NSA_KV_WIDTH]
            wg = w[:, NSA_Q_WIDTH + 6 * NSA_KV_WIDTH:].reshape(D, G, HPG, 3)
            wg = _pad_cols(wg.transpose(0, 1, 3, 2).reshape(D * G, 3 * HPG), LANE).reshape(D, G * LANE)
            w_cat = jnp.concatenate([wq, wkv, wg], axis=1).astype(BF16)
            segs = [(0, NSA_Q_WIDTH, "flat", dh ** -0.5 * LOG2E)]
            segs += [(NSA_Q_WIDTH + n * NSA_KV_WIDTH, NSA_KV_WIDTH, "group", 1.0) for n in range(6)]
            segs += [(NSA_Q_WIDTH + 6 * NSA_KV_WIDTH, G * LANE, "group", 1.0)]
            dts = [BF16] * 7 + [F32]
            q, kc, vc, ks, vs, kw, vw, gates = project(h, w_cat, tuple(segs), dts)
            half = NSA_CMP_STRIDE * dh
            pos2 = jnp.pad(nsa_cmp_pos[j].reshape(2, half), ((0, 6), (0, 0))).astype(BF16)
            w1k = jnp.concatenate([nsa_cmp_w1k[j][:half], nsa_cmp_w1k[j][half:]], axis=1).astype(BF16)
            w1v = jnp.concatenate([nsa_cmp_w1v[j][:half], nsa_cmp_w1v[j][half:]], axis=1).astype(BF16)
            k_cmp = nsa_compress(kc, pos2, w1k, nsa_cmp_w2k[j].astype(BF16))
            v_cmp = nsa_compress(vc, pos2, w1v, nsa_cmp_w2v[j].astype(BF16))
            y = nsa_attention(q, k_cmp, v_cmp, ks, vs, kw, vw, gates, cmpb, toep)
            x, h = out_ln([y], [nsa_w_out[j].astype(BF16)], x, g1, lng[0], lnb[0], (sc2, sh2), h_dtype=F32)
            h2 = h.reshape(n_tok, D)
            e_pos, w_top, counts = moe_route(h2, _pad_cols(moe_router[j], LANE))
            dest_flat, blk_e, n_used, n_rows = moe_layout(e_pos, counts, n_tok)
            xs = moe_dispatch_rows(h2, dest_flat, n_rows, recycled=xs_dead)
            xs_dead = xs
            stack = lambda w: w.reshape((-1,) + w.shape[2:])
            ys = moe_ffn(xs, blk_e + j * N_EXPERTS, n_used, stack(moe_w1), stack(moe_w3), stack(moe_w2))
            x, h = moe_combine_ln(ys, dest_flat, w_top, x, g2, lng[1], lnb[1], nxt2)
    return x
```

```python
import functools
import math

import jax
import jax.numpy as jnp
import numpy as np
from jax import lax
from jax.experimental import pallas as pl
from jax.experimental.pallas import tpu as pltpu

F32 = jnp.float32
BF16 = jnp.bfloat16
NEG_INF = float("-inf")

DEPTH = 4
ALPHA_DN = (2 * DEPTH) ** 0.25
LN_EPS = 1e-5
RMS_EPS = 1e-6

N_BUCKETS = 32
MAX_DISTANCE = 1024

MOBA_HEADS = 8
MOBA_HEAD_DIM = 64
MOBA_BLOCK = 256
MOBA_TOPK = 3
MOBA_WIDTH = MOBA_HEADS * MOBA_HEAD_DIM

GLA_HEADS = 4
GLA_DK = 64
GLA_DV = 128
GLA_GATE_RANK = 16
GLA_TAU = 16.0
GLA_CHUNK = 64
GLA_QK_WIDTH = GLA_HEADS * GLA_DK
GLA_V_WIDTH = GLA_HEADS * GLA_DV

NSA_HEADS = 8
NSA_KV_GROUPS = 2
NSA_HPG = NSA_HEADS // NSA_KV_GROUPS
NSA_HEAD_DIM = 128
NSA_CMP_LEN = 32
NSA_CMP_STRIDE = 16
NSA_SLC_BLOCK = 64
NSA_SLC_TOPN = 16
NSA_WINDOW = 512
NSA_Q_WIDTH = NSA_HEADS * NSA_HEAD_DIM
NSA_KV_WIDTH = NSA_KV_GROUPS * NSA_HEAD_DIM

N_EXPERTS = 8
TOP_K = 2
MOE_BLOCKS_PER_EXPERT = 4


def _moe_row_block(n_tok):
    balanced = n_tok * TOP_K // N_EXPERTS
    rows = balanced // MOE_BLOCKS_PER_EXPERT + balanced // 128
    return -(-rows // 16) * 16

LOG2E = math.log2(math.e)
LANE = 128
BIAS_ROWS = 128


def _split_bf16(a):
    hi = a.astype(BF16)
    return hi, (a - hi.astype(F32)).astype(BF16)


def _cparams(sem, vmem_mib=40, **extra):
    return pltpu.CompilerParams(dimension_semantics=sem, vmem_limit_bytes=vmem_mib << 20, **extra)


def _adaln_kernel(c_ref, w_ref, b_ref, o_ref):
    c = c_ref[...]
    cond = c * jax.nn.sigmoid(c)
    o_ref[0] = jnp.dot(cond, w_ref[0], preferred_element_type=F32) + b_ref[0]


def adaln_mod(c, ada_w, ada_b):
    B, D = c.shape
    L, _, N = ada_w.shape
    rows = 8
    cp = jnp.zeros((rows, D), F32).at[:B].set(c)
    tn = N // 4
    out = pl.pallas_call(
        _adaln_kernel,
        grid=(L, N // tn),
        in_specs=[
            pl.BlockSpec((rows, D), lambda l, j: (0, 0)),
            pl.BlockSpec((1, D, tn), lambda l, j: (l, 0, j)),
            pl.BlockSpec((1, 1, tn), lambda l, j: (l, 0, j)),
        ],
        out_specs=pl.BlockSpec((1, rows, tn), lambda l, j: (l, 0, j)),
        out_shape=jax.ShapeDtypeStruct((L, rows, N), F32),
        compiler_params=_cparams(("arbitrary", "arbitrary")),
        name="adaln_mod",
    )(cp, ada_w, ada_b.reshape(L, 1, N))
    return out[:, :B]


def _modulate_kernel(x_ref, sc_ref, sh_ref, o_ref):
    o_ref[0] = (x_ref[0] * (1.0 + sc_ref[0]) + sh_ref[0]).astype(o_ref.dtype)


def modulate(x, sc, sh, tm=512):
    B, S, D = x.shape
    return pl.pallas_call(
        _modulate_kernel,
        grid=(B, S // tm),
        in_specs=[
            pl.BlockSpec((1, tm, D), lambda b, i: (b, i, 0)),
            pl.BlockSpec((1, 1, D), lambda b, i: (b, 0, 0)),
            pl.BlockSpec((1, 1, D), lambda b, i: (b, 0, 0)),
        ],
        out_specs=pl.BlockSpec((1, tm, D), lambda b, i: (b, i, 0)),
        out_shape=jax.ShapeDtypeStruct((B, S, D), BF16),
        compiler_params=_cparams(("arbitrary", "arbitrary")),
        name="modulate",
    )(x, sc, sh)


def _t5_bucket(dist):
    max_exact = N_BUCKETS // 2
    n = jnp.maximum(dist, 0)
    nf = jnp.maximum(n, 1).astype(jnp.float32)
    large = max_exact + (jnp.log(nf / max_exact) / math.log(MAX_DISTANCE / max_exact) * (N_BUCKETS - max_exact)).astype(jnp.int32)
    large = jnp.minimum(large, N_BUCKETS - 1)
    return jnp.where(n < max_exact, n, large)


def _bias_table_kernel(rb_ref, bm_ref, o_ref, *, rows_per_step):
    h = pl.program_id(0)
    n_steps = bm_ref.shape[0] // rows_per_step

    def body(i, carry):
        r0 = pl.multiple_of(i * rows_per_step, rows_per_step)
        bm = bm_ref[pl.ds(r0, rows_per_step), :]
        acc = jnp.zeros(bm.shape, F32)
        for b in range(N_BUCKETS):
            acc = jnp.where(bm == b, rb_ref[b, h] * LOG2E, acc)
        o_ref[0, pl.ds(r0, rows_per_step), :] = acc
        return carry

    lax.fori_loop(0, n_steps, body, 0)


def bias_table(rel_bias, bucket_map, rows_per_step):
    R, W = bucket_map.shape
    H = rel_bias.shape[1]
    return pl.pallas_call(
        functools.partial(_bias_table_kernel, rows_per_step=rows_per_step),
        grid=(H,),
        in_specs=[
            pl.BlockSpec(memory_space=pltpu.SMEM),
            pl.BlockSpec((R, W), lambda h: (0, 0)),
        ],
        out_specs=pl.BlockSpec((1, R, W), lambda h: (h, 0, 0)),
        out_shape=jax.ShapeDtypeStruct((H, R, W), F32),
        compiler_params=_cparams(("arbitrary",)),
        name="bias_table",
    )(rel_bias, bucket_map)


def _toep_width(S):
    return S + BIAS_ROWS


def _toep_col(q0, k0):
    return pl.multiple_of(q0 - k0 + BIAS_ROWS, LANE)


def build_bias_tables(rel_bias, S):
    j = jnp.arange(BIAS_ROWS, dtype=jnp.int32)[:, None]
    c = jnp.arange(_toep_width(S), dtype=jnp.int32)[None, :]
    toep_map = _t5_bucket(c - j - BIAS_ROWS)
    n_cmp_pad = S // NSA_CMP_STRIDE
    t = jnp.arange(S, dtype=jnp.int32)[None, :]
    cmp_end = jnp.arange(n_cmp_pad, dtype=jnp.int32)[:, None] * NSA_CMP_STRIDE + NSA_CMP_LEN - 1
    cmp_map = _t5_bucket(t - cmp_end)
    return bias_table(rel_bias, toep_map, 8), bias_table(rel_bias, cmp_map, 8)


def _proj_kernel(h_ref, w_ref, wt_ref, *o_refs, segs):
    h = h_ref[0]
    for (off, width, kind, scale), o_ref in zip(segs, o_refs):
        if kind == "flat_t":
            acc = lax.dot_general(wt_ref[off:off + width, :], h, _NT, preferred_element_type=F32)
        else:
            acc = jnp.dot(h, w_ref[:, off:off + width], preferred_element_type=F32)
        if scale != 1.0:
            acc = acc * scale
        if kind in ("flat", "flat_t"):
            o_ref[0] = acc.astype(o_ref.dtype)
        else:
            for g in range(width // LANE):
                o_ref[0, g] = acc[:, g * LANE:(g + 1) * LANE].astype(o_ref.dtype)


def project(h, w_cat, segs, out_dtypes, wt=None, tm=512):
    B, S, D = h.shape
    if wt is None:
        wt = jnp.zeros((8, D), w_cat.dtype)
    out_shapes, out_specs = [], []
    for (off, width, kind, scale), dt in zip(segs, out_dtypes):
        if kind == "flat":
            out_shapes.append(jax.ShapeDtypeStruct((B, S, width), dt))
            out_specs.append(pl.BlockSpec((1, tm, width), lambda b, i: (b, i, 0)))
        elif kind == "flat_t":
            out_shapes.append(jax.ShapeDtypeStruct((B, width, S), dt))
            out_specs.append(pl.BlockSpec((1, width, tm), lambda b, i: (b, 0, i)))
        else:
            G = width // LANE
            out_shapes.append(jax.ShapeDtypeStruct((B, G, S, LANE), dt))
            out_specs.append(pl.BlockSpec((1, G, tm, LANE), lambda b, i: (b, 0, i, 0)))
    return pl.pallas_call(
        functools.partial(_proj_kernel, segs=segs),
        grid=(B, S // tm),
        in_specs=[
            pl.BlockSpec((1, tm, D), lambda b, i: (b, i, 0)),
            pl.BlockSpec(w_cat.shape, lambda b, i: (0, 0)),
            pl.BlockSpec(wt.shape, lambda b, i: (0, 0)),
        ],
        out_specs=out_specs,
        out_shape=out_shapes,
        compiler_params=_cparams(("arbitrary", "arbitrary"), 48),
        name="in_proj",
    )(h, w_cat, wt)


_NT = (((1,), (1,)), ((), ()))
_TN = (((0,), (0,)), ((), ()))


def _softmax_step(carry, s, v_tile):
    m, l, acc = carry
    m_new = jnp.maximum(m, jnp.max(s, axis=0, keepdims=True))
    alpha = jnp.exp2(m - m_new)
    p = jnp.exp2(s - m_new)
    l_new = alpha * l + jnp.sum(p, axis=0, keepdims=True)
    acc_new = alpha * acc + lax.dot_general(v_tile, p.astype(BF16), _TN, preferred_element_type=F32)
    return m_new, l_new, acc_new


def _moba_kernel(q_ref, k_ref, v_ref, tb_ref, o_ref, kmean_sc, sel_sc, s0_sc, s1_sc, p0_sc, p1_sc,
                 *, S, heads_per_step):
    L = MOBA_BLOCK
    dh = MOBA_HEAD_DIM
    nb = S // L
    qi = pl.program_id(2)
    q0 = qi * L

    @pl.when(qi == 0)
    def _():
        kf = k_ref[0].astype(F32)
        kmean_sc[...] = jnp.mean(kf.reshape(nb, L, kf.shape[-1]), axis=1)

    q2 = q_ref[0]
    lane_q = lax.broadcasted_iota(jnp.int32, q2.shape, 1)
    lane_m = lax.broadcasted_iota(jnp.int32, kmean_sc.shape, 1)
    blk = lax.broadcasted_iota(jnp.int32, (nb, L), 0)
    qz = []
    for hh in range(heads_per_step):
        in_head = (lane_q >= hh * dh) & (lane_q < (hh + 1) * dh)
        qz.append(jnp.where(in_head, q2, jnp.zeros_like(q2)))
        km = jnp.where((lane_m >= hh * dh) & (lane_m < (hh + 1) * dh), kmean_sc[...], 0.0)
        km_hi, km_lo = _split_bf16(km)
        gate = (lax.dot_general(km_hi, q2, _NT, preferred_element_type=F32)
                + lax.dot_general(km_lo, q2, _NT, preferred_element_type=F32))
        valid = blk < qi
        gm = jnp.where(valid, gate, NEG_INF)
        rank = jnp.zeros((nb, L), jnp.int32)
        for m in range(nb):
            gmm = gm[m:m + 1, :]
            beats = (gmm > gm) | ((gmm == gm) & (m < blk))
            rank = rank + beats.astype(jnp.int32)
        sel_sc[hh] = jnp.where(valid & (rank < MOBA_TOPK), 1.0, 0.0)

    def bias_tile(hh, k0, n_keys):
        return jnp.concatenate([tb_ref[hh, :, pl.ds(_toep_col(q0, k0 + r * BIAS_ROWS), L)]
                                for r in range(n_keys // BIAS_ROWS)], axis=0)

    krow = lax.broadcasted_iota(jnp.int32, (L, L), 0)
    qcol = lax.broadcasted_iota(jnp.int32, (L, L), 1)
    k_own = k_ref[0, pl.ds(pl.multiple_of(q0, L), L), :]
    v_tile = lambda hh, k0, n_keys: v_ref[0, hh * dh:(hh + 1) * dh, pl.ds(k0, n_keys)]
    carry = []
    for hh in range(heads_per_step):
        s = lax.dot_general(k_own, qz[hh], _NT, preferred_element_type=F32) + bias_tile(hh, q0, L)
        s = jnp.where(krow <= qcol, s, NEG_INF)
        m0 = jnp.max(s, axis=0, keepdims=True)
        p = jnp.exp2(s - m0)
        l0 = jnp.sum(p, axis=0, keepdims=True)
        acc0 = jnp.dot(v_tile(hh, pl.multiple_of(q0, L), L), p.astype(BF16), preferred_element_type=F32)
        carry += [m0, l0, acc0, jnp.ones((1, L), F32)]

    n = (qi + 1) // 2
    s_bufs = (s0_sc, s1_sc)
    p_bufs = (p0_sc, p1_sc)
    tile_k0 = lambda i: pl.multiple_of(jnp.clip(i, 0, jnp.maximum(n - 1, 0)) * 2 * L, 2 * L)

    def put_scores(i, buf):
        kt = k_ref[0, pl.ds(tile_k0(i), 2 * L), :]
        for hh in range(heads_per_step):
            buf[hh] = lax.dot_general(kt, qz[hh], _NT, preferred_element_type=F32)

    put_scores(0, s_bufs[0])
    p_bufs[1][...] = jnp.zeros(p_bufs[1].shape, BF16)

    def half(i, cur, carry):
        put_scores(i + 1, s_bufs[1 - cur])
        k0 = tile_k0(i)
        blk0 = lax.shift_right_logical(k0, L.bit_length() - 1)
        thresh = jnp.where(i < n, 0.5, 2.0)
        out = []
        for hh in range(heads_per_step):
            m, l, acc, alpha_prev = carry[4 * hh:4 * hh + 4]
            flags = jnp.concatenate(
                [jnp.broadcast_to(sel_sc[hh, pl.ds(blk0 + j, 1), :], (L, L)) for j in range(2)], axis=0)
            s = jnp.where(flags > thresh, s_bufs[cur][hh] + bias_tile(hh, k0, 2 * L), NEG_INF)
            m_new = jnp.maximum(m, jnp.max(s, axis=0, keepdims=True))
            alpha = jnp.exp2(m - m_new)
            p = jnp.exp2(s - m_new)
            l_new = alpha * l + jnp.sum(p, axis=0, keepdims=True)
            p_bufs[cur][hh] = p.astype(BF16)
            pv = jnp.dot(v_tile(hh, tile_k0(i - 1), 2 * L), p_bufs[1 - cur][hh], preferred_element_type=F32)
            out += [m_new, l_new, alpha_prev * acc + pv, alpha]
        return tuple(out)

    def body(jj, carry):
        return half(2 * jj + 1, 1, half(2 * jj, 0, carry))

    n_pairs = (n + 1) // 2
    carry = lax.fori_loop(0, n_pairs, body, tuple(carry))
    outs = []
    for hh in range(heads_per_step):
        _, l, acc, alpha = carry[4 * hh:4 * hh + 4]
        pv = jnp.dot(v_tile(hh, tile_k0(2 * n_pairs - 1), 2 * L), p_bufs[1][hh], preferred_element_type=F32)
        outs.append((alpha * acc + pv) * (1.0 / l))
    o_ref[0] = jnp.concatenate(outs, axis=0).T.astype(o_ref.dtype)


def moba_attention(q, k, v, toep):
    B, S, _ = q.shape
    L = MOBA_BLOCK
    hps = 4
    wl = hps * MOBA_HEAD_DIM
    n_hp = MOBA_HEADS // hps
    nb = S // L
    return pl.pallas_call(
        functools.partial(_moba_kernel, S=S, heads_per_step=hps),
        grid=(n_hp, B, nb),
        in_specs=[
            pl.BlockSpec((1, L, wl), lambda hp, b, i: (b, i, hp)),
            pl.BlockSpec((1, S, wl), lambda hp, b, i: (b, 0, hp)),
            pl.BlockSpec((1, wl, S), lambda hp, b, i: (b, hp, 0)),
            pl.BlockSpec((hps, BIAS_ROWS, _toep_width(S)), lambda hp, b, i: (hp, 0, 0)),
        ],
        out_specs=pl.BlockSpec((1, L, wl), lambda hp, b, i: (b, i, hp)),
        out_shape=jax.ShapeDtypeStruct((B, S, MOBA_WIDTH), BF16),
        scratch_shapes=[pltpu.VMEM((nb, wl), F32), pltpu.VMEM((hps, nb, L), F32)]
        + [pltpu.VMEM((hps, 2 * L, L), F32)] * 2 + [pltpu.VMEM((hps, 2 * L, L), BF16)] * 2,
        compiler_params=_cparams(("arbitrary", "arbitrary", "arbitrary"), 48),
        name="moba_attn",
    )(q, k, v, toep)


def _log_sigmoid(z):
    return jnp.minimum(z, 0.0) - jnp.log1p(jnp.exp(-jnp.abs(z)))


def _gla_kernel(q_ref, k_ref, v_ref, r_ref, lr_ref, wg2_ref, bg2_ref, ng_ref, o_ref, st_sc, *, batch):
    C = GLA_CHUNK
    dk, dv = GLA_DK, GLA_DV

    @pl.when(pl.program_id(0) == 0)
    def _():
        st_sc[...] = jnp.zeros(st_sc.shape, F32)

    row = lax.broadcasted_iota(jnp.int32, (C, C), 0)
    col = lax.broadcasted_iota(jnp.int32, (C, C), 1)
    causal = col <= row
    tri = jnp.where(causal, 1.0, 0.0)
    for b in range(batch):
        z = jnp.dot(lr_ref[b], wg2_ref[...], preferred_element_type=F32) + bg2_ref[...]
        la = _log_sigmoid(z) / GLA_TAU
        la_hi, la_lo = _split_bf16(la)
        tri_b = tri.astype(BF16)
        bc = (jnp.dot(tri_b, la_hi, preferred_element_type=F32)
              + jnp.dot(tri_b, la_lo, preferred_element_type=F32))
        b_last = bc[C - 1:C, :]
        qd = q_ref[b] * (dk ** -0.5) * jnp.exp(bc)
        kd = k_ref[b] * jnp.exp(-bc)
        kl = k_ref[b] * jnp.exp(b_last - bc)
        e_last = jnp.exp(b_last)
        v = v_ref[b]
        r = r_ref[b]
        outs = []
        for h in range(GLA_HEADS):
            ks = slice(h * dk, (h + 1) * dk)
            vs = slice(h * dv, (h + 1) * dv)
            qh = qd[:, ks].astype(BF16)
            kh = kd[:, ks].astype(BF16)
            vh = v[:, vs].astype(BF16)
            attn = lax.dot_general(qh, kh, (((1,), (1,)), ((), ())), preferred_element_type=F32)
            attn = jnp.where(causal, attn, 0.0)
            st = st_sc[b, h]
            o = jnp.dot(attn.astype(BF16), vh, preferred_element_type=F32)
            o = o + lax.dot_general(qh, st.astype(BF16), (((1,), (1,)), ((), ())), preferred_element_type=F32)
            upd = lax.dot_general(vh, kl[:, ks].astype(BF16), (((0,), (0,)), ((), ())), preferred_element_type=F32)
            st_sc[b, h] = st * e_last[:, ks] + upd
            o = o * lax.rsqrt(jnp.mean(o * o, axis=-1, keepdims=True) + RMS_EPS)
            rg = r[:, vs]
            outs.append(o * ng_ref[:, vs] * (rg * jax.nn.sigmoid(rg)))
        o_ref[b] = jnp.concatenate(outs, axis=-1).astype(o_ref.dtype)


def gla_mixer(q, k, v, r, lr, wg2_pad, bg2, norm_g):
    B, S, _ = q.shape
    C = GLA_CHUNK
    nc = S // C
    blk = lambda w: pl.BlockSpec((B, C, w), lambda c: (0, c, 0))
    full = lambda a: pl.BlockSpec(a.shape, lambda c: (0,) * a.ndim)
    return pl.pallas_call(
        functools.partial(_gla_kernel, batch=B),
        grid=(nc,),
        in_specs=[blk(GLA_QK_WIDTH), blk(GLA_QK_WIDTH), blk(GLA_V_WIDTH), blk(GLA_V_WIDTH), blk(LANE),
                  full(wg2_pad), full(bg2), full(norm_g)],
        out_specs=blk(GLA_V_WIDTH),
        out_shape=jax.ShapeDtypeStruct((B, S, GLA_V_WIDTH), BF16),
        scratch_shapes=[pltpu.VMEM((B, GLA_HEADS, GLA_DV, GLA_DK), F32)],
        compiler_params=_cparams(("arbitrary",)),
        name="gla",
    )(q, k, v, r, lr, wg2_pad, bg2, norm_g)


def _deepnorm_ln(x, y, gate, ln_g, ln_b):
    z = ALPHA_DN * x + (1.0 + gate) * y
    mu = jnp.mean(z, axis=-1, keepdims=True)
    zc = z - mu
    var = jnp.mean(zc * zc, axis=-1, keepdims=True)
    return zc * lax.rsqrt(var + LN_EPS) * ln_g + ln_b


def _out_ln_kernel(*refs, n_parts, matmul, has_next):
    y_refs = refs[:n_parts]
    pos = n_parts
    if matmul:
        w_refs = refs[pos:pos + n_parts]
        pos += n_parts
    x_ref, g_ref, lng_ref, lnb_ref = refs[pos:pos + 4]
    pos += 4
    if has_next:
        sc_ref, sh_ref = refs[pos:pos + 2]
        pos += 2
    o_ref = refs[pos]
    y = None
    for i in range(n_parts):
        part = jnp.dot(y_refs[i][0], w_refs[i][...], preferred_element_type=F32) if matmul else y_refs[i][0]
        y = part if y is None else y + part
    xn = _deepnorm_ln(x_ref[0], y, g_ref[0], lng_ref[...], lnb_ref[...])
    o_ref[0] = xn
    if has_next:
        refs[pos + 1][0] = (xn * (1.0 + sc_ref[0]) + sh_ref[0]).astype(refs[pos + 1].dtype)


def out_ln(ys, ws, x, gate, ln_g, ln_b, nxt, h_dtype=BF16, tm=512):
    B, S, D = x.shape
    tile = lambda w: pl.BlockSpec((1, tm, w), lambda b, i: (b, i, 0))
    vec = pl.BlockSpec((1, 1, D), lambda b, i: (b, 0, 0))
    par = pl.BlockSpec((1, D), lambda b, i: (0, 0))
    args = list(ys)
    in_specs = [tile(y.shape[-1]) for y in ys]
    if ws is not None:
        args += list(ws)
        in_specs += [pl.BlockSpec(w.shape, lambda b, i: (0, 0)) for w in ws]
    args += [x, gate, ln_g, ln_b]
    in_specs += [tile(D), vec, par, par]
    out_shape = [jax.ShapeDtypeStruct((B, S, D), F32)]
    out_specs = [tile(D)]
    if nxt is not None:
        args += list(nxt)
        in_specs += [vec, vec]
        out_shape.append(jax.ShapeDtypeStruct((B, S, D), h_dtype))
        out_specs.append(tile(D))
    res = pl.pallas_call(
        functools.partial(_out_ln_kernel, n_parts=len(ys), matmul=ws is not None, has_next=nxt is not None),
        grid=(B, S // tm),
        in_specs=in_specs,
        out_specs=out_specs,
        out_shape=out_shape,
        compiler_params=_cparams(("arbitrary", "arbitrary")),
        name="out_ln",
    )(*args)
    return (res[0], res[1]) if nxt is not None else (res[0], None)


def _ffn_ln_kernel(h_ref, w1_ref, w3_ref, w2_ref, x_ref, g_ref, lng_ref, lnb_ref, sc_ref, sh_ref,
                   o_ref, hn_ref, acc_sc):
    j = pl.program_id(2)

    @pl.when(j == 0)
    def _():
        acc_sc[...] = jnp.zeros(acc_sc.shape, F32)

    h = h_ref[0]
    a = jnp.dot(h, w1_ref[...], preferred_element_type=F32)
    b = jnp.dot(h, w3_ref[...], preferred_element_type=F32)
    u = (a * jax.nn.sigmoid(a)) * b
    acc_sc[...] += jnp.dot(u.astype(BF16), w2_ref[...], preferred_element_type=F32)

    @pl.when(j == pl.num_programs(2) - 1)
    def _():
        xn = _deepnorm_ln(x_ref[0], acc_sc[...], g_ref[0], lng_ref[...], lnb_ref[...])
        o_ref[0] = xn
        hn_ref[0] = (xn * (1.0 + sc_ref[0]) + sh_ref[0]).astype(BF16)


def ffn_ln(h, w1, w3, w2, x, gate, ln_g, ln_b, nxt, tm=512):
    B, S, D = x.shape
    F = w1.shape[1]
    tf = F // 2
    tile = pl.BlockSpec((1, tm, D), lambda b, i, j: (b, i, 0))
    vec = pl.BlockSpec((1, 1, D), lambda b, i, j: (b, 0, 0))
    par = pl.BlockSpec((1, D), lambda b, i, j: (0, 0))
    return pl.pallas_call(
        _ffn_ln_kernel,
        grid=(B, S // tm, F // tf),
        in_specs=[tile,
                  pl.BlockSpec((D, tf), lambda b, i, j: (0, j)),
                  pl.BlockSpec((D, tf), lambda b, i, j: (0, j)),
                  pl.BlockSpec((tf, D), lambda b, i, j: (j, 0)),
                  tile, vec, par, par, vec, vec],
        out_specs=[tile, tile],
        out_shape=[jax.ShapeDtypeStruct((B, S, D), F32), jax.ShapeDtypeStruct((B, S, D), BF16)],
        scratch_shapes=[pltpu.VMEM((tm, D), F32)],
        compiler_params=_cparams(("arbitrary", "arbitrary", "arbitrary"), 48),
        name="ffn_ln",
    )(h, w1, w3, w2, x, gate, ln_g, ln_b, nxt[0], nxt[1])


def _compress_kernel(x_ref, pos_ref, w1_ref, w2_ref, o_ref):
    dh = NSA_HEAD_DIM
    x = x_ref[0, 0]
    uv = jnp.dot(x, w1_ref[...], preferred_element_type=F32)
    pc = jnp.dot(pos_ref[...], w1_ref[...], preferred_element_type=F32)
    c = pc[0:1, :dh] + pc[1:2, dh:]
    n = uv.shape[0]
    nxt = pltpu.roll(uv[:, dh:], n - 1, 0)
    hid = uv[:, :dh] + nxt + c
    act = hid * jax.nn.sigmoid(hid)
    o_ref[0, 0] = jnp.dot(act.astype(BF16), w2_ref[...], preferred_element_type=F32).astype(o_ref.dtype)


def nsa_compress(kv, pos2, w1cat, w2):
    B, G, S, dh = kv.shape
    seg = NSA_CMP_STRIDE
    n = S // seg
    x = kv.reshape(B, G, n, seg * dh)
    return pl.pallas_call(
        _compress_kernel,
        grid=(B, G),
        in_specs=[
            pl.BlockSpec((1, 1, n, seg * dh), lambda b, g: (b, g, 0, 0)),
            pl.BlockSpec(pos2.shape, lambda b, g: (0, 0)),
            pl.BlockSpec(w1cat.shape, lambda b, g: (0, 0)),
            pl.BlockSpec(w2.shape, lambda b, g: (0, 0)),
        ],
        out_specs=pl.BlockSpec((1, 1, n, dh), lambda b, g: (b, g, 0, 0)),
        out_shape=jax.ShapeDtypeStruct((B, G, n, dh), BF16),
        compiler_params=_cparams(("arbitrary", "arbitrary")),
        name="nsa_compress",
    )(x, pos2, w1cat, w2)


def _nsa_kernel(q_ref, kc_ref, vc_ref, ks_ref, vs_ref, kw_ref, vw_ref, gt_ref, cb_ref, tb_ref, o_ref, sel_sc,
                s0_sc, s1_sc, p0_sc, p1_sc, *, S, TQ):
    dh = NSA_HEAD_DIM
    HPG = NSA_HPG
    LS = NSA_SLC_BLOCK
    TK = 4 * LS
    R = HPG * TQ
    n_cmp = (S - NSA_CMP_LEN) // NSA_CMP_STRIDE + 1
    n_cp = S // NSA_CMP_STRIDE
    n_slc = S // LS
    n_top = min(NSA_SLC_TOPN, n_slc)
    qi = pl.program_id(2)
    q0 = qi * TQ

    q4 = jnp.concatenate([q_ref[0, :, h * dh:(h + 1) * dh] for h in range(HPG)], axis=0)
    lanes4 = lambda a: jnp.concatenate([a] * HPG, axis=1)

    s = lax.dot_general(kc_ref[0, 0], q4, _NT, preferred_element_type=F32)
    s = s + jnp.concatenate([cb_ref[h] for h in range(HPG)], axis=1)
    n_row = lax.broadcasted_iota(jnp.int32, (n_cp, TQ), 0)
    t_q = q0 + lax.broadcasted_iota(jnp.int32, (n_cp, TQ), 1)
    valid_c = lanes4((n_row * NSA_CMP_STRIDE + (NSA_CMP_LEN - 1) <= t_q) & (n_row < n_cmp))
    s = jnp.where(valid_c, s, NEG_INF)
    m = jnp.max(s, axis=0, keepdims=True)
    m = jnp.where(m == NEG_INF, 0.0, m)
    e = jnp.exp2(s - m)
    p_c = e * (1.0 / jnp.maximum(jnp.sum(e, axis=0, keepdims=True), jnp.finfo(F32).tiny))
    o_c = lax.dot_general(vc_ref[0, 0], p_c.astype(BF16), _TN, preferred_element_type=F32)

    p_sum = p_c[:, 0:TQ]
    for h in range(1, HPG):
        p_sum = p_sum + p_c[:, h * TQ:(h + 1) * TQ]
    om = lax.broadcasted_iota(jnp.int32, (n_slc, n_cp), 0)
    c_lo = lax.broadcasted_iota(jnp.int32, (n_slc, n_cp), 1) * NSA_CMP_STRIDE
    ov = jnp.maximum(jnp.minimum(c_lo + NSA_CMP_LEN, om * LS + LS) - jnp.maximum(c_lo, om * LS), 0)
    overlap_t = ov.astype(F32) / NSA_CMP_LEN
    p_hi, p_lo = _split_bf16(p_sum)
    ov_b = overlap_t.astype(BF16)
    imp_t = (jnp.dot(ov_b, p_hi, preferred_element_type=F32)
             + jnp.dot(ov_b, p_lo, preferred_element_type=F32))
    blk = lax.broadcasted_iota(jnp.int32, (n_slc, TQ), 0)
    tq_lane = q0 + lax.broadcasted_iota(jnp.int32, (n_slc, TQ), 1)
    tb = lax.shift_right_logical(tq_lane, LS.bit_length() - 1)
    forced = (blk == 0) | (blk == tb) | (blk == tb - 1)
    key = jnp.where(blk > tb, NEG_INF, jnp.where(forced, jnp.inf, imp_t))
    rank = jnp.zeros((n_slc, TQ), jnp.int32)
    for mm in range(n_slc):
        km = key[mm:mm + 1, :]
        beats = (km > key) | ((km == key) & (mm < blk))
        rank = rank + beats.astype(jnp.int32)
    sel_sc[...] = jnp.where((rank < n_top) & (blk <= tb), 1.0, 0.0)

    def toep_tile(k0, n_keys):
        return jnp.concatenate(
            [jnp.concatenate([tb_ref[h, :, pl.ds(_toep_col(q0, k0 + r * BIAS_ROWS), TQ)] for h in range(HPG)], axis=1)
             for r in range(n_keys // BIAS_ROWS)], axis=0)

    def pipelined_attention(n, scores_fn, post_fn, v_fn):
        s_bufs = (s0_sc, s1_sc)
        p_bufs = (p0_sc, p1_sc)
        s_bufs[0][...] = scores_fn(0)
        p_bufs[1][...] = jnp.zeros(p_bufs[1].shape, BF16)

        def half(i, cur, carry):
            m, l, acc, alpha_prev = carry
            s_bufs[1 - cur][...] = scores_fn(jnp.minimum(i + 1, n - 1))
            s = post_fn(i, s_bufs[cur][...])
            m_new = jnp.maximum(m, jnp.max(s, axis=0, keepdims=True))
            alpha = jnp.exp2(m - m_new)
            p = jnp.exp2(s - m_new)
            l_new = alpha * l + jnp.sum(p, axis=0, keepdims=True)
            p_bufs[cur][...] = p.astype(BF16)
            pv = lax.dot_general(v_fn(jnp.maximum(i - 1, 0)), p_bufs[1 - cur][...], _TN, preferred_element_type=F32)
            return m_new, l_new, alpha_prev * acc + pv, alpha

        def body(jj, carry):
            return half(2 * jj + 1, 1, half(2 * jj, 0, carry))

        n_pairs = (n + 1) // 2
        init = (jnp.full((1, R), NEG_INF, F32), jnp.zeros((1, R), F32), jnp.zeros((dh, R), F32),
                jnp.ones((1, R), F32))
        _, l, acc, alpha = lax.fori_loop(0, n_pairs, body, init)
        pv = lax.dot_general(v_fn(jnp.minimum(2 * n_pairs - 1, n - 1)), p_bufs[1][...], _TN,
                             preferred_element_type=F32)
        return (alpha * acc + pv) * (1.0 / l)

    krow_s = lax.broadcasted_iota(jnp.int32, (TK, TQ), 0)
    tq_s = q0 + lax.broadcasted_iota(jnp.int32, (TK, TQ), 1)
    n_kt = (q0 + TQ - 1) // TK + 1

    def slc_scores(kt):
        return lax.dot_general(ks_ref[0, 0, pl.ds(pl.multiple_of(kt * TK, TK), TK), :], q4, _NT,
                               preferred_element_type=F32)

    def slc_post(kt, s):
        live = kt < n_kt
        ktc = jnp.minimum(kt, n_kt - 1)
        k0 = pl.multiple_of(ktc * TK, TK)
        flags = jnp.concatenate(
            [jnp.broadcast_to(sel_sc[pl.ds(ktc * (TK // LS) + j, 1), :], (LS, TQ)) for j in range(TK // LS)], axis=0)
        mask = lanes4((flags > jnp.where(live, 0.5, 2.0)) & (k0 + krow_s <= tq_s))
        return jnp.where(mask, s + toep_tile(k0, TK), NEG_INF)

    o_s = pipelined_attention(n_kt, slc_scores, slc_post,
                              lambda kt: vs_ref[0, 0, pl.ds(pl.multiple_of(kt * TK, TK), TK), :])

    krow_w = lax.broadcasted_iota(jnp.int32, (TQ, TQ), 0)
    tq_w = q0 + lax.broadcasted_iota(jnp.int32, (TQ, TQ), 1)
    n_wt = jnp.minimum(qi, NSA_WINDOW // TQ) + 1

    def win_body(j, carry):
        k0 = pl.multiple_of(q0 - j * TQ, TQ)
        dist = tq_w - (k0 + krow_w)
        mask = lanes4((dist >= 0) & (dist < NSA_WINDOW))
        s = lax.dot_general(kw_ref[0, 0, pl.ds(k0, TQ), :], q4, _NT, preferred_element_type=F32) + toep_tile(k0, TQ)
        return _softmax_step(carry, jnp.where(mask, s, NEG_INF), vw_ref[0, 0, pl.ds(k0, TQ), :])

    init_w = (jnp.full((1, R), NEG_INF, F32), jnp.zeros((1, R), F32), jnp.zeros((dh, R), F32))
    _, l_w, acc_w = lax.fori_loop(0, n_wt, win_body, init_w)
    o_w = acc_w * (1.0 / l_w)

    g_t = jax.nn.sigmoid(gt_ref[0, 0]).T
    outs = []
    for h in range(HPG):
        cols = slice(h * TQ, (h + 1) * TQ)
        o = (g_t[h:h + 1] * o_c[:, cols] + g_t[HPG + h:HPG + h + 1] * o_s[:, cols]
             + g_t[2 * HPG + h:2 * HPG + h + 1] * o_w[:, cols])
        outs.append(o.T)
    o_ref[0] = jnp.concatenate(outs, axis=1).astype(o_ref.dtype)


def nsa_attention(q, kc, vc, ks, vs, kw, vw, gates, cmpb, toep, TQ=256):
    B, S, _ = q.shape
    G, HPG, dh = NSA_KV_GROUPS, NSA_HPG, NSA_HEAD_DIM
    n_cp = S // NSA_CMP_STRIDE
    full_kv = pl.BlockSpec((1, 1, S, dh), lambda b, g, i: (b, g, 0, 0))
    cmp_kv = pl.BlockSpec((1, 1, n_cp, dh), lambda b, g, i: (b, g, 0, 0))
    return pl.pallas_call(
        functools.partial(_nsa_kernel, S=S, TQ=TQ),
        grid=(B, G, S // TQ),
        in_specs=[
            pl.BlockSpec((1, TQ, HPG * dh), lambda b, g, i: (b, i, g)),
            cmp_kv, cmp_kv, full_kv, full_kv, full_kv, full_kv,
            pl.BlockSpec((1, 1, TQ, LANE), lambda b, g, i: (b, g, i, 0)),
            pl.BlockSpec((HPG, n_cp, TQ), lambda b, g, i: (g, 0, i)),
            pl.BlockSpec((HPG, BIAS_ROWS, _toep_width(S)), lambda b, g, i: (g, 0, 0)),
        ],
        out_specs=pl.BlockSpec((1, TQ, HPG * dh), lambda b, g, i: (b, i, g)),
        out_shape=jax.ShapeDtypeStruct((B, S, NSA_Q_WIDTH), BF16),
        scratch_shapes=[pltpu.VMEM((S // NSA_SLC_BLOCK, TQ), F32)]
        + [pltpu.VMEM((TQ, HPG * TQ), F32)] * 2 + [pltpu.VMEM((TQ, HPG * TQ), BF16)] * 2,
        compiler_params=_cparams(("arbitrary", "arbitrary", "arbitrary"), 48),
        name="nsa_attn",
    )(q, kc, vc, ks, vs, kw, vw, gates, cmpb, toep)


def _router_kernel(h_ref, wr_ref, e_ref, w_ref, cnt_ref):
    @pl.when(pl.program_id(0) == 0)
    def _():
        cnt_ref[...] = jnp.zeros(cnt_ref.shape, F32)

    h_hi, h_lo = _split_bf16(h_ref[...])
    w_hi, w_lo = _split_bf16(wr_ref[...])
    logits = (jnp.dot(h_hi, w_hi, preferred_element_type=F32) + jnp.dot(h_hi, w_lo, preferred_element_type=F32)
              + jnp.dot(h_lo, w_hi, preferred_element_type=F32))
    tm = logits.shape[0]
    lane = lax.broadcasted_iota(jnp.int32, logits.shape, 1)
    logits = jnp.where(lane < N_EXPERTS, logits, NEG_INF)
    l1 = jnp.max(logits, axis=-1, keepdims=True)
    i1 = jnp.min(jnp.where(logits == l1, lane, LANE), axis=-1, keepdims=True)
    rest = jnp.where(lane == i1, NEG_INF, logits)
    l2 = jnp.max(rest, axis=-1, keepdims=True)
    i2 = jnp.min(jnp.where(rest == l2, lane, LANE), axis=-1, keepdims=True)
    e2 = jnp.exp(l2 - l1)
    den = 1.0 + e2
    w_ref[...] = jnp.where(lane == 0, 1.0 / den, jnp.where(lane == 1, e2 / den, 0.0))

    oh1 = lane == i1
    oh2 = lane == i2
    both = jnp.where(oh1 | oh2, 1.0, 0.0)
    earlier = lax.broadcasted_iota(jnp.int32, (tm, tm), 1) < lax.broadcasted_iota(jnp.int32, (tm, tm), 0)
    prefix = jnp.dot(jnp.where(earlier, 1.0, 0.0).astype(BF16), both.astype(BF16),
                     preferred_element_type=F32)
    seen = prefix + cnt_ref[0:1, :]
    p1 = jnp.sum(jnp.where(oh1, seen, 0.0), axis=-1, keepdims=True).astype(jnp.int32)
    p2 = jnp.sum(jnp.where(oh2, seen, 0.0), axis=-1, keepdims=True).astype(jnp.int32)
    e_ref[...] = jnp.where(lane == 0, i1, jnp.where(lane == 1, i2, jnp.where(lane == 2, p1,
                                                                              jnp.where(lane == 3, p2, 0))))
    cnt_ref[...] = cnt_ref[...] + jnp.sum(both, axis=0, keepdims=True)


def moe_route(h2, wr_pad, tm=512):
    N, D = h2.shape
    return pl.pallas_call(
        _router_kernel,
        grid=(N // tm,),
        in_specs=[pl.BlockSpec((tm, D), lambda i: (i, 0)), pl.BlockSpec(wr_pad.shape, lambda i: (0, 0))],
        out_specs=[pl.BlockSpec((tm, LANE), lambda i: (i, 0)), pl.BlockSpec((tm, LANE), lambda i: (i, 0)),
                   pl.BlockSpec((8, LANE), lambda i: (0, 0))],
        out_shape=[jax.ShapeDtypeStruct((N, LANE), jnp.int32), jax.ShapeDtypeStruct((N, LANE), F32),
                   jax.ShapeDtypeStruct((8, LANE), F32)],
        compiler_params=_cparams(("arbitrary",)),
        name="moe_router",
    )(h2, wr_pad)


def _dispatch_kernel(dest_ref, h_ref, xs_in_ref, xs_ref, sem):
    del xs_in_ref
    tm = h_ref.shape[0]
    base = pl.program_id(0) * tm * TOP_K

    def issue(r, carry):
        for k in range(TOP_K):
            d = dest_ref[base + r * TOP_K + k]
            pltpu.make_async_copy(h_ref.at[r], xs_ref.at[d], sem).start(priority=k)
        return carry

    lax.fori_loop(0, tm, issue, 0, unroll=16)
    for k in range(TOP_K):
        pltpu.make_async_copy(h_ref, xs_ref.at[pl.ds(0, tm)], sem).wait()


def moe_dispatch_rows(h2, dest_flat, n_rows, recycled=None, tm=512):
    N, D = h2.shape
    xs0 = jnp.zeros((n_rows, D), h2.dtype) if recycled is None else recycled
    grid_spec = pltpu.PrefetchScalarGridSpec(
        num_scalar_prefetch=1,
        grid=(N // tm,),
        in_specs=[pl.BlockSpec((tm, D), lambda i, d: (i, 0)), pl.BlockSpec(memory_space=pl.ANY)],
        out_specs=pl.BlockSpec(memory_space=pl.ANY),
        scratch_shapes=[pltpu.SemaphoreType.DMA],
    )
    return pl.pallas_call(
        _dispatch_kernel,
        grid_spec=grid_spec,
        out_shape=jax.ShapeDtypeStruct((n_rows, D), h2.dtype),
        input_output_aliases={2: 0},
        compiler_params=_cparams(("arbitrary",)),
        name="moe_dispatch",
    )(dest_flat, h2, xs0)


def _combine_ln_kernel(dest_ref, w_ref, ys_ref, x_ref, g_ref, lng_ref, lnb_ref, *rest, has_next):
    if has_next:
        sc_ref, sh_ref, o_ref, hn_ref, buf, sem = rest
    else:
        o_ref, buf, sem = rest
    tm = x_ref.shape[0]
    base = pl.program_id(0) * tm * TOP_K

    def issue(r, carry):
        for k in range(TOP_K):
            d = dest_ref[base + r * TOP_K + k]
            pltpu.make_async_copy(ys_ref.at[d], buf.at[k, r], sem).start(priority=k)
        return carry

    lax.fori_loop(0, tm, issue, 0, unroll=16)
    for k in range(TOP_K):
        pltpu.make_async_copy(ys_ref.at[pl.ds(0, tm)], buf.at[k], sem).wait()
    w = w_ref[...]
    y = w[:, 0:1] * buf[0] + w[:, 1:2] * buf[1]
    xn = _deepnorm_ln(x_ref[...], y, g_ref[0], lng_ref[...], lnb_ref[...])
    o_ref[...] = xn
    if has_next:
        hn_ref[...] = (xn * (1.0 + sc_ref[0]) + sh_ref[0]).astype(hn_ref.dtype)


def moe_combine_ln(ys, dest_flat, w_top, x, gate, ln_g, ln_b, nxt, tm=512):
    B, S, D = x.shape
    N = B * S
    per_b = S // tm
    tile = pl.BlockSpec((tm, D), lambda i, d: (i, 0))
    vec = pl.BlockSpec((1, 1, D), lambda i, d: (i // per_b, 0, 0))
    par = pl.BlockSpec((1, D), lambda i, d: (0, 0))
    in_specs = [pl.BlockSpec((tm, LANE), lambda i, d: (i, 0)), pl.BlockSpec(memory_space=pl.ANY), tile, vec, par, par]
    args = [w_top, ys, x.reshape(N, D), gate, ln_g, ln_b]
    out_shape = [jax.ShapeDtypeStruct((N, D), F32)]
    out_specs = [tile]
    if nxt is not None:
        in_specs += [vec, vec]
        args += list(nxt)
        out_shape.append(jax.ShapeDtypeStruct((N, D), BF16))
        out_specs.append(tile)
    grid_spec = pltpu.PrefetchScalarGridSpec(
        num_scalar_prefetch=1,
        grid=(N // tm,),
        in_specs=in_specs,
        out_specs=out_specs,
        scratch_shapes=[pltpu.VMEM((TOP_K, tm, D), F32), pltpu.SemaphoreType.DMA],
    )
    res = pl.pallas_call(
        functools.partial(_combine_ln_kernel, has_next=nxt is not None),
        grid_spec=grid_spec,
        out_shape=out_shape,
        compiler_params=_cparams(("arbitrary",)),
        name="moe_combine_ln",
    )(dest_flat, *args)
    x_new = res[0].reshape(B, S, D)
    return (x_new, res[1].reshape(B, S, D)) if nxt is not None else (x_new, None)


def _moe_ffn_kernel(be_ref, nu_ref, x_ref, w1_ref, w3_ref, w2_ref, o_ref, acc_sc, xb_sc):
    i = pl.program_id(0)
    j = pl.program_id(1)
    last = pl.num_programs(1) - 1
    used = i < nu_ref[0]

    @pl.when(j == 0)
    def _():
        acc_sc[...] = jnp.zeros(acc_sc.shape, F32)
        xb_sc[...] = x_ref[...].astype(BF16)

    @pl.when(used)
    def _():
        x = xb_sc[...]
        a = jnp.dot(x, w1_ref[0].astype(BF16), preferred_element_type=F32)
        b = jnp.dot(x, w3_ref[0].astype(BF16), preferred_element_type=F32)
        u = (a * jax.nn.sigmoid(a)) * b
        acc_sc[...] += jnp.dot(u.astype(BF16), w2_ref[0].astype(BF16), preferred_element_type=F32)

    @pl.when(j == last)
    def _():
        o_ref[...] = acc_sc[...]


def moe_ffn(xs, blk_e, n_used, w1, w3, w2, tf=512):
    n_rows, D = xs.shape
    n_blocks = blk_e.shape[0]
    RB = n_rows // n_blocks
    F = w1.shape[2]
    nf = F // tf

    def wcol(i, j, be, nu):
        return (be[i], 0, jnp.where(i < nu[0], j, nf - 1))

    def wrow(i, j, be, nu):
        return (be[i], jnp.where(i < nu[0], j, nf - 1), 0)

    grid_spec = pltpu.PrefetchScalarGridSpec(
        num_scalar_prefetch=2,
        grid=(n_blocks, nf),
        in_specs=[
            pl.BlockSpec((RB, D), lambda i, j, be, nu: (i, 0)),
            pl.BlockSpec((1, D, tf), wcol),
            pl.BlockSpec((1, D, tf), wcol),
            pl.BlockSpec((1, tf, D), wrow),
        ],
        out_specs=pl.BlockSpec((RB, D), lambda i, j, be, nu: (i, 0)),
        scratch_shapes=[pltpu.VMEM((RB, D), F32), pltpu.VMEM((RB, D), BF16)],
    )
    return pl.pallas_call(
        _moe_ffn_kernel,
        grid_spec=grid_spec,
        out_shape=jax.ShapeDtypeStruct((n_rows, D), F32),
        compiler_params=_cparams(("arbitrary", "arbitrary"), 56),
        name="moe_ffn",
    )(blk_e, n_used, xs, w1, w3, w2)


def moe_layout(e_pos, counts_f, n_tok):
    RB = _moe_row_block(n_tok)
    n_assign = n_tok * TOP_K
    n_blocks = (n_assign + N_EXPERTS * (RB - 1) + RB - 1) // RB
    counts = counts_f[0, :N_EXPERTS].astype(jnp.int32)
    padded = (counts + RB - 1) // RB * RB
    pad_end = jnp.cumsum(padded)
    pad_start = pad_end - padded
    e = e_pos[:, 0:TOP_K]
    pos = e_pos[:, TOP_K:2 * TOP_K]
    start = jnp.zeros_like(e)
    for x in range(N_EXPERTS):
        start = jnp.where(e == x, pad_start[x], start)
    dest_flat = (start + pos).reshape(-1)
    blk_start = jnp.arange(n_blocks, dtype=jnp.int32) * RB
    blk_e = jnp.minimum(jnp.sum((blk_start[:, None] >= pad_end[None, :]).astype(jnp.int32), axis=1), N_EXPERTS - 1)
    n_used = (pad_end[-1] // RB).astype(jnp.int32).reshape(1)
    return dest_flat, blk_e.astype(jnp.int32), n_used, n_blocks * RB


def _pad_cols(w, width):
    return jnp.pad(w, ((0, 0), (0, width - w.shape[1])))


def kernel(x, c, rel_bias, ada_w, ada_b, ln_g, ln_b, ab_w_in, gla_wg2, gla_bg2, gla_norm_g, ab_w_out,
           ffn_w1, ffn_w3, ffn_w2, nsa_w_in, nsa_cmp_pos, nsa_cmp_w1k, nsa_cmp_w2k, nsa_cmp_w1v, nsa_cmp_w2v,
           nsa_w_out, moe_router, moe_w1, moe_w3, moe_w2):
    B, S, D = x.shape
    n_tok = B * S
    mod = adaln_mod(c, ada_w, ada_b)
    mods = [[m.reshape(B, 1, D) for m in jnp.split(mod[i], 6, axis=-1)] for i in range(DEPTH)]
    toep, cmpb = build_bias_tables(rel_bias, S)

    xs_dead = None
    sh1, sc1 = mods[0][0], mods[0][1]
    h = modulate(x, sc1, sh1)
    for i in range(DEPTH):
        j = i // 2
        _, _, g1, sh2, sc2, g2 = mods[i]
        nxt2 = (mods[i + 1][1], mods[i + 1][0]) if i + 1 < DEPTH else None
        lng = ln_g[i].reshape(2, 1, D)
        lnb = ln_b[i].reshape(2, 1, D)
        if i % 2 == 0:
            w = ab_w_in[j]
            cuts = np.cumsum((MOBA_WIDTH, MOBA_WIDTH, MOBA_WIDTH, GLA_QK_WIDTH, GLA_QK_WIDTH, GLA_V_WIDTH,
                              GLA_V_WIDTH))
            parts = jnp.split(w, cuts.tolist(), axis=1)
            parts[-1] = _pad_cols(parts[-1], LANE)
            w_cat = jnp.concatenate(parts, axis=1).astype(BF16)
            widths = [p.shape[1] for p in parts]
            offs = np.concatenate([[0], np.cumsum(widths)[:-1]]).tolist()
            scales = [MOBA_HEAD_DIM ** -0.5 * LOG2E] + [1.0] * 7
            kinds = ["flat", "flat", "flat_t"] + ["flat"] * 5
            offs[2] = 0
            segs = tuple((o, wd, kd, s) for o, wd, kd, s in zip(offs, widths, kinds, scales))
            dts = [BF16, BF16, BF16, F32, F32, F32, F32, F32]
            mq, mk, mv, gq, gk, gv, gr, glr = project(h, w_cat, segs, dts, wt=parts[2].T.astype(BF16))
            y_moba = moba_attention(mq, mk, mv, toep)
            wg2_pad = jnp.pad(gla_wg2[j], ((0, LANE - GLA_GATE_RANK), (0, 0)))
            y_gla = gla_mixer(gq, gk, gv, gr, glr, wg2_pad, gla_bg2[j].reshape(1, -1), gla_norm_g[j].reshape(1, -1))
            w_out = ab_w_out[j].astype(BF16)
            x, h = out_ln([y_moba, y_gla], [w_out[:MOBA_WIDTH], w_out[MOBA_WIDTH:]], x, g1, lng[0], lnb[0],
                          (sc2, sh2))
            x, h = ffn_ln(h, ffn_w1[j].astype(BF16), ffn_w3[j].astype(BF16), ffn_w2[j].astype(BF16), x, g2,
                          lng[1], lnb[1], nxt2)
        else:
            G, HPG, dh = NSA_KV_GROUPS, NSA_HPG, NSA_HEAD_DIM
            w = nsa_w_in[j]
            wq = w[:, :NSA_Q_WIDTH]
            wkv = w[:, NSA_Q_WIDTH:NSA_Q_WIDTH + 6 * NSA_KV_WIDTH]
            wg = w[:, NSA_Q_WIDTH + 6 * NSA_KV_WIDTH:].reshape(D, G, HPG, 3)
            wg = _pad_cols(wg.transpose(0, 1, 3, 2).reshape(D * G, 3 * HPG), LANE).reshape(D, G * LANE)
            w_cat = jnp.concatenate([wq, wkv, wg], axis=1).astype(BF16)
            segs = [(0, NSA_Q_WIDTH, "flat", dh ** -0.5 * LOG2E)]
            segs += [(NSA_Q_WIDTH + n * NSA_KV_WIDTH, NSA_KV_WIDTH, "group", 1.0) for n in range(6)]
            segs += [(NSA_Q_WIDTH + 6 * NSA_KV_WIDTH, G * LANE, "group", 1.0)]
            dts = [BF16] * 7 + [F32]
            q, kc, vc, ks, vs, kw, vw, gates = project(h, w_cat, tuple(segs), dts)
            half = NSA_CMP_STRIDE * dh
            pos2 = jnp.pad(nsa_cmp_pos[j].reshape(2, half), ((0, 6), (0, 0))).astype(BF16)
            w1k = jnp.concatenate([nsa_cmp_w1k[j][:half], nsa_cmp_w1k[j][half:]], axis=1).astype(BF16)
            w1v = jnp.concatenate([nsa_cmp_w1v[j][:half], nsa_cmp_w1v[j][half:]], axis=1).astype(BF16)
            k_cmp = nsa_compress(kc, pos2, w1k, nsa_cmp_w2k[j].astype(BF16))
            v_cmp = nsa_compress(vc, pos2, w1v, nsa_cmp_w2v[j].astype(BF16))
            y = nsa_attention(q, k_cmp, v_cmp, ks, vs, kw, vw, gates, cmpb, toep)
            x, h = out_ln([y], [nsa_w_out[j].astype(BF16)], x, g1, lng[0], lnb[0], (sc2, sh2), h_dtype=F32)
            h2 = h.reshape(n_tok, D)
            e_pos, w_top, counts = moe_route(h2, _pad_cols(moe_router[j], LANE))
            dest_flat, blk_e, n_used, n_rows = moe_layout(e_pos, counts, n_tok)
            xs = moe_dispatch_rows(h2, dest_flat, n_rows, recycled=xs_dead)
            xs_dead = xs
            stack = lambda w: w.reshape((-1,) + w.shape[2:])
            ys = moe_ffn(xs, blk_e + j * N_EXPERTS, n_used, stack(moe_w1), stack(moe_w3), stack(moe_w2))
            x, h = moe_combine_ln(ys, dest_flat, w_top, x, g2, lng[1], lnb[1], nxt2)
    return x
```

```python
import functools
import math

import jax
import jax.numpy as jnp
import numpy as np
from jax import lax
from jax.experimental import pallas as pl
from jax.experimental.pallas import tpu as pltpu

F32 = jnp.float32
BF16 = jnp.bfloat16
NEG_INF = float("-inf")

DEPTH = 4
ALPHA_DN = (2 * DEPTH) ** 0.25
LN_EPS = 1e-5
RMS_EPS = 1e-6

N_BUCKETS = 32
MAX_DISTANCE = 1024

MOBA_HEADS = 8
MOBA_HEAD_DIM = 64
MOBA_BLOCK = 256
MOBA_TOPK = 3
MOBA_WIDTH = MOBA_HEADS * MOBA_HEAD_DIM

GLA_HEADS = 4
GLA_DK = 64
GLA_DV = 128
GLA_GATE_RANK = 16
GLA_TAU = 16.0
GLA_CHUNK = 64
GLA_QK_WIDTH = GLA_HEADS * GLA_DK
GLA_V_WIDTH = GLA_HEADS * GLA_DV

NSA_HEADS = 8
NSA_KV_GROUPS = 2
NSA_HPG = NSA_HEADS // NSA_KV_GROUPS
NSA_HEAD_DIM = 128
NSA_CMP_LEN = 32
NSA_CMP_STRIDE = 16
NSA_SLC_BLOCK = 64
NSA_SLC_TOPN = 16
NSA_WINDOW = 512
NSA_Q_WIDTH = NSA_HEADS * NSA_HEAD_DIM
NSA_KV_WIDTH = NSA_KV_GROUPS * NSA_HEAD_DIM

N_EXPERTS = 8
TOP_K = 2
MOE_BLOCKS_PER_EXPERT = 4


def _moe_row_block(n_tok):
    balanced = n_tok * TOP_K // N_EXPERTS
    rows = balanced // MOE_BLOCKS_PER_EXPERT + balanced // 128
    return -(-rows // 16) * 16

LOG2E = math.log2(math.e)
LANE = 128
BIAS_ROWS = 128


def _split_bf16(a):
    hi = a.astype(BF16)
    return hi, (a - hi.astype(F32)).astype(BF16)


def _cparams(sem, vmem_mib=40, **extra):
    return pltpu.CompilerParams(dimension_semantics=sem, vmem_limit_bytes=vmem_mib << 20, **extra)


def _adaln_kernel(c_ref, w_ref, b_ref, o_ref):
    c = c_ref[...]
    cond = c * jax.nn.sigmoid(c)
    o_ref[0] = jnp.dot(cond, w_ref[0], preferred_element_type=F32) + b_ref[0]


def adaln_mod(c, ada_w, ada_b):
    B, D = c.shape
    L, _, N = ada_w.shape
    rows = 8
    cp = jnp.zeros((rows, D), F32).at[:B].set(c)
    tn = N // 4
    out = pl.pallas_call(
        _adaln_kernel,
        grid=(L, N // tn),
        in_specs=[
            pl.BlockSpec((rows, D), lambda l, j: (0, 0)),
            pl.BlockSpec((1, D, tn), lambda l, j: (l, 0, j)),
            pl.BlockSpec((1, 1, tn), lambda l, j: (l, 0, j)),
        ],
        out_specs=pl.BlockSpec((1, rows, tn), lambda l, j: (l, 0, j)),
        out_shape=jax.ShapeDtypeStruct((L, rows, N), F32),
        compiler_params=_cparams(("arbitrary", "arbitrary")),
        name="adaln_mod",
    )(cp, ada_w, ada_b.reshape(L, 1, N))
    return out[:, :B]


def _modulate_kernel(x_ref, sc_ref, sh_ref, o_ref):
    o_ref[0] = (x_ref[0] * (1.0 + sc_ref[0]) + sh_ref[0]).astype(o_ref.dtype)


def modulate(x, sc, sh, tm=512):
    B, S, D = x.shape
    return pl.pallas_call(
        _modulate_kernel,
        grid=(B, S // tm),
        in_specs=[
            pl.BlockSpec((1, tm, D), lambda b, i: (b, i, 0)),
            pl.BlockSpec((1, 1, D), lambda b, i: (b, 0, 0)),
            pl.BlockSpec((1, 1, D), lambda b, i: (b, 0, 0)),
        ],
        out_specs=pl.BlockSpec((1, tm, D), lambda b, i: (b, i, 0)),
        out_shape=jax.ShapeDtypeStruct((B, S, D), BF16),
        compiler_params=_cparams(("arbitrary", "arbitrary")),
        name="modulate",
    )(x, sc, sh)


def _t5_bucket(dist):
    max_exact = N_BUCKETS // 2
    n = jnp.maximum(dist, 0)
    nf = jnp.maximum(n, 1).astype(jnp.float32)
    large = max_exact + (jnp.log(nf / max_exact) / math.log(MAX_DISTANCE / max_exact) * (N_BUCKETS - max_exact)).astype(jnp.int32)
    large = jnp.minimum(large, N_BUCKETS - 1)
    return jnp.where(n < max_exact, n, large)


def _bias_table_kernel(rb_ref, bm_ref, o_ref, *, rows_per_step):
    h = pl.program_id(0)
    n_steps = bm_ref.shape[0] // rows_per_step

    def body(i, carry):
        r0 = pl.multiple_of(i * rows_per_step, rows_per_step)
        bm = bm_ref[pl.ds(r0, rows_per_step), :]
        acc = jnp.zeros(bm.shape, F32)
        for b in range(N_BUCKETS):
            acc = jnp.where(bm == b, rb_ref[b, h] * LOG2E, acc)
        o_ref[0, pl.ds(r0, rows_per_step), :] = acc
        return carry

    lax.fori_loop(0, n_steps, body, 0)


def bias_table(rel_bias, bucket_map, rows_per_step):
    R, W = bucket_map.shape
    H = rel_bias.shape[1]
    return pl.pallas_call(
        functools.partial(_bias_table_kernel, rows_per_step=rows_per_step),
        grid=(H,),
        in_specs=[
            pl.BlockSpec(memory_space=pltpu.SMEM),
            pl.BlockSpec((R, W), lambda h: (0, 0)),
        ],
        out_specs=pl.BlockSpec((1, R, W), lambda h: (h, 0, 0)),
        out_shape=jax.ShapeDtypeStruct((H, R, W), F32),
        compiler_params=_cparams(("arbitrary",)),
        name="bias_table",
    )(rel_bias, bucket_map)


def _toep_width(S):
    return S + BIAS_ROWS


def _toep_col(q0, k0):
    return pl.multiple_of(q0 - k0 + BIAS_ROWS, LANE)


def build_bias_tables(rel_bias, S):
    j = jnp.arange(BIAS_ROWS, dtype=jnp.int32)[:, None]
    c = jnp.arange(_toep_width(S), dtype=jnp.int32)[None, :]
    toep_map = _t5_bucket(c - j - BIAS_ROWS)
    n_cmp_pad = S // NSA_CMP_STRIDE
    t = jnp.arange(S, dtype=jnp.int32)[None, :]
    cmp_end = jnp.arange(n_cmp_pad, dtype=jnp.int32)[:, None] * NSA_CMP_STRIDE + NSA_CMP_LEN - 1
    cmp_map = _t5_bucket(t - cmp_end)
    return bias_table(rel_bias, toep_map, 8), bias_table(rel_bias, cmp_map, 8)


def _proj_kernel(h_ref, w_ref, wt_ref, *o_refs, segs):
    h = h_ref[0]
    for (off, width, kind, scale), o_ref in zip(segs, o_refs):
        if kind == "flat_t":
            acc = lax.dot_general(wt_ref[off:off + width, :], h, _NT, preferred_element_type=F32)
        else:
            acc = jnp.dot(h, w_ref[:, off:off + width], preferred_element_type=F32)
        if scale != 1.0:
            acc = acc * scale
        if kind in ("flat", "flat_t"):
            o_ref[0] = acc.astype(o_ref.dtype)
        else:
            for g in range(width // LANE):
                o_ref[0, g] = acc[:, g * LANE:(g + 1) * LANE].astype(o_ref.dtype)


def project(h, w_cat, segs, out_dtypes, wt=None, tm=512):
    B, S, D = h.shape
    if wt is None:
        wt = jnp.zeros((8, D), w_cat.dtype)
    out_shapes, out_specs = [], []
    for (off, width, kind, scale), dt in zip(segs, out_dtypes):
        if kind == "flat":
            out_shapes.append(jax.ShapeDtypeStruct((B, S, width), dt))
            out_specs.append(pl.BlockSpec((1, tm, width), lambda b, i: (b, i, 0)))
        elif kind == "flat_t":
            out_shapes.append(jax.ShapeDtypeStruct((B, width, S), dt))
            out_specs.append(pl.BlockSpec((1, width, tm), lambda b, i: (b, 0, i)))
        else:
            G = width // LANE
            out_shapes.append(jax.ShapeDtypeStruct((B, G, S, LANE), dt))
            out_specs.append(pl.BlockSpec((1, G, tm, LANE), lambda b, i: (b, 0, i, 0)))
    return pl.pallas_call(
        functools.partial(_proj_kernel, segs=segs),
        grid=(B, S // tm),
        in_specs=[
            pl.BlockSpec((1, tm, D), lambda b, i: (b, i, 0)),
            pl.BlockSpec(w_cat.shape, lambda b, i: (0, 0)),
            pl.BlockSpec(wt.shape, lambda b, i: (0, 0)),
        ],
        out_specs=out_specs,
        out_shape=out_shapes,
        compiler_params=_cparams(("arbitrary", "arbitrary"), 48),
        name="in_proj",
    )(h, w_cat, wt)


_NT = (((1,), (1,)), ((), ()))
_TN = (((0,), (0,)), ((), ()))


def _softmax_step(carry, s, v_tile):
    m, l, acc = carry
    m_new = jnp.maximum(m, jnp.max(s, axis=0, keepdims=True))
    alpha = jnp.exp2(m - m_new)
    p = jnp.exp2(s - m_new)
    l_new = alpha * l + jnp.sum(p, axis=0, keepdims=True)
    acc_new = alpha * acc + lax.dot_general(v_tile, p.astype(BF16), _TN, preferred_element_type=F32)
    return m_new, l_new, acc_new


def _moba_kernel(q_ref, k_ref, v_ref, tb_ref, o_ref, kmean_sc, sel_sc, s0_sc, s1_sc, p0_sc, p1_sc,
                 *, S, heads_per_step):
    L = MOBA_BLOCK
    dh = MOBA_HEAD_DIM
    nb = S // L
    qi = pl.program_id(2)
    q0 = qi * L

    @pl.when(qi == 0)
    def _():
        kf = k_ref[0].astype(F32)
        kmean_sc[...] = jnp.mean(kf.reshape(nb, L, kf.shape[-1]), axis=1)

    q2 = q_ref[0]
    lane_q = lax.broadcasted_iota(jnp.int32, q2.shape, 1)
    lane_m = lax.broadcasted_iota(jnp.int32, kmean_sc.shape, 1)
    blk = lax.broadcasted_iota(jnp.int32, (nb, L), 0)
    qz = []
    for hh in range(heads_per_step):
        in_head = (lane_q >= hh * dh) & (lane_q < (hh + 1) * dh)
        qz.append(jnp.where(in_head, q2, jnp.zeros_like(q2)))
        km = jnp.where((lane_m >= hh * dh) & (lane_m < (hh + 1) * dh), kmean_sc[...], 0.0)
        km_hi, km_lo = _split_bf16(km)
        gate = (lax.dot_general(km_hi, q2, _NT, preferred_element_type=F32)
                + lax.dot_general(km_lo, q2, _NT, preferred_element_type=F32))
        valid = blk < qi
        gm = jnp.where(valid, gate, NEG_INF)
        rank = jnp.zeros((nb, L), jnp.int32)
        for m in range(nb):
            gmm = gm[m:m + 1, :]
            beats = (gmm > gm) | ((gmm == gm) & (m < blk))
            rank = rank + beats.astype(jnp.int32)
        sel_sc[hh] = jnp.where(valid & (rank < MOBA_TOPK), 1.0, 0.0)

    def bias_tile(hh, k0, n_keys):
        return jnp.concatenate([tb_ref[hh, :, pl.ds(_toep_col(q0, k0 + r * BIAS_ROWS), L)]
                                for r in range(n_keys // BIAS_ROWS)], axis=0)

    krow = lax.broadcasted_iota(jnp.int32, (L, L), 0)
    qcol = lax.broadcasted_iota(jnp.int32, (L, L), 1)
    k_own = k_ref[0, pl.ds(pl.multiple_of(q0, L), L), :]
    v_tile = lambda hh, k0, n_keys: v_ref[0, hh * dh:(hh + 1) * dh, pl.ds(k0, n_keys)]
    carry = []
    for hh in range(heads_per_step):
        s = lax.dot_general(k_own, qz[hh], _NT, preferred_element_type=F32) + bias_tile(hh, q0, L)
        s = jnp.where(krow <= qcol, s, NEG_INF)
        m0 = jnp.max(s, axis=0, keepdims=True)
        p = jnp.exp2(s - m0)
        l0 = jnp.sum(p, axis=0, keepdims=True)
        acc0 = jnp.dot(v_tile(hh, pl.multiple_of(q0, L), L), p.astype(BF16), preferred_element_type=F32)
        carry += [m0, l0, acc0, jnp.ones((1, L), F32)]

    n = (qi + 1) // 2
    s_bufs = (s0_sc, s1_sc)
    p_bufs = (p0_sc, p1_sc)
    tile_k0 = lambda i: pl.multiple_of(jnp.clip(i, 0, jnp.maximum(n - 1, 0)) * 2 * L, 2 * L)

    def put_scores(i, buf):
        kt = k_ref[0, pl.ds(tile_k0(i), 2 * L), :]
        for hh in range(heads_per_step):
            buf[hh] = lax.dot_general(kt, qz[hh], _NT, preferred_element_type=F32)

    put_scores(0, s_bufs[0])
    p_bufs[1][...] = jnp.zeros(p_bufs[1].shape, BF16)

    def half(i, cur, carry):
        put_scores(i + 1, s_bufs[1 - cur])
        k0 = tile_k0(i)
        blk0 = lax.shift_right_logical(k0, L.bit_length() - 1)
        thresh = jnp.where(i < n, 0.5, 2.0)
        out = []
        for hh in range(heads_per_step):
            m, l, acc, alpha_prev = carry[4 * hh:4 * hh + 4]
            flags = jnp.concatenate(
                [jnp.broadcast_to(sel_sc[hh, pl.ds(blk0 + j, 1), :], (L, L)) for j in range(2)], axis=0)
            s = jnp.where(flags > thresh, s_bufs[cur][hh] + bias_tile(hh, k0, 2 * L), NEG_INF)
            m_new = jnp.maximum(m, jnp.max(s, axis=0, keepdims=True))
            alpha = jnp.exp2(m - m_new)
            p = jnp.exp2(s - m_new)
            l_new = alpha * l + jnp.sum(p, axis=0, keepdims=True)
            p_bufs[cur][hh] = p.astype(BF16)
            pv = jnp.dot(v_tile(hh, tile_k0(i - 1), 2 * L), p_bufs[1 - cur][hh], preferred_element_type=F32)
            out += [m_new, l_new, alpha_prev * acc + pv, alpha]
        return tuple(out)

    def body(jj, carry):
        return half(2 * jj + 1, 1, half(2 * jj, 0, carry))

    n_pairs = (n + 1) // 2
    carry = lax.fori_loop(0, n_pairs, body, tuple(carry))
    outs = []
    for hh in range(heads_per_step):
        _, l, acc, alpha = carry[4 * hh:4 * hh + 4]
        pv = jnp.dot(v_tile(hh, tile_k0(2 * n_pairs - 1), 2 * L), p_bufs[1][hh], preferred_element_type=F32)
        outs.append((alpha * acc + pv) * (1.0 / l))
    o_ref[0] = jnp.concatenate(outs, axis=0).T.astype(o_ref.dtype)


def moba_attention(q, k, v, toep):
    B, S, _ = q.shape
    L = MOBA_BLOCK
    hps = 4
    wl = hps * MOBA_HEAD_DIM
    n_hp = MOBA_HEADS // hps
    nb = S // L
    return pl.pallas_call(
        functools.partial(_moba_kernel, S=S, heads_per_step=hps),
        grid=(n_hp, B, nb),
        in_specs=[
            pl.BlockSpec((1, L, wl), lambda hp, b, i: (b, i, hp)),
            pl.BlockSpec((1, S, wl), lambda hp, b, i: (b, 0, hp)),
            pl.BlockSpec((1, wl, S), lambda hp, b, i: (b, hp, 0)),
            pl.BlockSpec((hps, BIAS_ROWS, _toep_width(S)), lambda hp, b, i: (hp, 0, 0)),
        ],
        out_specs=pl.BlockSpec((1, L, wl), lambda hp, b, i: (b, i, hp)),
        out_shape=jax.ShapeDtypeStruct((B, S, MOBA_WIDTH), BF16),
        scratch_shapes=[pltpu.VMEM((nb, wl), F32), pltpu.VMEM((hps, nb, L), F32)]
        + [pltpu.VMEM((hps, 2 * L, L), F32)] * 2 + [pltpu.VMEM((hps, 2 * L, L), BF16)] * 2,
        compiler_params=_cparams(("arbitrary", "arbitrary", "arbitrary"), 48),
        name="moba_attn",
    )(q, k, v, toep)


def _log_sigmoid(z):
    return jnp.minimum(z, 0.0) - jnp.log1p(jnp.exp(-jnp.abs(z)))


def _gla_kernel(q_ref, k_ref, v_ref, r_ref, lr_ref, wg2_ref, bg2_ref, ng_ref, o_ref, st_sc, *, batch):
    C = GLA_CHUNK
    dk, dv = GLA_DK, GLA_DV

    @pl.when(pl.program_id(0) == 0)
    def _():
        st_sc[...] = jnp.zeros(st_sc.shape, F32)

    row = lax.broadcasted_iota(jnp.int32, (C, C), 0)
    col = lax.broadcasted_iota(jnp.int32, (C, C), 1)
    causal = col <= row
    tri = jnp.where(causal, 1.0, 0.0)
    for b in range(batch):
        z = jnp.dot(lr_ref[b], wg2_ref[...], preferred_element_type=F32) + bg2_ref[...]
        la = _log_sigmoid(z) / GLA_TAU
        la_hi, la_lo = _split_bf16(la)
        tri_b = tri.astype(BF16)
        bc = (jnp.dot(tri_b, la_hi, preferred_element_type=F32)
              + jnp.dot(tri_b, la_lo, preferred_element_type=F32))
        b_last = bc[C - 1:C, :]
        qd = q_ref[b] * (dk ** -0.5) * jnp.exp(bc)
        kd = k_ref[b] * jnp.exp(-bc)
        kl = k_ref[b] * jnp.exp(b_last - bc)
        e_last = jnp.exp(b_last)
        v = v_ref[b]
        r = r_ref[b]
        outs = []
        for h in range(GLA_HEADS):
            ks = slice(h * dk, (h + 1) * dk)
            vs = slice(h * dv, (h + 1) * dv)
            qh = qd[:, ks].astype(BF16)
            kh = kd[:, ks].astype(BF16)
            vh = v[:, vs].astype(BF16)
            attn = lax.dot_general(qh, kh, (((1,), (1,)), ((), ())), preferred_element_type=F32)
            attn = jnp.where(causal, attn, 0.0)
            st = st_sc[b, h]
            o = jnp.dot(attn.astype(BF16), vh, preferred_element_type=F32)
            o = o + lax.dot_general(qh, st.astype(BF16), (((1,), (1,)), ((), ())), preferred_element_type=F32)
            upd = lax.dot_general(vh, kl[:, ks].astype(BF16), (((0,), (0,)), ((), ())), preferred_element_type=F32)
            st_sc[b, h] = st * e_last[:, ks] + upd
            o = o * lax.rsqrt(jnp.mean(o * o, axis=-1, keepdims=True) + RMS_EPS)
            rg = r[:, vs]
            outs.append(o * ng_ref[:, vs] * (rg * jax.nn.sigmoid(rg)))
        o_ref[b] = jnp.concatenate(outs, axis=-1).astype(o_ref.dtype)


def gla_mixer(q, k, v, r, lr, wg2_pad, bg2, norm_g):
    B, S, _ = q.shape
    C = GLA_CHUNK
    nc = S // C
    blk = lambda w: pl.BlockSpec((B, C, w), lambda c: (0, c, 0))
    full = lambda a: pl.BlockSpec(a.shape, lambda c: (0,) * a.ndim)
    return pl.pallas_call(
        functools.partial(_gla_kernel, batch=B),
        grid=(nc,),
        in_specs=[blk(GLA_QK_WIDTH), blk(GLA_QK_WIDTH), blk(GLA_V_WIDTH), blk(GLA_V_WIDTH), blk(LANE),
                  full(wg2_pad), full(bg2), full(norm_g)],
        out_specs=blk(GLA_V_WIDTH),
        out_shape=jax.ShapeDtypeStruct((B, S, GLA_V_WIDTH), BF16),
        scratch_shapes=[pltpu.VMEM((B, GLA_HEADS, GLA_DV, GLA_DK), F32)],
        compiler_params=_cparams(("arbitrary",)),
        name="gla",
    )(q, k, v, r, lr, wg2_pad, bg2, norm_g)


def _deepnorm_ln(x, y, gate, ln_g, ln_b):
    z = ALPHA_DN * x + (1.0 + gate) * y
    mu = jnp.mean(z, axis=-1, keepdims=True)
    zc = z - mu
    var = jnp.mean(zc * zc, axis=-1, keepdims=True)
    return zc * lax.rsqrt(var + LN_EPS) * ln_g + ln_b


def _out_ln_kernel(*refs, n_parts, matmul, has_next):
    y_refs = refs[:n_parts]
    pos = n_parts
    if matmul:
        w_refs = refs[pos:pos + n_parts]
        pos += n_parts
    x_ref, g_ref, lng_ref, lnb_ref = refs[pos:pos + 4]
    pos += 4
    if has_next:
        sc_ref, sh_ref = refs[pos:pos + 2]
        pos += 2
    o_ref = refs[pos]
    y = None
    for i in range(n_parts):
        part = jnp.dot(y_refs[i][0], w_refs[i][...], preferred_element_type=F32) if matmul else y_refs[i][0]
        y = part if y is None else y + part
    xn = _deepnorm_ln(x_ref[0], y, g_ref[0], lng_ref[...], lnb_ref[...])
    o_ref[0] = xn
    if has_next:
        refs[pos + 1][0] = (xn * (1.0 + sc_ref[0]) + sh_ref[0]).astype(refs[pos + 1].dtype)


def out_ln(ys, ws, x, gate, ln_g, ln_b, nxt, h_dtype=BF16, tm=512):
    B, S, D = x.shape
    tile = lambda w: pl.BlockSpec((1, tm, w), lambda b, i: (b, i, 0))
    vec = pl.BlockSpec((1, 1, D), lambda b, i: (b, 0, 0))
    par = pl.BlockSpec((1, D), lambda b, i: (0, 0))
    args = list(ys)
    in_specs = [tile(y.shape[-1]) for y in ys]
    if ws is not None:
        args += list(ws)
        in_specs += [pl.BlockSpec(w.shape, lambda b, i: (0, 0)) for w in ws]
    args += [x, gate, ln_g, ln_b]
    in_specs += [tile(D), vec, par, par]
    out_shape = [jax.ShapeDtypeStruct((B, S, D), F32)]
    out_specs = [tile(D)]
    if nxt is not None:
        args += list(nxt)
        in_specs += [vec, vec]
        out_shape.append(jax.ShapeDtypeStruct((B, S, D), h_dtype))
        out_specs.append(tile(D))
    res = pl.pallas_call(
        functools.partial(_out_ln_kernel, n_parts=len(ys), matmul=ws is not None, has_next=nxt is not None),
        grid=(B, S // tm),
        in_specs=in_specs,
        out_specs=out_specs,
        out_shape=out_shape,
        compiler_params=_cparams(("arbitrary", "arbitrary")),
        name="out_ln",
    )(*args)
    return (res[0], res[1]) if nxt is not None else (res[0], None)


def _ffn_ln_kernel(h_ref, w1_ref, w3_ref, w2_ref, x_ref, g_ref, lng_ref, lnb_ref, sc_ref, sh_ref,
                   o_ref, hn_ref, acc_sc):
    j = pl.program_id(2)

    @pl.when(j == 0)
    def _():
        acc_sc[...] = jnp.zeros(acc_sc.shape, F32)

    h = h_ref[0]
    a = jnp.dot(h, w1_ref[...], preferred_element_type=F32)
    b = jnp.dot(h, w3_ref[...], preferred_element_type=F32)
    u = (a * jax.nn.sigmoid(a)) * b
    acc_sc[...] += jnp.dot(u.astype(BF16), w2_ref[...], preferred_element_type=F32)

    @pl.when(j == pl.num_programs(2) - 1)
    def _():
        xn = _deepnorm_ln(x_ref[0], acc_sc[...], g_ref[0], lng_ref[...], lnb_ref[...])
        o_ref[0] = xn
        hn_ref[0] = (xn * (1.0 + sc_ref[0]) + sh_ref[0]).astype(BF16)


def ffn_ln(h, w1, w3, w2, x, gate, ln_g, ln_b, nxt, tm=512):
    B, S, D = x.shape
    F = w1.shape[1]
    tf = F // 2
    tile = pl.BlockSpec((1, tm, D), lambda b, i, j: (b, i, 0))
    vec = pl.BlockSpec((1, 1, D), lambda b, i, j: (b, 0, 0))
    par = pl.BlockSpec((1, D), lambda b, i, j: (0, 0))
    return pl.pallas_call(
        _ffn_ln_kernel,
        grid=(B, S // tm, F // tf),
        in_specs=[tile,
                  pl.BlockSpec((D, tf), lambda b, i, j: (0, j)),
                  pl.BlockSpec((D, tf), lambda b, i, j: (0, j)),
                  pl.BlockSpec((tf, D), lambda b, i, j: (j, 0)),
                  tile, vec, par, par, vec, vec],
        out_specs=[tile, tile],
        out_shape=[jax.ShapeDtypeStruct((B, S, D), F32), jax.ShapeDtypeStruct((B, S, D), BF16)],
        scratch_shapes=[pltpu.VMEM((tm, D), F32)],
        compiler_params=_cparams(("arbitrary", "arbitrary", "arbitrary"), 48),
        name="ffn_ln",
    )(h, w1, w3, w2, x, gate, ln_g, ln_b, nxt[0], nxt[1])


def _compress_kernel(x_ref, pos_ref, w1_ref, w2_ref, o_ref):
    dh = NSA_HEAD_DIM
    x = x_ref[0, 0]
    uv = jnp.dot(x, w1_ref[...], preferred_element_type=F32)
    pc = jnp.dot(pos_ref[...], w1_ref[...], preferred_element_type=F32)
    c = pc[0:1, :dh] + pc[1:2, dh:]
    n = uv.shape[0]
    nxt = pltpu.roll(uv[:, dh:], n - 1, 0)
    hid = uv[:, :dh] + nxt + c
    act = hid * jax.nn.sigmoid(hid)
    o_ref[0, 0] = jnp.dot(act.astype(BF16), w2_ref[...], preferred_element_type=F32).astype(o_ref.dtype)


def nsa_compress(kv, pos2, w1cat, w2):
    B, G, S, dh = kv.shape
    seg = NSA_CMP_STRIDE
    n = S // seg
    x = kv.reshape(B, G, n, seg * dh)
    return pl.pallas_call(
        _compress_kernel,
        grid=(B, G),
        in_specs=[
            pl.BlockSpec((1, 1, n, seg * dh), lambda b, g: (b, g, 0, 0)),
            pl.BlockSpec(pos2.shape, lambda b, g: (0, 0)),
            pl.BlockSpec(w1cat.shape, lambda b, g: (0, 0)),
            pl.BlockSpec(w2.shape, lambda b, g: (0, 0)),
        ],
        out_specs=pl.BlockSpec((1, 1, n, dh), lambda b, g: (b, g, 0, 0)),
        out_shape=jax.ShapeDtypeStruct((B, G, n, dh), BF16),
        compiler_params=_cparams(("arbitrary", "arbitrary")),
        name="nsa_compress",
    )(x, pos2, w1cat, w2)


def _nsa_kernel(q_ref, kc_ref, vc_ref, ks_ref, vs_ref, kw_ref, vw_ref, gt_ref, cb_ref, tb_ref, o_ref, sel_sc,
                s0_sc, s1_sc, p0_sc, p1_sc, *, S, TQ):
    dh = NSA_HEAD_DIM
    HPG = NSA_HPG
    LS = NSA_SLC_BLOCK
    TK = 4 * LS
    R = HPG * TQ
    n_cmp = (S - NSA_CMP_LEN) // NSA_CMP_STRIDE + 1
    n_cp = S // NSA_CMP_STRIDE
    n_slc = S // LS
    n_top = min(NSA_SLC_TOPN, n_slc)
    qi = pl.program_id(2)
    q0 = qi * TQ

    q4 = jnp.concatenate([q_ref[0, :, h * dh:(h + 1) * dh] for h in range(HPG)], axis=0)
    lanes4 = lambda a: jnp.concatenate([a] * HPG, axis=1)

    s = lax.dot_general(kc_ref[0, 0], q4, _NT, preferred_element_type=F32)
    s = s + jnp.concatenate([cb_ref[h] for h in range(HPG)], axis=1)
    n_row = lax.broadcasted_iota(jnp.int32, (n_cp, TQ), 0)
    t_q = q0 + lax.broadcasted_iota(jnp.int32, (n_cp, TQ), 1)
    valid_c = lanes4((n_row * NSA_CMP_STRIDE + (NSA_CMP_LEN - 1) <= t_q) & (n_row < n_cmp))
    s = jnp.where(valid_c, s, NEG_INF)
    m = jnp.max(s, axis=0, keepdims=True)
    m = jnp.where(m == NEG_INF, 0.0, m)
    e = jnp.exp2(s - m)
    p_c = e * (1.0 / jnp.maximum(jnp.sum(e, axis=0, keepdims=True), jnp.finfo(F32).tiny))
    o_c = lax.dot_general(vc_ref[0, 0], p_c.astype(BF16), _TN, preferred_element_type=F32)

    p_sum = p_c[:, 0:TQ]
    for h in range(1, HPG):
        p_sum = p_sum + p_c[:, h * TQ:(h + 1) * TQ]
    om = lax.broadcasted_iota(jnp.int32, (n_slc, n_cp), 0)
    c_lo = lax.broadcasted_iota(jnp.int32, (n_slc, n_cp), 1) * NSA_CMP_STRIDE
    ov = jnp.maximum(jnp.minimum(c_lo + NSA_CMP_LEN, om * LS + LS) - jnp.maximum(c_lo, om * LS), 0)
    overlap_t = ov.astype(F32) / NSA_CMP_LEN
    p_hi, p_lo = _split_bf16(p_sum)
    ov_b = overlap_t.astype(BF16)
    imp_t = (jnp.dot(ov_b, p_hi, preferred_element_type=F32)
             + jnp.dot(ov_b, p_lo, preferred_element_type=F32))
    blk = lax.broadcasted_iota(jnp.int32, (n_slc, TQ), 0)
    tq_lane = q0 + lax.broadcasted_iota(jnp.int32, (n_slc, TQ), 1)
    tb = lax.shift_right_logical(tq_lane, LS.bit_length() - 1)
    forced = (blk == 0) | (blk == tb) | (blk == tb - 1)
    key = jnp.where(blk > tb, NEG_INF, jnp.where(forced, jnp.inf, imp_t))
    rank = jnp.zeros((n_slc, TQ), jnp.int32)
    for mm in range(n_slc):
        km = key[mm:mm + 1, :]
        beats = (km > key) | ((km == key) & (mm < blk))
        rank = rank + beats.astype(jnp.int32)
    sel_sc[...] = jnp.where((rank < n_top) & (blk <= tb), 1.0, 0.0)

    def toep_tile(k0, n_keys):
        return jnp.concatenate(
            [jnp.concatenate([tb_ref[h, :, pl.ds(_toep_col(q0, k0 + r * BIAS_ROWS), TQ)] for h in range(HPG)], axis=1)
             for r in range(n_keys // BIAS_ROWS)], axis=0)

    def pipelined_attention(n, scores_fn, post_fn, v_fn):
        s_bufs = (s0_sc, s1_sc)
        p_bufs = (p0_sc, p1_sc)
        s_bufs[0][...] = scores_fn(0)
        p_bufs[1][...] = jnp.zeros(p_bufs[1].shape, BF16)

        def half(i, cur, carry):
            m, l, acc, alpha_prev = carry
            s_bufs[1 - cur][...] = scores_fn(jnp.minimum(i + 1, n - 1))
            s = post_fn(i, s_bufs[cur][...])
            m_new = jnp.maximum(m, jnp.max(s, axis=0, keepdims=True))
            alpha = jnp.exp2(m - m_new)
            p = jnp.exp2(s - m_new)
            l_new = alpha * l + jnp.sum(p, axis=0, keepdims=True)
            p_bufs[cur][...] = p.astype(BF16)
            pv = lax.dot_general(v_fn(jnp.maximum(i - 1, 0)), p_bufs[1 - cur][...], _TN, preferred_element_type=F32)
            return m_new, l_new, alpha_prev * acc + pv, alpha

        def body(jj, carry):
            return half(2 * jj + 1, 1, half(2 * jj, 0, carry))

        n_pairs = (n + 1) // 2
        init = (jnp.full((1, R), NEG_INF, F32), jnp.zeros((1, R), F32), jnp.zeros((dh, R), F32),
                jnp.ones((1, R), F32))
        _, l, acc, alpha = lax.fori_loop(0, n_pairs, body, init)
        pv = lax.dot_general(v_fn(jnp.minimum(2 * n_pairs - 1, n - 1)), p_bufs[1][...], _TN,
                             preferred_element_type=F32)
        return (alpha * acc + pv) * (1.0 / l)

    krow_s = lax.broadcasted_iota(jnp.int32, (TK, TQ), 0)
    tq_s = q0 + lax.broadcasted_iota(jnp.int32, (TK, TQ), 1)
    n_kt = (q0 + TQ - 1) // TK + 1

    def slc_scores(kt):
        return lax.dot_general(ks_ref[0, 0, pl.ds(pl.multiple_of(kt * TK, TK), TK), :], q4, _NT,
                               preferred_element_type=F32)

    def slc_post(kt, s):
        live = kt < n_kt
        ktc = jnp.minimum(kt, n_kt - 1)
        k0 = pl.multiple_of(ktc * TK, TK)
        flags = jnp.concatenate(
            [jnp.broadcast_to(sel_sc[pl.ds(ktc * (TK // LS) + j, 1), :], (LS, TQ)) for j in range(TK // LS)], axis=0)
        mask = lanes4((flags > jnp.where(live, 0.5, 2.0)) & (k0 + krow_s <= tq_s))
        return jnp.where(mask, s + toep_tile(k0, TK), NEG_INF)

    o_s = pipelined_attention(n_kt, slc_scores, slc_post,
                              lambda kt: vs_ref[0, 0, pl.ds(pl.multiple_of(kt * TK, TK), TK), :])

    krow_w = lax.broadcasted_iota(jnp.int32, (TQ, TQ), 0)
    tq_w = q0 + lax.broadcasted_iota(jnp.int32, (TQ, TQ), 1)
    n_wt = jnp.minimum(qi, NSA_WINDOW // TQ) + 1

    def win_body(j, carry):
        k0 = pl.multiple_of(q0 - j * TQ, TQ)
        dist = tq_w - (k0 + krow_w)
        mask = lanes4((dist >= 0) & (dist < NSA_WINDOW))
        s = lax.dot_general(kw_ref[0, 0, pl.ds(k0, TQ), :], q4, _NT, preferred_element_type=F32) + toep_tile(k0, TQ)
        return _softmax_step(carry, jnp.where(mask, s, NEG_INF), vw_ref[0, 0, pl.ds(k0, TQ), :])

    init_w = (jnp.full((1, R), NEG_INF, F32), jnp.zeros((1, R), F32), jnp.zeros((dh, R), F32))
    _, l_w, acc_w = lax.fori_loop(0, n_wt, win_body, init_w)
    o_w = acc_w * (1.0 / l_w)

    g_t = jax.nn.sigmoid(gt_ref[0, 0]).T
    outs = []
    for h in range(HPG):
        cols = slice(h * TQ, (h + 1) * TQ)
        o = (g_t[h:h + 1] * o_c[:, cols] + g_t[HPG + h:HPG + h + 1] * o_s[:, cols]
             + g_t[2 * HPG + h:2 * HPG + h + 1] * o_w[:, cols])
        outs.append(o.T)
    o_ref[0] = jnp.concatenate(outs, axis=1).astype(o_ref.dtype)


def nsa_attention(q, kc, vc, ks, vs, kw, vw, gates, cmpb, toep, TQ=256):
    B, S, _ = q.shape
    G, HPG, dh = NSA_KV_GROUPS, NSA_HPG, NSA_HEAD_DIM
    n_cp = S // NSA_CMP_STRIDE
    full_kv = pl.BlockSpec((1, 1, S, dh), lambda b, g, i: (b, g, 0, 0))
    cmp_kv = pl.BlockSpec((1, 1, n_cp, dh), lambda b, g, i: (b, g, 0, 0))
    return pl.pallas_call(
        functools.partial(_nsa_kernel, S=S, TQ=TQ),
        grid=(B, G, S // TQ),
        in_specs=[
            pl.BlockSpec((1, TQ, HPG * dh), lambda b, g, i: (b, i, g)),
            cmp_kv, cmp_kv, full_kv, full_kv, full_kv, full_kv,
            pl.BlockSpec((1, 1, TQ, LANE), lambda b, g, i: (b, g, i, 0)),
            pl.BlockSpec((HPG, n_cp, TQ), lambda b, g, i: (g, 0, i)),
            pl.BlockSpec((HPG, BIAS_ROWS, _toep_width(S)), lambda b, g, i: (g, 0, 0)),
        ],
        out_specs=pl.BlockSpec((1, TQ, HPG * dh), lambda b, g, i: (b, i, g)),
        out_shape=jax.ShapeDtypeStruct((B, S, NSA_Q_WIDTH), BF16),
        scratch_shapes=[pltpu.VMEM((S // NSA_SLC_BLOCK, TQ), F32)]
        + [pltpu.VMEM((TQ, HPG * TQ), F32)] * 2 + [pltpu.VMEM((TQ, HPG * TQ), BF16)] * 2,
        compiler_params=_cparams(("arbitrary", "arbitrary", "arbitrary"), 48),
        name="nsa_attn",
    )(q, kc, vc, ks, vs, kw, vw, gates, cmpb, toep)


def _router_kernel(h_ref, wr_ref, e_ref, w_ref, cnt_ref):
    @pl.when(pl.program_id(0) == 0)
    def _():
        cnt_ref[...] = jnp.zeros(cnt_ref.shape, F32)

    h_hi, h_lo = _split_bf16(h_ref[...])
    w_hi, w_lo = _split_bf16(wr_ref[...])
    logits = (jnp.dot(h_hi, w_hi, preferred_element_type=F32) + jnp.dot(h_hi, w_lo, preferred_element_type=F32)
              + jnp.dot(h_lo, w_hi, preferred_element_type=F32))
    tm = logits.shape[0]
    lane = lax.broadcasted_iota(jnp.int32, logits.shape, 1)
    logits = jnp.where(lane < N_EXPERTS, logits, NEG_INF)
    l1 = jnp.max(logits, axis=-1, keepdims=True)
    i1 = jnp.min(jnp.where(logits == l1, lane, LANE), axis=-1, keepdims=True)
    rest = jnp.where(lane == i1, NEG_INF, logits)
    l2 = jnp.max(rest, axis=-1, keepdims=True)
    i2 = jnp.min(jnp.where(rest == l2, lane, LANE), axis=-1, keepdims=True)
    e2 = jnp.exp(l2 - l1)
    den = 1.0 + e2
    w_ref[...] = jnp.where(lane == 0, 1.0 / den, jnp.where(lane == 1, e2 / den, 0.0))

    oh1 = lane == i1
    oh2 = lane == i2
    both = jnp.where(oh1 | oh2, 1.0, 0.0)
    earlier = lax.broadcasted_iota(jnp.int32, (tm, tm), 1) < lax.broadcasted_iota(jnp.int32, (tm, tm), 0)
    prefix = jnp.dot(jnp.where(earlier, 1.0, 0.0).astype(BF16), both.astype(BF16),
                     preferred_element_type=F32)
    seen = prefix + cnt_ref[0:1, :]
    p1 = jnp.sum(jnp.where(oh1, seen, 0.0), axis=-1, keepdims=True).astype(jnp.int32)
    p2 = jnp.sum(jnp.where(oh2, seen, 0.0), axis=-1, keepdims=True).astype(jnp.int32)
    e_ref[...] = jnp.where(lane == 0, i1, jnp.where(lane == 1, i2, jnp.where(lane == 2, p1,
                                                                              jnp.where(lane == 3, p2, 0))))
    cnt_ref[...] = cnt_ref[...] + jnp.sum(both, axis=0, keepdims=True)


def moe_route(h2, wr_pad, tm=512):
    N, D = h2.shape
    return pl.pallas_call(
        _router_kernel,
        grid=(N // tm,),
        in_specs=[pl.BlockSpec((tm, D), lambda i: (i, 0)), pl.BlockSpec(wr_pad.shape, lambda i: (0, 0))],
        out_specs=[pl.BlockSpec((tm, LANE), lambda i: (i, 0)), pl.BlockSpec((tm, LANE), lambda i: (i, 0)),
                   pl.BlockSpec((8, LANE), lambda i: (0, 0))],
        out_shape=[jax.ShapeDtypeStruct((N, LANE), jnp.int32), jax.ShapeDtypeStruct((N, LANE), F32),
                   jax.ShapeDtypeStruct((8, LANE), F32)],
        compiler_params=_cparams(("arbitrary",)),
        name="moe_router",
    )(h2, wr_pad)


def _dispatch_kernel(dest_ref, h_ref, xs_in_ref, xs_ref, sem):
    del xs_in_ref
    tm = h_ref.shape[0]
    base = pl.program_id(0) * tm * TOP_K

    def issue(r, carry):
        for k in range(TOP_K):
            d = dest_ref[base + r * TOP_K + k]
            pltpu.make_async_copy(h_ref.at[r], xs_ref.at[d], sem).start(priority=k)
        return carry

    lax.fori_loop(0, tm, issue, 0, unroll=16)
    for k in range(TOP_K):
        pltpu.make_async_copy(h_ref, xs_ref.at[pl.ds(0, tm)], sem).wait()


def moe_dispatch_rows(h2, dest_flat, n_rows, recycled=None, tm=1024):
    N, D = h2.shape
    xs0 = jnp.zeros((n_rows, D), h2.dtype) if recycled is None else recycled
    grid_spec = pltpu.PrefetchScalarGridSpec(
        num_scalar_prefetch=1,
        grid=(N // tm,),
        in_specs=[pl.BlockSpec((tm, D), lambda i, d: (i, 0)), pl.BlockSpec(memory_space=pl.ANY)],
        out_specs=pl.BlockSpec(memory_space=pl.ANY),
        scratch_shapes=[pltpu.SemaphoreType.DMA],
    )
    return pl.pallas_call(
        _dispatch_kernel,
        grid_spec=grid_spec,
        out_shape=jax.ShapeDtypeStruct((n_rows, D), h2.dtype),
        input_output_aliases={2: 0},
        compiler_params=_cparams(("arbitrary",)),
        name="moe_dispatch",
    )(dest_flat, h2, xs0)


def _combine_ln_kernel(dest_ref, w_ref, ys_ref, x_ref, g_ref, lng_ref, lnb_ref, *rest, has_next):
    if has_next:
        sc_ref, sh_ref, o_ref, hn_ref, buf, sem = rest
    else:
        o_ref, buf, sem = rest
    tm = x_ref.shape[0]
    base = pl.program_id(0) * tm * TOP_K

    def issue(r, carry):
        for k in range(TOP_K):
            d = dest_ref[base + r * TOP_K + k]
            pltpu.make_async_copy(ys_ref.at[d], buf.at[k, r], sem).start(priority=k)
        return carry

    lax.fori_loop(0, tm, issue, 0, unroll=16)
    for k in range(TOP_K):
        pltpu.make_async_copy(ys_ref.at[pl.ds(0, tm)], buf.at[k], sem).wait()
    w = w_ref[...]
    y = w[:, 0:1] * buf[0] + w[:, 1:2] * buf[1]
    xn = _deepnorm_ln(x_ref[...], y, g_ref[0], lng_ref[...], lnb_ref[...])
    o_ref[...] = xn
    if has_next:
        hn_ref[...] = (xn * (1.0 + sc_ref[0]) + sh_ref[0]).astype(hn_ref.dtype)


def moe_combine_ln(ys, dest_flat, w_top, x, gate, ln_g, ln_b, nxt, tm=1024):
    B, S, D = x.shape
    N = B * S
    per_b = S // tm
    tile = pl.BlockSpec((tm, D), lambda i, d: (i, 0))
    vec = pl.BlockSpec((1, 1, D), lambda i, d: (i // per_b, 0, 0))
    par = pl.BlockSpec((1, D), lambda i, d: (0, 0))
    in_specs = [pl.BlockSpec((tm, LANE), lambda i, d: (i, 0)), pl.BlockSpec(memory_space=pl.ANY), tile, vec, par, par]
    args = [w_top, ys, x.reshape(N, D), gate, ln_g, ln_b]
    out_shape = [jax.ShapeDtypeStruct((N, D), F32)]
    out_specs = [tile]
    if nxt is not None:
        in_specs += [vec, vec]
        args += list(nxt)
        out_shape.append(jax.ShapeDtypeStruct((N, D), BF16))
        out_specs.append(tile)
    grid_spec = pltpu.PrefetchScalarGridSpec(
        num_scalar_prefetch=1,
        grid=(N // tm,),
        in_specs=in_specs,
        out_specs=out_specs,
        scratch_shapes=[pltpu.VMEM((TOP_K, tm, D), F32), pltpu.SemaphoreType.DMA],
    )
    res = pl.pallas_call(
        functools.partial(_combine_ln_kernel, has_next=nxt is not None),
        grid_spec=grid_spec,
        out_shape=out_shape,
        compiler_params=_cparams(("arbitrary",)),
        name="moe_combine_ln",
    )(dest_flat, *args)
    x_new = res[0].reshape(B, S, D)
    return (x_new, res[1].reshape(B, S, D)) if nxt is not None else (x_new, None)


def _moe_ffn_kernel(be_ref, nu_ref, x_ref, w1_ref, w3_ref, w2_ref, o_ref, acc_sc, xb_sc):
    i = pl.program_id(0)
    j = pl.program_id(1)
    last = pl.num_programs(1) - 1
    used = i < nu_ref[0]

    @pl.when(j == 0)
    def _():
        acc_sc[...] = jnp.zeros(acc_sc.shape, F32)
        xb_sc[...] = x_ref[...].astype(BF16)

    @pl.when(used)
    def _():
        x = xb_sc[...]
        a = jnp.dot(x, w1_ref[0].astype(BF16), preferred_element_type=F32)
        b = jnp.dot(x, w3_ref[0].astype(BF16), preferred_element_type=F32)
        u = (a * jax.nn.sigmoid(a)) * b
        acc_sc[...] += jnp.dot(u.astype(BF16), w2_ref[0].astype(BF16), preferred_element_type=F32)

    @pl.when(j == last)
    def _():
        o_ref[...] = acc_sc[...]


def moe_ffn(xs, blk_e, n_used, w1, w3, w2, tf=512):
    n_rows, D = xs.shape
    n_blocks = blk_e.shape[0]
    RB = n_rows // n_blocks
    F = w1.shape[2]
    nf = F // tf

    def wcol(i, j, be, nu):
        return (be[i], 0, jnp.where(i < nu[0], j, nf - 1))

    def wrow(i, j, be, nu):
        return (be[i], jnp.where(i < nu[0], j, nf - 1), 0)

    grid_spec = pltpu.PrefetchScalarGridSpec(
        num_scalar_prefetch=2,
        grid=(n_blocks, nf),
        in_specs=[
            pl.BlockSpec((RB, D), lambda i, j, be, nu: (i, 0)),
            pl.BlockSpec((1, D, tf), wcol),
            pl.BlockSpec((1, D, tf), wcol),
            pl.BlockSpec((1, tf, D), wrow),
        ],
        out_specs=pl.BlockSpec((RB, D), lambda i, j, be, nu: (i, 0)),
        scratch_shapes=[pltpu.VMEM((RB, D), F32), pltpu.VMEM((RB, D), BF16)],
    )
    return pl.pallas_call(
        _moe_ffn_kernel,
        grid_spec=grid_spec,
        out_shape=jax.ShapeDtypeStruct((n_rows, D), F32),
        compiler_params=_cparams(("arbitrary", "arbitrary"), 56),
        name="moe_ffn",
    )(blk_e, n_used, xs, w1, w3, w2)


def moe_layout(e_pos, counts_f, n_tok):
    RB = _moe_row_block(n_tok)
    n_assign = n_tok * TOP_K
    n_blocks = (n_assign + N_EXPERTS * (RB - 1) + RB - 1) // RB
    counts = counts_f[0, :N_EXPERTS].astype(jnp.int32)
    padded = (counts + RB - 1) // RB * RB
    pad_end = jnp.cumsum(padded)
    pad_start = pad_end - padded
    e = e_pos[:, 0:TOP_K]
    pos = e_pos[:, TOP_K:2 * TOP_K]
    start = jnp.zeros_like(e)
    for x in range(N_EXPERTS):
        start = jnp.where(e == x, pad_start[x], start)
    dest_flat = (start + pos).reshape(-1)
    blk_start = jnp.arange(n_blocks, dtype=jnp.int32) * RB
    blk_e = jnp.minimum(jnp.sum((blk_start[:, None] >= pad_end[None, :]).astype(jnp.int32), axis=1), N_EXPERTS - 1)
    n_used = (pad_end[-1] // RB).astype(jnp.int32).reshape(1)
    return dest_flat, blk_e.astype(jnp.int32), n_used, n_blocks * RB


def _pad_cols(w, width):
    return jnp.pad(w, ((0, 0), (0, width - w.shape[1])))


def kernel(x, c, rel_bias, ada_w, ada_b, ln_g, ln_b, ab_w_in, gla_wg2, gla_bg2, gla_norm_g, ab_w_out,
           ffn_w1, ffn_w3, ffn_w2, nsa_w_in, nsa_cmp_pos, nsa_cmp_w1k, nsa_cmp_w2k, nsa_cmp_w1v, nsa_cmp_w2v,
           nsa_w_out, moe_router, moe_w1, moe_w3, moe_w2):
    B, S, D = x.shape
    n_tok = B * S
    mod = adaln_mod(c, ada_w, ada_b)
    mods = [[m.reshape(B, 1, D) for m in jnp.split(mod[i], 6, axis=-1)] for i in range(DEPTH)]
    toep, cmpb = build_bias_tables(rel_bias, S)

    xs_dead = None
    sh1, sc1 = mods[0][0], mods[0][1]
    h = modulate(x, sc1, sh1)
    for i in range(DEPTH):
        j = i // 2
        _, _, g1, sh2, sc2, g2 = mods[i]
        nxt2 = (mods[i + 1][1], mods[i + 1][0]) if i + 1 < DEPTH else None
        lng = ln_g[i].reshape(2, 1, D)
        lnb = ln_b[i].reshape(2, 1, D)
        if i % 2 == 0:
            w = ab_w_in[j]
            cuts = np.cumsum((MOBA_WIDTH, MOBA_WIDTH, MOBA_WIDTH, GLA_QK_WIDTH, GLA_QK_WIDTH, GLA_V_WIDTH,
                              GLA_V_WIDTH))
            parts = jnp.split(w, cuts.tolist(), axis=1)
            parts[-1] = _pad_cols(parts[-1], LANE)
            w_cat = jnp.concatenate(parts, axis=1).astype(BF16)
            widths = [p.shape[1] for p in parts]
            offs = np.concatenate([[0], np.cumsum(widths)[:-1]]).tolist()
            scales = [MOBA_HEAD_DIM ** -0.5 * LOG2E] + [1.0] * 7
            kinds = ["flat", "flat", "flat_t"] + ["flat"] * 5
            offs[2] = 0
            segs = tuple((o, wd, kd, s) for o, wd, kd, s in zip(offs, widths, kinds, scales))
            dts = [BF16, BF16, BF16, F32, F32, F32, F32, F32]
            mq, mk, mv, gq, gk, gv, gr, glr = project(h, w_cat, segs, dts, wt=parts[2].T.astype(BF16))
            y_moba = moba_attention(mq, mk, mv, toep)
            wg2_pad = jnp.pad(gla_wg2[j], ((0, LANE - GLA_GATE_RANK), (0, 0)))
            y_gla = gla_mixer(gq, gk, gv, gr, glr, wg2_pad, gla_bg2[j].reshape(1, -1), gla_norm_g[j].reshape(1, -1))
            w_out = ab_w_out[j].astype(BF16)
            x, h = out_ln([y_moba, y_gla], [w_out[:MOBA_WIDTH], w_out[MOBA_WIDTH:]], x, g1, lng[0], lnb[0],
                          (sc2, sh2))
            x, h = ffn_ln(h, ffn_w1[j].astype(BF16), ffn_w3[j].astype(BF16), ffn_w2[j].astype(BF16), x, g2,
                          lng[1], lnb[1], nxt2)
        else:
            G, HPG, dh = NSA_KV_GROUPS, NSA_HPG, NSA_HEAD_DIM
            w = nsa_w_in[j]
            wq = w[:, :NSA_Q_WIDTH]
            wkv = w[:, NSA_Q_WIDTH:NSA_Q_WIDTH + 6 * NSA_KV_WIDTH]
            wg = w[:, NSA_Q_WIDTH + 6 * NSA_KV_WIDTH:].reshape(D, G, HPG, 3)
            wg = _pad_cols(wg.transpose(0, 1, 3, 2).reshape(D * G, 3 * HPG), LANE).reshape(D, G * LANE)
            w_cat = jnp.concatenate([wq, wkv, wg], axis=1).astype(BF16)
            segs = [(0, NSA_Q_WIDTH, "flat", dh ** -0.5 * LOG2E)]
            segs += [(NSA_Q_WIDTH + n * NSA_KV_WIDTH, NSA_KV_WIDTH, "group", 1.0) for n in range(6)]
            segs += [(NSA_Q_WIDTH + 6 * NSA_KV_WIDTH, G * LANE, "group", 1.0)]
            dts = [BF16] * 7 + [F32]
            q, kc, vc, ks, vs, kw, vw, gates = project(h, w_cat, tuple(segs), dts)
            half = NSA_CMP_STRIDE * dh
            pos2 = jnp.pad(nsa_cmp_pos[j].reshape(2, half), ((0, 6), (0, 0))).astype(BF16)
            w1k = jnp.concatenate([nsa_cmp_w1k[j][:half], nsa_cmp_w1k[j][half:]], axis=1).astype(BF16)
            w1v = jnp.concatenate([nsa_cmp_w1v[j][:half], nsa_cmp_w1v[j][half:]], axis=1).astype(BF16)
            k_cmp = nsa_compress(kc, pos2, w1k, nsa_cmp_w2k[j].astype(BF16))
            v_cmp = nsa_compress(vc, pos2, w1v, nsa_cmp_w2v[j].astype(BF16))
            y = nsa_attention(q, k_cmp, v_cmp, ks, vs, kw, vw, gates, cmpb, toep)
            x, h = out_ln([y], [nsa_w_out[j].astype(BF16)], x, g1, lng[0], lnb[0], (sc2, sh2), h_dtype=F32)
            h2 = h.reshape(n_tok, D)
            e_pos, w_top, counts = moe_route(h2, _pad_cols(moe_router[j], LANE))
            dest_flat, blk_e, n_used, n_rows = moe_layout(e_pos, counts, n_tok)
            xs = moe_dispatch_rows(h2, dest_flat, n_rows, recycled=xs_dead)
            xs_dead = xs
            stack = lambda w: w.reshape((-1,) + w.shape[2:])
            ys = moe_ffn(xs, blk_e + j * N_EXPERTS, n_used, stack(moe_w1), stack(moe_w3), stack(moe_w2))
            x, h = moe_combine_ln(ys, dest_flat, w_top, x, g2, lng[1], lnb[1], nxt2)
    return x
```

```python
import functools
import math

import jax
import jax.numpy as jnp
import numpy as np
from jax import lax
from jax.experimental import pallas as pl
from jax.experimental.pallas import tpu as pltpu

F32 = jnp.float32
BF16 = jnp.bfloat16
NEG_INF = float("-inf")

DEPTH = 4
ALPHA_DN = (2 * DEPTH) ** 0.25
LN_EPS = 1e-5
RMS_EPS = 1e-6

N_BUCKETS = 32
MAX_DISTANCE = 1024

MOBA_HEADS = 8
MOBA_HEAD_DIM = 64
MOBA_BLOCK = 256
MOBA_TOPK = 3
MOBA_WIDTH = MOBA_HEADS * MOBA_HEAD_DIM

GLA_HEADS = 4
GLA_DK = 64
GLA_DV = 128
GLA_GATE_RANK = 16
GLA_TAU = 16.0
GLA_CHUNK = 64
GLA_QK_WIDTH = GLA_HEADS * GLA_DK
GLA_V_WIDTH = GLA_HEADS * GLA_DV

NSA_HEADS = 8
NSA_KV_GROUPS = 2
NSA_HPG = NSA_HEADS // NSA_KV_GROUPS
NSA_HEAD_DIM = 128
NSA_CMP_LEN = 32
NSA_CMP_STRIDE = 16
NSA_SLC_BLOCK = 64
NSA_SLC_TOPN = 16
NSA_WINDOW = 512
NSA_Q_WIDTH = NSA_HEADS * NSA_HEAD_DIM
NSA_KV_WIDTH = NSA_KV_GROUPS * NSA_HEAD_DIM

N_EXPERTS = 8
TOP_K = 2
MOE_BLOCKS_PER_EXPERT = 4


def _moe_row_block(n_tok):
    balanced = n_tok * TOP_K // N_EXPERTS
    rows = balanced // MOE_BLOCKS_PER_EXPERT + balanced // 64
    return -(-rows // 16) * 16

LOG2E = math.log2(math.e)
LANE = 128
BIAS_ROWS = 128


def _split_bf16(a):
    hi = a.astype(BF16)
    return hi, (a - hi.astype(F32)).astype(BF16)


def _cparams(sem, vmem_mib=40, **extra):
    return pltpu.CompilerParams(dimension_semantics=sem, vmem_limit_bytes=vmem_mib << 20, **extra)


def _adaln_kernel(c_ref, w_ref, b_ref, o_ref):
    c = c_ref[...]
    cond = c * jax.nn.sigmoid(c)
    o_ref[0] = jnp.dot(cond, w_ref[0], preferred_element_type=F32) + b_ref[0]


def adaln_mod(c, ada_w, ada_b):
    B, D = c.shape
    L, _, N = ada_w.shape
    rows = 8
    cp = jnp.zeros((rows, D), F32).at[:B].set(c)
    tn = N // 4
    out = pl.pallas_call(
        _adaln_kernel,
        grid=(L, N // tn),
        in_specs=[
            pl.BlockSpec((rows, D), lambda l, j: (0, 0)),
            pl.BlockSpec((1, D, tn), lambda l, j: (l, 0, j)),
            pl.BlockSpec((1, 1, tn), lambda l, j: (l, 0, j)),
        ],
        out_specs=pl.BlockSpec((1, rows, tn), lambda l, j: (l, 0, j)),
        out_shape=jax.ShapeDtypeStruct((L, rows, N), F32),
        compiler_params=_cparams(("arbitrary", "arbitrary")),
        name="adaln_mod",
    )(cp, ada_w, ada_b.reshape(L, 1, N))
    return out[:, :B]


def _modulate_kernel(x_ref, sc_ref, sh_ref, o_ref):
    o_ref[0] = (x_ref[0] * (1.0 + sc_ref[0]) + sh_ref[0]).astype(o_ref.dtype)


def modulate(x, sc, sh, tm=512):
    B, S, D = x.shape
    return pl.pallas_call(
        _modulate_kernel,
        grid=(B, S // tm),
        in_specs=[
            pl.BlockSpec((1, tm, D), lambda b, i: (b, i, 0)),
            pl.BlockSpec((1, 1, D), lambda b, i: (b, 0, 0)),
            pl.BlockSpec((1, 1, D), lambda b, i: (b, 0, 0)),
        ],
        out_specs=pl.BlockSpec((1, tm, D), lambda b, i: (b, i, 0)),
        out_shape=jax.ShapeDtypeStruct((B, S, D), BF16),
        compiler_params=_cparams(("arbitrary", "arbitrary")),
        name="modulate",
    )(x, sc, sh)


def _t5_bucket(dist):
    max_exact = N_BUCKETS // 2
    n = jnp.maximum(dist, 0)
    nf = jnp.maximum(n, 1).astype(jnp.float32)
    large = max_exact + (jnp.log(nf / max_exact) / math.log(MAX_DISTANCE / max_exact) * (N_BUCKETS - max_exact)).astype(jnp.int32)
    large = jnp.minimum(large, N_BUCKETS - 1)
    return jnp.where(n < max_exact, n, large)


def _bias_table_kernel(rb_ref, bm_ref, o_ref, *, rows_per_step):
    h = pl.program_id(0)
    n_steps = bm_ref.shape[0] // rows_per_step

    def body(i, carry):
        r0 = pl.multiple_of(i * rows_per_step, rows_per_step)
        bm = bm_ref[pl.ds(r0, rows_per_step), :]
        acc = jnp.zeros(bm.shape, F32)
        for b in range(N_BUCKETS):
            acc = jnp.where(bm == b, rb_ref[b, h] * LOG2E, acc)
        o_ref[0, pl.ds(r0, rows_per_step), :] = acc
        return carry

    lax.fori_loop(0, n_steps, body, 0)


def bias_table(rel_bias, bucket_map, rows_per_step):
    R, W = bucket_map.shape
    H = rel_bias.shape[1]
    return pl.pallas_call(
        functools.partial(_bias_table_kernel, rows_per_step=rows_per_step),
        grid=(H,),
        in_specs=[
            pl.BlockSpec(memory_space=pltpu.SMEM),
            pl.BlockSpec((R, W), lambda h: (0, 0)),
        ],
        out_specs=pl.BlockSpec((1, R, W), lambda h: (h, 0, 0)),
        out_shape=jax.ShapeDtypeStruct((H, R, W), F32),
        compiler_params=_cparams(("arbitrary",)),
        name="bias_table",
    )(rel_bias, bucket_map)


def _toep_width(S):
    return S + BIAS_ROWS


def _toep_col(q0, k0):
    return pl.multiple_of(q0 - k0 + BIAS_ROWS, LANE)


def build_bias_tables(rel_bias, S):
    j = jnp.arange(BIAS_ROWS, dtype=jnp.int32)[:, None]
    c = jnp.arange(_toep_width(S), dtype=jnp.int32)[None, :]
    toep_map = _t5_bucket(c - j - BIAS_ROWS)
    n_cmp_pad = S // NSA_CMP_STRIDE
    t = jnp.arange(S, dtype=jnp.int32)[None, :]
    cmp_end = jnp.arange(n_cmp_pad, dtype=jnp.int32)[:, None] * NSA_CMP_STRIDE + NSA_CMP_LEN - 1
    cmp_map = _t5_bucket(t - cmp_end)
    return bias_table(rel_bias, toep_map, 8), bias_table(rel_bias, cmp_map, 8)


def _proj_kernel(h_ref, w_ref, wt_ref, *o_refs, segs):
    h = h_ref[0]
    for (off, width, kind, scale), o_ref in zip(segs, o_refs):
        if kind == "flat_t":
            acc = lax.dot_general(wt_ref[off:off + width, :], h, _NT, preferred_element_type=F32)
        else:
            acc = jnp.dot(h, w_ref[:, off:off + width], preferred_element_type=F32)
        if scale != 1.0:
            acc = acc * scale
        if kind in ("flat", "flat_t"):
            o_ref[0] = acc.astype(o_ref.dtype)
        else:
            for g in range(width // LANE):
                o_ref[0, g] = acc[:, g * LANE:(g + 1) * LANE].astype(o_ref.dtype)


def project(h, w_cat, segs, out_dtypes, wt=None, tm=512):
    B, S, D = h.shape
    if wt is None:
        wt = jnp.zeros((8, D), w_cat.dtype)
    out_shapes, out_specs = [], []
    for (off, width, kind, scale), dt in zip(segs, out_dtypes):
        if kind == "flat":
            out_shapes.append(jax.ShapeDtypeStruct((B, S, width), dt))
            out_specs.append(pl.BlockSpec((1, tm, width), lambda b, i: (b, i, 0)))
        elif kind == "flat_t":
            out_shapes.append(jax.ShapeDtypeStruct((B, width, S), dt))
            out_specs.append(pl.BlockSpec((1, width, tm), lambda b, i: (b, 0, i)))
        else:
            G = width // LANE
            out_shapes.append(jax.ShapeDtypeStruct((B, G, S, LANE), dt))
            out_specs.append(pl.BlockSpec((1, G, tm, LANE), lambda b, i: (b, 0, i, 0)))
    return pl.pallas_call(
        functools.partial(_proj_kernel, segs=segs),
        grid=(B, S // tm),
        in_specs=[
            pl.BlockSpec((1, tm, D), lambda b, i: (b, i, 0)),
            pl.BlockSpec(w_cat.shape, lambda b, i: (0, 0)),
            pl.BlockSpec(wt.shape, lambda b, i: (0, 0)),
        ],
        out_specs=out_specs,
        out_shape=out_shapes,
        compiler_params=_cparams(("arbitrary", "arbitrary"), 48),
        name="in_proj",
    )(h, w_cat, wt)


_NT = (((1,), (1,)), ((), ()))
_TN = (((0,), (0,)), ((), ()))


def _softmax_step(carry, s, v_tile):
    m, l, acc = carry
    m_new = jnp.maximum(m, jnp.max(s, axis=0, keepdims=True))
    alpha = jnp.exp2(m - m_new)
    p = jnp.exp2(s - m_new)
    l_new = alpha * l + jnp.sum(p, axis=0, keepdims=True)
    acc_new = alpha * acc + lax.dot_general(v_tile, p.astype(BF16), _TN, preferred_element_type=F32)
    return m_new, l_new, acc_new


def _moba_kernel(q_ref, k_ref, v_ref, tb_ref, o_ref, kmean_sc, sel_sc, s0_sc, s1_sc, p0_sc, p1_sc,
                 *, S, heads_per_step):
    L = MOBA_BLOCK
    dh = MOBA_HEAD_DIM
    nb = S // L
    qi = pl.program_id(2)
    q0 = qi * L

    @pl.when(qi == 0)
    def _():
        kf = k_ref[0].astype(F32)
        kmean_sc[...] = jnp.mean(kf.reshape(nb, L, kf.shape[-1]), axis=1)

    q2 = q_ref[0]
    lane_q = lax.broadcasted_iota(jnp.int32, q2.shape, 1)
    lane_m = lax.broadcasted_iota(jnp.int32, kmean_sc.shape, 1)
    blk = lax.broadcasted_iota(jnp.int32, (nb, L), 0)
    qz = []
    for hh in range(heads_per_step):
        in_head = (lane_q >= hh * dh) & (lane_q < (hh + 1) * dh)
        qz.append(jnp.where(in_head, q2, jnp.zeros_like(q2)))
        km = jnp.where((lane_m >= hh * dh) & (lane_m < (hh + 1) * dh), kmean_sc[...], 0.0)
        km_hi, km_lo = _split_bf16(km)
        gate = (lax.dot_general(km_hi, q2, _NT, preferred_element_type=F32)
                + lax.dot_general(km_lo, q2, _NT, preferred_element_type=F32))
        valid = blk < qi
        gm = jnp.where(valid, gate, NEG_INF)
        rank = jnp.zeros((nb, L), jnp.int32)
        for m in range(nb):
            gmm = gm[m:m + 1, :]
            beats = (gmm > gm) | ((gmm == gm) & (m < blk))
            rank = rank + beats.astype(jnp.int32)
        sel_sc[hh] = jnp.where(valid & (rank < MOBA_TOPK), 1.0, 0.0)

    def bias_tile(hh, k0, n_keys):
        return jnp.concatenate([tb_ref[hh, :, pl.ds(_toep_col(q0, k0 + r * BIAS_ROWS), L)]
                                for r in range(n_keys // BIAS_ROWS)], axis=0)

    krow = lax.broadcasted_iota(jnp.int32, (L, L), 0)
    qcol = lax.broadcasted_iota(jnp.int32, (L, L), 1)
    k_own = k_ref[0, pl.ds(pl.multiple_of(q0, L), L), :]
    v_tile = lambda hh, k0, n_keys: v_ref[0, hh * dh:(hh + 1) * dh, pl.ds(k0, n_keys)]
    carry = []
    for hh in range(heads_per_step):
        s = lax.dot_general(k_own, qz[hh], _NT, preferred_element_type=F32) + bias_tile(hh, q0, L)
        s = jnp.where(krow <= qcol, s, NEG_INF)
        m0 = jnp.max(s, axis=0, keepdims=True)
        p = jnp.exp2(s - m0)
        l0 = jnp.sum(p, axis=0, keepdims=True)
        acc0 = jnp.dot(v_tile(hh, pl.multiple_of(q0, L), L), p.astype(BF16), preferred_element_type=F32)
        carry += [m0, l0, acc0, jnp.ones((1, L), F32)]

    n = (qi + 1) // 2
    s_bufs = (s0_sc, s1_sc)
    p_bufs = (p0_sc, p1_sc)
    tile_k0 = lambda i: pl.multiple_of(jnp.clip(i, 0, jnp.maximum(n - 1, 0)) * 2 * L, 2 * L)

    def put_scores(i, buf):
        kt = k_ref[0, pl.ds(tile_k0(i), 2 * L), :]
        for hh in range(heads_per_step):
            buf[hh] = lax.dot_general(kt, qz[hh], _NT, preferred_element_type=F32)

    put_scores(0, s_bufs[0])
    p_bufs[1][...] = jnp.zeros(p_bufs[1].shape, BF16)

    def half(i, cur, carry):
        put_scores(i + 1, s_bufs[1 - cur])
        k0 = tile_k0(i)
        blk0 = lax.shift_right_logical(k0, L.bit_length() - 1)
        thresh = jnp.where(i < n, 0.5, 2.0)
        out = []
        for hh in range(heads_per_step):
            m, l, acc, alpha_prev = carry[4 * hh:4 * hh + 4]
            flags = jnp.concatenate(
                [jnp.broadcast_to(sel_sc[hh, pl.ds(blk0 + j, 1), :], (L, L)) for j in range(2)], axis=0)
            s = jnp.where(flags > thresh, s_bufs[cur][hh] + bias_tile(hh, k0, 2 * L), NEG_INF)
            m_new = jnp.maximum(m, jnp.max(s, axis=0, keepdims=True))
            alpha = jnp.exp2(m - m_new)
            p = jnp.exp2(s - m_new)
            l_new = alpha * l + jnp.sum(p, axis=0, keepdims=True)
            p_bufs[cur][hh] = p.astype(BF16)
            pv = jnp.dot(v_tile(hh, tile_k0(i - 1), 2 * L), p_bufs[1 - cur][hh], preferred_element_type=F32)
            out += [m_new, l_new, alpha_prev * acc + pv, alpha]
        return tuple(out)

    def body(jj, carry):
        return half(2 * jj + 1, 1, half(2 * jj, 0, carry))

    n_pairs = (n + 1) // 2
    carry = lax.fori_loop(0, n_pairs, body, tuple(carry))
    outs = []
    for hh in range(heads_per_step):
        _, l, acc, alpha = carry[4 * hh:4 * hh + 4]
        pv = jnp.dot(v_tile(hh, tile_k0(2 * n_pairs - 1), 2 * L), p_bufs[1][hh], preferred_element_type=F32)
        outs.append((alpha * acc + pv) * (1.0 / l))
    o_ref[0] = jnp.concatenate(outs, axis=0).T.astype(o_ref.dtype)


def moba_attention(q, k, v, toep):
    B, S, _ = q.shape
    L = MOBA_BLOCK
    hps = 4
    wl = hps * MOBA_HEAD_DIM
    n_hp = MOBA_HEADS // hps
    nb = S // L
    return pl.pallas_call(
        functools.partial(_moba_kernel, S=S, heads_per_step=hps),
        grid=(n_hp, B, nb),
        in_specs=[
            pl.BlockSpec((1, L, wl), lambda hp, b, i: (b, i, hp)),
            pl.BlockSpec((1, S, wl), lambda hp, b, i: (b, 0, hp)),
            pl.BlockSpec((1, wl, S), lambda hp, b, i: (b, hp, 0)),
            pl.BlockSpec((hps, BIAS_ROWS, _toep_width(S)), lambda hp, b, i: (hp, 0, 0)),
        ],
        out_specs=pl.BlockSpec((1, L, wl), lambda hp, b, i: (b, i, hp)),
        out_shape=jax.ShapeDtypeStruct((B, S, MOBA_WIDTH), BF16),
        scratch_shapes=[pltpu.VMEM((nb, wl), F32), pltpu.VMEM((hps, nb, L), F32)]
        + [pltpu.VMEM((hps, 2 * L, L), F32)] * 2 + [pltpu.VMEM((hps, 2 * L, L), BF16)] * 2,
        compiler_params=_cparams(("arbitrary", "arbitrary", "arbitrary"), 48),
        name="moba_attn",
    )(q, k, v, toep)


def _log_sigmoid(z):
    return jnp.minimum(z, 0.0) - jnp.log1p(jnp.exp(-jnp.abs(z)))


def _gla_kernel(q_ref, k_ref, v_ref, r_ref, lr_ref, wg2_ref, bg2_ref, ng_ref, o_ref, st_sc, *, batch):
    C = GLA_CHUNK
    dk, dv = GLA_DK, GLA_DV

    @pl.when(pl.program_id(0) == 0)
    def _():
        st_sc[...] = jnp.zeros(st_sc.shape, F32)

    row = lax.broadcasted_iota(jnp.int32, (C, C), 0)
    col = lax.broadcasted_iota(jnp.int32, (C, C), 1)
    causal = col <= row
    tri = jnp.where(causal, 1.0, 0.0)
    for b in range(batch):
        z = jnp.dot(lr_ref[b], wg2_ref[...], preferred_element_type=F32) + bg2_ref[...]
        la = _log_sigmoid(z) / GLA_TAU
        la_hi, la_lo = _split_bf16(la)
        tri_b = tri.astype(BF16)
        bc = (jnp.dot(tri_b, la_hi, preferred_element_type=F32)
              + jnp.dot(tri_b, la_lo, preferred_element_type=F32))
        b_last = bc[C - 1:C, :]
        qd = q_ref[b] * (dk ** -0.5) * jnp.exp(bc)
        kd = k_ref[b] * jnp.exp(-bc)
        kl = k_ref[b] * jnp.exp(b_last - bc)
        e_last = jnp.exp(b_last)
        v = v_ref[b]
        r = r_ref[b]
        outs = []
        for h in range(GLA_HEADS):
            ks = slice(h * dk, (h + 1) * dk)
            vs = slice(h * dv, (h + 1) * dv)
            qh = qd[:, ks].astype(BF16)
            kh = kd[:, ks].astype(BF16)
            vh = v[:, vs].astype(BF16)
            attn = lax.dot_general(qh, kh, (((1,), (1,)), ((), ())), preferred_element_type=F32)
            attn = jnp.where(causal, attn, 0.0)
            st = st_sc[b, h]
            o = jnp.dot(attn.astype(BF16), vh, preferred_element_type=F32)
            o = o + lax.dot_general(qh, st.astype(BF16), (((1,), (1,)), ((), ())), preferred_element_type=F32)
            upd = lax.dot_general(vh, kl[:, ks].astype(BF16), (((0,), (0,)), ((), ())), preferred_element_type=F32)
            st_sc[b, h] = st * e_last[:, ks] + upd
            o = o * lax.rsqrt(jnp.mean(o * o, axis=-1, keepdims=True) + RMS_EPS)
            rg = r[:, vs]
            outs.append(o * ng_ref[:, vs] * (rg * jax.nn.sigmoid(rg)))
        o_ref[b] = jnp.concatenate(outs, axis=-1).astype(o_ref.dtype)


def gla_mixer(q, k, v, r, lr, wg2_pad, bg2, norm_g):
    B, S, _ = q.shape
    C = GLA_CHUNK
    nc = S // C
    blk = lambda w: pl.BlockSpec((B, C, w), lambda c: (0, c, 0))
    full = lambda a: pl.BlockSpec(a.shape, lambda c: (0,) * a.ndim)
    return pl.pallas_call(
        functools.partial(_gla_kernel, batch=B),
        grid=(nc,),
        in_specs=[blk(GLA_QK_WIDTH), blk(GLA_QK_WIDTH), blk(GLA_V_WIDTH), blk(GLA_V_WIDTH), blk(LANE),
                  full(wg2_pad), full(bg2), full(norm_g)],
        out_specs=blk(GLA_V_WIDTH),
        out_shape=jax.ShapeDtypeStruct((B, S, GLA_V_WIDTH), BF16),
        scratch_shapes=[pltpu.VMEM((B, GLA_HEADS, GLA_DV, GLA_DK), F32)],
        compiler_params=_cparams(("arbitrary",)),
        name="gla",
    )(q, k, v, r, lr, wg2_pad, bg2, norm_g)


def _deepnorm_ln(x, y, gate, ln_g, ln_b):
    z = ALPHA_DN * x + (1.0 + gate) * y
    mu = jnp.mean(z, axis=-1, keepdims=True)
    zc = z - mu
    var = jnp.mean(zc * zc, axis=-1, keepdims=True)
    return zc * lax.rsqrt(var + LN_EPS) * ln_g + ln_b


def _out_ln_kernel(*refs, n_parts, matmul, has_next):
    y_refs = refs[:n_parts]
    pos = n_parts
    if matmul:
        w_refs = refs[pos:pos + n_parts]
        pos += n_parts
    x_ref, g_ref, lng_ref, lnb_ref = refs[pos:pos + 4]
    pos += 4
    if has_next:
        sc_ref, sh_ref = refs[pos:pos + 2]
        pos += 2
    o_ref = refs[pos]
    y = None
    for i in range(n_parts):
        part = jnp.dot(y_refs[i][0], w_refs[i][...], preferred_element_type=F32) if matmul else y_refs[i][0]
        y = part if y is None else y + part
    xn = _deepnorm_ln(x_ref[0], y, g_ref[0], lng_ref[...], lnb_ref[...])
    o_ref[0] = xn
    if has_next:
        refs[pos + 1][0] = (xn * (1.0 + sc_ref[0]) + sh_ref[0]).astype(refs[pos + 1].dtype)


def out_ln(ys, ws, x, gate, ln_g, ln_b, nxt, h_dtype=BF16, tm=512):
    B, S, D = x.shape
    tile = lambda w: pl.BlockSpec((1, tm, w), lambda b, i: (b, i, 0))
    vec = pl.BlockSpec((1, 1, D), lambda b, i: (b, 0, 0))
    par = pl.BlockSpec((1, D), lambda b, i: (0, 0))
    args = list(ys)
    in_specs = [tile(y.shape[-1]) for y in ys]
    if ws is not None:
        args += list(ws)
        in_specs += [pl.BlockSpec(w.shape, lambda b, i: (0, 0)) for w in ws]
    args += [x, gate, ln_g, ln_b]
    in_specs += [tile(D), vec, par, par]
    out_shape = [jax.ShapeDtypeStruct((B, S, D), F32)]
    out_specs = [tile(D)]
    if nxt is not None:
        args += list(nxt)
        in_specs += [vec, vec]
        out_shape.append(jax.ShapeDtypeStruct((B, S, D), h_dtype))
        out_specs.append(tile(D))
    res = pl.pallas_call(
        functools.partial(_out_ln_kernel, n_parts=len(ys), matmul=ws is not None, has_next=nxt is not None),
        grid=(B, S // tm),
        in_specs=in_specs,
        out_specs=out_specs,
        out_shape=out_shape,
        compiler_params=_cparams(("arbitrary", "arbitrary")),
        name="out_ln",
    )(*args)
    return (res[0], res[1]) if nxt is not None else (res[0], None)


def _ffn_ln_kernel(h_ref, w1_ref, w3_ref, w2_ref, x_ref, g_ref, lng_ref, lnb_ref, sc_ref, sh_ref,
                   o_ref, hn_ref, acc_sc):
    j = pl.program_id(2)

    @pl.when(j == 0)
    def _():
        acc_sc[...] = jnp.zeros(acc_sc.shape, F32)

    h = h_ref[0]
    a = jnp.dot(h, w1_ref[...], preferred_element_type=F32)
    b = jnp.dot(h, w3_ref[...], preferred_element_type=F32)
    u = (a * jax.nn.sigmoid(a)) * b
    acc_sc[...] += jnp.dot(u.astype(BF16), w2_ref[...], preferred_element_type=F32)

    @pl.when(j == pl.num_programs(2) - 1)
    def _():
        xn = _deepnorm_ln(x_ref[0], acc_sc[...], g_ref[0], lng_ref[...], lnb_ref[...])
        o_ref[0] = xn
        hn_ref[0] = (xn * (1.0 + sc_ref[0]) + sh_ref[0]).astype(BF16)


def ffn_ln(h, w1, w3, w2, x, gate, ln_g, ln_b, nxt, tm=512):
    B, S, D = x.shape
    F = w1.shape[1]
    tf = F // 2
    tile = pl.BlockSpec((1, tm, D), lambda b, i, j: (b, i, 0))
    vec = pl.BlockSpec((1, 1, D), lambda b, i, j: (b, 0, 0))
    par = pl.BlockSpec((1, D), lambda b, i, j: (0, 0))
    return pl.pallas_call(
        _ffn_ln_kernel,
        grid=(B, S // tm, F // tf),
        in_specs=[tile,
                  pl.BlockSpec((D, tf), lambda b, i, j: (0, j)),
                  pl.BlockSpec((D, tf), lambda b, i, j: (0, j)),
                  pl.BlockSpec((tf, D), lambda b, i, j: (j, 0)),
                  tile, vec, par, par, vec, vec],
        out_specs=[tile, tile],
        out_shape=[jax.ShapeDtypeStruct((B, S, D), F32), jax.ShapeDtypeStruct((B, S, D), BF16)],
        scratch_shapes=[pltpu.VMEM((tm, D), F32)],
        compiler_params=_cparams(("arbitrary", "arbitrary", "arbitrary"), 48),
        name="ffn_ln",
    )(h, w1, w3, w2, x, gate, ln_g, ln_b, nxt[0], nxt[1])


def _compress_kernel(x_ref, pos_ref, w1_ref, w2_ref, o_ref):
    dh = NSA_HEAD_DIM
    x = x_ref[0, 0]
    uv = jnp.dot(x, w1_ref[...], preferred_element_type=F32)
    pc = jnp.dot(pos_ref[...], w1_ref[...], preferred_element_type=F32)
    c = pc[0:1, :dh] + pc[1:2, dh:]
    n = uv.shape[0]
    nxt = pltpu.roll(uv[:, dh:], n - 1, 0)
    hid = uv[:, :dh] + nxt + c
    act = hid * jax.nn.sigmoid(hid)
    o_ref[0, 0] = jnp.dot(act.astype(BF16), w2_ref[...], preferred_element_type=F32).astype(o_ref.dtype)


def nsa_compress(kv, pos2, w1cat, w2):
    B, G, S, dh = kv.shape
    seg = NSA_CMP_STRIDE
    n = S // seg
    x = kv.reshape(B, G, n, seg * dh)
    return pl.pallas_call(
        _compress_kernel,
        grid=(B, G),
        in_specs=[
            pl.BlockSpec((1, 1, n, seg * dh), lambda b, g: (b, g, 0, 0)),
            pl.BlockSpec(pos2.shape, lambda b, g: (0, 0)),
            pl.BlockSpec(w1cat.shape, lambda b, g: (0, 0)),
            pl.BlockSpec(w2.shape, lambda b, g: (0, 0)),
        ],
        out_specs=pl.BlockSpec((1, 1, n, dh), lambda b, g: (b, g, 0, 0)),
        out_shape=jax.ShapeDtypeStruct((B, G, n, dh), BF16),
        compiler_params=_cparams(("arbitrary", "arbitrary")),
        name="nsa_compress",
    )(x, pos2, w1cat, w2)


def _nsa_kernel(q_ref, kc_ref, vc_ref, ks_ref, vs_ref, kw_ref, vw_ref, gt_ref, cb_ref, tb_ref, o_ref, sel_sc,
                s0_sc, s1_sc, p0_sc, p1_sc, *, S, TQ):
    dh = NSA_HEAD_DIM
    HPG = NSA_HPG
    LS = NSA_SLC_BLOCK
    TK = 4 * LS
    R = HPG * TQ
    n_cmp = (S - NSA_CMP_LEN) // NSA_CMP_STRIDE + 1
    n_cp = S // NSA_CMP_STRIDE
    n_slc = S // LS
    n_top = min(NSA_SLC_TOPN, n_slc)
    qi = pl.program_id(2)
    q0 = qi * TQ

    q4 = jnp.concatenate([q_ref[0, :, h * dh:(h + 1) * dh] for h in range(HPG)], axis=0)
    lanes4 = lambda a: jnp.concatenate([a] * HPG, axis=1)

    s = lax.dot_general(kc_ref[0, 0], q4, _NT, preferred_element_type=F32)
    s = s + jnp.concatenate([cb_ref[h] for h in range(HPG)], axis=1)
    n_row = lax.broadcasted_iota(jnp.int32, (n_cp, TQ), 0)
    t_q = q0 + lax.broadcasted_iota(jnp.int32, (n_cp, TQ), 1)
    valid_c = lanes4((n_row * NSA_CMP_STRIDE + (NSA_CMP_LEN - 1) <= t_q) & (n_row < n_cmp))
    s = jnp.where(valid_c, s, NEG_INF)
    m = jnp.max(s, axis=0, keepdims=True)
    m = jnp.where(m == NEG_INF, 0.0, m)
    e = jnp.exp2(s - m)
    p_c = e * (1.0 / jnp.maximum(jnp.sum(e, axis=0, keepdims=True), jnp.finfo(F32).tiny))
    o_c = lax.dot_general(vc_ref[0, 0], p_c.astype(BF16), _TN, preferred_element_type=F32)

    p_sum = p_c[:, 0:TQ]
    for h in range(1, HPG):
        p_sum = p_sum + p_c[:, h * TQ:(h + 1) * TQ]
    om = lax.broadcasted_iota(jnp.int32, (n_slc, n_cp), 0)
    c_lo = lax.broadcasted_iota(jnp.int32, (n_slc, n_cp), 1) * NSA_CMP_STRIDE
    ov = jnp.maximum(jnp.minimum(c_lo + NSA_CMP_LEN, om * LS + LS) - jnp.maximum(c_lo, om * LS), 0)
    overlap_t = ov.astype(F32) / NSA_CMP_LEN
    p_hi, p_lo = _split_bf16(p_sum)
    ov_b = overlap_t.astype(BF16)
    imp_t = (jnp.dot(ov_b, p_hi, preferred_element_type=F32)
             + jnp.dot(ov_b, p_lo, preferred_element_type=F32))
    blk = lax.broadcasted_iota(jnp.int32, (n_slc, TQ), 0)
    tq_lane = q0 + lax.broadcasted_iota(jnp.int32, (n_slc, TQ), 1)
    tb = lax.shift_right_logical(tq_lane, LS.bit_length() - 1)
    forced = (blk == 0) | (blk == tb) | (blk == tb - 1)
    key = jnp.where(blk > tb, NEG_INF, jnp.where(forced, jnp.inf, imp_t))
    rank = jnp.zeros((n_slc, TQ), jnp.int32)
    for mm in range(n_slc):
        km = key[mm:mm + 1, :]
        beats = (km > key) | ((km == key) & (mm < blk))
        rank = rank + beats.astype(jnp.int32)
    sel_sc[...] = jnp.where((rank < n_top) & (blk <= tb), 1.0, 0.0)

    def toep_tile(k0, n_keys):
        return jnp.concatenate(
            [jnp.concatenate([tb_ref[h, :, pl.ds(_toep_col(q0, k0 + r * BIAS_ROWS), TQ)] for h in range(HPG)], axis=1)
             for r in range(n_keys // BIAS_ROWS)], axis=0)

    def pipelined_attention(n, scores_fn, post_fn, v_fn):
        s_bufs = (s0_sc, s1_sc)
        p_bufs = (p0_sc, p1_sc)
        s_bufs[0][...] = scores_fn(0)
        p_bufs[1][...] = jnp.zeros(p_bufs[1].shape, BF16)

        def half(i, cur, carry):
            m, l, acc, alpha_prev = carry
            s_bufs[1 - cur][...] = scores_fn(jnp.minimum(i + 1, n - 1))
            s = post_fn(i, s_bufs[cur][...])
            m_new = jnp.maximum(m, jnp.max(s, axis=0, keepdims=True))
            alpha = jnp.exp2(m - m_new)
            p = jnp.exp2(s - m_new)
            l_new = alpha * l + jnp.sum(p, axis=0, keepdims=True)
            p_bufs[cur][...] = p.astype(BF16)
            pv = lax.dot_general(v_fn(jnp.maximum(i - 1, 0)), p_bufs[1 - cur][...], _TN, preferred_element_type=F32)
            return m_new, l_new, alpha_prev * acc + pv, alpha

        def body(jj, carry):
            return half(2 * jj + 1, 1, half(2 * jj, 0, carry))

        n_pairs = (n + 1) // 2
        init = (jnp.full((1, R), NEG_INF, F32), jnp.zeros((1, R), F32), jnp.zeros((dh, R), F32),
                jnp.ones((1, R), F32))
        _, l, acc, alpha = lax.fori_loop(0, n_pairs, body, init)
        pv = lax.dot_general(v_fn(jnp.minimum(2 * n_pairs - 1, n - 1)), p_bufs[1][...], _TN,
                             preferred_element_type=F32)
        return (alpha * acc + pv) * (1.0 / l)

    krow_s = lax.broadcasted_iota(jnp.int32, (TK, TQ), 0)
    tq_s = q0 + lax.broadcasted_iota(jnp.int32, (TK, TQ), 1)
    n_kt = (q0 + TQ - 1) // TK + 1

    def slc_scores(kt):
        return lax.dot_general(ks_ref[0, 0, pl.ds(pl.multiple_of(kt * TK, TK), TK), :], q4, _NT,
                               preferred_element_type=F32)

    def slc_post(kt, s):
        live = kt < n_kt
        ktc = jnp.minimum(kt, n_kt - 1)
        k0 = pl.multiple_of(ktc * TK, TK)
        flags = jnp.concatenate(
            [jnp.broadcast_to(sel_sc[pl.ds(ktc * (TK // LS) + j, 1), :], (LS, TQ)) for j in range(TK // LS)], axis=0)
        mask = lanes4((flags > jnp.where(live, 0.5, 2.0)) & (k0 + krow_s <= tq_s))
        return jnp.where(mask, s + toep_tile(k0, TK), NEG_INF)

    o_s = pipelined_attention(n_kt, slc_scores, slc_post,
                              lambda kt: vs_ref[0, 0, pl.ds(pl.multiple_of(kt * TK, TK), TK), :])

    krow_w = lax.broadcasted_iota(jnp.int32, (TQ, TQ), 0)
    tq_w = q0 + lax.broadcasted_iota(jnp.int32, (TQ, TQ), 1)
    n_wt = jnp.minimum(qi, NSA_WINDOW // TQ) + 1

    def win_body(j, carry):
        k0 = pl.multiple_of(q0 - j * TQ, TQ)
        dist = tq_w - (k0 + krow_w)
        mask = lanes4((dist >= 0) & (dist < NSA_WINDOW))
        s = lax.dot_general(kw_ref[0, 0, pl.ds(k0, TQ), :], q4, _NT, preferred_element_type=F32) + toep_tile(k0, TQ)
        return _softmax_step(carry, jnp.where(mask, s, NEG_INF), vw_ref[0, 0, pl.ds(k0, TQ), :])

    init_w = (jnp.full((1, R), NEG_INF, F32), jnp.zeros((1, R), F32), jnp.zeros((dh, R), F32))
    _, l_w, acc_w = lax.fori_loop(0, n_wt, win_body, init_w)
    o_w = acc_w * (1.0 / l_w)

    g_t = jax.nn.sigmoid(gt_ref[0, 0]).T
    outs = []
    for h in range(HPG):
        cols = slice(h * TQ, (h + 1) * TQ)
        o = (g_t[h:h + 1] * o_c[:, cols] + g_t[HPG + h:HPG + h + 1] * o_s[:, cols]
             + g_t[2 * HPG + h:2 * HPG + h + 1] * o_w[:, cols])
        outs.append(o.T)
    o_ref[0] = jnp.concatenate(outs, axis=1).astype(o_ref.dtype)


def nsa_attention(q, kc, vc, ks, vs, kw, vw, gates, cmpb, toep, TQ=256):
    B, S, _ = q.shape
    G, HPG, dh = NSA_KV_GROUPS, NSA_HPG, NSA_HEAD_DIM
    n_cp = S // NSA_CMP_STRIDE
    full_kv = pl.BlockSpec((1, 1, S, dh), lambda b, g, i: (b, g, 0, 0))
    cmp_kv = pl.BlockSpec((1, 1, n_cp, dh), lambda b, g, i: (b, g, 0, 0))
    return pl.pallas_call(
        functools.partial(_nsa_kernel, S=S, TQ=TQ),
        grid=(B, G, S // TQ),
        in_specs=[
            pl.BlockSpec((1, TQ, HPG * dh), lambda b, g, i: (b, i, g)),
            cmp_kv, cmp_kv, full_kv, full_kv, full_kv, full_kv,
            pl.BlockSpec((1, 1, TQ, LANE), lambda b, g, i: (b, g, i, 0)),
            pl.BlockSpec((HPG, n_cp, TQ), lambda b, g, i: (g, 0, i)),
            pl.BlockSpec((HPG, BIAS_ROWS, _toep_width(S)), lambda b, g, i: (g, 0, 0)),
        ],
        out_specs=pl.BlockSpec((1, TQ, HPG * dh), lambda b, g, i: (b, i, g)),
        out_shape=jax.ShapeDtypeStruct((B, S, NSA_Q_WIDTH), BF16),
        scratch_shapes=[pltpu.VMEM((S // NSA_SLC_BLOCK, TQ), F32)]
        + [pltpu.VMEM((TQ, HPG * TQ), F32)] * 2 + [pltpu.VMEM((TQ, HPG * TQ), BF16)] * 2,
        compiler_params=_cparams(("arbitrary", "arbitrary", "arbitrary"), 48),
        name="nsa_attn",
    )(q, kc, vc, ks, vs, kw, vw, gates, cmpb, toep)


def _router_kernel(h_ref, wr_ref, e_ref, w_ref, cnt_ref):
    @pl.when(pl.program_id(0) == 0)
    def _():
        cnt_ref[...] = jnp.zeros(cnt_ref.shape, F32)

    h_hi, h_lo = _split_bf16(h_ref[...])
    w_hi, w_lo = _split_bf16(wr_ref[...])
    logits = (jnp.dot(h_hi, w_hi, preferred_element_type=F32) + jnp.dot(h_hi, w_lo, preferred_element_type=F32)
              + jnp.dot(h_lo, w_hi, preferred_element_type=F32))
    tm = logits.shape[0]
    lane = lax.broadcasted_iota(jnp.int32, logits.shape, 1)
    logits = jnp.where(lane < N_EXPERTS, logits, NEG_INF)
    l1 = jnp.max(logits, axis=-1, keepdims=True)
    i1 = jnp.min(jnp.where(logits == l1, lane, LANE), axis=-1, keepdims=True)
    rest = jnp.where(lane == i1, NEG_INF, logits)
    l2 = jnp.max(rest, axis=-1, keepdims=True)
    i2 = jnp.min(jnp.where(rest == l2, lane, LANE), axis=-1, keepdims=True)
    e2 = jnp.exp(l2 - l1)
    den = 1.0 + e2
    w_ref[...] = jnp.where(lane == 0, 1.0 / den, jnp.where(lane == 1, e2 / den, 0.0))

    oh1 = lane == i1
    oh2 = lane == i2
    both = jnp.where(oh1 | oh2, 1.0, 0.0)
    earlier = lax.broadcasted_iota(jnp.int32, (tm, tm), 1) < lax.broadcasted_iota(jnp.int32, (tm, tm), 0)
    prefix = jnp.dot(jnp.where(earlier, 1.0, 0.0).astype(BF16), both.astype(BF16),
                     preferred_element_type=F32)
    seen = prefix + cnt_ref[0:1, :]
    p1 = jnp.sum(jnp.where(oh1, seen, 0.0), axis=-1, keepdims=True).astype(jnp.int32)
    p2 = jnp.sum(jnp.where(oh2, seen, 0.0), axis=-1, keepdims=True).astype(jnp.int32)
    e_ref[...] = jnp.where(lane == 0, i1, jnp.where(lane == 1, i2, jnp.where(lane == 2, p1,
                                                                              jnp.where(lane == 3, p2, 0))))
    cnt_ref[...] = cnt_ref[...] + jnp.sum(both, axis=0, keepdims=True)


def moe_route(h2, wr_pad, tm=512):
    N, D = h2.shape
    return pl.pallas_call(
        _router_kernel,
        grid=(N // tm,),
        in_specs=[pl.BlockSpec((tm, D), lambda i: (i, 0)), pl.BlockSpec(wr_pad.shape, lambda i: (0, 0))],
        out_specs=[pl.BlockSpec((tm, LANE), lambda i: (i, 0)), pl.BlockSpec((tm, LANE), lambda i: (i, 0)),
                   pl.BlockSpec((8, LANE), lambda i: (0, 0))],
        out_shape=[jax.ShapeDtypeStruct((N, LANE), jnp.int32), jax.ShapeDtypeStruct((N, LANE), F32),
                   jax.ShapeDtypeStruct((8, LANE), F32)],
        compiler_params=_cparams(("arbitrary",)),
        name="moe_router",
    )(h2, wr_pad)


def _dispatch_kernel(dest_ref, h_ref, xs_in_ref, xs_ref, sem):
    del xs_in_ref
    tm = h_ref.shape[0]
    base = pl.program_id(0) * tm * TOP_K

    def issue(r, carry):
        for k in range(TOP_K):
            d = dest_ref[base + r * TOP_K + k]
            pltpu.make_async_copy(h_ref.at[r], xs_ref.at[d], sem).start(priority=k)
        return carry

    lax.fori_loop(0, tm, issue, 0, unroll=16)
    for k in range(TOP_K):
        pltpu.make_async_copy(h_ref, xs_ref.at[pl.ds(0, tm)], sem).wait()


def moe_dispatch_rows(h2, dest_flat, n_rows, recycled=None, tm=1024):
    N, D = h2.shape
    xs0 = jnp.zeros((n_rows, D), h2.dtype) if recycled is None else recycled
    grid_spec = pltpu.PrefetchScalarGridSpec(
        num_scalar_prefetch=1,
        grid=(N // tm,),
        in_specs=[pl.BlockSpec((tm, D), lambda i, d: (i, 0)), pl.BlockSpec(memory_space=pl.ANY)],
        out_specs=pl.BlockSpec(memory_space=pl.ANY),
        scratch_shapes=[pltpu.SemaphoreType.DMA],
    )
    return pl.pallas_call(
        _dispatch_kernel,
        grid_spec=grid_spec,
        out_shape=jax.ShapeDtypeStruct((n_rows, D), h2.dtype),
        input_output_aliases={2: 0},
        compiler_params=_cparams(("arbitrary",)),
        name="moe_dispatch",
    )(dest_flat, h2, xs0)


def _combine_ln_kernel(dest_ref, w_ref, ys_ref, x_ref, g_ref, lng_ref, lnb_ref, *rest, has_next):
    if has_next:
        sc_ref, sh_ref, o_ref, hn_ref, buf, sem = rest
    else:
        o_ref, buf, sem = rest
    tm = x_ref.shape[0]
    base = pl.program_id(0) * tm * TOP_K

    def issue(r, carry):
        for k in range(TOP_K):
            d = dest_ref[base + r * TOP_K + k]
            pltpu.make_async_copy(ys_ref.at[d], buf.at[k, r], sem).start(priority=k)
        return carry

    lax.fori_loop(0, tm, issue, 0, unroll=16)
    for k in range(TOP_K):
        pltpu.make_async_copy(ys_ref.at[pl.ds(0, tm)], buf.at[k], sem).wait()
    w = w_ref[...]
    y = w[:, 0:1] * buf[0] + w[:, 1:2] * buf[1]
    xn = _deepnorm_ln(x_ref[...], y, g_ref[0], lng_ref[...], lnb_ref[...])
    o_ref[...] = xn
    if has_next:
        hn_ref[...] = (xn * (1.0 + sc_ref[0]) + sh_ref[0]).astype(hn_ref.dtype)


def moe_combine_ln(ys, dest_flat, w_top, x, gate, ln_g, ln_b, nxt, tm=1024):
    B, S, D = x.shape
    N = B * S
    per_b = S // tm
    tile = pl.BlockSpec((tm, D), lambda i, d: (i, 0))
    vec = pl.BlockSpec((1, 1, D), lambda i, d: (i // per_b, 0, 0))
    par = pl.BlockSpec((1, D), lambda i, d: (0, 0))
    in_specs = [pl.BlockSpec((tm, LANE), lambda i, d: (i, 0)), pl.BlockSpec(memory_space=pl.ANY), tile, vec, par, par]
    args = [w_top, ys, x.reshape(N, D), gate, ln_g, ln_b]
    out_shape = [jax.ShapeDtypeStruct((N, D), F32)]
    out_specs = [tile]
    if nxt is not None:
        in_specs += [vec, vec]
        args += list(nxt)
        out_shape.append(jax.ShapeDtypeStruct((N, D), BF16))
        out_specs.append(tile)
    grid_spec = pltpu.PrefetchScalarGridSpec(
        num_scalar_prefetch=1,
        grid=(N // tm,),
        in_specs=in_specs,
        out_specs=out_specs,
        scratch_shapes=[pltpu.VMEM((TOP_K, tm, D), F32), pltpu.SemaphoreType.DMA],
    )
    res = pl.pallas_call(
        functools.partial(_combine_ln_kernel, has_next=nxt is not None),
        grid_spec=grid_spec,
        out_shape=out_shape,
        compiler_params=_cparams(("arbitrary",)),
        name="moe_combine_ln",
    )(dest_flat, *args)
    x_new = res[0].reshape(B, S, D)
    return (x_new, res[1].reshape(B, S, D)) if nxt is not None else (x_new, None)


def _moe_ffn_kernel(be_ref, nu_ref, x_ref, w1_ref, w3_ref, w2_ref, o_ref, acc_sc, xb_sc):
    i = pl.program_id(0)
    j = pl.program_id(1)
    last = pl.num_programs(1) - 1
    used = i < nu_ref[0]

    @pl.when(j == 0)
    def _():
        acc_sc[...] = jnp.zeros(acc_sc.shape, F32)
        xb_sc[...] = x_ref[...].astype(BF16)

    @pl.when(used)
    def _():
        x = xb_sc[...]
        a = jnp.dot(x, w1_ref[0].astype(BF16), preferred_element_type=F32)
        b = jnp.dot(x, w3_ref[0].astype(BF16), preferred_element_type=F32)
        u = (a * jax.nn.sigmoid(a)) * b
        acc_sc[...] += jnp.dot(u.astype(BF16), w2_ref[0].astype(BF16), preferred_element_type=F32)

    @pl.when(j == last)
    def _():
        o_ref[...] = acc_sc[...]


def moe_ffn(xs, blk_e, n_used, w1, w3, w2, tf=512):
    n_rows, D = xs.shape
    n_blocks = blk_e.shape[0]
    RB = n_rows // n_blocks
    F = w1.shape[2]
    nf = F // tf

    def wcol(i, j, be, nu):
        return (be[i], 0, jnp.where(i < nu[0], j, nf - 1))

    def wrow(i, j, be, nu):
        return (be[i], jnp.where(i < nu[0], j, nf - 1), 0)

    grid_spec = pltpu.PrefetchScalarGridSpec(
        num_scalar_prefetch=2,
        grid=(n_blocks, nf),
        in_specs=[
            pl.BlockSpec((RB, D), lambda i, j, be, nu: (i, 0)),
            pl.BlockSpec((1, D, tf), wcol),
            pl.BlockSpec((1, D, tf), wcol),
            pl.BlockSpec((1, tf, D), wrow),
        ],
        out_specs=pl.BlockSpec((RB, D), lambda i, j, be, nu: (i, 0)),
        scratch_shapes=[pltpu.VMEM((RB, D), F32), pltpu.VMEM((RB, D), BF16)],
    )
    return pl.pallas_call(
        _moe_ffn_kernel,
        grid_spec=grid_spec,
        out_shape=jax.ShapeDtypeStruct((n_rows, D), F32),
        compiler_params=_cparams(("arbitrary", "arbitrary"), 56),
        name="moe_ffn",
    )(blk_e, n_used, xs, w1, w3, w2)


def moe_layout(e_pos, counts_f, n_tok):
    RB = _moe_row_block(n_tok)
    n_assign = n_tok * TOP_K
    n_blocks = (n_assign + N_EXPERTS * (RB - 1) + RB - 1) // RB
    counts = counts_f[0, :N_EXPERTS].astype(jnp.int32)
    padded = (counts + RB - 1) // RB * RB
    pad_end = jnp.cumsum(padded)
    pad_start = pad_end - padded
    e = e_pos[:, 0:TOP_K]
    pos = e_pos[:, TOP_K:2 * TOP_K]
    start = jnp.zeros_like(e)
    for x in range(N_EXPERTS):
        start = jnp.where(e == x, pad_start[x], start)
    dest_flat = (start + pos).reshape(-1)
    blk_start = jnp.arange(n_blocks, dtype=jnp.int32) * RB
    blk_e = jnp.minimum(jnp.sum((blk_start[:, None] >= pad_end[None, :]).astype(jnp.int32), axis=1), N_EXPERTS - 1)
    n_used = (pad_end[-1] // RB).astype(jnp.int32).reshape(1)
    return dest_flat, blk_e.astype(jnp.int32), n_used, n_blocks * RB


def _pad_cols(w, width):
    return jnp.pad(w, ((0, 0), (0, width - w.shape[1])))


def kernel(x, c, rel_bias, ada_w, ada_b, ln_g, ln_b, ab_w_in, gla_wg2, gla_bg2, gla_norm_g, ab_w_out,
           ffn_w1, ffn_w3, ffn_w2, nsa_w_in, nsa_cmp_pos, nsa_cmp_w1k, nsa_cmp_w2k, nsa_cmp_w1v, nsa_cmp_w2v,
           nsa_w_out, moe_router, moe_w1, moe_w3, moe_w2):
    B, S, D = x.shape
    n_tok = B * S
    mod = adaln_mod(c, ada_w, ada_b)
    mods = [[m.reshape(B, 1, D) for m in jnp.split(mod[i], 6, axis=-1)] for i in range(DEPTH)]
    toep, cmpb = build_bias_tables(rel_bias, S)

    xs_dead = None
    sh1, sc1 = mods[0][0], mods[0][1]
    h = modulate(x, sc1, sh1)
    for i in range(DEPTH):
        j = i // 2
        _, _, g1, sh2, sc2, g2 = mods[i]
        nxt2 = (mods[i + 1][1], mods[i + 1][0]) if i + 1 < DEPTH else None
        lng = ln_g[i].reshape(2, 1, D)
        lnb = ln_b[i].reshape(2, 1, D)
        if i % 2 == 0:
            w = ab_w_in[j]
            cuts = np.cumsum((MOBA_WIDTH, MOBA_WIDTH, MOBA_WIDTH, GLA_QK_WIDTH, GLA_QK_WIDTH, GLA_V_WIDTH,
                              GLA_V_WIDTH))
            parts = jnp.split(w, cuts.tolist(), axis=1)
            parts[-1] = _pad_cols(parts[-1], LANE)
            w_cat = jnp.concatenate(parts, axis=1).astype(BF16)
            widths = [p.shape[1] for p in parts]
            offs = np.concatenate([[0], np.cumsum(widths)[:-1]]).tolist()
            scales = [MOBA_HEAD_DIM ** -0.5 * LOG2E] + [1.0] * 7
            kinds = ["flat", "flat", "flat_t"] + ["flat"] * 5
            offs[2] = 0
            segs = tuple((o, wd, kd, s) for o, wd, kd, s in zip(offs, widths, kinds, scales))
            dts = [BF16, BF16, BF16, F32, F32, F32, F32, F32]
            mq, mk, mv, gq, gk, gv, gr, glr = project(h, w_cat, segs, dts, wt=parts[2].T.astype(BF16))
            y_moba = moba_attention(mq, mk, mv, toep)
            wg2_pad = jnp.pad(gla_wg2[j], ((0, LANE - GLA_GATE_RANK), (0, 0)))
            y_gla = gla_mixer(gq, gk, gv, gr, glr, wg2_pad, gla_bg2[j].reshape(1, -1), gla_norm_g[j].reshape(1, -1))
            w_out = ab_w_out[j].astype(BF16)
            x, h = out_ln([y_moba, y_gla], [w_out[:MOBA_WIDTH], w_out[MOBA_WIDTH:]], x, g1, lng[0], lnb[0],
                          (sc2, sh2))
            x, h = ffn_ln(h, ffn_w1[j].astype(BF16), ffn_w3[j].astype(BF16), ffn_w2[j].astype(BF16), x, g2,
                          lng[1], lnb[1], nxt2)
        else:
            G, HPG, dh = NSA_KV_GROUPS, NSA_HPG, NSA_HEAD_DIM
            w = nsa_w_in[j]
            wq = w[:, :NSA_Q_WIDTH]
            wkv = w[:, NSA_Q_WIDTH:NSA_Q_WIDTH + 6 * NSA_KV_WIDTH]
            wg = w[:, NSA_Q_WIDTH + 6 * NSA_KV_WIDTH:].reshape(D, G, HPG, 3)
            wg = _pad_cols(wg.transpose(0, 1, 3, 2).reshape(D * G, 3 * HPG), LANE).reshape(D, G * LANE)
            w_cat = jnp.concatenate([wq, wkv, wg], axis=1).astype(BF16)
            segs = [(0, NSA_Q_WIDTH, "flat", dh ** -0.5 * LOG2E)]
            segs += [(NSA_Q_WIDTH + n * NSA_KV_WIDTH, NSA_KV_WIDTH, "group", 1.0) for n in range(6)]
            segs += [(NSA_Q_WIDTH + 6 * NSA_KV_WIDTH, G * LANE, "group", 1.0)]
            dts = [BF16] * 7 + [F32]
            q, kc, vc, ks, vs, kw, vw, gates = project(h, w_cat, tuple(segs), dts)
            half = NSA_CMP_STRIDE * dh
            pos2 = jnp.pad(nsa_cmp_pos[j].reshape(2, half), ((0, 6), (0, 0))).astype(BF16)
            w1k = jnp.concatenate([nsa_cmp_w1k[j][:half], nsa_cmp_w1k[j][half:]], axis=1).astype(BF16)
            w1v = jnp.concatenate([nsa_cmp_w1v[j][:half], nsa_cmp_w1v[j][half:]], axis=1).astype(BF16)
            k_cmp = nsa_compress(kc, pos2, w1k, nsa_cmp_w2k[j].astype(BF16))
            v_cmp = nsa_compress(vc, pos2, w1v, nsa_cmp_w2v[j].astype(BF16))
            y = nsa_attention(q, k_cmp, v_cmp, ks, vs, kw, vw, gates, cmpb, toep)
            x, h = out_ln([y], [nsa_w_out[j].astype(BF16)], x, g1, lng[0], lnb[0], (sc2, sh2), h_dtype=F32)
            h2 = h.reshape(n_tok, D)
            e_pos, w_top, counts = moe_route(h2, _pad_cols(moe_router[j], LANE))
            dest_flat, blk_e, n_used, n_rows = moe_layout(e_pos, counts, n_tok)
            xs = moe_dispatch_rows(h2, dest_flat, n_rows, recycled=xs_dead)
            xs_dead = xs
            stack = lambda w: w.reshape((-1,) + w.shape[2:])
            ys = moe_ffn(xs, blk_e + j * N_EXPERTS, n_used, stack(moe_w1), stack(moe_w3), stack(moe_w2))
            x, h = moe_combine_ln(ys, dest_flat, w_top, x, g2, lng[1], lnb[1], nxt2)
    return x
```
